```python
import jax
import jax.numpy as jnp
from jax import lax
import numpy as np

D_MODEL = 1024
BATCH = 8
SEQ = 2048
DEPTH = 1
DEC_BATCH = 128
DEC_SEQ = 8
PAST_LEN = 8192
PAGE_SIZE = 128

GLA_HEADS = 4
GLA_DK = D_MODEL // 16
GLA_DV = D_MODEL // 8
GLA_GATE_RANK = 16
GLA_TAU = 16.0
GLA_CHUNK = 64
NSA_HEADS = 8
NSA_KV_HEADS = 2
NSA_GROUP = NSA_HEADS // NSA_KV_HEADS
NSA_HD = D_MODEL // 16
NSA_BLOCK = 64
NSA_TOPN = 16
NSA_WINDOW = 512
NSA_WIN_QBLOCK = 128
NSA_SEL_QBLOCK = 16
MEM_LEN = 256
MEM_HEADS = 4
MEM_HD = D_MODEL // 8
ROPE_THETA = 500000.0
ROPE_DIMS = NSA_HD // 4
N_EXPERTS = 32
TOP_K = 4
D_FF = D_MODEL
SWIGLU_LIMIT = 7.0
SWIGLU_ALPHA = 1.702
N_BRANCH = 3
EPS = 1e-6

kernel_name = 'hybrid_gla_nsa_memory_moe_step'

SPLIT_NAMES = ('gla_q', 'gla_k', 'gla_v', 'gla_r', 'gla_lr', 'nsa_q', 'nsa_kv', 'nsa_gate', 'mem_q', 'merge')


def split_sizes():
    return (GLA_HEADS * GLA_DK, GLA_HEADS * GLA_DK, GLA_HEADS * GLA_DV, GLA_HEADS * GLA_DV, GLA_GATE_RANK,
            NSA_HEADS * NSA_HD, 6 * NSA_KV_HEADS * NSA_HD, 3 * NSA_HEADS, MEM_HEADS * MEM_HD, N_BRANCH * D_MODEL)


def split_input_projection(z):
    offs = np.cumsum(split_sizes())[:-1].tolist()
    return dict(zip(SPLIT_NAMES, jnp.split(z, offs, axis=-1)))


def rms_norm(x, g):
    xf = x.astype(jnp.float32)
    y = xf * lax.rsqrt(jnp.mean(xf * xf, axis=-1, keepdims=True) + EPS)
    return (y * g.astype(jnp.float32)).astype(x.dtype)


def rope_partial(x, pos):
    half = ROPE_DIMS // 2
    inv = 1.0 / (ROPE_THETA ** (jnp.arange(half, dtype=jnp.float32) / half))
    ang = pos.astype(jnp.float32)[:, None] * inv[None, :]
    cos = jnp.cos(ang)[None, :, None, :]
    sin = jnp.sin(ang)[None, :, None, :]
    xr = x[..., :ROPE_DIMS].astype(jnp.float32)
    x1, x2 = xr[..., :half], xr[..., half:]
    rot = jnp.concatenate([x1 * cos - x2 * sin, x2 * cos + x1 * sin], axis=-1)
    return jnp.concatenate([rot.astype(x.dtype), x[..., ROPE_DIMS:]], axis=-1)


def masked_softmax(s, mask):
    s = jnp.where(mask, s.astype(jnp.float32), -jnp.inf)
    m = jnp.max(s, axis=-1, keepdims=True)
    m = jnp.where(jnp.isfinite(m), m, 0.0)
    e = jnp.where(mask, jnp.exp(s - m), 0.0)
    return e / jnp.maximum(jnp.sum(e, axis=-1, keepdims=True), 1e-30)


def over_query_blocks(fn, qb, qpos, *xs):
    T = qpos.shape[0]
    nq = T // qb
    def split(a):
        return jnp.swapaxes(a.reshape((a.shape[0], nq, qb) + a.shape[2:]), 0, 1)
    out = lax.map(lambda args: fn(*args), (qpos.reshape(nq, qb),) + tuple(split(a) for a in xs))
    out = jnp.swapaxes(out, 0, 1)
    return out.reshape((out.shape[0], T) + out.shape[3:])


def gla_chunked(q, k, v, log_a, s0):
    B, T, H, DK = q.shape
    DV = v.shape[-1]
    C = GLA_CHUNK
    n = -(-T // C)
    pad = n * C - T
    def prep(a):
        a = jnp.pad(a.astype(jnp.float32), ((0, 0), (0, pad), (0, 0), (0, 0)))
        return a.reshape(B, n, C, H, a.shape[-1]).transpose(1, 0, 3, 2, 4)
    qc, kc, vc, ac = prep(q), prep(k), prep(v), prep(log_a)
    causal = jnp.tril(jnp.ones((C, C), dtype=bool))[None, None, :, :, None]
    def step(S, inp):
        qi, ki, vi, ai = inp
        b = jnp.cumsum(ai, axis=2)
        o_inter = jnp.einsum('bhtd,bhde->bhte', qi * jnp.exp(b), S)
        diff = b[:, :, :, None, :] - b[:, :, None, :, :]
        dec = jnp.exp(jnp.where(causal, diff, -jnp.inf))
        att = jnp.einsum('bhtd,bhsd,bhtsd->bhts', qi, ki, dec)
        o_intra = jnp.einsum('bhts,bhse->bhte', att, vi)
        b_last = b[:, :, -1:, :]
        S_new = S * jnp.exp(b_last[:, :, 0, :, None]) + jnp.einsum('bhsd,bhse->bhde', ki * jnp.exp(b_last - b), vi)
        return S_new, o_inter + o_intra
    S, o = lax.scan(step, s0.astype(jnp.float32), (qc, kc, vc, ac))
    o = o.transpose(1, 0, 3, 2, 4).reshape(B, n * C, H, DV)[:, :T]
    return o, S


def gla_branch(z, s0, p):
    B, T = z['gla_q'].shape[:2]
    q = z['gla_q'].reshape(B, T, GLA_HEADS, GLA_DK) * (GLA_DK ** -0.5)
    k = z['gla_k'].reshape(B, T, GLA_HEADS, GLA_DK)
    v = z['gla_v'].reshape(B, T, GLA_HEADS, GLA_DV)
    log_a = jax.nn.log_sigmoid((z['gla_lr'] @ p['w_gla_gate'] + p['b_gla_gate']).astype(jnp.float32)) / GLA_TAU
    o, S = gla_chunked(q, k, v, log_a.reshape(B, T, GLA_HEADS, GLA_DK), s0)
    o = rms_norm(o, p['g_gla_out']) * jax.nn.silu(z['gla_r'].reshape(B, T, GLA_HEADS, GLA_DV).astype(jnp.float32))
    return o.reshape(B, T, GLA_HEADS * GLA_DV), S


def pad_to_block(x):
    pad = (-x.shape[1]) % NSA_BLOCK
    return jnp.pad(x, ((0, 0), (0, pad)) + ((0, 0),) * (x.ndim - 2))


def to_blocks(x):
    x = pad_to_block(x)
    B, L, G, D = x.shape
    return x.reshape(B, L // NSA_BLOCK, NSA_BLOCK, G, D).transpose(0, 1, 3, 2, 4)


def compress_blocks(x, w_pos):
    return jnp.einsum('bngjd,jd->bngd', to_blocks(x), w_pos)


def compressed_attend(q, qpos, kc, vc):
    B, T = q.shape[:2]
    nb = kc.shape[1]
    qg = q.reshape(B, T, NSA_KV_HEADS, NSA_GROUP, NSA_HD)
    s = jnp.einsum('btghd,bngd->btghn', qg, kc) * (NSA_HD ** -0.5)
    vis = (jnp.arange(nb)[None, :] + 1) * NSA_BLOCK <= qpos[:, None] + 1
    pr = masked_softmax(s, vis[None, :, None, None, :])
    o = jnp.einsum('btghn,bngd->btghd', pr, vc.astype(jnp.float32))
    return o.reshape(B, T, NSA_HEADS, NSA_HD), jnp.sum(pr, axis=3)


def select_blocks(imp, qpos):
    B, T, G, nb = imp.shape
    n_top = NSA_TOPN - 1
    cur = (qpos // NSA_BLOCK).astype(jnp.int32)
    cand = jnp.arange(nb)[None, :] < cur[:, None]
    score = jnp.where(cand[None, :, None, :], imp.astype(jnp.float32), -jnp.inf)
    if nb < n_top:
        score = jnp.pad(score, ((0, 0), (0, 0), (0, 0), (0, n_top - nb)), constant_values=-jnp.inf)
    top_s, top_i = lax.top_k(score, n_top)
    ok = jnp.isfinite(top_s)
    cur_i = jnp.broadcast_to(cur[None, :, None, None], (B, T, G, 1))
    idx = jnp.concatenate([cur_i, jnp.where(ok, top_i, 0).astype(jnp.int32)], axis=-1)
    valid = jnp.concatenate([jnp.ones((B, T, G, 1), dtype=bool), ok], axis=-1)
    return idx, valid


def gather_local(kb, vb, idx):
    B, _, G = idx.shape[:3]
    bi = jnp.arange(B)[:, None, None, None]
    gi = jnp.arange(G)[None, None, :, None]
    return kb[bi, idx, gi], vb[bi, idx, gi]


def gather_paged(pool_k, pool_v, page_table, kb_new, vb_new, n_past_blocks, idx):
    B, _, G = idx.shape[:3]
    bpp = pool_k.shape[1] // NSA_BLOCK
    bi = jnp.arange(B)[:, None, None, None]
    gi = jnp.arange(G)[None, None, :, None]
    in_past = idx < n_past_blocks
    pb = jnp.minimum(idx, n_past_blocks - 1)
    phys = page_table[bi, pb // bpp]
    off = (pb % bpp)[..., None] * NSA_BLOCK + jnp.arange(NSA_BLOCK)
    k_past = pool_k[phys[..., None], off, gi[..., None]]
    v_past = pool_v[phys[..., None], off, gi[..., None]]
    lb = jnp.clip(idx - n_past_blocks, 0, kb_new.shape[1] - 1)
    k_new = kb_new[bi, lb, gi]
    v_new = vb_new[bi, lb, gi]
    sel = in_past[..., None, None]
    return jnp.where(sel, k_past, k_new.astype(k_past.dtype)), jnp.where(sel, v_past, v_new.astype(v_past.dtype))


def selected_attend(qpos, q, idx, valid, gather):
    B, T = q.shape[:2]
    ks, vs = gather(idx)
    kpos = idx[..., None] * NSA_BLOCK + jnp.arange(NSA_BLOCK)
    mask = valid[..., None] & (kpos <= qpos[None, :, None, None, None])
    n = NSA_TOPN * NSA_BLOCK
    qg = q.reshape(B, T, NSA_KV_HEADS, NSA_GROUP, NSA_HD)
    s = jnp.einsum('btghd,btgkd->btghk', qg, ks.reshape(B, T, NSA_KV_HEADS, n, NSA_HD)) * (NSA_HD ** -0.5)
    pr = masked_softmax(s, mask.reshape(B, T, NSA_KV_HEADS, 1, n))
    o = jnp.einsum('btghk,btgkd->btghd', pr, vs.reshape(B, T, NSA_KV_HEADS, n, NSA_HD).astype(jnp.float32))
    return o.reshape(B, T, NSA_HEADS, NSA_HD)


def window_attend(q, qpos, k_ext, v_ext, kpos_ext, prefix, qb):
    B, T = q.shape[:2]
    nq = T // qb
    slab = jnp.arange(nq)[:, None] * qb + jnp.arange(prefix + qb)[None, :]
    ks = k_ext[:, slab]
    vs = v_ext[:, slab]
    kp = kpos_ext[slab]
    rel = qpos.reshape(nq, qb)[:, :, None] - kp[:, None, :]
    mask = (rel >= 0) & (rel < NSA_WINDOW) & (kp[:, None, :] >= 0)
    qg = q.reshape(B, nq, qb, NSA_KV_HEADS, NSA_GROUP, NSA_HD)
    s = jnp.einsum('bnqghd,bnkgd->bnqghk', qg, ks) * (NSA_HD ** -0.5)
    pr = masked_softmax(s, mask[None, :, :, None, None, :])
    o = jnp.einsum('bnqghk,bnkgd->bnqghd', pr, vs.astype(jnp.float32))
    return o.reshape(B, T, NSA_HEADS, NSA_HD)


def nsa_combine(z, o_cmp, o_sel, o_win, dtype):
    B, T = o_cmp.shape[:2]
    gt = jax.nn.sigmoid(z['nsa_gate'].reshape(B, T, NSA_HEADS, 3).astype(jnp.float32))
    o = gt[..., 0:1] * o_cmp + gt[..., 1:2] * o_sel + gt[..., 2:3] * o_win
    return o.reshape(B, T, NSA_HEADS * NSA_HD).astype(dtype)


def memory_kv(mem, p):
    B, M, _ = mem.shape
    kv = (rms_norm(mem, p['g_mem']) @ p['w_mem_kv']).reshape(B, M, 2, MEM_HEADS, MEM_HD)
    return rms_norm(kv[:, :, 0], p['g_k_mem']), kv[:, :, 1]


def memory_attend(q, mk, mv, dtype):
    B, T = q.shape[:2]
    s = jnp.einsum('bthd,bmhd->bthm', q, mk) * (MEM_HD ** -0.5)
    pr = jax.nn.softmax(s.astype(jnp.float32), axis=-1)
    o = jnp.einsum('bthm,bmhd->bthd', pr, mv.astype(jnp.float32))
    return o.reshape(B, T, MEM_HEADS * MEM_HD).astype(dtype)


def mixer_inputs(x, pos, p):
    B, T, _ = x.shape
    z = split_input_projection(rms_norm(x, p['g_attn']) @ p['w_in'])
    q = rope_partial(rms_norm(z['nsa_q'].reshape(B, T, NSA_HEADS, NSA_HD), p['g_q_nsa']), pos)
    kv = z['nsa_kv'].reshape(B, T, 6, NSA_KV_HEADS, NSA_HD)
    g_k = p['g_k_nsa']
    k_c = rope_partial(rms_norm(kv[:, :, 0], g_k[0]), pos)
    k_s = rope_partial(rms_norm(kv[:, :, 2], g_k[1]), pos)
    k_w = rope_partial(rms_norm(kv[:, :, 4], g_k[2]), pos)
    q_mem = rms_norm(z['mem_q'].reshape(B, T, MEM_HEADS, MEM_HD), p['g_q_mem'])
    return z, q, (k_c, kv[:, :, 1], k_s, kv[:, :, 3], k_w, kv[:, :, 5]), q_mem


def merge_branches(x, z, o_gla, o_nsa, o_mem, p):
    B, T, _ = x.shape
    g = jax.nn.sigmoid(z['merge'].reshape(B, T, N_BRANCH, D_MODEL))
    y = (g[:, :, 0] * (o_gla.astype(x.dtype) @ p['w_up_gla'])
         + g[:, :, 1] * (o_nsa @ p['w_up_nsa'])
         + g[:, :, 2] * (o_mem @ p['w_up_mem']))
    return x + y @ p['w_out']


def moe_ffn(t, p):
    h = rms_norm(t, p['g_ffn'])
    logits = (h @ p['w_router'] + p['b_router']).astype(jnp.float32)
    top_v, top_i = lax.top_k(logits, TOP_K)
    top_w = jax.nn.softmax(top_v, axis=-1)
    combine = jnp.einsum('nk,nke->ne', top_w, jax.nn.one_hot(top_i, N_EXPERTS, dtype=jnp.float32))
    out = jnp.zeros(t.shape, jnp.float32)
    for e in range(N_EXPERTS):
        gu = h @ p['w_gate_up'][e] + p['b_gate_up'][e]
        gate = jnp.minimum(gu[:, 0::2], SWIGLU_LIMIT)
        up = jnp.clip(gu[:, 1::2], -SWIGLU_LIMIT, SWIGLU_LIMIT)
        act = (up + 1.0) * gate * jax.nn.sigmoid(SWIGLU_ALPHA * gate)
        out = out + combine[:, e:e + 1] * (act @ p['w_down'][e] + p['b_down'][e])
    return out.astype(t.dtype)


def prompt_mixer(x, mem, p):
    B, S, _ = x.shape
    pos = jnp.arange(S, dtype=jnp.int32)
    z, q, (k_c, v_c, k_s, v_s, k_w, v_w), q_mem = mixer_inputs(x, pos, p)
    o_gla, s_gla = gla_branch(z, jnp.zeros((B, GLA_HEADS, GLA_DK, GLA_DV), jnp.float32), p)
    kc_blk = compress_blocks(k_c, p['w_cmp_pos'][0])
    vc_blk = compress_blocks(v_c, p['w_cmp_pos'][1])
    o_cmp, imp = compressed_attend(q, pos, kc_blk, vc_blk)
    idx, valid = select_blocks(imp, pos)
    kb, vb = to_blocks(k_s), to_blocks(v_s)
    gather = lambda jj: gather_local(kb, vb, jj)
    o_sel = over_query_blocks(lambda qp, qi, ii, vi: selected_attend(qp, qi, ii, vi, gather),
                              min(NSA_SEL_QBLOCK, S), pos, q, idx, valid)
    pad = ((0, 0), (NSA_WINDOW, 0), (0, 0), (0, 0))
    o_win = window_attend(q, pos, jnp.pad(k_w, pad), jnp.pad(v_w, pad),
                          jnp.arange(-NSA_WINDOW, S, dtype=jnp.int32), NSA_WINDOW, min(NSA_WIN_QBLOCK, S))
    o_nsa = nsa_combine(z, o_cmp, o_sel, o_win, x.dtype)
    mem_k, mem_v = memory_kv(mem, p)
    o_mem = memory_attend(q_mem, mem_k, mem_v, x.dtype)
    h = merge_branches(x, z, o_gla, o_nsa, o_mem, p)
    wb = min(NSA_WINDOW, S)
    return h, (k_c, v_c, k_s, v_s, k_w[:, S - wb:], v_w[:, S - wb:], s_gla.astype(x.dtype), mem_k, mem_v)


def sample_mixer(x, c_cmp_k, c_cmp_v, c_sel_k, c_sel_v, c_win_k, c_win_v, s_gla0, c_mem_k, c_mem_v, page_table, p):
    B, T, _ = x.shape
    past_len = page_table.shape[1] * c_cmp_k.shape[1]
    pos = past_len + jnp.arange(T, dtype=jnp.int32)
    z, q, (k_c, v_c, k_s, v_s, k_w, v_w), q_mem = mixer_inputs(x, pos, p)
    o_gla, s_gla = gla_branch(z, s_gla0, p)
    def past_rows(pool):
        return pool[page_table].reshape((B, past_len) + pool.shape[2:])
    kc_blk = jnp.concatenate([compress_blocks(past_rows(c_cmp_k), p['w_cmp_pos'][0]),
                              compress_blocks(k_c.astype(c_cmp_k.dtype), p['w_cmp_pos'][0])], axis=1)
    vc_blk = jnp.concatenate([compress_blocks(past_rows(c_cmp_v), p['w_cmp_pos'][1]),
                              compress_blocks(v_c.astype(c_cmp_v.dtype), p['w_cmp_pos'][1])], axis=1)
    o_cmp, imp = compressed_attend(q, pos, kc_blk, vc_blk)
    idx, valid = select_blocks(imp, pos)
    n_past_blocks = past_len // NSA_BLOCK
    kb_new, vb_new = to_blocks(k_s), to_blocks(v_s)
    gather = lambda jj: gather_paged(c_sel_k, c_sel_v, page_table, kb_new, vb_new, n_past_blocks, jj)
    o_sel = over_query_blocks(lambda qp, qi, ii, vi: selected_attend(qp, qi, ii, vi, gather), 1, pos, q, idx, valid)
    wb = c_win_k.shape[1]
    k_ext = jnp.concatenate([c_win_k, k_w.astype(c_win_k.dtype)], axis=1)
    v_ext = jnp.concatenate([c_win_v, v_w.astype(c_win_v.dtype)], axis=1)
    kpos = past_len - wb + jnp.arange(wb + T, dtype=jnp.int32)
    o_win = window_attend(q, pos, k_ext, v_ext, kpos, wb, T)
    o_nsa = nsa_combine(z, o_cmp, o_sel, o_win, x.dtype)
    o_mem = memory_attend(q_mem, c_mem_k, c_mem_v, x.dtype)
    h = merge_branches(x, z, o_gla, o_nsa, o_mem, p)
    return h, (k_c, v_c, k_s, v_s, k_ext[:, T:], v_ext[:, T:], s_gla.astype(s_gla0.dtype))


def stack_layers(states, i):
    return jnp.stack([st[i] for st in states], axis=0)


def setup_inputs(seed: int = 0) -> dict:
    key = jax.random.key(seed)
    keys = iter(jax.random.split(key, 64))
    def nrm(shape, scale=1.0):
        return jax.random.normal(next(keys), shape, jnp.float32) * scale
    def gain(shape):
        return 1.0 + 0.01 * jax.random.normal(next(keys), shape, jnp.float32)
    L = DEPTH
    n_pages = PAST_LEN // PAGE_SIZE
    n_used = DEC_BATCH * n_pages
    n_pool = n_used + max(1, n_used // 4)
    win_buf = min(NSA_WINDOW, PAST_LEN)
    kvs = (NSA_KV_HEADS, NSA_HD)
    in_width = sum(split_sizes())
    gla_w = GLA_HEADS * GLA_DV
    nsa_w = NSA_HEADS * NSA_HD
    mem_w = MEM_HEADS * MEM_HD
    d = {}
    d['x_prompt'] = nrm((BATCH, SEQ, D_MODEL))
    d['x_sample'] = nrm((DEC_BATCH, DEC_SEQ, D_MODEL))
    d['cache_cmp_k'] = nrm((L, n_pool, PAGE_SIZE) + kvs)
    d['cache_cmp_v'] = nrm((L, n_pool, PAGE_SIZE) + kvs)
    d['cache_sel_k'] = nrm((L, n_pool, PAGE_SIZE) + kvs)
    d['cache_sel_v'] = nrm((L, n_pool, PAGE_SIZE) + kvs)
    d['cache_win_k'] = nrm((L, DEC_BATCH, win_buf) + kvs)
    d['cache_win_v'] = nrm((L, DEC_BATCH, win_buf) + kvs)
    d['state_gla'] = nrm((L, DEC_BATCH, GLA_HEADS, GLA_DK, GLA_DV), 0.5)
    d['cache_mem_k'] = nrm((L, DEC_BATCH, MEM_LEN, MEM_HEADS, MEM_HD))
    d['cache_mem_v'] = nrm((L, DEC_BATCH, MEM_LEN, MEM_HEADS, MEM_HD))
    d['page_table'] = jax.random.permutation(next(keys), n_pool)[:n_used].reshape(DEC_BATCH, n_pages).astype(jnp.int32)
    d['mem_prompt'] = nrm((BATCH, MEM_LEN, D_MODEL))
    d['g_attn'] = gain((L, D_MODEL))
    d['w_in'] = nrm((L, D_MODEL, in_width), D_MODEL ** -0.5)
    d['w_gla_gate'] = nrm((L, GLA_GATE_RANK, GLA_HEADS * GLA_DK), GLA_GATE_RANK ** -0.5)
    d['b_gla_gate'] = nrm((L, GLA_HEADS * GLA_DK), 0.1)
    d['g_gla_out'] = gain((L, GLA_DV))
    d['g_q_nsa'] = gain((L, NSA_HD))
    d['g_k_nsa'] = gain((L, 3, NSA_HD))
    d['w_cmp_pos'] = (1.0 + 0.1 * nrm((L, 2, NSA_BLOCK, NSA_HD))) / NSA_BLOCK
    d['g_q_mem'] = gain((L, MEM_HD))
    d['g_k_mem'] = gain((L, MEM_HD))
    d['g_mem'] = gain((L, D_MODEL))
    d['w_mem_kv'] = nrm((L, D_MODEL, 2 * mem_w), D_MODEL ** -0.5)
    d['w_up_gla'] = nrm((L, gla_w, D_MODEL), gla_w ** -0.5)
    d['w_up_nsa'] = nrm((L, nsa_w, D_MODEL), nsa_w ** -0.5)
    d['w_up_mem'] = nrm((L, mem_w, D_MODEL), mem_w ** -0.5)
    d['w_out'] = nrm((L, D_MODEL, D_MODEL), D_MODEL ** -0.5)
    d['g_ffn'] = gain((L, D_MODEL))
    d['w_router'] = nrm((L, D_MODEL, N_EXPERTS), D_MODEL ** -0.5)
    d['b_router'] = nrm((L, N_EXPERTS), 0.01)
    d['w_gate_up'] = nrm((L, N_EXPERTS, D_MODEL, 2 * D_FF), D_MODEL ** -0.5)
    d['b_gate_up'] = nrm((L, N_EXPERTS, 2 * D_FF), 0.01)
    d['w_down'] = nrm((L, N_EXPERTS, D_FF, D_MODEL), D_FF ** -0.5)
    d['b_down'] = nrm((L, N_EXPERTS, D_MODEL), 0.01)
    return d


def reference(x_prompt, x_sample, cache_cmp_k, cache_cmp_v, cache_sel_k, cache_sel_v, cache_win_k, cache_win_v,
              state_gla, cache_mem_k, cache_mem_v, page_table, mem_prompt,
              g_attn, w_in, w_gla_gate, b_gla_gate, g_gla_out, g_q_nsa, g_k_nsa, w_cmp_pos, g_q_mem, g_k_mem,
              g_mem, w_mem_kv, w_up_gla, w_up_nsa, w_up_mem, w_out, g_ffn, w_router, b_router,
              w_gate_up, b_gate_up, w_down, b_down):
    B, S, _ = x_prompt.shape
    DB, T, _ = x_sample.shape
    h_p, h_s = x_prompt, x_sample
    new_p, new_s = [], []
    for l in range(DEPTH):
        p = {'g_attn': g_attn[l], 'w_in': w_in[l], 'w_gla_gate': w_gla_gate[l], 'b_gla_gate': b_gla_gate[l],
             'g_gla_out': g_gla_out[l], 'g_q_nsa': g_q_nsa[l], 'g_k_nsa': g_k_nsa[l], 'w_cmp_pos': w_cmp_pos[l],
             'g_q_mem': g_q_mem[l], 'g_k_mem': g_k_mem[l], 'g_mem': g_mem[l], 'w_mem_kv': w_mem_kv[l],
             'w_up_gla': w_up_gla[l], 'w_up_nsa': w_up_nsa[l], 'w_up_mem': w_up_mem[l], 'w_out': w_out[l],
             'g_ffn': g_ffn[l], 'w_router': w_router[l], 'b_router': b_router[l], 'w_gate_up': w_gate_up[l],
             'b_gate_up': b_gate_up[l], 'w_down': w_down[l], 'b_down': b_down[l]}
        h_p, st_p = prompt_mixer(h_p, mem_prompt, p)
        h_s, st_s = sample_mixer(h_s, cache_cmp_k[l], cache_cmp_v[l], cache_sel_k[l], cache_sel_v[l],
                                 cache_win_k[l], cache_win_v[l], state_gla[l], cache_mem_k[l], cache_mem_v[l],
                                 page_table, p)
        f = moe_ffn(jnp.concatenate([h_p.reshape(B * S, D_MODEL), h_s.reshape(DB * T, D_MODEL)], axis=0), p)
        h_p = h_p + f[:B * S].reshape(B, S, D_MODEL)
        h_s = h_s + f[B * S:].reshape(DB, T, D_MODEL)
        new_p.append(st_p)
        new_s.append(st_s)
    return (h_p, h_s,
            stack_layers(new_p, 0), stack_layers(new_p, 1), stack_layers(new_p, 2), stack_layers(new_p, 3),
            stack_layers(new_p, 4), stack_layers(new_p, 5), stack_layers(new_p, 6), stack_layers(new_p, 7),
            stack_layers(new_p, 8),
            stack_layers(new_s, 0), stack_layers(new_s, 1), stack_layers(new_s, 2), stack_layers(new_s, 3),
            stack_layers(new_s, 4), stack_layers(new_s, 5), stack_layers(new_s, 6))
```

```python
import functools

import numpy as np
import jax
import jax.numpy as jnp
from jax import lax
from jax.experimental import pallas as pl
from jax.experimental.pallas import tpu as pltpu

F32 = jnp.float32
BF16 = jnp.bfloat16

GLA_HEADS = 4
GLA_GATE_RANK = 16
GLA_TAU = 16.0
GLA_CHUNK = 64
GLA_SUB = 16
NSA_HEADS = 8
NSA_KV_HEADS = 2
NSA_GROUP = NSA_HEADS // NSA_KV_HEADS
NSA_BLOCK = 64
NSA_TOPN = 16
NSA_WINDOW = 512
MEM_HEADS = 4
ROPE_THETA = 500000.0
N_BRANCH = 3
TOP_K = 4
SWIGLU_LIMIT = 7.0
SWIGLU_ALPHA = 1.702
EPS = 1e-6

LANES = 128
NEG = -1e30
VMEM_LIMIT = 56 * 1024 * 1024


def _cp(sem):
    return pltpu.CompilerParams(dimension_semantics=sem, vmem_limit_bytes=VMEM_LIMIT)


def _dot(a, b):
    return jnp.dot(a.astype(BF16), b.astype(BF16), preferred_element_type=F32)


def _dot_nt(a, b):
    return lax.dot_general(a.astype(BF16), b.astype(BF16), (((1,), (1,)), ((), ())),
                           preferred_element_type=F32)


def _dot_tn(a, b):
    return lax.dot_general(a.astype(BF16), b.astype(BF16), (((0,), (0,)), ((), ())),
                           preferred_element_type=F32)


def _split3(x):
    hi = x.astype(BF16)
    r = x - hi.astype(F32)
    mid = r.astype(BF16)
    lo = (r - mid.astype(F32)).astype(BF16)
    return hi, mid, lo


def _dot_exact_rhs(x, m):
    hi, mid, lo = _split3(x)
    d = functools.partial(jnp.dot, preferred_element_type=F32)
    return d(hi, m) + d(mid, m) + d(lo, m)


def _dot_exact_lhs(m, x):
    hi, mid, lo = _split3(x)
    d = functools.partial(jnp.dot, preferred_element_type=F32)
    return d(m, hi) + d(m, mid) + d(m, lo)


def _dot_hi(x, w_hi, w_lo):
    hi, mid, _ = _split3(x)
    d = functools.partial(jnp.dot, preferred_element_type=F32)
    return d(hi, w_hi) + d(mid, w_hi) + d(hi, w_lo)


def _msoftmax(s, mask):
    s = jnp.where(mask, s, NEG)
    m = jnp.max(s, axis=-1, keepdims=True)
    e = jnp.where(mask, jnp.exp(s - m), 0.0)
    return e / jnp.maximum(jnp.sum(e, axis=-1, keepdims=True), 1e-30)


def _seg_ones(width, seg):
    r = lax.broadcasted_iota(jnp.int32, (width, width), 0) // seg
    c = lax.broadcasted_iota(jnp.int32, (width, width), 1) // seg
    return jnp.where(r == c, 1.0, 0.0).astype(BF16)


def _seg_rms(x, seg, gain):
    ms = _dot_exact_rhs(x * x, _seg_ones(x.shape[-1], seg)) * (1.0 / seg)
    return x * lax.rsqrt(ms + EPS) * gain


def _sigmoid(x):
    return 1.0 / (1.0 + jnp.exp(-x))


def _proj_kernel(x_ref, g_ref, wa_ref, wb_ref, wc_ref, wd_ref, za_ref, zb_ref, zc_ref, zd_ref):
    x = x_ref[...]
    u = x * lax.rsqrt(jnp.mean(x * x, axis=-1, keepdims=True) + EPS) * g_ref[...]
    ub = u.astype(BF16)
    za_ref[...] = jnp.dot(ub, wa_ref[...], preferred_element_type=F32)
    zb_ref[...] = jnp.dot(ub, wb_ref[...], preferred_element_type=F32)
    zc_ref[...] = _sigmoid(jnp.dot(ub, wc_ref[...], preferred_element_type=F32)).astype(BF16)
    zd_ref[...] = jnp.dot(ub, wd_ref[...], preferred_element_type=F32)


def _project(x, g, wa, wb, wc, wd, tm):
    n, d = x.shape
    full = lambda w: pl.BlockSpec(w.shape, lambda i: (0, 0))
    row = lambda w: pl.BlockSpec((tm, w), lambda i: (i, 0))
    return pl.pallas_call(
        _proj_kernel,
        grid=(n // tm,),
        in_specs=[row(d), full(g), full(wa), full(wb), full(wc), full(wd)],
        out_specs=[row(wa.shape[1]), row(wb.shape[1]), row(wc.shape[1]), row(wd.shape[1])],
        out_shape=[jax.ShapeDtypeStruct((n, wa.shape[1]), F32), jax.ShapeDtypeStruct((n, wb.shape[1]), F32),
                   jax.ShapeDtypeStruct((n, wc.shape[1]), BF16), jax.ShapeDtypeStruct((n, wd.shape[1]), F32)],
        compiler_params=_cp(("parallel",)),
        name="proj",
    )(x, g, wa, wb, wc, wd)


def _rope(x, tab):
    c, s1, s2 = tab[:, 0:LANES], tab[:, LANES:2 * LANES], tab[:, 2 * LANES:3 * LANES]
    half = 8
    return x * c + pltpu.roll(x, LANES - half, 1) * s1 + pltpu.roll(x, half, 1) * s2


def _prep_kernel(zb_ref, tab_ref, gq_ref, gk_ref, gm_ref,
                 q_ref, kc_ref, vc_ref, ks_ref, vs_ref, kw_ref, vw_ref, qm_ref, qb_ref, kvb_ref):
    z = zb_ref[...]
    tab = tab_ref[...]
    hd = NSA_BLOCK
    nq = NSA_HEADS * hd
    qn = _seg_rms(z[:, 0:nq], hd, gq_ref[...])
    q = jnp.concatenate([_rope(qn[:, j * LANES:(j + 1) * LANES], tab) for j in range(nq // LANES)], axis=1)
    q = q * (hd ** -0.5)
    q_ref[...] = q
    qb_ref[...] = q.astype(BF16)
    kv = z[:, nq:nq + 6 * LANES]
    gk = gk_ref[...]
    ks = []
    for j, ref in ((0, kc_ref), (1, ks_ref), (2, kw_ref)):
        kj = _rope(_seg_rms(kv[:, 2 * j * LANES:(2 * j + 1) * LANES], hd, gk[j:j + 1, :]), tab)
        ref[...] = kj
        ks.append(kj)
    vs = []
    for j, ref in ((0, vc_ref), (1, vs_ref), (2, vw_ref)):
        vj = kv[:, (2 * j + 1) * LANES:(2 * j + 2) * LANES]
        ref[...] = vj
        vs.append(vj)
    kvb_ref[...] = jnp.concatenate([ks[1], vs[1], ks[2], vs[2]], axis=1).astype(BF16)
    zm = z[:, nq + 6 * LANES:]
    mhd = zm.shape[1] // MEM_HEADS
    qm_ref[...] = (_seg_rms(zm, mhd, gm_ref[...]) * (mhd ** -0.5)).astype(BF16)


def _prep(zb, tab, gq, gk, gm, tm, n_prompt_tiles, tab_tiles):
    n = zb.shape[0]
    row = lambda w: pl.BlockSpec((tm, w), lambda i: (i, 0))
    full = lambda a: pl.BlockSpec(a.shape, lambda i: (0, 0))
    tab_spec = pl.BlockSpec((tm, 3 * LANES), lambda i: (jnp.where(i < n_prompt_tiles, i % tab_tiles, tab_tiles), 0))
    sd = lambda w, dt=F32: jax.ShapeDtypeStruct((n, w), dt)
    return pl.pallas_call(
        _prep_kernel,
        grid=(n // tm,),
        in_specs=[row(zb.shape[1]), tab_spec, full(gq), full(gk), full(gm)],
        out_specs=[row(512)] + [row(LANES)] * 6 + [row(512), row(512), row(512)],
        out_shape=[sd(512)] + [sd(LANES)] * 6 + [sd(512, BF16), sd(512, BF16), sd(512, BF16)],
        compiler_params=_cp(("parallel",)),
        name="nsa_prep",
    )(zb, tab, gq, gk, gm)


def _pool_rows(x, w):
    nblk = x.shape[0] // NSA_BLOCK
    return jnp.sum(x.reshape(nblk, NSA_BLOCK, LANES) * w[None], axis=1)


def _compress_kernel(k_ref, v_ref, w_ref, ck_ref, cv_ref):
    ck_ref[...] = _pool_rows(k_ref[...], w_ref[0])
    cv_ref[...] = _pool_rows(v_ref[...], w_ref[1])


def _compress(kc, vc, w2, n_rows, tm):
    row = pl.BlockSpec((tm, LANES), lambda i: (i, 0))
    out = pl.BlockSpec((tm // NSA_BLOCK, LANES), lambda i: (i, 0))
    sd = jax.ShapeDtypeStruct((n_rows // NSA_BLOCK, LANES), F32)
    return pl.pallas_call(
        _compress_kernel,
        grid=(n_rows // tm,),
        in_specs=[row, row, pl.BlockSpec(w2.shape, lambda i: (0, 0, 0))],
        out_specs=[out, out],
        out_shape=[sd, sd],
        compiler_params=_cp(("parallel",)),
        name="compress",
    )(kc, vc, w2)


def _select(imp, cur, n_cand):
    n_iota = lax.broadcasted_iota(jnp.int32, imp.shape, 1)
    cnt = jnp.zeros(imp.shape, F32)
    for m in range(n_cand):
        col = imp[:, m:m + 1]
        beats = (col > imp) | ((col == imp) & (n_iota > m))
        cnt = cnt + jnp.where(beats & (cur > m), 1.0, 0.0)
    keep = (n_iota == cur) | ((n_iota < cur) & (cnt < NSA_TOPN - 1))
    return jnp.where(keep, 1.0, 0.0)


def _expand_blocks(nb, nkeys, first_key=0):
    kb = (lax.broadcasted_iota(jnp.int32, (nb, nkeys), 1) + first_key) // NSA_BLOCK
    return jnp.where(kb == lax.broadcasted_iota(jnp.int32, (nb, nkeys), 0), 1.0, 0.0).astype(BF16)


def _gate_col(gate, head, j):
    c = GLA_GATE_RANK + head * 3 + j
    return _sigmoid(gate[:, c:c + 1])


def _nsa_prompt_kernel(q_ref, ck_ref, cv_ref, kv_ref, gate_ref, o_ref, *, tq, seq, wslab):
    i = pl.program_id(1)
    t0 = i * tq
    hd = NSA_BLOCK
    nb = seq // NSA_BLOCK
    q = q_ref[...]
    gate = gate_ref[...]
    pos = t0 + lax.broadcasted_iota(jnp.int32, (tq, 1), 0)
    cur = pos // NSA_BLOCK
    blk = lax.broadcasted_iota(jnp.int32, (tq, nb), 1)
    vis = (blk + 1) * NSA_BLOCK <= pos + 1
    kpos = lax.broadcasted_iota(jnp.int32, (tq, seq), 1)
    causal = kpos <= pos
    wstart = pl.multiple_of(jnp.maximum(t0 + tq - wslab, 0), LANES)
    wpos = wstart + lax.broadcasted_iota(jnp.int32, (tq, wslab), 1)
    wrel = pos - wpos
    wmask = (wrel >= 0) & (wrel < NSA_WINDOW)
    expand = _expand_blocks(nb, seq)
    ck = ck_ref[...]
    cv = cv_ref[...]
    outs = []
    for g in range(NSA_KV_HEADS):
        ckg = ck[:, g * hd:(g + 1) * hd]
        cvg = cv[:, g * hd:(g + 1) * hd]
        o_cmp = []
        imp = jnp.zeros((tq, nb), F32)
        for h in range(NSA_GROUP):
            qh = q[:, (g * NSA_GROUP + h) * hd:(g * NSA_GROUP + h + 1) * hd]
            pr = _msoftmax(_dot_nt(qh, ckg), vis)
            imp = imp + pr
            o_cmp.append(_dot(pr, cvg))
        sel = _select(imp, cur, nb)
        smask = (_dot(sel, expand) > 0.5) & causal
        ksg = kv_ref[:, g * hd:(g + 1) * hd]
        vsg = kv_ref[:, LANES + g * hd:LANES + (g + 1) * hd]
        kwg = kv_ref[pl.ds(wstart, wslab), 2 * LANES + g * hd:2 * LANES + (g + 1) * hd]
        vwg = kv_ref[pl.ds(wstart, wslab), 3 * LANES + g * hd:3 * LANES + (g + 1) * hd]
        for h in range(NSA_GROUP):
            head = g * NSA_GROUP + h
            qh = q[:, head * hd:(head + 1) * hd]
            o_sel = _dot(_msoftmax(_dot_nt(qh, ksg), smask), vsg)
            o_win = _dot(_msoftmax(_dot_nt(qh, kwg), wmask), vwg)
            outs.append(_gate_col(gate, head, 0) * o_cmp[h] + _gate_col(gate, head, 1) * o_sel
                        + _gate_col(gate, head, 2) * o_win)
    o_ref[...] = jnp.concatenate(outs, axis=1).astype(BF16)


def _nsa_prompt(qb, ck, cv, kvb, zd, batch, seq, tq):
    nq = seq // tq
    nb = seq // NSA_BLOCK
    wslab = min(seq, NSA_WINDOW + tq)
    kern = functools.partial(_nsa_prompt_kernel, tq=tq, seq=seq, wslab=wslab)
    return pl.pallas_call(
        kern,
        grid=(batch, nq),
        in_specs=[pl.BlockSpec((tq, 512), lambda b, i: (b * nq + i, 0)),
                  pl.BlockSpec((nb, LANES), lambda b, i: (b, 0)),
                  pl.BlockSpec((nb, LANES), lambda b, i: (b, 0)),
                  pl.BlockSpec((seq, 512), lambda b, i: (b, 0)),
                  pl.BlockSpec((tq, LANES), lambda b, i: (b * nq + i, 0))],
        out_specs=pl.BlockSpec((tq, 512), lambda b, i: (b * nq + i, 0)),
        out_shape=jax.ShapeDtypeStruct((batch * seq, 512), BF16),
        compiler_params=_cp(("parallel", "parallel")),
        name="nsa_prompt",
    )(qb, ck, cv, kvb, zd)


def _qbd(q, rows_per_head):
    hd = NSA_BLOCK
    lane_grp = lax.broadcasted_iota(jnp.int32, (rows_per_head, LANES), 1) // hd
    parts = []
    for head in range(NSA_HEADS):
        g = head // NSA_GROUP
        qh = q[:, head * hd:(head + 1) * hd]
        two = jnp.concatenate([qh, qh], axis=1)
        parts.append(jnp.where(lane_grp == g, two, 0.0))
    return jnp.concatenate(parts, axis=0)


def _nsa_sample_cmp_kernel(pt_ref, *refs, npg, t_new, past_len):
    kpages = refs[0:npg]
    vpages = refs[npg:2 * npg]
    q_ref, kn_ref, vn_ref, w_ref, ocmp_ref, sel_ref, pk_ref, pv_ref = refs[2 * npg:]
    c = pl.program_id(1)
    nch = pl.num_programs(1)
    hd = NSA_BLOCK
    bpp = kpages[0].shape[0] // NSA_BLOCK
    nbc = npg * bpp
    nb_pad = pk_ref.shape[0]
    n_past = past_len // NSA_BLOCK

    @pl.when(c == 0)
    def _():
        pk_ref[...] = jnp.zeros(pk_ref.shape, F32)
        pv_ref[...] = jnp.zeros(pv_ref.shape, F32)

    wk = w_ref[0]
    wv = w_ref[1]
    pk = jnp.concatenate([_pool_rows(r[...], wk) for r in kpages], axis=0)
    pv = jnp.concatenate([_pool_rows(r[...], wv) for r in vpages], axis=0)
    off = pl.multiple_of(c * nbc, nbc)
    pk_ref[pl.ds(off, nbc), :] = pk
    pv_ref[pl.ds(off, nbc), :] = pv

    @pl.when(c == nch - 1)
    def _():
        newk = jnp.sum(kn_ref[...] * wk[0:t_new], axis=0, keepdims=True)
        newv = jnp.sum(vn_ref[...] * wv[0:t_new], axis=0, keepdims=True)
        pk_ref[pl.ds(n_past, 1), :] = newk
        pv_ref[pl.ds(n_past, 1), :] = newv
        q = q_ref[...]
        rows = NSA_HEADS * t_new
        qbd = _qbd(q, t_new)
        trow = lax.broadcasted_iota(jnp.int32, (rows, 1), 0) % t_new
        pos = past_len + trow
        blk = lax.broadcasted_iota(jnp.int32, (rows, nb_pad), 1)
        vis = ((blk + 1) * NSA_BLOCK <= pos + 1) & (blk <= n_past)
        pr = _msoftmax(_dot_nt(qbd, pk_ref[...]), vis)
        o = _dot(pr, pv_ref[...])
        outs = []
        for head in range(NSA_HEADS):
            g = head // NSA_GROUP
            outs.append(o[head * t_new:(head + 1) * t_new, g * hd:(g + 1) * hd])
        ocmp_ref[...] = jnp.concatenate(outs, axis=1)
        imps = []
        for g in range(NSA_KV_HEADS):
            acc = pr[g * NSA_GROUP * t_new:(g * NSA_GROUP + 1) * t_new]
            for h in range(1, NSA_GROUP):
                acc = acc + pr[(g * NSA_GROUP + h) * t_new:(g * NSA_GROUP + h + 1) * t_new]
            imps.append(acc)
        imp = jnp.concatenate(imps, axis=0)
        cur = (past_len + lax.broadcasted_iota(jnp.int32, (NSA_KV_HEADS * t_new, 1), 0) % t_new) // NSA_BLOCK
        sel_ref[...] = _select(imp, cur, n_past)


def _nsa_sample_cmp(page_table, pool_k, pool_v, q, kn, vn, w2, t_new, npg):
    db, n_pages = page_table.shape
    page = pool_k.shape[1]
    past_len = n_pages * page
    nch = n_pages // npg
    nb_pad = -(-(past_len // NSA_BLOCK + 1) // LANES) * LANES
    row0 = q.shape[0] // t_new - db

    def page_spec(j):
        return pl.BlockSpec((None, page, LANES), lambda b, c, pt: (pt[b, c * npg + j], 0, 0))

    tok = lambda w: pl.BlockSpec((t_new, w), lambda b, c, pt: (row0 + b, 0))
    kern = functools.partial(_nsa_sample_cmp_kernel, npg=npg, t_new=t_new, past_len=past_len)
    grid_spec = pltpu.PrefetchScalarGridSpec(
        num_scalar_prefetch=1,
        grid=(db, nch),
        in_specs=[page_spec(j) for j in range(npg)] + [page_spec(j) for j in range(npg)]
        + [tok(512), tok(LANES), tok(LANES), pl.BlockSpec(w2.shape, lambda b, c, pt: (0, 0, 0))],
        out_specs=[pl.BlockSpec((None, t_new, 512), lambda b, c, pt: (b, 0, 0)),
                   pl.BlockSpec((None, NSA_KV_HEADS * t_new, nb_pad), lambda b, c, pt: (b, 0, 0))],
        scratch_shapes=[pltpu.VMEM((nb_pad, LANES), F32), pltpu.VMEM((nb_pad, LANES), F32)],
    )
    return pl.pallas_call(
        kern,
        grid_spec=grid_spec,
        out_shape=[jax.ShapeDtypeStruct((db, t_new, 512), F32),
                   jax.ShapeDtypeStruct((db, NSA_KV_HEADS * t_new, nb_pad), F32)],
        compiler_params=_cp(("parallel", "arbitrary")),
        name="nsa_sample_cmp",
    )(page_table, *([pool_k] * npg), *([pool_v] * npg), q, kn, vn, w2)


def _online_update(s, mask, v, m_ref, l_ref, acc_ref):
    s = jnp.where(mask, s, NEG)
    m_old = m_ref[...]
    m_new = jnp.maximum(m_old, jnp.max(s, axis=-1, keepdims=True))
    alpha = jnp.exp(m_old - m_new)
    e = jnp.where(mask, jnp.exp(s - m_new), 0.0)
    l_ref[...] = alpha * l_ref[...] + jnp.sum(e, axis=-1, keepdims=True)
    acc_ref[...] = alpha * acc_ref[...] + _dot(e, v)
    m_ref[...] = m_new


def _nsa_sample_sel_kernel(pt_ref, *refs, npg, t_new, past_len):
    kpages = refs[0:npg]
    vpages = refs[npg:2 * npg]
    (q_ref, sel_ref, ksn_ref, vsn_ref, wk_ref, wv_ref, kwn_ref, vwn_ref, ocmp_ref, gate_ref,
     o_ref, m_ref, l_ref, acc_ref, pad_ref) = refs[2 * npg:]
    c = pl.program_id(1)
    nch = pl.num_programs(1)
    hd = NSA_BLOCK
    page = kpages[0].shape[0]
    bpp = page // NSA_BLOCK
    nbc = npg * bpp
    rows = NSA_HEADS * t_new
    n_past = past_len // NSA_BLOCK
    q = q_ref[...]
    qbd = _qbd(q, t_new)

    @pl.when(c == 0)
    def _():
        m_ref[...] = jnp.full(m_ref.shape, NEG, F32)
        l_ref[...] = jnp.zeros(l_ref.shape, F32)
        acc_ref[...] = jnp.zeros(acc_ref.shape, F32)

    sel = sel_ref[...]
    nb_pad = sel.shape[1]
    sel_rows = jnp.concatenate([sel[(head // NSA_GROUP) * t_new:(head // NSA_GROUP + 1) * t_new]
                                for head in range(NSA_HEADS)], axis=0)
    blk_iota = lax.broadcasted_iota(jnp.int32, (nb_pad, npg * page), 0)
    key_blk = c * nbc + lax.broadcasted_iota(jnp.int32, (nb_pad, npg * page), 1) // NSA_BLOCK
    expand = jnp.where(blk_iota == key_blk, 1.0, 0.0).astype(BF16)
    smask = _dot(sel_rows, expand) > 0.5
    kcat = jnp.concatenate([r[...].astype(BF16) for r in kpages], axis=0)
    vcat = jnp.concatenate([r[...].astype(BF16) for r in vpages], axis=0)
    _online_update(_dot_nt(qbd, kcat), smask, vcat, m_ref, l_ref, acc_ref)

    @pl.when(c == nch - 1)
    def _():
        trow = lax.broadcasted_iota(jnp.int32, (rows, 1), 0) % t_new
        pad_ref[...] = jnp.zeros(pad_ref.shape, F32)
        pad_ref[0:t_new, :] = ksn_ref[...]
        kn = pad_ref[...]
        pad_ref[0:t_new, :] = vsn_ref[...]
        vn = pad_ref[...]
        jn = lax.broadcasted_iota(jnp.int32, (rows, LANES), 1)
        _online_update(_dot_nt(qbd, kn), jn <= trow, vn, m_ref, l_ref, acc_ref)
        o_sel = acc_ref[...] / jnp.maximum(l_ref[...], 1e-30)
        wb = wk_ref.shape[0]
        jw = lax.broadcasted_iota(jnp.int32, (rows, wb), 1)
        rel = trow + wb - jw
        wmask = jnp.concatenate([(rel >= 0) & (rel < NSA_WINDOW), jn <= trow], axis=1)
        pad_ref[0:t_new, :] = kwn_ref[...]
        kw = jnp.concatenate([wk_ref[...], pad_ref[...]], axis=0)
        pad_ref[0:t_new, :] = vwn_ref[...]
        vw = jnp.concatenate([wv_ref[...], pad_ref[...]], axis=0)
        o_win = _dot(_msoftmax(_dot_nt(qbd, kw), wmask), vw)
        gate = gate_ref[...]
        ocmp = ocmp_ref[...]
        outs = []
        for head in range(NSA_HEADS):
            g = head // NSA_GROUP
            r0, r1 = head * t_new, (head + 1) * t_new
            outs.append(_gate_col(gate, head, 0) * ocmp[:, head * hd:(head + 1) * hd]
                        + _gate_col(gate, head, 1) * o_sel[r0:r1, g * hd:(g + 1) * hd]
                        + _gate_col(gate, head, 2) * o_win[r0:r1, g * hd:(g + 1) * hd])
        o_ref[...] = jnp.concatenate(outs, axis=1)


def _nsa_sample_sel(page_table, pool_k, pool_v, q, sel, ksn, vsn, win_k, win_v, kwn, vwn, ocmp, zd, t_new, npg):
    db, n_pages = page_table.shape
    page = pool_k.shape[1]
    past_len = n_pages * page
    nch = n_pages // npg
    wb = win_k.shape[1]
    row0 = q.shape[0] // t_new - db
    rows = NSA_HEADS * t_new

    def page_spec(j):
        return pl.BlockSpec((None, page, LANES), lambda b, c, pt: (pt[b, c * npg + j], 0, 0))

    tok = lambda w: pl.BlockSpec((t_new, w), lambda b, c, pt: (row0 + b, 0))
    per_seq = lambda a: pl.BlockSpec((None,) + a.shape[1:], lambda b, c, pt: (b, 0, 0))
    kern = functools.partial(_nsa_sample_sel_kernel, npg=npg, t_new=t_new, past_len=past_len)
    grid_spec = pltpu.PrefetchScalarGridSpec(
        num_scalar_prefetch=1,
        grid=(db, nch),
        in_specs=[page_spec(j) for j in range(npg)] + [page_spec(j) for j in range(npg)]
        + [tok(512), per_seq(sel), tok(LANES), tok(LANES), per_seq(win_k), per_seq(win_v), tok(LANES), tok(LANES),
           per_seq(ocmp), tok(LANES)],
        out_specs=pl.BlockSpec((t_new, 512), lambda b, c, pt: (b, 0)),
        scratch_shapes=[pltpu.VMEM((rows, 1), F32), pltpu.VMEM((rows, 1), F32), pltpu.VMEM((rows, LANES), F32),
                        pltpu.VMEM((LANES, LANES), F32)],
    )
    return pl.pallas_call(
        kern,
        grid_spec=grid_spec,
        out_shape=jax.ShapeDtypeStruct((db * t_new, 512), F32),
        compiler_params=_cp(("parallel", "arbitrary")),
        name="nsa_sample_sel",
    )(page_table, *([pool_k] * npg), *([pool_v] * npg), q, sel, ksn, vsn, win_k, win_v, kwn, vwn, ocmp, zd)


def _log_sigmoid(x):
    return jnp.minimum(x, 0.0) - jnp.log(1.0 + jnp.exp(-jnp.abs(x)))


def _gla_kernel(za_ref, zd_ref, wgh_ref, wgl_ref, bg_ref, gout_ref, s0_ref, o_ref, s_ref,
                st_ref, kp_ref, bp_ref, vp_ref, *, nchunk, t_valid):
    j = pl.program_id(1)
    nj = pl.num_programs(1)
    C = GLA_CHUNK
    sub = GLA_SUB
    nsub = C // sub
    dk = 64
    dv = 128
    hk = GLA_HEADS * dk
    hv = GLA_HEADS * dv

    @pl.when(j == 0)
    def _():
        st_ref[...] = jnp.concatenate([s0_ref[h].T for h in range(GLA_HEADS)], axis=1)
        kp_ref[...] = jnp.zeros(kp_ref.shape, F32)
        bp_ref[...] = jnp.zeros(bp_ref.shape, F32)
        vp_ref[...] = jnp.zeros(vp_ref.shape, F32)

    tril = jnp.where(lax.broadcasted_iota(jnp.int32, (C, C), 0) >= lax.broadcasted_iota(jnp.int32, (C, C), 1),
                     1.0, 0.0).astype(BF16)
    head_ones = jnp.where(lax.broadcasted_iota(jnp.int32, (hk, LANES), 0) // dk
                          == lax.broadcasted_iota(jnp.int32, (hk, LANES), 1), 1.0, 0.0).astype(BF16)
    tmod = lax.broadcasted_iota(jnp.int32, (C, 1), 0) % sub
    gout = gout_ref[...]

    def chunk(c, carry):
        r0 = pl.multiple_of(c * C, C)
        z = za_ref[pl.ds(r0, C), :]
        q = z[:, 0:hk] * (dk ** -0.5)
        k = z[:, hk:2 * hk]
        v = z[:, 2 * hk:2 * hk + hv]
        r = z[:, 2 * hk + hv:2 * hk + 2 * hv]
        lr = zd_ref[pl.ds(r0, C), :][:, 0:GLA_GATE_RANK]
        la = _log_sigmoid(_dot_hi(lr, wgh_ref[...], wgl_ref[...]) + bg_ref[...]) * (1.0 / GLA_TAU)
        tglob = (j * nchunk + c) * C + lax.broadcasted_iota(jnp.int32, (C, 1), 0)
        la = jnp.where(tglob < t_valid, la, 0.0)
        b = _dot_exact_lhs(tril, la)
        st = st_ref[...]
        qe = q * jnp.exp(b)
        kp_ref[sub:sub + C, :] = k
        bp_ref[sub:sub + C, :] = b
        vp_ref[sub:sub + C, :] = v
        xs = []
        for d in range(sub):
            kd = kp_ref[sub - d:sub - d + C, :]
            bd = bp_ref[sub - d:sub - d + C, :]
            xs.append(jnp.where(tmod >= d, q * kd * jnp.exp(b - bd), 0.0).astype(BF16))
        rr = jnp.dot(jnp.concatenate(xs, axis=0), head_ones, preferred_element_type=F32)
        outs = []
        for h in range(GLA_HEADS):
            kh = slice(h * dk, (h + 1) * dk)
            vh = slice(h * dv, (h + 1) * dv)
            o = _dot_nt(qe[:, kh], st[:, kh])
            for d in range(sub):
                o = o + rr[d * C:(d + 1) * C, h:h + 1] * vp_ref[sub - d:sub - d + C, vh]
            offs = [jnp.zeros((sub, dv), F32)]
            for i in range(1, nsub):
                anchor = b[i * sub - 1:i * sub, kh]
                qt = q[i * sub:(i + 1) * sub, kh] * jnp.exp(b[i * sub:(i + 1) * sub, kh] - anchor)
                kt = k[0:i * sub, kh] * jnp.exp(anchor - b[0:i * sub, kh])
                offs.append(_dot(_dot_nt(qt, kt), v[0:i * sub, vh]))
            o = o + jnp.concatenate(offs, axis=0)
            on = o * lax.rsqrt(jnp.mean(o * o, axis=-1, keepdims=True) + EPS) * gout
            rh = r[:, vh]
            outs.append(on * (rh * _sigmoid(rh)))
        o_ref[pl.ds(r0, C), :] = jnp.concatenate(outs, axis=1).astype(BF16)
        b_last = b[C - 1:C, :]
        kk = k * jnp.exp(b_last - b)
        upd = jnp.concatenate([_dot_tn(v[:, h * dv:(h + 1) * dv], kk[:, h * dk:(h + 1) * dk])
                               for h in range(GLA_HEADS)], axis=1)
        st_ref[...] = st * jnp.exp(b_last) + upd
        return carry

    lax.fori_loop(0, nchunk, chunk, 0)

    @pl.when(j == nj - 1)
    def _():
        st = st_ref[...]
        for h in range(GLA_HEADS):
            s_ref[h] = st[:, h * dk:(h + 1) * dk].T


def _gla(za, zd, wgh, wgl, bg, gout, s0, batch, t_pad, t_valid, ct):
    nj = t_pad // ct
    hv = GLA_HEADS * 128
    kern = functools.partial(_gla_kernel, nchunk=ct // GLA_CHUNK, t_valid=t_valid)
    full = lambda a: pl.BlockSpec(a.shape, lambda b, j: (0,) * a.ndim)
    return pl.pallas_call(
        kern,
        grid=(batch, nj),
        in_specs=[pl.BlockSpec((ct, za.shape[1]), lambda b, j: (b * nj + j, 0)),
                  pl.BlockSpec((ct, LANES), lambda b, j: (b * nj + j, 0)),
                  full(wgh), full(wgl), full(bg), full(gout),
                  pl.BlockSpec((None,) + s0.shape[1:], lambda b, j: (b, 0, 0, 0))],
        out_specs=[pl.BlockSpec((ct, hv), lambda b, j: (b * nj + j, 0)),
                   pl.BlockSpec((None,) + s0.shape[1:], lambda b, j: (b, 0, 0, 0))],
        out_shape=[jax.ShapeDtypeStruct((batch * t_pad, hv), BF16), jax.ShapeDtypeStruct(s0.shape, F32)],
        scratch_shapes=[pltpu.VMEM((128, GLA_HEADS * 64), F32),
                        pltpu.VMEM((GLA_SUB + GLA_CHUNK, GLA_HEADS * 64), F32),
                        pltpu.VMEM((GLA_SUB + GLA_CHUNK, GLA_HEADS * 64), F32),
                        pltpu.VMEM((GLA_SUB + GLA_CHUNK, hv), F32)],
        compiler_params=_cp(("parallel", "arbitrary")),
        name="gla",
    )(za, zd, wgh, wgl, bg, gout, s0)


def _mem_kv_kernel(m_ref, g_ref, w_ref, gk_ref, k_ref, v_ref):
    x = m_ref[...]
    u = x * lax.rsqrt(jnp.mean(x * x, axis=-1, keepdims=True) + EPS) * g_ref[...]
    kv = jnp.dot(u.astype(BF16), w_ref[...], preferred_element_type=F32)
    half = kv.shape[1] // 2
    k_ref[...] = _seg_rms(kv[:, 0:half], half // MEM_HEADS, gk_ref[...])
    v_ref[...] = kv[:, half:]


def _mem_kv(mem, g, w, gk, tm):
    n, d = mem.shape
    half = w.shape[1] // 2
    full = lambda a: pl.BlockSpec(a.shape, lambda i: (0, 0))
    sd = jax.ShapeDtypeStruct((n, half), F32)
    return pl.pallas_call(
        _mem_kv_kernel,
        grid=(n // tm,),
        in_specs=[pl.BlockSpec((tm, d), lambda i: (i, 0)), full(g), full(w), full(gk)],
        out_specs=[pl.BlockSpec((tm, half), lambda i: (i, 0))] * 2,
        out_shape=[sd, sd],
        compiler_params=_cp(("parallel",)),
        name="mem_kv",
    )(mem, g, w, gk)


def _mem_prompt_kernel(q_ref, k_ref, v_ref, o_ref):
    q = q_ref[...]
    k = k_ref[...].astype(BF16)
    v = v_ref[...].astype(BF16)
    hd = q.shape[1] // MEM_HEADS
    outs = []
    for h in range(MEM_HEADS):
        sl = slice(h * hd, (h + 1) * hd)
        s = _dot_nt(q[:, sl], k[:, sl])
        m = jnp.max(s, axis=-1, keepdims=True)
        e = jnp.exp(s - m)
        outs.append(_dot(e / jnp.sum(e, axis=-1, keepdims=True), v[:, sl]))
    o_ref[...] = jnp.concatenate(outs, axis=1).astype(BF16)


def _mem_prompt(qm, mk, mv, batch, seq, mlen, tq):
    nq = seq // tq
    w = qm.shape[1]
    return pl.pallas_call(
        _mem_prompt_kernel,
        grid=(batch, nq),
        in_specs=[pl.BlockSpec((tq, w), lambda b, i: (b * nq + i, 0)),
                  pl.BlockSpec((mlen, w), lambda b, i: (b, 0)),
                  pl.BlockSpec((mlen, w), lambda b, i: (b, 0))],
        out_specs=pl.BlockSpec((tq, w), lambda b, i: (b * nq + i, 0)),
        out_shape=jax.ShapeDtypeStruct((batch * seq, w), BF16),
        compiler_params=_cp(("parallel", "parallel")),
        name="mem_prompt",
    )(qm, mk, mv)


def _mem_sample_kernel(q_ref, k_ref, v_ref, o_ref, *, sb, t_new):
    w = q_ref.shape[1]
    hd = w // MEM_HEADS
    rows = MEM_HEADS * t_new
    lane_head = lax.broadcasted_iota(jnp.int32, (t_new, w), 1) // hd
    for s_i in range(sb):
        q = q_ref[s_i * t_new:(s_i + 1) * t_new, :].astype(F32)
        qbd = jnp.concatenate([jnp.where(lane_head == h, q, 0.0) for h in range(MEM_HEADS)], axis=0)
        k = k_ref[s_i].astype(BF16)
        v = v_ref[s_i].astype(BF16)
        s = _dot_nt(qbd, k)
        m = jnp.max(s, axis=-1, keepdims=True)
        e = jnp.exp(s - m)
        o = _dot(e / jnp.sum(e, axis=-1, keepdims=True), v)
        o_ref[s_i * t_new:(s_i + 1) * t_new, :] = jnp.concatenate(
            [o[h * t_new:(h + 1) * t_new, h * hd:(h + 1) * hd] for h in range(MEM_HEADS)], axis=1).astype(BF16)


def _mem_sample(qm, ck, cv, row0_blocks, db, t_new, sb):
    w = qm.shape[1]
    mlen = ck.shape[1]
    kern = functools.partial(_mem_sample_kernel, sb=sb, t_new=t_new)
    return pl.pallas_call(
        kern,
        grid=(db // sb,),
        in_specs=[pl.BlockSpec((sb * t_new, w), lambda i: (row0_blocks + i, 0)),
                  pl.BlockSpec((sb, mlen, w), lambda i: (i, 0, 0)),
                  pl.BlockSpec((sb, mlen, w), lambda i: (i, 0, 0))],
        out_specs=pl.BlockSpec((sb * t_new, w), lambda i: (i, 0)),
        out_shape=jax.ShapeDtypeStruct((db * t_new, w), BF16),
        compiler_params=_cp(("parallel",)),
        name="mem_sample",
    )(qm, ck, cv)


def _merge_kernel(x_ref, og_ref, on_ref, om_ref, gate_ref, wg_ref, wn_ref, wm_ref, wo_ref, gf_ref,
                  wrh_ref, wrl_ref, br_ref, h_ref, hn_ref, cmb_ref, *, n_experts):
    d = x_ref.shape[1]
    gate = gate_ref[...].astype(F32)
    y = (gate[:, 0:d] * jnp.dot(og_ref[...], wg_ref[...], preferred_element_type=F32)
         + gate[:, d:2 * d] * jnp.dot(on_ref[...], wn_ref[...], preferred_element_type=F32)
         + gate[:, 2 * d:3 * d] * jnp.dot(om_ref[...], wm_ref[...], preferred_element_type=F32))
    h = x_ref[...] + jnp.dot(y.astype(BF16), wo_ref[...], preferred_element_type=F32)
    h_ref[...] = h
    hn = h * lax.rsqrt(jnp.mean(h * h, axis=-1, keepdims=True) + EPS) * gf_ref[...]
    hn_ref[...] = hn.astype(BF16)
    logits = _dot_hi(hn, wrh_ref[...], wrl_ref[...]) + br_ref[...]
    lane = lax.broadcasted_iota(jnp.int32, logits.shape, 1).astype(F32)
    work = jnp.where(lane < n_experts, logits, NEG)
    top = jnp.max(work, axis=-1, keepdims=True)
    chosen = jnp.zeros(logits.shape, F32)
    for _ in range(TOP_K):
        mx = jnp.max(work, axis=-1, keepdims=True)
        first = jnp.min(jnp.where(work == mx, lane, float(LANES)), axis=-1, keepdims=True)
        pick = lane == first
        chosen = jnp.where(pick, 1.0, chosen)
        work = jnp.where(pick, NEG, work)
    e = jnp.where(chosen > 0.5, jnp.exp(logits - top), 0.0)
    cmb_ref[...] = e / jnp.sum(e, axis=-1, keepdims=True)


def _merge(x, og, on, om, gate, wg, wn, wm, wo, gf, wrh, wrl, br, n_experts, tm):
    n, d = x.shape
    row = lambda a: pl.BlockSpec((tm, a.shape[1]), lambda i: (i, 0))
    full = lambda a: pl.BlockSpec(a.shape, lambda i: (0, 0))
    kern = functools.partial(_merge_kernel, n_experts=n_experts)
    return pl.pallas_call(
        kern,
        grid=(n // tm,),
        in_specs=[row(x), row(og), row(on), row(om), row(gate), full(wg), full(wn), full(wm), full(wo), full(gf),
                  full(wrh), full(wrl), full(br)],
        out_specs=[pl.BlockSpec((tm, d), lambda i: (i, 0)), pl.BlockSpec((tm, d), lambda i: (i, 0)),
                   pl.BlockSpec((tm, LANES), lambda i: (i, 0))],
        out_shape=[jax.ShapeDtypeStruct((n, d), F32), jax.ShapeDtypeStruct((n, d), BF16),
                   jax.ShapeDtypeStruct((n, LANES), F32)],
        compiler_params=_cp(("parallel",)),
        name="merge_router",
    )(x, og, on, om, gate, wg, wn, wm, wo, gf, wrh, wrl, br)


def _moe_kernel(hn_ref, h_ref, cmb_ref, wg_ref, wu_ref, bg_ref, bu_ref, wd_ref, bd_ref, y_ref):
    e = pl.program_id(1)

    @pl.when(e == 0)
    def _():
        y_ref[...] = h_ref[...]

    hn = hn_ref[...]
    gate = jnp.minimum(jnp.dot(hn, wg_ref[...], preferred_element_type=F32) + bg_ref[...], SWIGLU_LIMIT)
    up = jnp.clip(jnp.dot(hn, wu_ref[...], preferred_element_type=F32) + bu_ref[...], -SWIGLU_LIMIT, SWIGLU_LIMIT)
    act = (up + 1.0) * gate * _sigmoid(SWIGLU_ALPHA * gate)
    out = jnp.dot(act.astype(BF16), wd_ref[...], preferred_element_type=F32) + bd_ref[...]
    cmb = cmb_ref[...]
    lane = lax.broadcasted_iota(jnp.int32, cmb.shape, 1)
    wcol = jnp.sum(jnp.where(lane == e, cmb, 0.0), axis=-1, keepdims=True)
    y_ref[...] += wcol * out


def _moe(hn, h, cmb, wg, wu, bg, bu, wd, bd, tm):
    n, d = h.shape
    ne, _, dff = wg.shape
    row = lambda w: pl.BlockSpec((tm, w), lambda i, e: (i, 0))
    return pl.pallas_call(
        _moe_kernel,
        grid=(n // tm, ne),
        in_specs=[row(d), row(d), row(LANES),
                  pl.BlockSpec((None, d, dff), lambda i, e: (e, 0, 0)),
                  pl.BlockSpec((None, d, dff), lambda i, e: (e, 0, 0)),
                  pl.BlockSpec((None, 1, dff), lambda i, e: (e, 0, 0)),
                  pl.BlockSpec((None, 1, dff), lambda i, e: (e, 0, 0)),
                  pl.BlockSpec((None, dff, d), lambda i, e: (e, 0, 0)),
                  pl.BlockSpec((None, 1, d), lambda i, e: (e, 0, 0))],
        out_specs=row(d),
        out_shape=jax.ShapeDtypeStruct((n, d), F32),
        compiler_params=_cp(("parallel", "arbitrary")),
        name="moe",
    )(hn, h, cmb, wg, wu, bg, bu, wd, bd)


def _rope_table(pos):
    half = 8
    inv = 1.0 / (ROPE_THETA ** (jnp.arange(half, dtype=F32) / half))
    ang = pos.astype(F32)[:, None] * inv[None, :]
    cos, sin = jnp.cos(ang), jnp.sin(ang)
    n = pos.shape[0]
    one = jnp.ones((n, 64 - 2 * half), F32)
    zero8 = jnp.zeros((n, half), F32)
    zero = jnp.zeros((n, 64 - 2 * half), F32)
    c = jnp.concatenate([cos, cos, one], axis=1)
    s1 = jnp.concatenate([-sin, zero8, zero], axis=1)
    s2 = jnp.concatenate([zero8, sin, zero], axis=1)
    return jnp.concatenate([c, c, s1, s1, s2, s2], axis=1)


def _hi_lo(w):
    hi = w.astype(BF16)
    return hi, (w - hi.astype(F32)).astype(BF16)


def kernel(x_prompt, x_sample, cache_cmp_k, cache_cmp_v, cache_sel_k, cache_sel_v, cache_win_k, cache_win_v, state_gla, cache_mem_k, cache_mem_v, page_table, mem_prompt, g_attn, w_in, w_gla_gate, b_gla_gate, g_gla_out, g_q_nsa, g_k_nsa, w_cmp_pos, g_q_mem, g_k_mem, g_mem, w_mem_kv, w_up_gla, w_up_nsa, w_up_mem, w_out, g_ffn, w_router, b_router, w_gate_up, b_gate_up, w_down, b_down):
    B, S, D = x_prompt.shape
    DB, T, _ = x_sample.shape
    n_pool, page = cache_cmp_k.shape[1:3]
    n_pages = page_table.shape[1]
    past_len = n_pages * page
    wb = cache_win_k.shape[2]
    mlen = mem_prompt.shape[1]
    ne = w_router.shape[2]
    Np, Ns = B * S, DB * T
    N = Np + Ns
    TM = 256
    assert cache_cmp_k.shape[0] == 1 and D == 1024 and S % TM == 0 and Ns % TM == 0 and TM % T == 0
    assert T <= NSA_BLOCK and past_len % NSA_BLOCK == 0 and S % NSA_BLOCK == 0

    gq, gk, gv, gr, glr, nq, nkv, ng, mq, mg = np.cumsum(
        [0, 256, 256, 512, 512, GLA_GATE_RANK, 512, 768, 3 * NSA_HEADS, 512]).tolist()
    w = w_in[0]
    wa = w[:, gq:glr].astype(BF16)
    wbm = jnp.concatenate([w[:, nq:ng], w[:, mq:mg]], axis=1).astype(BF16)
    wc = w[:, mg:].astype(BF16)
    wd = jnp.concatenate([w[:, glr:nq], w[:, ng:mq], jnp.zeros((D, LANES - GLA_GATE_RANK - 3 * NSA_HEADS), F32)],
                         axis=1).astype(BF16)

    x_all = jnp.concatenate([x_prompt.reshape(Np, D), x_sample.reshape(Ns, D)], axis=0)
    za, zb, zc, zd = _project(x_all, g_attn, wa, wbm, wc, wd, TM)

    pos_sample = past_len + jnp.arange(T, dtype=jnp.int32)
    tab = jnp.concatenate([_rope_table(jnp.arange(S, dtype=jnp.int32)),
                           jnp.tile(_rope_table(pos_sample), (TM // T, 1))], axis=0)
    gq_t = jnp.tile(g_q_nsa[0], NSA_HEADS)[None, :]
    gk_t = jnp.tile(g_k_nsa[0], (1, NSA_KV_HEADS))
    gm_t = jnp.tile(g_q_mem[0], MEM_HEADS)[None, :]
    q, kc, vc, ks, vs, kw, vw, qm, qb, kvb = _prep(zb, tab, gq_t, gk_t, gm_t, TM, Np // TM, S // TM)

    w2 = jnp.tile(w_cmp_pos[0], (1, 1, NSA_KV_HEADS))
    ck, cv = _compress(kc, vc, w2, Np, 512)
    o_nsa_p = _nsa_prompt(qb, ck, cv, kvb, zd, B, S, 128)

    pool = lambda a: a[0].reshape(n_pool, page, LANES)
    npg = 16 if n_pages % 16 == 0 else n_pages
    ocmp_s, sel_s = _nsa_sample_cmp(page_table, pool(cache_cmp_k), pool(cache_cmp_v), q, kc, vc, w2, T, npg)
    win_k = cache_win_k[0].reshape(DB, wb, LANES)
    win_v = cache_win_v[0].reshape(DB, wb, LANES)
    o_nsa_s = _nsa_sample_sel(page_table, pool(cache_sel_k), pool(cache_sel_v), q, sel_s, ks, vs, win_k, win_v,
                              kw, vw, ocmp_s, zd, T, npg)
    o_nsa = jnp.concatenate([o_nsa_p, o_nsa_s.astype(BF16)], axis=0)

    wgh, wgl = _hi_lo(w_gla_gate[0])
    bg = b_gla_gate[0][None, :]
    gout = g_gla_out[0][None, :]
    s0_p = jnp.zeros((B,) + state_gla.shape[2:], F32)
    o_gla_p, s_gla_p = _gla(za, zd, wgh, wgl, bg, gout, s0_p, B, S, S, 512)
    t_pad = GLA_CHUNK
    za_s = jnp.pad(za[Np:].reshape(DB, T, -1), ((0, 0), (0, t_pad - T), (0, 0))).reshape(DB * t_pad, -1)
    zd_s = jnp.pad(zd[Np:].reshape(DB, T, -1), ((0, 0), (0, t_pad - T), (0, 0))).reshape(DB * t_pad, -1)
    o_gla_s, s_gla_s = _gla(za_s, zd_s, wgh, wgl, bg, gout, state_gla[0], DB, t_pad, T, t_pad)
    o_gla = jnp.concatenate([o_gla_p, o_gla_s.reshape(DB, t_pad, -1)[:, :T].reshape(Ns, -1)], axis=0)

    gkm = jnp.tile(g_k_mem[0], MEM_HEADS)[None, :]
    mem_k, mem_v = _mem_kv(mem_prompt.reshape(B * mlen, D), g_mem, w_mem_kv[0].astype(BF16), gkm, TM)
    o_mem_p = _mem_prompt(qm, mem_k, mem_v, B, S, mlen, 512)
    mw = MEM_HEADS * cache_mem_k.shape[-1]
    sb = 8
    o_mem_s = _mem_sample(qm, cache_mem_k[0].reshape(DB, mlen, mw), cache_mem_v[0].reshape(DB, mlen, mw),
                          Np // (sb * T), DB, T, sb)
    o_mem = jnp.concatenate([o_mem_p, o_mem_s], axis=0)

    wr = jnp.pad(w_router[0], ((0, 0), (0, LANES - ne)))
    wrh, wrl = _hi_lo(wr)
    br = jnp.pad(b_router[0], (0, LANES - ne))[None, :]
    h, hn, cmb = _merge(x_all, o_gla, o_nsa, o_mem, zc, w_up_gla[0].astype(BF16), w_up_nsa[0].astype(BF16),
                        w_up_mem[0].astype(BF16), w_out[0].astype(BF16), g_ffn, wrh, wrl, br, ne, TM)

    wgu = w_gate_up[0]
    y = _moe(hn, h, cmb, wgu[:, :, 0::2].astype(BF16), wgu[:, :, 1::2].astype(BF16),
             b_gate_up[0][:, None, 0::2], b_gate_up[0][:, None, 1::2], w_down[0].astype(BF16), b_down[0][:, None, :],
             512 if N % 512 == 0 else TM)

    kvshape = (1, B, S, NSA_KV_HEADS, NSA_BLOCK)
    p_rows = lambda a: a[:Np].reshape(kvshape)
    s_rows = lambda a: a[Np:].reshape(1, DB, T, NSA_KV_HEADS, NSA_BLOCK)
    wbp = min(NSA_WINDOW, S)
    s_win = lambda cache, new: jnp.concatenate(
        [cache[:, :, T:], new[Np:].reshape(1, DB, T, NSA_KV_HEADS, NSA_BLOCK)], axis=2)
    mshape = (1, B, mlen, MEM_HEADS, mw // MEM_HEADS)
    return (y[:Np].reshape(B, S, D), y[Np:].reshape(DB, T, D),
            p_rows(kc), p_rows(vc), p_rows(ks), p_rows(vs),
            p_rows(kw)[:, :, S - wbp:], p_rows(vw)[:, :, S - wbp:],
            s_gla_p[None], mem_k.reshape(mshape), mem_v.reshape(mshape),
            s_rows(kc), s_rows(vc), s_rows(ks), s_rows(vs),
            s_win(cache_win_k, kw), s_win(cache_win_v, vw), s_gla_s[None])
```

```python
import functools

import numpy as np
import jax
import jax.numpy as jnp
from jax import lax
from jax.experimental import pallas as pl
from jax.experimental.pallas import tpu as pltpu

F32 = jnp.float32
BF16 = jnp.bfloat16

GLA_HEADS = 4
GLA_GATE_RANK = 16
GLA_TAU = 16.0
GLA_CHUNK = 64
GLA_SUB = 16
NSA_HEADS = 8
NSA_KV_HEADS = 2
NSA_GROUP = NSA_HEADS // NSA_KV_HEADS
NSA_BLOCK = 64
NSA_TOPN = 16
NSA_WINDOW = 512
MEM_HEADS = 4
ROPE_THETA = 500000.0
N_BRANCH = 3
TOP_K = 4
SWIGLU_LIMIT = 7.0
SWIGLU_ALPHA = 1.702
EPS = 1e-6

LANES = 128
NEG = -1e30
VMEM_LIMIT = 56 * 1024 * 1024


def _cp(sem):
    return pltpu.CompilerParams(dimension_semantics=sem, vmem_limit_bytes=VMEM_LIMIT)


def _dot(a, b):
    return jnp.dot(a.astype(BF16), b.astype(BF16), preferred_element_type=F32)


def _dot_nt(a, b):
    return lax.dot_general(a.astype(BF16), b.astype(BF16), (((1,), (1,)), ((), ())),
                           preferred_element_type=F32)


def _dot_tn(a, b):
    return lax.dot_general(a.astype(BF16), b.astype(BF16), (((0,), (0,)), ((), ())),
                           preferred_element_type=F32)


def _split3(x):
    hi = x.astype(BF16)
    r = x - hi.astype(F32)
    mid = r.astype(BF16)
    lo = (r - mid.astype(F32)).astype(BF16)
    return hi, mid, lo


def _dot_exact_rhs(x, m):
    hi, mid, lo = _split3(x)
    d = functools.partial(jnp.dot, preferred_element_type=F32)
    return d(hi, m) + d(mid, m) + d(lo, m)


def _dot_exact_lhs(m, x):
    hi, mid, lo = _split3(x)
    d = functools.partial(jnp.dot, preferred_element_type=F32)
    return d(m, hi) + d(m, mid) + d(m, lo)


def _dot_hi(x, w_hi, w_lo):
    hi, mid, _ = _split3(x)
    d = functools.partial(jnp.dot, preferred_element_type=F32)
    return d(hi, w_hi) + d(mid, w_hi) + d(hi, w_lo)


def _msoftmax(s, mask):
    s = jnp.where(mask, s, NEG)
    m = jnp.max(s, axis=-1, keepdims=True)
    e = jnp.where(mask, jnp.exp(s - m), 0.0)
    return e / jnp.maximum(jnp.sum(e, axis=-1, keepdims=True), 1e-30)


def _seg_ones(width, seg):
    r = lax.broadcasted_iota(jnp.int32, (width, width), 0) // seg
    c = lax.broadcasted_iota(jnp.int32, (width, width), 1) // seg
    return jnp.where(r == c, 1.0, 0.0).astype(BF16)


def _seg_rms(x, seg, gain):
    ms = _dot_exact_rhs(x * x, _seg_ones(x.shape[-1], seg)) * (1.0 / seg)
    return x * lax.rsqrt(ms + EPS) * gain


def _sigmoid(x):
    return 1.0 / (1.0 + jnp.exp(-x))


def _proj_kernel(x_ref, g_ref, wa_ref, wb_ref, wc_ref, wd_ref, za_ref, zb_ref, zc_ref, zd_ref):
    x = x_ref[...]
    u = x * lax.rsqrt(jnp.mean(x * x, axis=-1, keepdims=True) + EPS) * g_ref[...]
    ub = u.astype(BF16)
    za_ref[...] = jnp.dot(ub, wa_ref[...], preferred_element_type=F32)
    zb_ref[...] = jnp.dot(ub, wb_ref[...], preferred_element_type=F32)
    zc_ref[...] = _sigmoid(jnp.dot(ub, wc_ref[...], preferred_element_type=F32)).astype(BF16)
    zd_ref[...] = jnp.dot(ub, wd_ref[...], preferred_element_type=F32)


def _project(x, g, wa, wb, wc, wd, tm):
    n, d = x.shape
    full = lambda w: pl.BlockSpec(w.shape, lambda i: (0, 0))
    row = lambda w: pl.BlockSpec((tm, w), lambda i: (i, 0))
    return pl.pallas_call(
        _proj_kernel,
        grid=(n // tm,),
        in_specs=[row(d), full(g), full(wa), full(wb), full(wc), full(wd)],
        out_specs=[row(wa.shape[1]), row(wb.shape[1]), row(wc.shape[1]), row(wd.shape[1])],
        out_shape=[jax.ShapeDtypeStruct((n, wa.shape[1]), F32), jax.ShapeDtypeStruct((n, wb.shape[1]), F32),
                   jax.ShapeDtypeStruct((n, wc.shape[1]), BF16), jax.ShapeDtypeStruct((n, wd.shape[1]), F32)],
        compiler_params=_cp(("parallel",)),
        name="proj",
    )(x, g, wa, wb, wc, wd)


def _rope(x, tab):
    c, s1, s2 = tab[:, 0:LANES], tab[:, LANES:2 * LANES], tab[:, 2 * LANES:3 * LANES]
    half = 8
    return x * c + pltpu.roll(x, LANES - half, 1) * s1 + pltpu.roll(x, half, 1) * s2


def _prep_kernel(zb_ref, tab_ref, gq_ref, gk_ref, gm_ref,
                 q_ref, kc_ref, vc_ref, ks_ref, vs_ref, kw_ref, vw_ref, qm_ref, qb_ref, kvb_ref):
    z = zb_ref[...]
    tab = tab_ref[...]
    hd = NSA_BLOCK
    nq = NSA_HEADS * hd
    qn = _seg_rms(z[:, 0:nq], hd, gq_ref[...])
    q = jnp.concatenate([_rope(qn[:, j * LANES:(j + 1) * LANES], tab) for j in range(nq // LANES)], axis=1)
    q = q * (hd ** -0.5)
    q_ref[...] = q
    qb_ref[...] = q.astype(BF16)
    kv = z[:, nq:nq + 6 * LANES]
    gk = gk_ref[...]
    ks = []
    for j, ref in ((0, kc_ref), (1, ks_ref), (2, kw_ref)):
        kj = _rope(_seg_rms(kv[:, 2 * j * LANES:(2 * j + 1) * LANES], hd, gk[j:j + 1, :]), tab)
        ref[...] = kj
        ks.append(kj)
    vs = []
    for j, ref in ((0, vc_ref), (1, vs_ref), (2, vw_ref)):
        vj = kv[:, (2 * j + 1) * LANES:(2 * j + 2) * LANES]
        ref[...] = vj
        vs.append(vj)
    kvb_ref[...] = jnp.concatenate([ks[1], vs[1], ks[2], vs[2]], axis=1).astype(BF16)
    zm = z[:, nq + 6 * LANES:]
    mhd = zm.shape[1] // MEM_HEADS
    qm_ref[...] = (_seg_rms(zm, mhd, gm_ref[...]) * (mhd ** -0.5)).astype(BF16)


def _prep(zb, tab, gq, gk, gm, tm, n_prompt_tiles, tab_tiles):
    n = zb.shape[0]
    row = lambda w: pl.BlockSpec((tm, w), lambda i: (i, 0))
    full = lambda a: pl.BlockSpec(a.shape, lambda i: (0, 0))
    tab_spec = pl.BlockSpec((tm, 3 * LANES), lambda i: (jnp.where(i < n_prompt_tiles, i % tab_tiles, tab_tiles), 0))
    sd = lambda w, dt=F32: jax.ShapeDtypeStruct((n, w), dt)
    return pl.pallas_call(
        _prep_kernel,
        grid=(n // tm,),
        in_specs=[row(zb.shape[1]), tab_spec, full(gq), full(gk), full(gm)],
        out_specs=[row(512)] + [row(LANES)] * 6 + [row(512), row(512), row(512)],
        out_shape=[sd(512)] + [sd(LANES)] * 6 + [sd(512, BF16), sd(512, BF16), sd(512, BF16)],
        compiler_params=_cp(("parallel",)),
        name="nsa_prep",
    )(zb, tab, gq, gk, gm)


def _pool_rows(x, w):
    nblk = x.shape[0] // NSA_BLOCK
    return jnp.sum(x.reshape(nblk, NSA_BLOCK, LANES) * w[None], axis=1)


def _compress_kernel(k_ref, v_ref, w_ref, ck_ref, cv_ref):
    ck_ref[...] = _pool_rows(k_ref[...], w_ref[0])
    cv_ref[...] = _pool_rows(v_ref[...], w_ref[1])


def _compress(kc, vc, w2, n_rows, tm):
    row = pl.BlockSpec((tm, LANES), lambda i: (i, 0))
    out = pl.BlockSpec((tm // NSA_BLOCK, LANES), lambda i: (i, 0))
    sd = jax.ShapeDtypeStruct((n_rows // NSA_BLOCK, LANES), F32)
    return pl.pallas_call(
        _compress_kernel,
        grid=(n_rows // tm,),
        in_specs=[row, row, pl.BlockSpec(w2.shape, lambda i: (0, 0, 0))],
        out_specs=[out, out],
        out_shape=[sd, sd],
        compiler_params=_cp(("parallel",)),
        name="compress",
    )(kc, vc, w2)


def _select(imp, cur, n_cand):
    n_iota = lax.broadcasted_iota(jnp.int32, imp.shape, 1)
    cnt = jnp.zeros(imp.shape, F32)
    for m in range(n_cand):
        col = imp[:, m:m + 1]
        beats = (col > imp) | ((col == imp) & (n_iota > m))
        cnt = cnt + jnp.where(beats & (cur > m), 1.0, 0.0)
    keep = (n_iota == cur) | ((n_iota < cur) & (cnt < NSA_TOPN - 1))
    return jnp.where(keep, 1.0, 0.0)


def _expand_blocks(nb, nkeys, first_key=0):
    kb = (lax.broadcasted_iota(jnp.int32, (nb, nkeys), 1) + first_key) // NSA_BLOCK
    return jnp.where(kb == lax.broadcasted_iota(jnp.int32, (nb, nkeys), 0), 1.0, 0.0).astype(BF16)


def _gate_col(gate, head, j):
    c = GLA_GATE_RANK + head * 3 + j
    return _sigmoid(gate[:, c:c + 1])


def _nsa_prompt_kernel(q_ref, ck_ref, cv_ref, kv_ref, gate_ref, o_ref, *, tq, seq, wslab):
    i = pl.program_id(1)
    t0 = i * tq
    hd = NSA_BLOCK
    nb = seq // NSA_BLOCK
    q = q_ref[...]
    gate = gate_ref[...]
    pos = t0 + lax.broadcasted_iota(jnp.int32, (tq, 1), 0)
    cur = pos // NSA_BLOCK
    blk = lax.broadcasted_iota(jnp.int32, (tq, nb), 1)
    vis = (blk + 1) * NSA_BLOCK <= pos + 1
    kpos = lax.broadcasted_iota(jnp.int32, (tq, seq), 1)
    causal = kpos <= pos
    wstart = pl.multiple_of(jnp.maximum(t0 + tq - wslab, 0), LANES)
    wpos = wstart + lax.broadcasted_iota(jnp.int32, (tq, wslab), 1)
    wrel = pos - wpos
    wmask = (wrel >= 0) & (wrel < NSA_WINDOW)
    expand = _expand_blocks(nb, seq)
    ck = ck_ref[...]
    cv = cv_ref[...]
    outs = []
    for g in range(NSA_KV_HEADS):
        ckg = ck[:, g * hd:(g + 1) * hd]
        cvg = cv[:, g * hd:(g + 1) * hd]
        o_cmp = []
        imp = jnp.zeros((tq, nb), F32)
        for h in range(NSA_GROUP):
            qh = q[:, (g * NSA_GROUP + h) * hd:(g * NSA_GROUP + h + 1) * hd]
            pr = _msoftmax(_dot_nt(qh, ckg), vis)
            imp = imp + pr
            o_cmp.append(_dot(pr, cvg))
        sel = _select(imp, cur, nb)
        smask = (_dot(sel, expand) > 0.5) & causal
        ksg = kv_ref[:, g * hd:(g + 1) * hd]
        vsg = kv_ref[:, LANES + g * hd:LANES + (g + 1) * hd]
        kwg = kv_ref[pl.ds(wstart, wslab), 2 * LANES + g * hd:2 * LANES + (g + 1) * hd]
        vwg = kv_ref[pl.ds(wstart, wslab), 3 * LANES + g * hd:3 * LANES + (g + 1) * hd]
        for h in range(NSA_GROUP):
            head = g * NSA_GROUP + h
            qh = q[:, head * hd:(head + 1) * hd]
            o_sel = _dot(_msoftmax(_dot_nt(qh, ksg), smask), vsg)
            o_win = _dot(_msoftmax(_dot_nt(qh, kwg), wmask), vwg)
            outs.append(_gate_col(gate, head, 0) * o_cmp[h] + _gate_col(gate, head, 1) * o_sel
                        + _gate_col(gate, head, 2) * o_win)
    o_ref[...] = jnp.concatenate(outs, axis=1).astype(BF16)


def _nsa_prompt(qb, ck, cv, kvb, zd, batch, seq, tq):
    nq = seq // tq
    nb = seq // NSA_BLOCK
    wslab = min(seq, NSA_WINDOW + tq)
    kern = functools.partial(_nsa_prompt_kernel, tq=tq, seq=seq, wslab=wslab)
    return pl.pallas_call(
        kern,
        grid=(batch, nq),
        in_specs=[pl.BlockSpec((tq, 512), lambda b, i: (b * nq + i, 0)),
                  pl.BlockSpec((nb, LANES), lambda b, i: (b, 0)),
                  pl.BlockSpec((nb, LANES), lambda b, i: (b, 0)),
                  pl.BlockSpec((seq, 512), lambda b, i: (b, 0)),
                  pl.BlockSpec((tq, LANES), lambda b, i: (b * nq + i, 0))],
        out_specs=pl.BlockSpec((tq, 512), lambda b, i: (b * nq + i, 0)),
        out_shape=jax.ShapeDtypeStruct((batch * seq, 512), BF16),
        compiler_params=_cp(("parallel", "parallel")),
        name="nsa_prompt",
    )(qb, ck, cv, kvb, zd)


def _qbd(q, rows_per_head):
    hd = NSA_BLOCK
    lane_grp = lax.broadcasted_iota(jnp.int32, (rows_per_head, LANES), 1) // hd
    parts = []
    for head in range(NSA_HEADS):
        g = head // NSA_GROUP
        qh = q[:, head * hd:(head + 1) * hd]
        two = jnp.concatenate([qh, qh], axis=1)
        parts.append(jnp.where(lane_grp == g, two, 0.0))
    return jnp.concatenate(parts, axis=0)


def _nsa_sample_cmp_kernel(pt_ref, *refs, npg, t_new, past_len):
    kpages = refs[0:npg]
    vpages = refs[npg:2 * npg]
    q_ref, kn_ref, vn_ref, wt_ref, wn_ref, ocmp_ref, sel_ref, pk_ref, pv_ref = refs[2 * npg:]
    c = pl.program_id(1)
    nch = pl.num_programs(1)
    hd = NSA_BLOCK
    page = kpages[0].shape[1]
    bpp = page // NSA_BLOCK
    nbc = npg * bpp
    nb_pad = pk_ref.shape[1]
    n_past = past_len // NSA_BLOCK

    @pl.when(c == 0)
    def _():
        pk_ref[...] = jnp.zeros(pk_ref.shape, F32)
        pv_ref[...] = jnp.zeros(pv_ref.shape, F32)

    seg = jnp.where(lax.broadcasted_iota(jnp.int32, (npg * page, nbc), 0) // NSA_BLOCK
                    == lax.broadcasted_iota(jnp.int32, (npg * page, nbc), 1), 1.0, 0.0).astype(BF16)
    col = lax.broadcasted_iota(jnp.int32, (nbc, nb_pad), 1)
    place = jnp.where(col == c * nbc + lax.broadcasted_iota(jnp.int32, (nbc, nb_pad), 0), 1.0, 0.0).astype(BF16)
    for pages, w_t, pooled in ((kpages, wt_ref[0], pk_ref), (vpages, wt_ref[1], pv_ref)):
        x = jnp.concatenate([r[...] * w_t for r in pages], axis=1)
        pooled[...] += _dot_exact_rhs(_dot_exact_rhs(x, seg), place)

    @pl.when(c == nch - 1)
    def _():
        newk = jnp.sum(kn_ref[...] * wn_ref[0][0:t_new], axis=0, keepdims=True)
        newv = jnp.sum(vn_ref[...] * wn_ref[1][0:t_new], axis=0, keepdims=True)
        q = q_ref[...]
        rows = NSA_HEADS * t_new
        qbd = _qbd(q, t_new)
        trow = lax.broadcasted_iota(jnp.int32, (rows, 1), 0) % t_new
        pos = past_len + trow
        blk = lax.broadcasted_iota(jnp.int32, (rows, nb_pad), 1)
        vis = ((blk + 1) * NSA_BLOCK <= pos + 1) & (blk < n_past)
        vis_new = (n_past + 1) * NSA_BLOCK <= pos + 1
        s_past = jnp.where(vis, _dot(qbd, pk_ref[...]), NEG)
        s_new = jnp.where(vis_new, jnp.sum(qbd * newk, axis=-1, keepdims=True), NEG)
        m = jnp.maximum(jnp.max(s_past, axis=-1, keepdims=True), s_new)
        e_past = jnp.where(vis, jnp.exp(s_past - m), 0.0)
        e_new = jnp.where(vis_new, jnp.exp(s_new - m), 0.0)
        den = jnp.maximum(jnp.sum(e_past, axis=-1, keepdims=True) + e_new, 1e-30)
        pr = e_past / den
        o = _dot_nt(pr, pv_ref[...]) + (e_new / den) * newv
        outs = []
        for head in range(NSA_HEADS):
            g = head // NSA_GROUP
            outs.append(o[head * t_new:(head + 1) * t_new, g * hd:(g + 1) * hd])
        ocmp_ref[...] = jnp.concatenate(outs, axis=1)
        imps = []
        for g in range(NSA_KV_HEADS):
            acc = pr[g * NSA_GROUP * t_new:(g * NSA_GROUP + 1) * t_new]
            for h in range(1, NSA_GROUP):
                acc = acc + pr[(g * NSA_GROUP + h) * t_new:(g * NSA_GROUP + h + 1) * t_new]
            imps.append(acc)
        imp = jnp.concatenate(imps, axis=0)
        cur = (past_len + lax.broadcasted_iota(jnp.int32, (NSA_KV_HEADS * t_new, 1), 0) % t_new) // NSA_BLOCK
        sel_ref[...] = _select(imp, cur, n_past)


def _nsa_sample_cmp(page_table, pool_k, pool_v, q, kn, vn, wt, wn, t_new, npg):
    db, n_pages = page_table.shape
    page = pool_k.shape[2]
    past_len = n_pages * page
    nch = n_pages // npg
    nb_pad = -(-(past_len // NSA_BLOCK) // LANES) * LANES
    row0 = q.shape[0] // t_new - db

    def page_spec(j):
        return pl.BlockSpec((None, LANES, page), lambda b, c, pt: (pt[b, c * npg + j], 0, 0))

    tok = lambda w: pl.BlockSpec((t_new, w), lambda b, c, pt: (row0 + b, 0))
    full3 = lambda a: pl.BlockSpec(a.shape, lambda b, c, pt: (0, 0, 0))
    kern = functools.partial(_nsa_sample_cmp_kernel, npg=npg, t_new=t_new, past_len=past_len)
    grid_spec = pltpu.PrefetchScalarGridSpec(
        num_scalar_prefetch=1,
        grid=(db, nch),
        in_specs=[page_spec(j) for j in range(npg)] + [page_spec(j) for j in range(npg)]
        + [tok(512), tok(LANES), tok(LANES), full3(wt), full3(wn)],
        out_specs=[pl.BlockSpec((None, t_new, 512), lambda b, c, pt: (b, 0, 0)),
                   pl.BlockSpec((None, NSA_KV_HEADS * t_new, nb_pad), lambda b, c, pt: (b, 0, 0))],
        scratch_shapes=[pltpu.VMEM((LANES, nb_pad), F32), pltpu.VMEM((LANES, nb_pad), F32)],
    )
    return pl.pallas_call(
        kern,
        grid_spec=grid_spec,
        out_shape=[jax.ShapeDtypeStruct((db, t_new, 512), F32),
                   jax.ShapeDtypeStruct((db, NSA_KV_HEADS * t_new, nb_pad), F32)],
        compiler_params=_cp(("parallel", "arbitrary")),
        name="nsa_sample_cmp",
    )(page_table, *([pool_k] * npg), *([pool_v] * npg), q, kn, vn, wt, wn)


def _online_update(s, mask, v, m_ref, l_ref, acc_ref, v_transposed=False):
    s = jnp.where(mask, s, NEG)
    m_old = m_ref[...]
    m_new = jnp.maximum(m_old, jnp.max(s, axis=-1, keepdims=True))
    alpha = jnp.exp(m_old - m_new)
    e = jnp.where(mask, jnp.exp(s - m_new), 0.0)
    l_ref[...] = alpha * l_ref[...] + jnp.sum(e, axis=-1, keepdims=True)
    acc_ref[...] = alpha * acc_ref[...] + (_dot_nt(e, v) if v_transposed else _dot(e, v))
    m_ref[...] = m_new


def _nsa_sample_sel_kernel(pt_ref, *refs, npg, t_new, past_len):
    kpages = refs[0:npg]
    vpages = refs[npg:2 * npg]
    (q_ref, sel_ref, ksn_ref, vsn_ref, wk_ref, wv_ref, kwn_ref, vwn_ref, ocmp_ref, gate_ref,
     o_ref, m_ref, l_ref, acc_ref, pad_ref) = refs[2 * npg:]
    c = pl.program_id(1)
    nch = pl.num_programs(1)
    hd = NSA_BLOCK
    page = kpages[0].shape[1]
    bpp = page // NSA_BLOCK
    nbc = npg * bpp
    rows = NSA_HEADS * t_new
    q = q_ref[...]
    qbd = _qbd(q, t_new)

    @pl.when(c == 0)
    def _():
        m_ref[...] = jnp.full(m_ref.shape, NEG, F32)
        l_ref[...] = jnp.zeros(l_ref.shape, F32)
        acc_ref[...] = jnp.zeros(acc_ref.shape, F32)

    sel = sel_ref[...]
    nb_pad = sel.shape[1]
    sel_rows = jnp.concatenate([sel[(head // NSA_GROUP) * t_new:(head // NSA_GROUP + 1) * t_new]
                                for head in range(NSA_HEADS)], axis=0)
    blk_iota = lax.broadcasted_iota(jnp.int32, (nb_pad, npg * page), 0)
    key_blk = c * nbc + lax.broadcasted_iota(jnp.int32, (nb_pad, npg * page), 1) // NSA_BLOCK
    expand = jnp.where(blk_iota == key_blk, 1.0, 0.0).astype(BF16)
    smask = _dot(sel_rows, expand) > 0.5
    kcat = jnp.concatenate([r[...].astype(BF16) for r in kpages], axis=1)
    vcat = jnp.concatenate([r[...].astype(BF16) for r in vpages], axis=1)
    _online_update(_dot(qbd, kcat), smask, vcat, m_ref, l_ref, acc_ref, v_transposed=True)

    @pl.when(c == nch - 1)
    def _():
        trow = lax.broadcasted_iota(jnp.int32, (rows, 1), 0) % t_new
        pad_ref[...] = jnp.zeros(pad_ref.shape, F32)
        pad_ref[0:t_new, :] = ksn_ref[...]
        kn = pad_ref[...]
        pad_ref[0:t_new, :] = vsn_ref[...]
        vn = pad_ref[...]
        jn = lax.broadcasted_iota(jnp.int32, (rows, LANES), 1)
        _online_update(_dot_nt(qbd, kn), jn <= trow, vn, m_ref, l_ref, acc_ref)
        o_sel = acc_ref[...] / jnp.maximum(l_ref[...], 1e-30)
        wb = wk_ref.shape[1]
        jw = lax.broadcasted_iota(jnp.int32, (rows, wb), 1)
        rel = trow + wb - jw
        wmask = jnp.concatenate([(rel >= 0) & (rel < NSA_WINDOW), jn <= trow], axis=1)
        pad_ref[0:t_new, :] = kwn_ref[...]
        s_win = jnp.concatenate([_dot(qbd, wk_ref[...]), _dot_nt(qbd, pad_ref[...])], axis=1)
        p_win = _msoftmax(s_win, wmask)
        pad_ref[0:t_new, :] = vwn_ref[...]
        o_win = _dot_nt(p_win[:, 0:wb], wv_ref[...]) + _dot(p_win[:, wb:], pad_ref[...])
        gate = gate_ref[...]
        ocmp = ocmp_ref[...]
        outs = []
        for head in range(NSA_HEADS):
            g = head // NSA_GROUP
            r0, r1 = head * t_new, (head + 1) * t_new
            outs.append(_gate_col(gate, head, 0) * ocmp[:, head * hd:(head + 1) * hd]
                        + _gate_col(gate, head, 1) * o_sel[r0:r1, g * hd:(g + 1) * hd]
                        + _gate_col(gate, head, 2) * o_win[r0:r1, g * hd:(g + 1) * hd])
        o_ref[...] = jnp.concatenate(outs, axis=1)


def _nsa_sample_sel(page_table, pool_k, pool_v, q, sel, ksn, vsn, win_k, win_v, kwn, vwn, ocmp, zd, t_new, npg):
    db, n_pages = page_table.shape
    page = pool_k.shape[2]
    past_len = n_pages * page
    nch = n_pages // npg
    row0 = q.shape[0] // t_new - db
    rows = NSA_HEADS * t_new

    def page_spec(j):
        return pl.BlockSpec((None, LANES, page), lambda b, c, pt: (pt[b, c * npg + j], 0, 0))

    tok = lambda w: pl.BlockSpec((t_new, w), lambda b, c, pt: (row0 + b, 0))
    per_seq = lambda a: pl.BlockSpec((None,) + a.shape[1:], lambda b, c, pt: (b, 0, 0))
    kern = functools.partial(_nsa_sample_sel_kernel, npg=npg, t_new=t_new, past_len=past_len)
    grid_spec = pltpu.PrefetchScalarGridSpec(
        num_scalar_prefetch=1,
        grid=(db, nch),
        in_specs=[page_spec(j) for j in range(npg)] + [page_spec(j) for j in range(npg)]
        + [tok(512), per_seq(sel), tok(LANES), tok(LANES), per_seq(win_k), per_seq(win_v), tok(LANES), tok(LANES),
           per_seq(ocmp), tok(LANES)],
        out_specs=pl.BlockSpec((t_new, 512), lambda b, c, pt: (b, 0)),
        scratch_shapes=[pltpu.VMEM((rows, 1), F32), pltpu.VMEM((rows, 1), F32), pltpu.VMEM((rows, LANES), F32),
                        pltpu.VMEM((LANES, LANES), F32)],
    )
    return pl.pallas_call(
        kern,
        grid_spec=grid_spec,
        out_shape=jax.ShapeDtypeStruct((db * t_new, 512), F32),
        compiler_params=_cp(("parallel", "arbitrary")),
        name="nsa_sample_sel",
    )(page_table, *([pool_k] * npg), *([pool_v] * npg), q, sel, ksn, vsn, win_k, win_v, kwn, vwn, ocmp, zd)


def _log_sigmoid(x):
    return jnp.minimum(x, 0.0) - jnp.log(1.0 + jnp.exp(-jnp.abs(x)))


def _gla_kernel(za_ref, zd_ref, wgh_ref, wgl_ref, bg_ref, gout_ref, s0_ref, o_ref, s_ref,
                st_ref, kp_ref, bp_ref, vp_ref, *, nchunk, t_valid):
    j = pl.program_id(1)
    nj = pl.num_programs(1)
    C = GLA_CHUNK
    sub = GLA_SUB
    nsub = C // sub
    dk = 64
    dv = 128
    hk = GLA_HEADS * dk
    hv = GLA_HEADS * dv

    @pl.when(j == 0)
    def _():
        st_ref[...] = jnp.concatenate([s0_ref[h].T for h in range(GLA_HEADS)], axis=1)
        kp_ref[...] = jnp.zeros(kp_ref.shape, F32)
        bp_ref[...] = jnp.zeros(bp_ref.shape, F32)
        vp_ref[...] = jnp.zeros(vp_ref.shape, F32)

    tril = jnp.where(lax.broadcasted_iota(jnp.int32, (C, C), 0) >= lax.broadcasted_iota(jnp.int32, (C, C), 1),
                     1.0, 0.0).astype(BF16)
    head_ones = jnp.where(lax.broadcasted_iota(jnp.int32, (hk, LANES), 0) // dk
                          == lax.broadcasted_iota(jnp.int32, (hk, LANES), 1), 1.0, 0.0).astype(BF16)
    tmod = lax.broadcasted_iota(jnp.int32, (C, 1), 0) % sub
    gout = gout_ref[...]

    def chunk(c, carry):
        r0 = pl.multiple_of(c * C, C)
        z = za_ref[pl.ds(r0, C), :]
        q = z[:, 0:hk] * (dk ** -0.5)
        k = z[:, hk:2 * hk]
        v = z[:, 2 * hk:2 * hk + hv]
        r = z[:, 2 * hk + hv:2 * hk + 2 * hv]
        lr = zd_ref[pl.ds(r0, C), :][:, 0:GLA_GATE_RANK]
        la = _log_sigmoid(_dot_hi(lr, wgh_ref[...], wgl_ref[...]) + bg_ref[...]) * (1.0 / GLA_TAU)
        tglob = (j * nchunk + c) * C + lax.broadcasted_iota(jnp.int32, (C, 1), 0)
        la = jnp.where(tglob < t_valid, la, 0.0)
        b = _dot_exact_lhs(tril, la)
        st = st_ref[...]
        qe = q * jnp.exp(b)
        kp_ref[sub:sub + C, :] = k
        bp_ref[sub:sub + C, :] = b
        vp_ref[sub:sub + C, :] = v
        xs = []
        for d in range(sub):
            kd = kp_ref[sub - d:sub - d + C, :]
            bd = bp_ref[sub - d:sub - d + C, :]
            xs.append(jnp.where(tmod >= d, q * kd * jnp.exp(b - bd), 0.0).astype(BF16))
        rr = jnp.dot(jnp.concatenate(xs, axis=0), head_ones, preferred_element_type=F32)
        outs = []
        for h in range(GLA_HEADS):
            kh = slice(h * dk, (h + 1) * dk)
            vh = slice(h * dv, (h + 1) * dv)
            o = _dot_nt(qe[:, kh], st[:, kh])
            for d in range(sub):
                o = o + rr[d * C:(d + 1) * C, h:h + 1] * vp_ref[sub - d:sub - d + C, vh]
            offs = [jnp.zeros((sub, dv), F32)]
            for i in range(1, nsub):
                anchor = b[i * sub - 1:i * sub, kh]
                qt = q[i * sub:(i + 1) * sub, kh] * jnp.exp(b[i * sub:(i + 1) * sub, kh] - anchor)
                kt = k[0:i * sub, kh] * jnp.exp(anchor - b[0:i * sub, kh])
                offs.append(_dot(_dot_nt(qt, kt), v[0:i * sub, vh]))
            o = o + jnp.concatenate(offs, axis=0)
            on = o * lax.rsqrt(jnp.mean(o * o, axis=-1, keepdims=True) + EPS) * gout
            rh = r[:, vh]
            outs.append(on * (rh * _sigmoid(rh)))
        o_ref[pl.ds(r0, C), :] = jnp.concatenate(outs, axis=1).astype(BF16)
        b_last = b[C - 1:C, :]
        kk = k * jnp.exp(b_last - b)
        upd = jnp.concatenate([_dot_tn(v[:, h * dv:(h + 1) * dv], kk[:, h * dk:(h + 1) * dk])
                               for h in range(GLA_HEADS)], axis=1)
        st_ref[...] = st * jnp.exp(b_last) + upd
        return carry

    lax.fori_loop(0, nchunk, chunk, 0)

    @pl.when(j == nj - 1)
    def _():
        st = st_ref[...]
        for h in range(GLA_HEADS):
            s_ref[h] = st[:, h * dk:(h + 1) * dk].T


def _gla(za, zd, wgh, wgl, bg, gout, s0, batch, t_pad, t_valid, ct):
    nj = t_pad // ct
    hv = GLA_HEADS * 128
    kern = functools.partial(_gla_kernel, nchunk=ct // GLA_CHUNK, t_valid=t_valid)
    full = lambda a: pl.BlockSpec(a.shape, lambda b, j: (0,) * a.ndim)
    return pl.pallas_call(
        kern,
        grid=(batch, nj),
        in_specs=[pl.BlockSpec((ct, za.shape[1]), lambda b, j: (b * nj + j, 0)),
                  pl.BlockSpec((ct, LANES), lambda b, j: (b * nj + j, 0)),
                  full(wgh), full(wgl), full(bg), full(gout),
                  pl.BlockSpec((None,) + s0.shape[1:], lambda b, j: (b, 0, 0, 0))],
        out_specs=[pl.BlockSpec((ct, hv), lambda b, j: (b * nj + j, 0)),
                   pl.BlockSpec((None,) + s0.shape[1:], lambda b, j: (b, 0, 0, 0))],
        out_shape=[jax.ShapeDtypeStruct((batch * t_pad, hv), BF16), jax.ShapeDtypeStruct(s0.shape, F32)],
        scratch_shapes=[pltpu.VMEM((128, GLA_HEADS * 64), F32),
                        pltpu.VMEM((GLA_SUB + GLA_CHUNK, GLA_HEADS * 64), F32),
                        pltpu.VMEM((GLA_SUB + GLA_CHUNK, GLA_HEADS * 64), F32),
                        pltpu.VMEM((GLA_SUB + GLA_CHUNK, hv), F32)],
        compiler_params=_cp(("parallel", "arbitrary")),
        name="gla",
    )(za, zd, wgh, wgl, bg, gout, s0)


def _mem_kv_kernel(m_ref, g_ref, w_ref, gk_ref, k_ref, v_ref):
    x = m_ref[...]
    u = x * lax.rsqrt(jnp.mean(x * x, axis=-1, keepdims=True) + EPS) * g_ref[...]
    kv = jnp.dot(u.astype(BF16), w_ref[...], preferred_element_type=F32)
    half = kv.shape[1] // 2
    k_ref[...] = _seg_rms(kv[:, 0:half], half // MEM_HEADS, gk_ref[...])
    v_ref[...] = kv[:, half:]


def _mem_kv(mem, g, w, gk, tm):
    n, d = mem.shape
    half = w.shape[1] // 2
    full = lambda a: pl.BlockSpec(a.shape, lambda i: (0, 0))
    sd = jax.ShapeDtypeStruct((n, half), F32)
    return pl.pallas_call(
        _mem_kv_kernel,
        grid=(n // tm,),
        in_specs=[pl.BlockSpec((tm, d), lambda i: (i, 0)), full(g), full(w), full(gk)],
        out_specs=[pl.BlockSpec((tm, half), lambda i: (i, 0))] * 2,
        out_shape=[sd, sd],
        compiler_params=_cp(("parallel",)),
        name="mem_kv",
    )(mem, g, w, gk)


def _mem_prompt_kernel(q_ref, k_ref, v_ref, o_ref):
    q = q_ref[...]
    k = k_ref[...].astype(BF16)
    v = v_ref[...].astype(BF16)
    hd = q.shape[1] // MEM_HEADS
    outs = []
    for h in range(MEM_HEADS):
        sl = slice(h * hd, (h + 1) * hd)
        s = _dot_nt(q[:, sl], k[:, sl])
        m = jnp.max(s, axis=-1, keepdims=True)
        e = jnp.exp(s - m)
        outs.append(_dot(e / jnp.sum(e, axis=-1, keepdims=True), v[:, sl]))
    o_ref[...] = jnp.concatenate(outs, axis=1).astype(BF16)


def _mem_prompt(qm, mk, mv, batch, seq, mlen, tq):
    nq = seq // tq
    w = qm.shape[1]
    return pl.pallas_call(
        _mem_prompt_kernel,
        grid=(batch, nq),
        in_specs=[pl.BlockSpec((tq, w), lambda b, i: (b * nq + i, 0)),
                  pl.BlockSpec((mlen, w), lambda b, i: (b, 0)),
                  pl.BlockSpec((mlen, w), lambda b, i: (b, 0))],
        out_specs=pl.BlockSpec((tq, w), lambda b, i: (b * nq + i, 0)),
        out_shape=jax.ShapeDtypeStruct((batch * seq, w), BF16),
        compiler_params=_cp(("parallel", "parallel")),
        name="mem_prompt",
    )(qm, mk, mv)


def _mem_sample_kernel(q_ref, k_ref, v_ref, o_ref, *, sb, t_new):
    w = q_ref.shape[1]
    hd = w // MEM_HEADS
    rows = MEM_HEADS * t_new
    lane_head = lax.broadcasted_iota(jnp.int32, (t_new, w), 1) // hd
    for s_i in range(sb):
        q = q_ref[s_i * t_new:(s_i + 1) * t_new, :].astype(F32)
        qbd = jnp.concatenate([jnp.where(lane_head == h, q, 0.0) for h in range(MEM_HEADS)], axis=0)
        k = k_ref[s_i].astype(BF16)
        v = v_ref[s_i].astype(BF16)
        s = _dot_nt(qbd, k)
        m = jnp.max(s, axis=-1, keepdims=True)
        e = jnp.exp(s - m)
        o = _dot(e / jnp.sum(e, axis=-1, keepdims=True), v)
        o_ref[s_i * t_new:(s_i + 1) * t_new, :] = jnp.concatenate(
            [o[h * t_new:(h + 1) * t_new, h * hd:(h + 1) * hd] for h in range(MEM_HEADS)], axis=1).astype(BF16)


def _mem_sample(qm, ck, cv, row0_blocks, db, t_new, sb):
    w = qm.shape[1]
    mlen = ck.shape[1]
    kern = functools.partial(_mem_sample_kernel, sb=sb, t_new=t_new)
    return pl.pallas_call(
        kern,
        grid=(db // sb,),
        in_specs=[pl.BlockSpec((sb * t_new, w), lambda i: (row0_blocks + i, 0)),
                  pl.BlockSpec((sb, mlen, w), lambda i: (i, 0, 0)),
                  pl.BlockSpec((sb, mlen, w), lambda i: (i, 0, 0))],
        out_specs=pl.BlockSpec((sb * t_new, w), lambda i: (i, 0)),
        out_shape=jax.ShapeDtypeStruct((db * t_new, w), BF16),
        compiler_params=_cp(("parallel",)),
        name="mem_sample",
    )(qm, ck, cv)


def _merge_kernel(x_ref, og_ref, on_ref, om_ref, gate_ref, wg_ref, wn_ref, wm_ref, wo_ref, gf_ref,
                  wrh_ref, wrl_ref, br_ref, h_ref, hn_ref, cmb_ref, *, n_experts):
    d = x_ref.shape[1]
    gate = gate_ref[...].astype(F32)
    y = (gate[:, 0:d] * jnp.dot(og_ref[...], wg_ref[...], preferred_element_type=F32)
         + gate[:, d:2 * d] * jnp.dot(on_ref[...], wn_ref[...], preferred_element_type=F32)
         + gate[:, 2 * d:3 * d] * jnp.dot(om_ref[...], wm_ref[...], preferred_element_type=F32))
    h = x_ref[...] + jnp.dot(y.astype(BF16), wo_ref[...], preferred_element_type=F32)
    h_ref[...] = h
    hn = h * lax.rsqrt(jnp.mean(h * h, axis=-1, keepdims=True) + EPS) * gf_ref[...]
    hn_ref[...] = hn.astype(BF16)
    logits = _dot_hi(hn, wrh_ref[...], wrl_ref[...]) + br_ref[...]
    lane = lax.broadcasted_iota(jnp.int32, logits.shape, 1).astype(F32)
    work = jnp.where(lane < n_experts, logits, NEG)
    top = jnp.max(work, axis=-1, keepdims=True)
    chosen = jnp.zeros(logits.shape, F32)
    for _ in range(TOP_K):
        mx = jnp.max(work, axis=-1, keepdims=True)
        first = jnp.min(jnp.where(work == mx, lane, float(LANES)), axis=-1, keepdims=True)
        pick = lane == first
        chosen = jnp.where(pick, 1.0, chosen)
        work = jnp.where(pick, NEG, work)
    e = jnp.where(chosen > 0.5, jnp.exp(logits - top), 0.0)
    cmb_ref[...] = e / jnp.sum(e, axis=-1, keepdims=True)


def _merge(x, og, on, om, gate, wg, wn, wm, wo, gf, wrh, wrl, br, n_experts, tm):
    n, d = x.shape
    row = lambda a: pl.BlockSpec((tm, a.shape[1]), lambda i: (i, 0))
    full = lambda a: pl.BlockSpec(a.shape, lambda i: (0, 0))
    kern = functools.partial(_merge_kernel, n_experts=n_experts)
    return pl.pallas_call(
        kern,
        grid=(n // tm,),
        in_specs=[row(x), row(og), row(on), row(om), row(gate), full(wg), full(wn), full(wm), full(wo), full(gf),
                  full(wrh), full(wrl), full(br)],
        out_specs=[pl.BlockSpec((tm, d), lambda i: (i, 0)), pl.BlockSpec((tm, d), lambda i: (i, 0)),
                   pl.BlockSpec((tm, LANES), lambda i: (i, 0))],
        out_shape=[jax.ShapeDtypeStruct((n, d), F32), jax.ShapeDtypeStruct((n, d), BF16),
                   jax.ShapeDtypeStruct((n, LANES), F32)],
        compiler_params=_cp(("parallel",)),
        name="merge_router",
    )(x, og, on, om, gate, wg, wn, wm, wo, gf, wrh, wrl, br)


MOE_ROW_BLOCKS = (128, 64, 32)
MOE_MAIN_BLOCK = 256


def _moe_kernel(hn_ref, h_ref, cmb_ref, wgu_ref, bgu_ref, wd_ref, bd_ref, y_ref, rank_ref, rankt_ref, cnt_ref):
    e = pl.program_id(1)
    tt = hn_ref.shape[0]

    @pl.when(e == 0)
    def _():
        y_ref[...] = h_ref[...]
        sel = jnp.where(cmb_ref[...] > 0.0, 1.0, 0.0)
        lower = jnp.where(lax.broadcasted_iota(jnp.int32, (tt, tt), 0) > lax.broadcasted_iota(jnp.int32, (tt, tt), 1),
                          1.0, 0.0).astype(BF16)
        rank = jnp.where(sel > 0.0, jnp.dot(lower, sel.astype(BF16), preferred_element_type=F32), -1.0)
        rank_ref[...] = rank
        rankt_ref[...] = rank.T
        cnt_ref[...] = jnp.broadcast_to(jnp.sum(sel, axis=0, keepdims=True), cnt_ref.shape)

    lane = lax.broadcasted_iota(jnp.int32, (1, LANES), 1)
    cnt = jnp.sum(jnp.where(lane == e, cnt_ref[0:1, :], 0.0)).astype(jnp.int32)
    pick = jnp.where(lax.broadcasted_iota(jnp.int32, (LANES, LANES), 0) == e, 1.0, 0.0).astype(BF16)
    rank_col = _dot_exact_rhs(rank_ref[...], pick)
    w_col = _dot_exact_rhs(cmb_ref[...], pick)
    w_hi, w_mid, w_lo = _split3(w_col)
    rank_row = rankt_ref[pl.ds(e, 1), :]
    hn = hn_ref[...]
    dff2 = wgu_ref.shape[1]

    def block(r0, rb):
        rows = (r0 + lax.broadcasted_iota(jnp.int32, (rb, tt), 0)).astype(F32)
        p = jnp.where(rank_row == rows, 1.0, 0.0).astype(BF16)
        x = jnp.dot(p, hn, preferred_element_type=F32).astype(BF16)
        gu = jnp.dot(x, wgu_ref[...], preferred_element_type=F32) + bgu_ref[...]
        gate = jnp.minimum(gu, SWIGLU_LIMIT)
        up = pltpu.roll(jnp.clip(gu, -SWIGLU_LIMIT, SWIGLU_LIMIT), dff2 - 1, 1)
        act = (up + 1.0) * gate * _sigmoid(SWIGLU_ALPHA * gate)
        out = jnp.dot(act.astype(BF16), wd_ref[...], preferred_element_type=F32) + bd_ref[...]
        d = functools.partial(jnp.dot, preferred_element_type=F32)
        w_rows = d(p, w_hi) + d(p, w_mid) + d(p, w_lo)
        yw = (out * jnp.concatenate([w_rows] * (out.shape[1] // LANES), axis=1)).astype(BF16)
        cols = (r0 + lax.broadcasted_iota(jnp.int32, (tt, rb), 1)).astype(F32)
        rc = rank_col if rb <= LANES else jnp.concatenate([rank_col] * (rb // LANES), axis=1)
        pt = jnp.where(rc[:, 0:rb] == cols, 1.0, 0.0).astype(BF16)
        y_ref[...] += jnp.dot(pt, yw, preferred_element_type=F32)

    small = MOE_ROW_BLOCKS[-1]
    cnt_pad = (cnt + small - 1) // small * small
    n_main = cnt_pad // MOE_MAIN_BLOCK

    def main_body(i, carry):
        block(i * MOE_MAIN_BLOCK, MOE_MAIN_BLOCK)
        return carry

    lax.fori_loop(0, n_main, main_body, 0)
    r0 = n_main * MOE_MAIN_BLOCK
    rem = cnt_pad - r0
    for rb in MOE_ROW_BLOCKS:
        take = (rem & rb) != 0

        @pl.when(take)
        def _(r0=r0, rb=rb):
            block(r0, rb)

        r0 = r0 + jnp.where(take, rb, 0)


def _moe(hn, h, cmb, wgu, bgu, wd, bd, tt):
    n, d = h.shape
    ne, _, dff2 = wgu.shape
    row = lambda w: pl.BlockSpec((tt, w), lambda i, e: (i, 0))
    return pl.pallas_call(
        _moe_kernel,
        grid=(n // tt, ne),
        in_specs=[row(d), row(d), row(LANES),
                  pl.BlockSpec((None, d, dff2), lambda i, e: (e, 0, 0)),
                  pl.BlockSpec((None, 1, dff2), lambda i, e: (e, 0, 0)),
                  pl.BlockSpec((None, dff2, d), lambda i, e: (e, 0, 0)),
                  pl.BlockSpec((None, 1, d), lambda i, e: (e, 0, 0))],
        out_specs=row(d),
        out_shape=jax.ShapeDtypeStruct((n, d), F32),
        scratch_shapes=[pltpu.VMEM((tt, LANES), F32), pltpu.VMEM((LANES, tt), F32), pltpu.VMEM((8, LANES), F32)],
        compiler_params=_cp(("parallel", "arbitrary")),
        name="moe",
    )(hn, h, cmb, wgu, bgu, wd, bd)


def _rope_table(pos):
    half = 8
    inv = 1.0 / (ROPE_THETA ** (jnp.arange(half, dtype=F32) / half))
    ang = pos.astype(F32)[:, None] * inv[None, :]
    cos, sin = jnp.cos(ang), jnp.sin(ang)
    n = pos.shape[0]
    one = jnp.ones((n, 64 - 2 * half), F32)
    zero8 = jnp.zeros((n, half), F32)
    zero = jnp.zeros((n, 64 - 2 * half), F32)
    c = jnp.concatenate([cos, cos, one], axis=1)
    s1 = jnp.concatenate([-sin, zero8, zero], axis=1)
    s2 = jnp.concatenate([zero8, sin, zero], axis=1)
    return jnp.concatenate([c, c, s1, s1, s2, s2], axis=1)


def _hi_lo(w):
    hi = w.astype(BF16)
    return hi, (w - hi.astype(F32)).astype(BF16)


def kernel(x_prompt, x_sample, cache_cmp_k, cache_cmp_v, cache_sel_k, cache_sel_v, cache_win_k, cache_win_v, state_gla, cache_mem_k, cache_mem_v, page_table, mem_prompt, g_attn, w_in, w_gla_gate, b_gla_gate, g_gla_out, g_q_nsa, g_k_nsa, w_cmp_pos, g_q_mem, g_k_mem, g_mem, w_mem_kv, w_up_gla, w_up_nsa, w_up_mem, w_out, g_ffn, w_router, b_router, w_gate_up, b_gate_up, w_down, b_down):
    B, S, D = x_prompt.shape
    DB, T, _ = x_sample.shape
    n_pool, page = cache_cmp_k.shape[1:3]
    n_pages = page_table.shape[1]
    past_len = n_pages * page
    wb = cache_win_k.shape[2]
    mlen = mem_prompt.shape[1]
    ne = w_router.shape[2]
    Np, Ns = B * S, DB * T
    N = Np + Ns
    TM = 256
    assert cache_cmp_k.shape[0] == 1 and D == 1024 and S % TM == 0 and Ns % TM == 0 and TM % T == 0
    assert T <= NSA_BLOCK and past_len % NSA_BLOCK == 0 and S % NSA_BLOCK == 0

    gq, gk, gv, gr, glr, nq, nkv, ng, mq, mg = np.cumsum(
        [0, 256, 256, 512, 512, GLA_GATE_RANK, 512, 768, 3 * NSA_HEADS, 512]).tolist()
    w = w_in[0]
    wa = w[:, gq:glr].astype(BF16)
    wbm = jnp.concatenate([w[:, nq:ng], w[:, mq:mg]], axis=1).astype(BF16)
    wc = w[:, mg:].astype(BF16)
    wd = jnp.concatenate([w[:, glr:nq], w[:, ng:mq], jnp.zeros((D, LANES - GLA_GATE_RANK - 3 * NSA_HEADS), F32)],
                         axis=1).astype(BF16)

    x_all = jnp.concatenate([x_prompt.reshape(Np, D), x_sample.reshape(Ns, D)], axis=0)
    za, zb, zc, zd = _project(x_all, g_attn, wa, wbm, wc, wd, TM)

    pos_sample = past_len + jnp.arange(T, dtype=jnp.int32)
    tab = jnp.concatenate([_rope_table(jnp.arange(S, dtype=jnp.int32)),
                           jnp.tile(_rope_table(pos_sample), (TM // T, 1))], axis=0)
    gq_t = jnp.tile(g_q_nsa[0], NSA_HEADS)[None, :]
    gk_t = jnp.tile(g_k_nsa[0], (1, NSA_KV_HEADS))
    gm_t = jnp.tile(g_q_mem[0], MEM_HEADS)[None, :]
    q, kc, vc, ks, vs, kw, vw, qm, qb, kvb = _prep(zb, tab, gq_t, gk_t, gm_t, TM, Np // TM, S // TM)

    w2 = jnp.tile(w_cmp_pos[0], (1, 1, NSA_KV_HEADS))
    ck, cv = _compress(kc, vc, w2, Np, 512)
    o_nsa_p = _nsa_prompt(qb, ck, cv, kvb, zd, B, S, 128)

    rows_minor = lambda a: jnp.transpose(a[0], (0, 2, 3, 1)).reshape(a.shape[1], LANES, a.shape[2])
    pool = rows_minor
    npg = 16 if n_pages % 16 == 0 else n_pages
    wt = jnp.tile(jnp.transpose(w2, (0, 2, 1)), (1, 1, page // NSA_BLOCK))
    ocmp_s, sel_s = _nsa_sample_cmp(page_table, pool(cache_cmp_k), pool(cache_cmp_v), q, kc, vc, wt, w2, T, npg)
    win_k = rows_minor(cache_win_k)
    win_v = rows_minor(cache_win_v)
    o_nsa_s = _nsa_sample_sel(page_table, pool(cache_sel_k), pool(cache_sel_v), q, sel_s, ks, vs, win_k, win_v,
                              kw, vw, ocmp_s, zd, T, npg)
    o_nsa = jnp.concatenate([o_nsa_p, o_nsa_s.astype(BF16)], axis=0)

    wgh, wgl = _hi_lo(w_gla_gate[0])
    bg = b_gla_gate[0][None, :]
    gout = g_gla_out[0][None, :]
    s0_p = jnp.zeros((B,) + state_gla.shape[2:], F32)
    o_gla_p, s_gla_p = _gla(za, zd, wgh, wgl, bg, gout, s0_p, B, S, S, 512)
    t_pad = GLA_CHUNK
    za_s = jnp.pad(za[Np:].reshape(DB, T, -1), ((0, 0), (0, t_pad - T), (0, 0))).reshape(DB * t_pad, -1)
    zd_s = jnp.pad(zd[Np:].reshape(DB, T, -1), ((0, 0), (0, t_pad - T), (0, 0))).reshape(DB * t_pad, -1)
    o_gla_s, s_gla_s = _gla(za_s, zd_s, wgh, wgl, bg, gout, state_gla[0], DB, t_pad, T, t_pad)
    o_gla = jnp.concatenate([o_gla_p, o_gla_s.reshape(DB, t_pad, -1)[:, :T].reshape(Ns, -1)], axis=0)

    gkm = jnp.tile(g_k_mem[0], MEM_HEADS)[None, :]
    mem_k, mem_v = _mem_kv(mem_prompt.reshape(B * mlen, D), g_mem, w_mem_kv[0].astype(BF16), gkm, TM)
    o_mem_p = _mem_prompt(qm, mem_k, mem_v, B, S, mlen, 512)
    mw = MEM_HEADS * cache_mem_k.shape[-1]
    sb = 8
    o_mem_s = _mem_sample(qm, cache_mem_k[0].reshape(DB, mlen, mw), cache_mem_v[0].reshape(DB, mlen, mw),
                          Np // (sb * T), DB, T, sb)
    o_mem = jnp.concatenate([o_mem_p, o_mem_s], axis=0)

    wr = jnp.pad(w_router[0], ((0, 0), (0, LANES - ne)))
    wrh, wrl = _hi_lo(wr)
    br = jnp.pad(b_router[0], (0, LANES - ne))[None, :]
    h, hn, cmb = _merge(x_all, o_gla, o_nsa, o_mem, zc, w_up_gla[0].astype(BF16), w_up_nsa[0].astype(BF16),
                        w_up_mem[0].astype(BF16), w_out[0].astype(BF16), g_ffn, wrh, wrl, br, ne, TM)

    wd_exp = jnp.stack([w_down[0], jnp.zeros_like(w_down[0])], axis=2).reshape(ne, -1, D).astype(BF16)
    y = _moe(hn, h, cmb, w_gate_up[0].astype(BF16), b_gate_up[0][:, None, :], wd_exp, b_down[0][:, None, :],
             1024 if N % 1024 == 0 else TM)

    kvshape = (1, B, S, NSA_KV_HEADS, NSA_BLOCK)
    p_rows = lambda a: a[:Np].reshape(kvshape)
    s_rows = lambda a: a[Np:].reshape(1, DB, T, NSA_KV_HEADS, NSA_BLOCK)
    wbp = min(NSA_WINDOW, S)
    s_win = lambda cache, new: jnp.concatenate(
        [cache[:, :, T:], new[Np:].reshape(1, DB, T, NSA_KV_HEADS, NSA_BLOCK)], axis=2)
    mshape = (1, B, mlen, MEM_HEADS, mw // MEM_HEADS)
    return (y[:Np].reshape(B, S, D), y[Np:].reshape(DB, T, D),
            p_rows(kc), p_rows(vc), p_rows(ks), p_rows(vs),
            p_rows(kw)[:, :, S - wbp:], p_rows(vw)[:, :, S - wbp:],
            s_gla_p[None], mem_k.reshape(mshape), mem_v.reshape(mshape),
            s_rows(kc), s_rows(vc), s_rows(ks), s_rows(vs),
            s_win(cache_win_k, kw), s_win(cache_win_v, vw), s_gla_s[None])
```

```python
import functools

import numpy as np
import jax
import jax.numpy as jnp
from jax import lax
from jax.experimental import pallas as pl
from jax.experimental.pallas import tpu as pltpu

F32 = jnp.float32
BF16 = jnp.bfloat16

GLA_HEADS = 4
GLA_GATE_RANK = 16
GLA_TAU = 16.0
GLA_CHUNK = 64
GLA_SUB = 16
NSA_HEADS = 8
NSA_KV_HEADS = 2
NSA_GROUP = NSA_HEADS // NSA_KV_HEADS
NSA_BLOCK = 64
NSA_TOPN = 16
NSA_WINDOW = 512
MEM_HEADS = 4
ROPE_THETA = 500000.0
N_BRANCH = 3
TOP_K = 4
SWIGLU_LIMIT = 7.0
SWIGLU_ALPHA = 1.702
EPS = 1e-6

LANES = 128
NEG = -1e30
VMEM_LIMIT = 56 * 1024 * 1024


def _cp(sem):
    return pltpu.CompilerParams(dimension_semantics=sem, vmem_limit_bytes=VMEM_LIMIT)


def _dot(a, b):
    return jnp.dot(a.astype(BF16), b.astype(BF16), preferred_element_type=F32)


def _dot_nt(a, b):
    return lax.dot_general(a.astype(BF16), b.astype(BF16), (((1,), (1,)), ((), ())),
                           preferred_element_type=F32)


def _dot_tn(a, b):
    return lax.dot_general(a.astype(BF16), b.astype(BF16), (((0,), (0,)), ((), ())),
                           preferred_element_type=F32)


def _split3(x):
    hi = x.astype(BF16)
    r = x - hi.astype(F32)
    mid = r.astype(BF16)
    lo = (r - mid.astype(F32)).astype(BF16)
    return hi, mid, lo


def _dot_exact_rhs(x, m):
    hi, mid, lo = _split3(x)
    d = functools.partial(jnp.dot, preferred_element_type=F32)
    return d(hi, m) + d(mid, m) + d(lo, m)


def _dot_exact_lhs(m, x):
    hi, mid, lo = _split3(x)
    d = functools.partial(jnp.dot, preferred_element_type=F32)
    return d(m, hi) + d(m, mid) + d(m, lo)


def _dot_hi(x, w_hi, w_lo):
    hi, mid, _ = _split3(x)
    d = functools.partial(jnp.dot, preferred_element_type=F32)
    return d(hi, w_hi) + d(mid, w_hi) + d(hi, w_lo)


def _msoftmax(s, mask):
    s = jnp.where(mask, s, NEG)
    m = jnp.max(s, axis=-1, keepdims=True)
    e = jnp.where(mask, jnp.exp(s - m), 0.0)
    return e / jnp.maximum(jnp.sum(e, axis=-1, keepdims=True), 1e-30)


def _seg_ones(width, seg):
    r = lax.broadcasted_iota(jnp.int32, (width, width), 0) // seg
    c = lax.broadcasted_iota(jnp.int32, (width, width), 1) // seg
    return jnp.where(r == c, 1.0, 0.0).astype(BF16)


def _seg_rms(x, seg, gain):
    ms = _dot_exact_rhs(x * x, _seg_ones(x.shape[-1], seg)) * (1.0 / seg)
    return x * lax.rsqrt(ms + EPS) * gain


def _sigmoid(x):
    return 1.0 / (1.0 + jnp.exp(-x))


def _proj_kernel(x_ref, g_ref, wa_ref, wb_ref, wc_ref, wd_ref, za_ref, zb_ref, zc_ref, zd_ref):
    x = x_ref[...]
    u = x * lax.rsqrt(jnp.mean(x * x, axis=-1, keepdims=True) + EPS) * g_ref[...]
    ub = u.astype(BF16)
    za_ref[...] = jnp.dot(ub, wa_ref[...], preferred_element_type=F32)
    zb_ref[...] = jnp.dot(ub, wb_ref[...], preferred_element_type=F32)
    zc_ref[...] = _sigmoid(jnp.dot(ub, wc_ref[...], preferred_element_type=F32)).astype(BF16)
    zd_ref[...] = jnp.dot(ub, wd_ref[...], preferred_element_type=F32)


def _project(x, g, wa, wb, wc, wd, tm):
    n, d = x.shape
    full = lambda w: pl.BlockSpec(w.shape, lambda i: (0, 0))
    row = lambda w: pl.BlockSpec((tm, w), lambda i: (i, 0))
    return pl.pallas_call(
        _proj_kernel,
        grid=(n // tm,),
        in_specs=[row(d), full(g), full(wa), full(wb), full(wc), full(wd)],
        out_specs=[row(wa.shape[1]), row(wb.shape[1]), row(wc.shape[1]), row(wd.shape[1])],
        out_shape=[jax.ShapeDtypeStruct((n, wa.shape[1]), F32), jax.ShapeDtypeStruct((n, wb.shape[1]), F32),
                   jax.ShapeDtypeStruct((n, wc.shape[1]), BF16), jax.ShapeDtypeStruct((n, wd.shape[1]), F32)],
        compiler_params=_cp(("parallel",)),
        name="proj",
    )(x, g, wa, wb, wc, wd)


def _rope(x, tab):
    c, s1, s2 = tab[:, 0:LANES], tab[:, LANES:2 * LANES], tab[:, 2 * LANES:3 * LANES]
    half = 8
    return x * c + pltpu.roll(x, LANES - half, 1) * s1 + pltpu.roll(x, half, 1) * s2


def _prep_kernel(zb_ref, tab_ref, gq_ref, gk_ref, gm_ref,
                 q_ref, kc_ref, vc_ref, ks_ref, vs_ref, kw_ref, vw_ref, qm_ref, qb_ref, kvb_ref):
    z = zb_ref[...]
    tab = tab_ref[...]
    hd = NSA_BLOCK
    nq = NSA_HEADS * hd
    qn = _seg_rms(z[:, 0:nq], hd, gq_ref[...])
    q = jnp.concatenate([_rope(qn[:, j * LANES:(j + 1) * LANES], tab) for j in range(nq // LANES)], axis=1)
    q = q * (hd ** -0.5)
    q_ref[...] = q
    qb_ref[...] = q.astype(BF16)
    kv = z[:, nq:nq + 6 * LANES]
    gk = gk_ref[...]
    ks = []
    for j, ref in ((0, kc_ref), (1, ks_ref), (2, kw_ref)):
        kj = _rope(_seg_rms(kv[:, 2 * j * LANES:(2 * j + 1) * LANES], hd, gk[j:j + 1, :]), tab)
        ref[...] = kj
        ks.append(kj)
    vs = []
    for j, ref in ((0, vc_ref), (1, vs_ref), (2, vw_ref)):
        vj = kv[:, (2 * j + 1) * LANES:(2 * j + 2) * LANES]
        ref[...] = vj
        vs.append(vj)
    kvb_ref[...] = jnp.concatenate([ks[1], vs[1], ks[2], vs[2]], axis=1).astype(BF16)
    zm = z[:, nq + 6 * LANES:]
    mhd = zm.shape[1] // MEM_HEADS
    qm_ref[...] = (_seg_rms(zm, mhd, gm_ref[...]) * (mhd ** -0.5)).astype(BF16)


def _prep(zb, tab, gq, gk, gm, tm, n_prompt_tiles, tab_tiles):
    n = zb.shape[0]
    row = lambda w: pl.BlockSpec((tm, w), lambda i: (i, 0))
    full = lambda a: pl.BlockSpec(a.shape, lambda i: (0, 0))
    tab_spec = pl.BlockSpec((tm, 3 * LANES), lambda i: (jnp.where(i < n_prompt_tiles, i % tab_tiles, tab_tiles), 0))
    sd = lambda w, dt=F32: jax.ShapeDtypeStruct((n, w), dt)
    return pl.pallas_call(
        _prep_kernel,
        grid=(n // tm,),
        in_specs=[row(zb.shape[1]), tab_spec, full(gq), full(gk), full(gm)],
        out_specs=[row(512)] + [row(LANES)] * 6 + [row(512), row(512), row(512)],
        out_shape=[sd(512)] + [sd(LANES)] * 6 + [sd(512, BF16), sd(512, BF16), sd(512, BF16)],
        compiler_params=_cp(("parallel",)),
        name="nsa_prep",
    )(zb, tab, gq, gk, gm)


def _pool_rows(x, w):
    nblk = x.shape[0] // NSA_BLOCK
    return jnp.sum(x.reshape(nblk, NSA_BLOCK, LANES) * w[None], axis=1)


def _compress_kernel(k_ref, v_ref, w_ref, ck_ref, cv_ref):
    ck_ref[...] = _pool_rows(k_ref[...], w_ref[0])
    cv_ref[...] = _pool_rows(v_ref[...], w_ref[1])


def _compress(kc, vc, w2, n_rows, tm):
    row = pl.BlockSpec((tm, LANES), lambda i: (i, 0))
    out = pl.BlockSpec((tm // NSA_BLOCK, LANES), lambda i: (i, 0))
    sd = jax.ShapeDtypeStruct((n_rows // NSA_BLOCK, LANES), F32)
    return pl.pallas_call(
        _compress_kernel,
        grid=(n_rows // tm,),
        in_specs=[row, row, pl.BlockSpec(w2.shape, lambda i: (0, 0, 0))],
        out_specs=[out, out],
        out_shape=[sd, sd],
        compiler_params=_cp(("parallel",)),
        name="compress",
    )(kc, vc, w2)


def _select(imp, cur, n_cand):
    n_iota = lax.broadcasted_iota(jnp.int32, imp.shape, 1)
    cnt = jnp.zeros(imp.shape, F32)
    for m in range(n_cand):
        col = imp[:, m:m + 1]
        beats = (col > imp) | ((col == imp) & (n_iota > m))
        cnt = cnt + jnp.where(beats & (cur > m), 1.0, 0.0)
    keep = (n_iota == cur) | ((n_iota < cur) & (cnt < NSA_TOPN - 1))
    return jnp.where(keep, 1.0, 0.0)


def _expand_blocks(nb, nkeys, first_key=0):
    kb = (lax.broadcasted_iota(jnp.int32, (nb, nkeys), 1) + first_key) // NSA_BLOCK
    return jnp.where(kb == lax.broadcasted_iota(jnp.int32, (nb, nkeys), 0), 1.0, 0.0).astype(BF16)


def _gate_col(gate, head, j):
    c = GLA_GATE_RANK + head * 3 + j
    return _sigmoid(gate[:, c:c + 1])


def _attend(qh, k, bias, v_ones, hd):
    s = _dot_nt(qh, k) + bias
    e = jnp.exp((s - jnp.max(s, axis=-1, keepdims=True)).astype(BF16))
    pv = jnp.dot(e, v_ones, preferred_element_type=F32)
    return pv[:, 0:hd] / pv[:, hd:hd + 1]


def _nsa_prompt_kernel(q_ref, ck_ref, cv_ref, kv_ref, gate_ref, o_ref, osel_ref, *, tq, seq, wslab, nbr):
    i = pl.program_id(1)
    nq = pl.num_programs(1)
    t0 = i * tq
    hd = NSA_BLOCK
    nb = seq // NSA_BLOCK
    q = q_ref[...]
    gate = gate_ref[...]
    pos = t0 + lax.broadcasted_iota(jnp.int32, (tq, 1), 0)
    cur = pos // NSA_BLOCK
    blk = lax.broadcasted_iota(jnp.int32, (tq, nb), 1)
    vis = (blk + 1) * NSA_BLOCK <= pos + 1
    wstart = pl.multiple_of(jnp.maximum(t0 + tq - wslab, 0), LANES)
    wpos = wstart + lax.broadcasted_iota(jnp.int32, (tq, wslab), 1)
    wrel = pos - wpos
    wbias = jnp.where((wrel >= 0) & (wrel < NSA_WINDOW), 0.0, NEG)
    ck = ck_ref[...]
    cv = cv_ref[...]
    outs = []
    for g in range(NSA_KV_HEADS):
        ckg = ck[:, g * hd:(g + 1) * hd]
        cvg = cv[:, g * hd:(g + 1) * hd]
        o_cmp = []
        imp = jnp.zeros((tq, nb), F32)
        for h in range(NSA_GROUP):
            qh = q[:, (g * NSA_GROUP + h) * hd:(g * NSA_GROUP + h + 1) * hd]
            pr = _msoftmax(_dot_nt(qh, ckg), vis)
            imp = imp + pr
            o_cmp.append(_dot(pr, cvg))
        sel = _select(imp, cur, nb).astype(BF16)
        for j in range(nbr):
            klen = (j + 1) * seq // nbr

            @pl.when(i * nbr // nq == j)
            def _(g=g, klen=klen, sel=sel):
                kpos = lax.broadcasted_iota(jnp.int32, (tq, klen), 1)
                picked = jnp.dot(sel, _expand_blocks(nb, klen), preferred_element_type=F32) > 0.5
                sbias = jnp.where(picked & (kpos <= pos), 0.0, NEG)
                ksg = kv_ref[0:klen, g * hd:(g + 1) * hd]
                vsg = kv_ref[0:klen, LANES + g * hd:LANES + (g + 1) * hd]
                vs1 = jnp.concatenate([vsg, jnp.ones_like(vsg)], axis=1)
                for h in range(NSA_GROUP):
                    hl = slice((g * NSA_GROUP + h) * hd, (g * NSA_GROUP + h + 1) * hd)
                    osel_ref[:, hl] = _attend(q[:, hl], ksg, sbias, vs1, hd)

        kwg = kv_ref[pl.ds(wstart, wslab), 2 * LANES + g * hd:2 * LANES + (g + 1) * hd]
        vwg = kv_ref[pl.ds(wstart, wslab), 3 * LANES + g * hd:3 * LANES + (g + 1) * hd]
        vw1 = jnp.concatenate([vwg, jnp.ones_like(vwg)], axis=1)
        for h in range(NSA_GROUP):
            head = g * NSA_GROUP + h
            hl = slice(head * hd, (head + 1) * hd)
            o_win = _attend(q[:, hl], kwg, wbias, vw1, hd)
            outs.append(_gate_col(gate, head, 0) * o_cmp[h] + _gate_col(gate, head, 1) * osel_ref[:, hl]
                        + _gate_col(gate, head, 2) * o_win)
    o_ref[...] = jnp.concatenate(outs, axis=1).astype(BF16)


def _nsa_prompt(qb, ck, cv, kvb, zd, batch, seq, tq):
    nq = seq // tq
    nb = seq // NSA_BLOCK
    wslab = min(seq, NSA_WINDOW + tq)
    nbr = 4 if nq % 4 == 0 else 1
    kern = functools.partial(_nsa_prompt_kernel, tq=tq, seq=seq, wslab=wslab, nbr=nbr)
    return pl.pallas_call(
        kern,
        grid=(batch, nq),
        in_specs=[pl.BlockSpec((tq, 512), lambda b, i: (b * nq + i, 0)),
                  pl.BlockSpec((nb, LANES), lambda b, i: (b, 0)),
                  pl.BlockSpec((nb, LANES), lambda b, i: (b, 0)),
                  pl.BlockSpec((seq, 512), lambda b, i: (b, 0)),
                  pl.BlockSpec((tq, LANES), lambda b, i: (b * nq + i, 0))],
        out_specs=pl.BlockSpec((tq, 512), lambda b, i: (b * nq + i, 0)),
        out_shape=jax.ShapeDtypeStruct((batch * seq, 512), BF16),
        scratch_shapes=[pltpu.VMEM((tq, 512), F32)],
        compiler_params=_cp(("parallel", "parallel")),
        name="nsa_prompt",
    )(qb, ck, cv, kvb, zd)


def _qbd(q, rows_per_head):
    hd = NSA_BLOCK
    lane_grp = lax.broadcasted_iota(jnp.int32, (rows_per_head, LANES), 1) // hd
    parts = []
    for head in range(NSA_HEADS):
        g = head // NSA_GROUP
        qh = q[:, head * hd:(head + 1) * hd]
        two = jnp.concatenate([qh, qh], axis=1)
        parts.append(jnp.where(lane_grp == g, two, 0.0))
    return jnp.concatenate(parts, axis=0)


def _nsa_sample_cmp_kernel(pt_ref, *refs, npg, t_new, past_len):
    kpages = refs[0:npg]
    vpages = refs[npg:2 * npg]
    q_ref, kn_ref, vn_ref, wt_ref, wn_ref, ocmp_ref, sel_ref, pk_ref, pv_ref = refs[2 * npg:]
    c = pl.program_id(1)
    nch = pl.num_programs(1)
    hd = NSA_BLOCK
    page = kpages[0].shape[1]
    bpp = page // NSA_BLOCK
    nbc = npg * bpp
    nb_pad = pk_ref.shape[1]
    n_past = past_len // NSA_BLOCK

    @pl.when(c == 0)
    def _():
        pk_ref[...] = jnp.zeros(pk_ref.shape, F32)
        pv_ref[...] = jnp.zeros(pv_ref.shape, F32)

    seg = jnp.where(lax.broadcasted_iota(jnp.int32, (npg * page, nbc), 0) // NSA_BLOCK
                    == lax.broadcasted_iota(jnp.int32, (npg * page, nbc), 1), 1.0, 0.0).astype(BF16)
    col = lax.broadcasted_iota(jnp.int32, (nbc, nb_pad), 1)
    place = jnp.where(col == c * nbc + lax.broadcasted_iota(jnp.int32, (nbc, nb_pad), 0), 1.0, 0.0).astype(BF16)
    for pages, w_t, pooled in ((kpages, wt_ref[0], pk_ref), (vpages, wt_ref[1], pv_ref)):
        x = jnp.concatenate([r[...] * w_t for r in pages], axis=1)
        pooled[...] += _dot_exact_rhs(_dot_exact_rhs(x, seg), place)

    @pl.when(c == nch - 1)
    def _():
        newk = jnp.sum(kn_ref[...] * wn_ref[0][0:t_new], axis=0, keepdims=True)
        newv = jnp.sum(vn_ref[...] * wn_ref[1][0:t_new], axis=0, keepdims=True)
        q = q_ref[...]
        rows = NSA_HEADS * t_new
        qbd = _qbd(q, t_new)
        trow = lax.broadcasted_iota(jnp.int32, (rows, 1), 0) % t_new
        pos = past_len + trow
        blk = lax.broadcasted_iota(jnp.int32, (rows, nb_pad), 1)
        vis = ((blk + 1) * NSA_BLOCK <= pos + 1) & (blk < n_past)
        vis_new = (n_past + 1) * NSA_BLOCK <= pos + 1
        s_past = jnp.where(vis, _dot(qbd, pk_ref[...]), NEG)
        s_new = jnp.where(vis_new, jnp.sum(qbd * newk, axis=-1, keepdims=True), NEG)
        m = jnp.maximum(jnp.max(s_past, axis=-1, keepdims=True), s_new)
        e_past = jnp.where(vis, jnp.exp(s_past - m), 0.0)
        e_new = jnp.where(vis_new, jnp.exp(s_new - m), 0.0)
        den = jnp.maximum(jnp.sum(e_past, axis=-1, keepdims=True) + e_new, 1e-30)
        pr = e_past / den
        o = _dot_nt(pr, pv_ref[...]) + (e_new / den) * newv
        outs = []
        for head in range(NSA_HEADS):
            g = head // NSA_GROUP
            outs.append(o[head * t_new:(head + 1) * t_new, g * hd:(g + 1) * hd])
        ocmp_ref[...] = jnp.concatenate(outs, axis=1)
        imps = []
        for g in range(NSA_KV_HEADS):
            acc = pr[g * NSA_GROUP * t_new:(g * NSA_GROUP + 1) * t_new]
            for h in range(1, NSA_GROUP):
                acc = acc + pr[(g * NSA_GROUP + h) * t_new:(g * NSA_GROUP + h + 1) * t_new]
            imps.append(acc)
        imp = jnp.concatenate(imps, axis=0)
        cur = (past_len + lax.broadcasted_iota(jnp.int32, (NSA_KV_HEADS * t_new, 1), 0) % t_new) // NSA_BLOCK
        sel_ref[...] = _select(imp, cur, n_past)


def _nsa_sample_cmp(page_table, pool_k, pool_v, q, kn, vn, wt, wn, t_new, npg):
    db, n_pages = page_table.shape
    page = pool_k.shape[2]
    past_len = n_pages * page
    nch = n_pages // npg
    nb_pad = -(-(past_len // NSA_BLOCK) // LANES) * LANES
    row0 = q.shape[0] // t_new - db

    def page_spec(j):
        return pl.BlockSpec((None, LANES, page), lambda b, c, pt: (pt[b, c * npg + j], 0, 0))

    tok = lambda w: pl.BlockSpec((t_new, w), lambda b, c, pt: (row0 + b, 0))
    full3 = lambda a: pl.BlockSpec(a.shape, lambda b, c, pt: (0, 0, 0))
    kern = functools.partial(_nsa_sample_cmp_kernel, npg=npg, t_new=t_new, past_len=past_len)
    grid_spec = pltpu.PrefetchScalarGridSpec(
        num_scalar_prefetch=1,
        grid=(db, nch),
        in_specs=[page_spec(j) for j in range(npg)] + [page_spec(j) for j in range(npg)]
        + [tok(512), tok(LANES), tok(LANES), full3(wt), full3(wn)],
        out_specs=[pl.BlockSpec((None, t_new, 512), lambda b, c, pt: (b, 0, 0)),
                   pl.BlockSpec((None, NSA_KV_HEADS * t_new, nb_pad), lambda b, c, pt: (b, 0, 0))],
        scratch_shapes=[pltpu.VMEM((LANES, nb_pad), F32), pltpu.VMEM((LANES, nb_pad), F32)],
    )
    return pl.pallas_call(
        kern,
        grid_spec=grid_spec,
        out_shape=[jax.ShapeDtypeStruct((db, t_new, 512), F32),
                   jax.ShapeDtypeStruct((db, NSA_KV_HEADS * t_new, nb_pad), F32)],
        compiler_params=_cp(("parallel", "arbitrary")),
        name="nsa_sample_cmp",
    )(page_table, *([pool_k] * npg), *([pool_v] * npg), q, kn, vn, wt, wn)


def _online_update(s, mask, v, m_ref, l_ref, acc_ref, v_transposed=False):
    s = jnp.where(mask, s, NEG)
    m_old = m_ref[...]
    m_new = jnp.maximum(m_old, jnp.max(s, axis=-1, keepdims=True))
    alpha = jnp.exp(m_old - m_new)
    e = jnp.where(mask, jnp.exp(s - m_new), 0.0)
    l_ref[...] = alpha * l_ref[...] + jnp.sum(e, axis=-1, keepdims=True)
    acc_ref[...] = alpha * acc_ref[...] + (_dot_nt(e, v) if v_transposed else _dot(e, v))
    m_ref[...] = m_new


def _nsa_sample_sel_kernel(pt_ref, *refs, npg, t_new, past_len):
    kpages = refs[0:npg]
    vpages = refs[npg:2 * npg]
    (q_ref, sel_ref, ksn_ref, vsn_ref, wk_ref, wv_ref, kwn_ref, vwn_ref, ocmp_ref, gate_ref,
     o_ref, m_ref, l_ref, acc_ref, pad_ref) = refs[2 * npg:]
    c = pl.program_id(1)
    nch = pl.num_programs(1)
    hd = NSA_BLOCK
    page = kpages[0].shape[1]
    bpp = page // NSA_BLOCK
    nbc = npg * bpp
    rows = NSA_HEADS * t_new
    q = q_ref[...]
    qbd = _qbd(q, t_new)

    @pl.when(c == 0)
    def _():
        m_ref[...] = jnp.full(m_ref.shape, NEG, F32)
        l_ref[...] = jnp.zeros(l_ref.shape, F32)
        acc_ref[...] = jnp.zeros(acc_ref.shape, F32)

    sel = sel_ref[...]
    nb_pad = sel.shape[1]
    sel_rows = jnp.concatenate([sel[(head // NSA_GROUP) * t_new:(head // NSA_GROUP + 1) * t_new]
                                for head in range(NSA_HEADS)], axis=0)
    blk_iota = lax.broadcasted_iota(jnp.int32, (nb_pad, npg * page), 0)
    key_blk = c * nbc + lax.broadcasted_iota(jnp.int32, (nb_pad, npg * page), 1) // NSA_BLOCK
    expand = jnp.where(blk_iota == key_blk, 1.0, 0.0).astype(BF16)
    smask = _dot(sel_rows, expand) > 0.5
    kcat = jnp.concatenate([r[...].astype(BF16) for r in kpages], axis=1)
    vcat = jnp.concatenate([r[...].astype(BF16) for r in vpages], axis=1)
    _online_update(_dot(qbd, kcat), smask, vcat, m_ref, l_ref, acc_ref, v_transposed=True)

    @pl.when(c == nch - 1)
    def _():
        trow = lax.broadcasted_iota(jnp.int32, (rows, 1), 0) % t_new
        pad_ref[...] = jnp.zeros(pad_ref.shape, F32)
        pad_ref[0:t_new, :] = ksn_ref[...]
        kn = pad_ref[...]
        pad_ref[0:t_new, :] = vsn_ref[...]
        vn = pad_ref[...]
        jn = lax.broadcasted_iota(jnp.int32, (rows, LANES), 1)
        _online_update(_dot_nt(qbd, kn), jn <= trow, vn, m_ref, l_ref, acc_ref)
        o_sel = acc_ref[...] / jnp.maximum(l_ref[...], 1e-30)
        wb = wk_ref.shape[1]
        jw = lax.broadcasted_iota(jnp.int32, (rows, wb), 1)
        rel = trow + wb - jw
        wmask = jnp.concatenate([(rel >= 0) & (rel < NSA_WINDOW), jn <= trow], axis=1)
        pad_ref[0:t_new, :] = kwn_ref[...]
        s_win = jnp.concatenate([_dot(qbd, wk_ref[...]), _dot_nt(qbd, pad_ref[...])], axis=1)
        p_win = _msoftmax(s_win, wmask)
        pad_ref[0:t_new, :] = vwn_ref[...]
        o_win = _dot_nt(p_win[:, 0:wb], wv_ref[...]) + _dot(p_win[:, wb:], pad_ref[...])
        gate = gate_ref[...]
        ocmp = ocmp_ref[...]
        outs = []
        for head in range(NSA_HEADS):
            g = head // NSA_GROUP
            r0, r1 = head * t_new, (head + 1) * t_new
            outs.append(_gate_col(gate, head, 0) * ocmp[:, head * hd:(head + 1) * hd]
                        + _gate_col(gate, head, 1) * o_sel[r0:r1, g * hd:(g + 1) * hd]
                        + _gate_col(gate, head, 2) * o_win[r0:r1, g * hd:(g + 1) * hd])
        o_ref[...] = jnp.concatenate(outs, axis=1)


def _nsa_sample_sel(page_table, pool_k, pool_v, q, sel, ksn, vsn, win_k, win_v, kwn, vwn, ocmp, zd, t_new, npg):
    db, n_pages = page_table.shape
    page = pool_k.shape[2]
    past_len = n_pages * page
    nch = n_pages // npg
    row0 = q.shape[0] // t_new - db
    rows = NSA_HEADS * t_new

    def page_spec(j):
        return pl.BlockSpec((None, LANES, page), lambda b, c, pt: (pt[b, c * npg + j], 0, 0))

    tok = lambda w: pl.BlockSpec((t_new, w), lambda b, c, pt: (row0 + b, 0))
    per_seq = lambda a: pl.BlockSpec((None,) + a.shape[1:], lambda b, c, pt: (b, 0, 0))
    kern = functools.partial(_nsa_sample_sel_kernel, npg=npg, t_new=t_new, past_len=past_len)
    grid_spec = pltpu.PrefetchScalarGridSpec(
        num_scalar_prefetch=1,
        grid=(db, nch),
        in_specs=[page_spec(j) for j in range(npg)] + [page_spec(j) for j in range(npg)]
        + [tok(512), per_seq(sel), tok(LANES), tok(LANES), per_seq(win_k), per_seq(win_v), tok(LANES), tok(LANES),
           per_seq(ocmp), tok(LANES)],
        out_specs=pl.BlockSpec((t_new, 512), lambda b, c, pt: (b, 0)),
        scratch_shapes=[pltpu.VMEM((rows, 1), F32), pltpu.VMEM((rows, 1), F32), pltpu.VMEM((rows, LANES), F32),
                        pltpu.VMEM((LANES, LANES), F32)],
    )
    return pl.pallas_call(
        kern,
        grid_spec=grid_spec,
        out_shape=jax.ShapeDtypeStruct((db * t_new, 512), F32),
        compiler_params=_cp(("parallel", "arbitrary")),
        name="nsa_sample_sel",
    )(page_table, *([pool_k] * npg), *([pool_v] * npg), q, sel, ksn, vsn, win_k, win_v, kwn, vwn, ocmp, zd)


def _log_sigmoid(x):
    return jnp.minimum(x, 0.0) - jnp.log(1.0 + jnp.exp(-jnp.abs(x)))


def _gla_kernel(za_ref, zd_ref, wgh_ref, wgl_ref, bg_ref, gout_ref, s0_ref, o_ref, s_ref,
                st_ref, kp_ref, bp_ref, vp_ref, za_pad_ref, zd_pad_ref, *, nchunk, t_valid):
    j = pl.program_id(1)
    short = za_ref.shape[0] < GLA_CHUNK
    nj = pl.num_programs(1)
    C = GLA_CHUNK
    sub = GLA_SUB
    nsub = C // sub
    dk = 64
    dv = 128
    hk = GLA_HEADS * dk
    hv = GLA_HEADS * dv

    @pl.when(j == 0)
    def _():
        st_ref[...] = jnp.concatenate([s0_ref[h].T for h in range(GLA_HEADS)], axis=1)
        kp_ref[...] = jnp.zeros(kp_ref.shape, F32)
        bp_ref[...] = jnp.zeros(bp_ref.shape, F32)
        vp_ref[...] = jnp.zeros(vp_ref.shape, F32)

    tril = jnp.where(lax.broadcasted_iota(jnp.int32, (C, C), 0) >= lax.broadcasted_iota(jnp.int32, (C, C), 1),
                     1.0, 0.0).astype(BF16)
    head_ones = jnp.where(lax.broadcasted_iota(jnp.int32, (hk, LANES), 0) // dk
                          == lax.broadcasted_iota(jnp.int32, (hk, LANES), 1), 1.0, 0.0).astype(BF16)
    tmod = lax.broadcasted_iota(jnp.int32, (C, 1), 0) % sub
    gout = gout_ref[...]

    def chunk(c, carry):
        r0 = pl.multiple_of(c * C, C)
        if short:
            za_pad_ref[...] = jnp.zeros(za_pad_ref.shape, F32)
            zd_pad_ref[...] = jnp.zeros(zd_pad_ref.shape, F32)
            za_pad_ref[0:za_ref.shape[0], :] = za_ref[...]
            zd_pad_ref[0:zd_ref.shape[0], :] = zd_ref[...]
            z = za_pad_ref[...]
            zd = zd_pad_ref[...]
        else:
            z = za_ref[pl.ds(r0, C), :]
            zd = zd_ref[pl.ds(r0, C), :]
        q = z[:, 0:hk] * (dk ** -0.5)
        k = z[:, hk:2 * hk]
        v = z[:, 2 * hk:2 * hk + hv]
        r = z[:, 2 * hk + hv:2 * hk + 2 * hv]
        lr = zd[:, 0:GLA_GATE_RANK]
        la = _log_sigmoid(_dot_hi(lr, wgh_ref[...], wgl_ref[...]) + bg_ref[...]) * (1.0 / GLA_TAU)
        tglob = (j * nchunk + c) * C + lax.broadcasted_iota(jnp.int32, (C, 1), 0)
        la = jnp.where(tglob < t_valid, la, 0.0)
        b = _dot_exact_lhs(tril, la)
        st = st_ref[...]
        qe = q * jnp.exp(b)
        kp_ref[sub:sub + C, :] = k
        bp_ref[sub:sub + C, :] = b
        vp_ref[sub:sub + C, :] = v
        xs = []
        for d in range(sub):
            kd = kp_ref[sub - d:sub - d + C, :]
            bd = bp_ref[sub - d:sub - d + C, :]
            xs.append(jnp.where(tmod >= d, q * kd * jnp.exp(b - bd), 0.0).astype(BF16))
        rr = jnp.dot(jnp.concatenate(xs, axis=0), head_ones, preferred_element_type=F32)
        outs = []
        for h in range(GLA_HEADS):
            kh = slice(h * dk, (h + 1) * dk)
            vh = slice(h * dv, (h + 1) * dv)
            o = _dot_nt(qe[:, kh], st[:, kh])
            for d in range(sub):
                o = o + rr[d * C:(d + 1) * C, h:h + 1] * vp_ref[sub - d:sub - d + C, vh]
            offs = [jnp.zeros((sub, dv), F32)]
            for i in range(1, nsub):
                anchor = b[i * sub - 1:i * sub, kh]
                qt = q[i * sub:(i + 1) * sub, kh] * jnp.exp(b[i * sub:(i + 1) * sub, kh] - anchor)
                kt = k[0:i * sub, kh] * jnp.exp(anchor - b[0:i * sub, kh])
                offs.append(_dot(_dot_nt(qt, kt), v[0:i * sub, vh]))
            o = o + jnp.concatenate(offs, axis=0)
            on = o * lax.rsqrt(jnp.mean(o * o, axis=-1, keepdims=True) + EPS) * gout
            rh = r[:, vh]
            outs.append(on * (rh * _sigmoid(rh)))
        o_all = jnp.concatenate(outs, axis=1)
        if short:
            o_ref[...] = o_all[0:o_ref.shape[0], :].astype(o_ref.dtype)
        else:
            o_ref[pl.ds(r0, C), :] = o_all.astype(o_ref.dtype)
        b_last = b[C - 1:C, :]
        kk = k * jnp.exp(b_last - b)
        upd = jnp.concatenate([_dot_tn(v[:, h * dv:(h + 1) * dv], kk[:, h * dk:(h + 1) * dk])
                               for h in range(GLA_HEADS)], axis=1)
        st_ref[...] = st * jnp.exp(b_last) + upd
        return carry

    lax.fori_loop(0, nchunk, chunk, 0)

    @pl.when(j == nj - 1)
    def _():
        st = st_ref[...]
        for h in range(GLA_HEADS):
            s_ref[h] = st[:, h * dk:(h + 1) * dk].T


def _gla(za, zd, wgh, wgl, bg, gout, s0, batch, t_seq, ct, row0, out_dtype):
    nj = t_seq // ct
    blk0 = row0 // ct
    hv = GLA_HEADS * 128
    kern = functools.partial(_gla_kernel, nchunk=max(1, ct // GLA_CHUNK), t_valid=t_seq)
    full = lambda a: pl.BlockSpec(a.shape, lambda b, j: (0,) * a.ndim)
    return pl.pallas_call(
        kern,
        grid=(batch, nj),
        in_specs=[pl.BlockSpec((ct, za.shape[1]), lambda b, j: (blk0 + b * nj + j, 0)),
                  pl.BlockSpec((ct, LANES), lambda b, j: (blk0 + b * nj + j, 0)),
                  full(wgh), full(wgl), full(bg), full(gout),
                  pl.BlockSpec((None,) + s0.shape[1:], lambda b, j: (b, 0, 0, 0))],
        out_specs=[pl.BlockSpec((ct, hv), lambda b, j: (b * nj + j, 0)),
                   pl.BlockSpec((None,) + s0.shape[1:], lambda b, j: (b, 0, 0, 0))],
        out_shape=[jax.ShapeDtypeStruct((batch * t_seq, hv), out_dtype), jax.ShapeDtypeStruct(s0.shape, F32)],
        scratch_shapes=[pltpu.VMEM((128, GLA_HEADS * 64), F32),
                        pltpu.VMEM((GLA_SUB + GLA_CHUNK, GLA_HEADS * 64), F32),
                        pltpu.VMEM((GLA_SUB + GLA_CHUNK, GLA_HEADS * 64), F32),
                        pltpu.VMEM((GLA_SUB + GLA_CHUNK, hv), F32),
                        pltpu.VMEM((GLA_CHUNK, za.shape[1]), F32),
                        pltpu.VMEM((GLA_CHUNK, LANES), F32)],
        compiler_params=_cp(("parallel", "arbitrary")),
        name="gla",
    )(za, zd, wgh, wgl, bg, gout, s0)


def _mem_kv_kernel(m_ref, g_ref, w_ref, gk_ref, k_ref, v_ref):
    x = m_ref[...]
    u = x * lax.rsqrt(jnp.mean(x * x, axis=-1, keepdims=True) + EPS) * g_ref[...]
    kv = jnp.dot(u.astype(BF16), w_ref[...], preferred_element_type=F32)
    half = kv.shape[1] // 2
    k_ref[...] = _seg_rms(kv[:, 0:half], half // MEM_HEADS, gk_ref[...])
    v_ref[...] = kv[:, half:]


def _mem_kv(mem, g, w, gk, tm):
    n, d = mem.shape
    half = w.shape[1] // 2
    full = lambda a: pl.BlockSpec(a.shape, lambda i: (0, 0))
    sd = jax.ShapeDtypeStruct((n, half), F32)
    return pl.pallas_call(
        _mem_kv_kernel,
        grid=(n // tm,),
        in_specs=[pl.BlockSpec((tm, d), lambda i: (i, 0)), full(g), full(w), full(gk)],
        out_specs=[pl.BlockSpec((tm, half), lambda i: (i, 0))] * 2,
        out_shape=[sd, sd],
        compiler_params=_cp(("parallel",)),
        name="mem_kv",
    )(mem, g, w, gk)


def _mem_prompt_kernel(q_ref, k_ref, v_ref, o_ref):
    q = q_ref[...]
    k = k_ref[...].astype(BF16)
    v = v_ref[...].astype(BF16)
    hd = q.shape[1] // MEM_HEADS
    outs = []
    for h in range(MEM_HEADS):
        sl = slice(h * hd, (h + 1) * hd)
        s = _dot_nt(q[:, sl], k[:, sl])
        m = jnp.max(s, axis=-1, keepdims=True)
        e = jnp.exp(s - m)
        outs.append(_dot(e / jnp.sum(e, axis=-1, keepdims=True), v[:, sl]))
    o_ref[...] = jnp.concatenate(outs, axis=1).astype(BF16)


def _mem_prompt(qm, mk, mv, batch, seq, mlen, tq):
    nq = seq // tq
    w = qm.shape[1]
    return pl.pallas_call(
        _mem_prompt_kernel,
        grid=(batch, nq),
        in_specs=[pl.BlockSpec((tq, w), lambda b, i: (b * nq + i, 0)),
                  pl.BlockSpec((mlen, w), lambda b, i: (b, 0)),
                  pl.BlockSpec((mlen, w), lambda b, i: (b, 0))],
        out_specs=pl.BlockSpec((tq, w), lambda b, i: (b * nq + i, 0)),
        out_shape=jax.ShapeDtypeStruct((batch * seq, w), BF16),
        compiler_params=_cp(("parallel", "parallel")),
        name="mem_prompt",
    )(qm, mk, mv)


def _mem_sample_kernel(q_ref, k_ref, v_ref, o_ref, *, sb, t_new):
    w = q_ref.shape[1]
    hd = w // MEM_HEADS
    rows = MEM_HEADS * t_new
    lane_head = lax.broadcasted_iota(jnp.int32, (t_new, w), 1) // hd
    for s_i in range(sb):
        q = q_ref[s_i * t_new:(s_i + 1) * t_new, :].astype(F32)
        qbd = jnp.concatenate([jnp.where(lane_head == h, q, 0.0) for h in range(MEM_HEADS)], axis=0)
        k = k_ref[s_i].astype(BF16)
        v = v_ref[s_i].astype(BF16)
        s = _dot_nt(qbd, k)
        m = jnp.max(s, axis=-1, keepdims=True)
        e = jnp.exp(s - m)
        o = _dot(e / jnp.sum(e, axis=-1, keepdims=True), v)
        o_ref[s_i * t_new:(s_i + 1) * t_new, :] = jnp.concatenate(
            [o[h * t_new:(h + 1) * t_new, h * hd:(h + 1) * hd] for h in range(MEM_HEADS)], axis=1).astype(BF16)


def _mem_sample(qm, ck, cv, row0_blocks, db, t_new, sb):
    w = qm.shape[1]
    mlen = ck.shape[1]
    kern = functools.partial(_mem_sample_kernel, sb=sb, t_new=t_new)
    return pl.pallas_call(
        kern,
        grid=(db // sb,),
        in_specs=[pl.BlockSpec((sb * t_new, w), lambda i: (row0_blocks + i, 0)),
                  pl.BlockSpec((sb, mlen, w), lambda i: (i, 0, 0)),
                  pl.BlockSpec((sb, mlen, w), lambda i: (i, 0, 0))],
        out_specs=pl.BlockSpec((sb * t_new, w), lambda i: (i, 0)),
        out_shape=jax.ShapeDtypeStruct((db * t_new, w), BF16),
        compiler_params=_cp(("parallel",)),
        name="mem_sample",
    )(qm, ck, cv)


def _merge_kernel(x_ref, og_ref, on_ref, om_ref, gate_ref, wg_ref, wn_ref, wm_ref, wo_ref, gf_ref,
                  wrh_ref, wrl_ref, br_ref, h_ref, hn_ref, cmb_ref, *, n_experts, n_tiles):
    @pl.when(pl.program_id(0) >= n_tiles)
    def _():
        h_ref[...] = jnp.zeros(h_ref.shape, h_ref.dtype)
        hn_ref[...] = jnp.zeros(hn_ref.shape, hn_ref.dtype)
        cmb_ref[...] = jnp.zeros(cmb_ref.shape, cmb_ref.dtype)

    @pl.when(pl.program_id(0) < n_tiles)
    def _():
        _merge_body(x_ref, og_ref, on_ref, om_ref, gate_ref, wg_ref, wn_ref, wm_ref, wo_ref, gf_ref,
                    wrh_ref, wrl_ref, br_ref, h_ref, hn_ref, cmb_ref, n_experts)


def _merge_body(x_ref, og_ref, on_ref, om_ref, gate_ref, wg_ref, wn_ref, wm_ref, wo_ref, gf_ref,
                wrh_ref, wrl_ref, br_ref, h_ref, hn_ref, cmb_ref, n_experts):
    d = x_ref.shape[1]
    gate = gate_ref[...].astype(F32)
    y = (gate[:, 0:d] * jnp.dot(og_ref[...], wg_ref[...], preferred_element_type=F32)
         + gate[:, d:2 * d] * jnp.dot(on_ref[...], wn_ref[...], preferred_element_type=F32)
         + gate[:, 2 * d:3 * d] * jnp.dot(om_ref[...], wm_ref[...], preferred_element_type=F32))
    h = x_ref[...] + jnp.dot(y.astype(BF16), wo_ref[...], preferred_element_type=F32)
    h_ref[...] = h
    hn = h * lax.rsqrt(jnp.mean(h * h, axis=-1, keepdims=True) + EPS) * gf_ref[...]
    hn_ref[...] = hn.astype(BF16)
    logits = _dot_hi(hn, wrh_ref[...], wrl_ref[...]) + br_ref[...]
    lane = lax.broadcasted_iota(jnp.int32, logits.shape, 1).astype(F32)
    work = jnp.where(lane < n_experts, logits, NEG)
    top = jnp.max(work, axis=-1, keepdims=True)
    chosen = jnp.zeros(logits.shape, F32)
    for _ in range(TOP_K):
        mx = jnp.max(work, axis=-1, keepdims=True)
        first = jnp.min(jnp.where(work == mx, lane, float(LANES)), axis=-1, keepdims=True)
        pick = lane == first
        chosen = jnp.where(pick, 1.0, chosen)
        work = jnp.where(pick, NEG, work)
    e = jnp.where(chosen > 0.5, jnp.exp(logits - top), 0.0)
    cmb_ref[...] = e / jnp.sum(e, axis=-1, keepdims=True)


def _merge(x, og, on, om, gate, wg, wn, wm, wo, gf, wrh, wrl, br, n_experts, tm, n_pad):
    n, d = x.shape
    n_tiles = n // tm
    row = lambda a: pl.BlockSpec((tm, a.shape[1]), lambda i: (jnp.minimum(i, n_tiles - 1), 0))
    full = lambda a: pl.BlockSpec(a.shape, lambda i: (0, 0))
    kern = functools.partial(_merge_kernel, n_experts=n_experts, n_tiles=n_tiles)
    return pl.pallas_call(
        kern,
        grid=(n_pad // tm,),
        in_specs=[row(x), row(og), row(on), row(om), row(gate), full(wg), full(wn), full(wm), full(wo), full(gf),
                  full(wrh), full(wrl), full(br)],
        out_specs=[pl.BlockSpec((tm, d), lambda i: (i, 0)), pl.BlockSpec((tm, d), lambda i: (i, 0)),
                   pl.BlockSpec((tm, LANES), lambda i: (i, 0))],
        out_shape=[jax.ShapeDtypeStruct((n_pad, d), F32), jax.ShapeDtypeStruct((n_pad, d), BF16),
                   jax.ShapeDtypeStruct((n_pad, LANES), F32)],
        compiler_params=_cp(("parallel",)),
        name="merge_router",
    )(x, og, on, om, gate, wg, wn, wm, wo, gf, wrh, wrl, br)


MOE_SUBTILE = 1024
MOE_SUBTILES = 2
MOE_ROW_BLOCK = 192


def _moe_kernel(hn_ref, h_ref, cmb_ref, wgu_ref, bgu_ref, wd_ref, bd_ref, y_ref,
                rank_ref, rankt_ref, cmbt_ref, cnt_ref, *, ts):
    e = pl.program_id(1)
    ns = hn_ref.shape[0] // ts
    rb = MOE_ROW_BLOCK

    @pl.when(e == 0)
    def _():
        y_ref[...] = h_ref[...]
        lower = jnp.where(lax.broadcasted_iota(jnp.int32, (ts, ts), 0) > lax.broadcasted_iota(jnp.int32, (ts, ts), 1),
                          1.0, 0.0).astype(BF16)
        for s in range(ns):
            cmb = cmb_ref[s * ts:(s + 1) * ts, :]
            sel = jnp.where(cmb > 0.0, 1.0, 0.0)
            rank = jnp.where(sel > 0.0, jnp.dot(lower, sel.astype(BF16), preferred_element_type=F32), -1.0)
            rank_ref[s] = rank
            rankt_ref[s] = rank.T
            cmbt_ref[s] = cmb.T
            cnt_ref[s] = jnp.broadcast_to(jnp.sum(sel, axis=0, keepdims=True), cnt_ref.shape[1:])

    lane = lax.broadcasted_iota(jnp.int32, (1, LANES), 1)
    pick = jnp.where(lax.broadcasted_iota(jnp.int32, (LANES, LANES), 0) == e, 1.0, 0.0).astype(BF16)
    n_iter = 0
    rank_cols = []
    for s in range(ns):
        cnt = jnp.sum(jnp.where(lane == e, cnt_ref[s, 0:1, :], 0.0)).astype(jnp.int32)
        n_iter = jnp.maximum(n_iter, (cnt + rb - 1) // rb)
        r = rank_ref[s]
        r_hi = r.astype(BF16)
        r_lo = (r - r_hi.astype(F32)).astype(BF16)
        rc = (jnp.dot(r_hi, pick, preferred_element_type=F32)
              + jnp.dot(r_lo, pick, preferred_element_type=F32))
        rank_cols.append(jnp.concatenate([rc] * (-(-rb // LANES)), axis=1)[:, 0:rb])
    dff2 = wgu_ref.shape[1]

    def body(i, carry):
        r0 = i * rb
        rows = (r0 + lax.broadcasted_iota(jnp.int32, (rb, ts), 0)).astype(F32)
        xs, ws = [], []
        for s in range(ns):
            hit = rankt_ref[s, pl.ds(e, 1), :] == rows
            xs.append(jnp.dot(jnp.where(hit, 1.0, 0.0).astype(BF16), hn_ref[s * ts:(s + 1) * ts, :],
                              preferred_element_type=F32).astype(BF16))
            ws.append(jnp.sum(jnp.where(hit, cmbt_ref[s, pl.ds(e, 1), :], 0.0), axis=-1, keepdims=True))
        x = jnp.concatenate(xs, axis=0)
        gu = jnp.dot(x, wgu_ref[...], preferred_element_type=F32) + bgu_ref[...]
        gate = jnp.minimum(gu, SWIGLU_LIMIT)
        up = pltpu.roll(jnp.clip(gu, -SWIGLU_LIMIT, SWIGLU_LIMIT), dff2 - 1, 1)
        act = (up + 1.0) * gate * _sigmoid(SWIGLU_ALPHA * gate)
        out = jnp.dot(act.astype(BF16), wd_ref[...], preferred_element_type=F32) + bd_ref[...]
        cols = (r0 + lax.broadcasted_iota(jnp.int32, (ts, rb), 1)).astype(F32)
        for s in range(ns):
            yw = (out[s * rb:(s + 1) * rb, :] * ws[s]).astype(BF16)
            pt = jnp.where(rank_cols[s] == cols, 1.0, 0.0).astype(BF16)
            y_ref[s * ts:(s + 1) * ts, :] += jnp.dot(pt, yw, preferred_element_type=F32)
        return carry

    lax.fori_loop(0, n_iter, body, 0)


def _moe(hn, h, cmb, wgu, bgu, wd, bd, ts, ns):
    n, d = h.shape
    ne, _, dff2 = wgu.shape
    tt = ts * ns
    once = lambda w: pl.BlockSpec((tt, w), lambda i, e: (i, 0), pipeline_mode=pl.Buffered(1))
    return pl.pallas_call(
        functools.partial(_moe_kernel, ts=ts),
        grid=(n // tt, ne),
        in_specs=[once(d), once(d), once(LANES),
                  pl.BlockSpec((None, d, dff2), lambda i, e: (e, 0, 0)),
                  pl.BlockSpec((None, 1, dff2), lambda i, e: (e, 0, 0)),
                  pl.BlockSpec((None, dff2, d), lambda i, e: (e, 0, 0)),
                  pl.BlockSpec((None, 1, d), lambda i, e: (e, 0, 0))],
        out_specs=pl.BlockSpec((tt, d), lambda i, e: (i, 0)),
        out_shape=jax.ShapeDtypeStruct((n, d), F32),
        scratch_shapes=[pltpu.VMEM((ns, ts, LANES), F32), pltpu.VMEM((ns, LANES, ts), F32),
                        pltpu.VMEM((ns, LANES, ts), F32), pltpu.VMEM((ns, 8, LANES), F32)],
        compiler_params=_cp(("parallel", "arbitrary")),
        name="moe",
    )(hn, h, cmb, wgu, bgu, wd, bd)


def _rope_table(pos):
    half = 8
    inv = 1.0 / (ROPE_THETA ** (jnp.arange(half, dtype=F32) / half))
    ang = pos.astype(F32)[:, None] * inv[None, :]
    cos, sin = jnp.cos(ang), jnp.sin(ang)
    n = pos.shape[0]
    one = jnp.ones((n, 64 - 2 * half), F32)
    zero8 = jnp.zeros((n, half), F32)
    zero = jnp.zeros((n, 64 - 2 * half), F32)
    c = jnp.concatenate([cos, cos, one], axis=1)
    s1 = jnp.concatenate([-sin, zero8, zero], axis=1)
    s2 = jnp.concatenate([zero8, sin, zero], axis=1)
    return jnp.concatenate([c, c, s1, s1, s2, s2], axis=1)


def _hi_lo(w):
    hi = w.astype(BF16)
    return hi, (w - hi.astype(F32)).astype(BF16)


def kernel(x_prompt, x_sample, cache_cmp_k, cache_cmp_v, cache_sel_k, cache_sel_v, cache_win_k, cache_win_v, state_gla, cache_mem_k, cache_mem_v, page_table, mem_prompt, g_attn, w_in, w_gla_gate, b_gla_gate, g_gla_out, g_q_nsa, g_k_nsa, w_cmp_pos, g_q_mem, g_k_mem, g_mem, w_mem_kv, w_up_gla, w_up_nsa, w_up_mem, w_out, g_ffn, w_router, b_router, w_gate_up, b_gate_up, w_down, b_down):
    B, S, D = x_prompt.shape
    DB, T, _ = x_sample.shape
    n_pool, page = cache_cmp_k.shape[1:3]
    n_pages = page_table.shape[1]
    past_len = n_pages * page
    wb = cache_win_k.shape[2]
    mlen = mem_prompt.shape[1]
    ne = w_router.shape[2]
    Np, Ns = B * S, DB * T
    N = Np + Ns
    TM = 256
    assert cache_cmp_k.shape[0] == 1 and D == 1024 and S % TM == 0 and Ns % TM == 0 and TM % T == 0
    assert T <= NSA_BLOCK and past_len % NSA_BLOCK == 0 and S % NSA_BLOCK == 0

    gq, gk, gv, gr, glr, nq, nkv, ng, mq, mg = np.cumsum(
        [0, 256, 256, 512, 512, GLA_GATE_RANK, 512, 768, 3 * NSA_HEADS, 512]).tolist()
    w = w_in[0]
    wa = w[:, gq:glr].astype(BF16)
    wbm = jnp.concatenate([w[:, nq:ng], w[:, mq:mg]], axis=1).astype(BF16)
    wc = w[:, mg:].astype(BF16)
    wd = jnp.concatenate([w[:, glr:nq], w[:, ng:mq], jnp.zeros((D, LANES - GLA_GATE_RANK - 3 * NSA_HEADS), F32)],
                         axis=1).astype(BF16)

    x_all = jnp.concatenate([x_prompt.reshape(Np, D), x_sample.reshape(Ns, D)], axis=0)
    za, zb, zc, zd = _project(x_all, g_attn, wa, wbm, wc, wd, TM)

    pos_sample = past_len + jnp.arange(T, dtype=jnp.int32)
    tab = jnp.concatenate([_rope_table(jnp.arange(S, dtype=jnp.int32)),
                           jnp.tile(_rope_table(pos_sample), (TM // T, 1))], axis=0)
    gq_t = jnp.tile(g_q_nsa[0], NSA_HEADS)[None, :]
    gk_t = jnp.tile(g_k_nsa[0], (1, NSA_KV_HEADS))
    gm_t = jnp.tile(g_q_mem[0], MEM_HEADS)[None, :]
    q, kc, vc, ks, vs, kw, vw, qm, qb, kvb = _prep(zb, tab, gq_t, gk_t, gm_t, TM, Np // TM, S // TM)

    w2 = jnp.tile(w_cmp_pos[0], (1, 1, NSA_KV_HEADS))
    ck, cv = _compress(kc, vc, w2, Np, 512)
    o_nsa_p = _nsa_prompt(qb, ck, cv, kvb, zd, B, S, 128)

    rows_minor = lambda a: jnp.transpose(a[0], (0, 2, 3, 1)).reshape(a.shape[1], LANES, a.shape[2])
    pool = rows_minor
    npg = 16 if n_pages % 16 == 0 else n_pages
    wt = jnp.tile(jnp.transpose(w2, (0, 2, 1)), (1, 1, page // NSA_BLOCK))
    ocmp_s, sel_s = _nsa_sample_cmp(page_table, pool(cache_cmp_k), pool(cache_cmp_v), q, kc, vc, wt, w2, T, npg)
    win_k = rows_minor(cache_win_k)
    win_v = rows_minor(cache_win_v)
    o_nsa_s = _nsa_sample_sel(page_table, pool(cache_sel_k), pool(cache_sel_v), q, sel_s, ks, vs, win_k, win_v,
                              kw, vw, ocmp_s, zd, T, npg)
    o_nsa = jnp.concatenate([o_nsa_p, o_nsa_s.astype(BF16)], axis=0)

    wgh, wgl = _hi_lo(w_gla_gate[0])
    bg = b_gla_gate[0][None, :]
    gout = g_gla_out[0][None, :]
    s0_p = jnp.zeros((B,) + state_gla.shape[2:], F32)
    o_gla_p, s_gla_p = _gla(za, zd, wgh, wgl, bg, gout, s0_p, B, S, 512, 0, BF16)
    o_gla_s, s_gla_s = _gla(za, zd, wgh, wgl, bg, gout, state_gla[0], DB, T, T, Np, F32)
    o_gla = jnp.concatenate([o_gla_p, o_gla_s.astype(BF16)], axis=0)

    gkm = jnp.tile(g_k_mem[0], MEM_HEADS)[None, :]
    mem_k, mem_v = _mem_kv(mem_prompt.reshape(B * mlen, D), g_mem, w_mem_kv[0].astype(BF16), gkm, TM)
    o_mem_p = _mem_prompt(qm, mem_k, mem_v, B, S, mlen, 512)
    mw = MEM_HEADS * cache_mem_k.shape[-1]
    sb = 8
    o_mem_s = _mem_sample(qm, cache_mem_k[0].reshape(DB, mlen, mw), cache_mem_v[0].reshape(DB, mlen, mw),
                          Np // (sb * T), DB, T, sb)
    o_mem = jnp.concatenate([o_mem_p, o_mem_s], axis=0)

    moe_ts = MOE_SUBTILE if N % MOE_SUBTILE == 0 else TM
    n_pad = -(-N // (moe_ts * MOE_SUBTILES)) * (moe_ts * MOE_SUBTILES)
    wr = jnp.pad(w_router[0], ((0, 0), (0, LANES - ne)))
    wrh, wrl = _hi_lo(wr)
    br = jnp.pad(b_router[0], (0, LANES - ne))[None, :]
    h, hn, cmb = _merge(x_all, o_gla, o_nsa, o_mem, zc, w_up_gla[0].astype(BF16), w_up_nsa[0].astype(BF16),
                        w_up_mem[0].astype(BF16), w_out[0].astype(BF16), g_ffn, wrh, wrl, br, ne, TM, n_pad)

    wd_b = w_down[0].astype(BF16)
    wd_exp = jnp.stack([wd_b, jnp.zeros_like(wd_b)], axis=2).reshape(ne, -1, D)
    y = _moe(hn, h, cmb, w_gate_up[0].astype(BF16), b_gate_up[0][:, None, :], wd_exp, b_down[0][:, None, :],
             moe_ts, MOE_SUBTILES)

    kvshape = (1, B, S, NSA_KV_HEADS, NSA_BLOCK)
    p_rows = lambda a: a[:Np].reshape(kvshape)
    s_rows = lambda a: a[Np:].reshape(1, DB, T, NSA_KV_HEADS, NSA_BLOCK)
    wbp = min(NSA_WINDOW, S)
    s_win = lambda cache, new: jnp.concatenate(
        [cache[:, :, T:], new[Np:].reshape(1, DB, T, NSA_KV_HEADS, NSA_BLOCK)], axis=2)
    mshape = (1, B, mlen, MEM_HEADS, mw // MEM_HEADS)
    return (y[:Np].reshape(B, S, D), y[Np:N].reshape(DB, T, D),
            p_rows(kc), p_rows(vc), p_rows(ks), p_rows(vs),
            p_rows(kw)[:, :, S - wbp:], p_rows(vw)[:, :, S - wbp:],
            s_gla_p[None], mem_k.reshape(mshape), mem_v.reshape(mshape),
            s_rows(kc), s_rows(vc), s_rows(ks), s_rows(vs),
            s_win(cache_win_k, kw), s_win(cache_win_v, vw), s_gla_s[None])
```

```python
import functools

import numpy as np
import jax
import jax.numpy as jnp
from jax import lax
from jax.experimental import pallas as pl
from jax.experimental.pallas import tpu as pltpu

F32 = jnp.float32
BF16 = jnp.bfloat16

GLA_HEADS = 4
GLA_GATE_RANK = 16
GLA_TAU = 16.0
GLA_CHUNK = 64
GLA_SUB = 16
NSA_HEADS = 8
NSA_KV_HEADS = 2
NSA_GROUP = NSA_HEADS // NSA_KV_HEADS
NSA_BLOCK = 64
NSA_TOPN = 16
NSA_WINDOW = 512
MEM_HEADS = 4
ROPE_THETA = 500000.0
N_BRANCH = 3
TOP_K = 4
SWIGLU_LIMIT = 7.0
SWIGLU_ALPHA = 1.702
EPS = 1e-6

LANES = 128
NEG = -1e30
VMEM_LIMIT = 56 * 1024 * 1024


def _cp(sem):
    return pltpu.CompilerParams(dimension_semantics=sem, vmem_limit_bytes=VMEM_LIMIT)


def _dot(a, b):
    return jnp.dot(a.astype(BF16), b.astype(BF16), preferred_element_type=F32)


def _dot_nt(a, b):
    return lax.dot_general(a.astype(BF16), b.astype(BF16), (((1,), (1,)), ((), ())),
                           preferred_element_type=F32)


def _dot_tn(a, b):
    return lax.dot_general(a.astype(BF16), b.astype(BF16), (((0,), (0,)), ((), ())),
                           preferred_element_type=F32)


def _split3(x):
    hi = x.astype(BF16)
    r = x - hi.astype(F32)
    mid = r.astype(BF16)
    lo = (r - mid.astype(F32)).astype(BF16)
    return hi, mid, lo


def _dot_exact_rhs(x, m):
    hi, mid, lo = _split3(x)
    d = functools.partial(jnp.dot, preferred_element_type=F32)
    return d(hi, m) + d(mid, m) + d(lo, m)


def _dot_exact_lhs(m, x):
    hi, mid, lo = _split3(x)
    d = functools.partial(jnp.dot, preferred_element_type=F32)
    return d(m, hi) + d(m, mid) + d(m, lo)


def _dot_hi(x, w_hi, w_lo):
    hi, mid, _ = _split3(x)
    d = functools.partial(jnp.dot, preferred_element_type=F32)
    return d(hi, w_hi) + d(mid, w_hi) + d(hi, w_lo)


def _msoftmax(s, mask):
    s = jnp.where(mask, s, NEG)
    m = jnp.max(s, axis=-1, keepdims=True)
    e = jnp.where(mask, jnp.exp(s - m), 0.0)
    return e / jnp.maximum(jnp.sum(e, axis=-1, keepdims=True), 1e-30)


def _seg_ones(width, seg):
    r = lax.broadcasted_iota(jnp.int32, (width, width), 0) // seg
    c = lax.broadcasted_iota(jnp.int32, (width, width), 1) // seg
    return jnp.where(r == c, 1.0, 0.0).astype(BF16)


def _seg_rms(x, seg, gain):
    ms = _dot_exact_rhs(x * x, _seg_ones(x.shape[-1], seg)) * (1.0 / seg)
    return x * lax.rsqrt(ms + EPS) * gain


def _sigmoid(x):
    return 1.0 / (1.0 + jnp.exp(-x))


def _proj_kernel(x_ref, g_ref, wa_ref, wb_ref, wc_ref, wd_ref, za_ref, zb_ref, zc_ref, zd_ref):
    x = x_ref[...]
    u = x * lax.rsqrt(jnp.mean(x * x, axis=-1, keepdims=True) + EPS) * g_ref[...]
    ub = u.astype(BF16)
    za_ref[...] = jnp.dot(ub, wa_ref[...], preferred_element_type=F32)
    zb_ref[...] = jnp.dot(ub, wb_ref[...], preferred_element_type=F32)
    zc_ref[...] = _sigmoid(jnp.dot(ub, wc_ref[...], preferred_element_type=F32)).astype(BF16)
    zd_ref[...] = jnp.dot(ub, wd_ref[...], preferred_element_type=F32)


def _project(x, g, wa, wb, wc, wd, tm):
    n, d = x.shape
    full = lambda w: pl.BlockSpec(w.shape, lambda i: (0, 0))
    row = lambda w: pl.BlockSpec((tm, w), lambda i: (i, 0))
    return pl.pallas_call(
        _proj_kernel,
        grid=(n // tm,),
        in_specs=[row(d), full(g), full(wa), full(wb), full(wc), full(wd)],
        out_specs=[row(wa.shape[1]), row(wb.shape[1]), row(wc.shape[1]), row(wd.shape[1])],
        out_shape=[jax.ShapeDtypeStruct((n, wa.shape[1]), F32), jax.ShapeDtypeStruct((n, wb.shape[1]), F32),
                   jax.ShapeDtypeStruct((n, wc.shape[1]), BF16), jax.ShapeDtypeStruct((n, wd.shape[1]), F32)],
        compiler_params=_cp(("parallel",)),
        name="proj",
    )(x, g, wa, wb, wc, wd)


def _rope(x, tab):
    c, s1, s2 = tab[:, 0:LANES], tab[:, LANES:2 * LANES], tab[:, 2 * LANES:3 * LANES]
    half = 8
    return x * c + pltpu.roll(x, LANES - half, 1) * s1 + pltpu.roll(x, half, 1) * s2


def _prep_kernel(zb_ref, tab_ref, gq_ref, gk_ref, gm_ref,
                 q_ref, kc_ref, vc_ref, ks_ref, vs_ref, kw_ref, vw_ref, qm_ref, qb_ref, kvb_ref):
    z = zb_ref[...]
    tab = tab_ref[...]
    hd = NSA_BLOCK
    nq = NSA_HEADS * hd
    qn = _seg_rms(z[:, 0:nq], hd, gq_ref[...])
    q = jnp.concatenate([_rope(qn[:, j * LANES:(j + 1) * LANES], tab) for j in range(nq // LANES)], axis=1)
    q = q * (hd ** -0.5)
    q_ref[...] = q
    qb_ref[...] = q.astype(BF16)
    kv = z[:, nq:nq + 6 * LANES]
    gk = gk_ref[...]
    ks = []
    for j, ref in ((0, kc_ref), (1, ks_ref), (2, kw_ref)):
        kj = _rope(_seg_rms(kv[:, 2 * j * LANES:(2 * j + 1) * LANES], hd, gk[j:j + 1, :]), tab)
        ref[...] = kj
        ks.append(kj)
    vs = []
    for j, ref in ((0, vc_ref), (1, vs_ref), (2, vw_ref)):
        vj = kv[:, (2 * j + 1) * LANES:(2 * j + 2) * LANES]
        ref[...] = vj
        vs.append(vj)
    kvb_ref[...] = jnp.concatenate([ks[1], vs[1], ks[2], vs[2]], axis=1).astype(BF16)
    zm = z[:, nq + 6 * LANES:]
    mhd = zm.shape[1] // MEM_HEADS
    qm_ref[...] = (_seg_rms(zm, mhd, gm_ref[...]) * (mhd ** -0.5)).astype(BF16)


def _prep(zb, tab, gq, gk, gm, tm, n_prompt_tiles, tab_tiles):
    n = zb.shape[0]
    row = lambda w: pl.BlockSpec((tm, w), lambda i: (i, 0))
    full = lambda a: pl.BlockSpec(a.shape, lambda i: (0, 0))
    tab_spec = pl.BlockSpec((tm, 3 * LANES), lambda i: (jnp.where(i < n_prompt_tiles, i % tab_tiles, tab_tiles), 0))
    sd = lambda w, dt=F32: jax.ShapeDtypeStruct((n, w), dt)
    return pl.pallas_call(
        _prep_kernel,
        grid=(n // tm,),
        in_specs=[row(zb.shape[1]), tab_spec, full(gq), full(gk), full(gm)],
        out_specs=[row(512)] + [row(LANES)] * 6 + [row(512), row(512), row(512)],
        out_shape=[sd(512)] + [sd(LANES)] * 6 + [sd(512, BF16), sd(512, BF16), sd(512, BF16)],
        compiler_params=_cp(("parallel",)),
        name="nsa_prep",
    )(zb, tab, gq, gk, gm)


def _pool_rows(x, w):
    nblk = x.shape[0] // NSA_BLOCK
    return jnp.sum(x.reshape(nblk, NSA_BLOCK, LANES) * w[None], axis=1)


def _compress_kernel(k_ref, v_ref, w_ref, ck_ref, cv_ref):
    ck_ref[...] = _pool_rows(k_ref[...], w_ref[0])
    cv_ref[...] = _pool_rows(v_ref[...], w_ref[1])


def _compress(kc, vc, w2, n_rows, tm):
    row = pl.BlockSpec((tm, LANES), lambda i: (i, 0))
    out = pl.BlockSpec((tm // NSA_BLOCK, LANES), lambda i: (i, 0))
    sd = jax.ShapeDtypeStruct((n_rows // NSA_BLOCK, LANES), F32)
    return pl.pallas_call(
        _compress_kernel,
        grid=(n_rows // tm,),
        in_specs=[row, row, pl.BlockSpec(w2.shape, lambda i: (0, 0, 0))],
        out_specs=[out, out],
        out_shape=[sd, sd],
        compiler_params=_cp(("parallel",)),
        name="compress",
    )(kc, vc, w2)


def _select(imp, cur, n_cand):
    n_iota = lax.broadcasted_iota(jnp.int32, imp.shape, 1)
    cnt = jnp.zeros(imp.shape, F32)
    for m in range(n_cand):
        col = imp[:, m:m + 1]
        beats = (col > imp) | ((col == imp) & (n_iota > m))
        cnt = cnt + jnp.where(beats & (cur > m), 1.0, 0.0)
    keep = (n_iota == cur) | ((n_iota < cur) & (cnt < NSA_TOPN - 1))
    return jnp.where(keep, 1.0, 0.0)


def _expand_blocks(nb, nkeys, first_key=0):
    kb = (lax.broadcasted_iota(jnp.int32, (nb, nkeys), 1) + first_key) // NSA_BLOCK
    return jnp.where(kb == lax.broadcasted_iota(jnp.int32, (nb, nkeys), 0), 1.0, 0.0).astype(BF16)


def _gate_col(gate, head, j):
    c = GLA_GATE_RANK + head * 3 + j
    return _sigmoid(gate[:, c:c + 1])


def _attend(qh, k, bias, v_ones, hd):
    s = _dot_nt(qh, k) + bias
    e = jnp.exp((s - jnp.max(s, axis=-1, keepdims=True)).astype(BF16))
    pv = jnp.dot(e, v_ones, preferred_element_type=F32)
    return pv[:, 0:hd] / pv[:, hd:hd + 1]


def _nsa_prompt_kernel(q_ref, ck_ref, cv_ref, kv_ref, gate_ref, o_ref, osel_ref, *, tq, seq, wslab, nbr):
    i = pl.program_id(1)
    nq = pl.num_programs(1)
    t0 = i * tq
    hd = NSA_BLOCK
    nb = seq // NSA_BLOCK
    q = q_ref[...]
    gate = gate_ref[...]
    pos = t0 + lax.broadcasted_iota(jnp.int32, (tq, 1), 0)
    cur = pos // NSA_BLOCK
    blk = lax.broadcasted_iota(jnp.int32, (tq, nb), 1)
    vis = (blk + 1) * NSA_BLOCK <= pos + 1
    wstart = pl.multiple_of(jnp.maximum(t0 + tq - wslab, 0), LANES)
    wpos = wstart + lax.broadcasted_iota(jnp.int32, (tq, wslab), 1)
    wrel = pos - wpos
    wbias = jnp.where((wrel >= 0) & (wrel < NSA_WINDOW), 0.0, NEG)
    ck = ck_ref[...]
    cv = cv_ref[...]
    outs = []
    for g in range(NSA_KV_HEADS):
        ckg = ck[:, g * hd:(g + 1) * hd]
        cvg = cv[:, g * hd:(g + 1) * hd]
        o_cmp = []
        imp = jnp.zeros((tq, nb), F32)
        for h in range(NSA_GROUP):
            qh = q[:, (g * NSA_GROUP + h) * hd:(g * NSA_GROUP + h + 1) * hd]
            pr = _msoftmax(_dot_nt(qh, ckg), vis)
            imp = imp + pr
            o_cmp.append(_dot(pr, cvg))
        sel = _select(imp, cur, nb).astype(BF16)
        for j in range(nbr):
            klen = (j + 1) * seq // nbr

            @pl.when(i * nbr // nq == j)
            def _(g=g, klen=klen, sel=sel):
                kpos = lax.broadcasted_iota(jnp.int32, (tq, klen), 1)
                picked = jnp.dot(sel, _expand_blocks(nb, klen), preferred_element_type=F32) > 0.5
                sbias = jnp.where(picked & (kpos <= pos), 0.0, NEG)
                ksg = kv_ref[0:klen, g * hd:(g + 1) * hd]
                vsg = kv_ref[0:klen, LANES + g * hd:LANES + (g + 1) * hd]
                vs1 = jnp.concatenate([vsg, jnp.ones_like(vsg)], axis=1)
                for h in range(NSA_GROUP):
                    hl = slice((g * NSA_GROUP + h) * hd, (g * NSA_GROUP + h + 1) * hd)
                    osel_ref[:, hl] = _attend(q[:, hl], ksg, sbias, vs1, hd)

        kwg = kv_ref[pl.ds(wstart, wslab), 2 * LANES + g * hd:2 * LANES + (g + 1) * hd]
        vwg = kv_ref[pl.ds(wstart, wslab), 3 * LANES + g * hd:3 * LANES + (g + 1) * hd]
        vw1 = jnp.concatenate([vwg, jnp.ones_like(vwg)], axis=1)
        for h in range(NSA_GROUP):
            head = g * NSA_GROUP + h
            hl = slice(head * hd, (head + 1) * hd)
            o_win = _attend(q[:, hl], kwg, wbias, vw1, hd)
            outs.append(_gate_col(gate, head, 0) * o_cmp[h] + _gate_col(gate, head, 1) * osel_ref[:, hl]
                        + _gate_col(gate, head, 2) * o_win)
    o_ref[...] = jnp.concatenate(outs, axis=1).astype(BF16)


def _nsa_prompt(qb, ck, cv, kvb, zd, batch, seq, tq):
    nq = seq // tq
    nb = seq // NSA_BLOCK
    wslab = min(seq, NSA_WINDOW + tq)
    nbr = 4 if nq % 4 == 0 else 1
    kern = functools.partial(_nsa_prompt_kernel, tq=tq, seq=seq, wslab=wslab, nbr=nbr)
    return pl.pallas_call(
        kern,
        grid=(batch, nq),
        in_specs=[pl.BlockSpec((tq, 512), lambda b, i: (b * nq + i, 0)),
                  pl.BlockSpec((nb, LANES), lambda b, i: (b, 0)),
                  pl.BlockSpec((nb, LANES), lambda b, i: (b, 0)),
                  pl.BlockSpec((seq, 512), lambda b, i: (b, 0)),
                  pl.BlockSpec((tq, LANES), lambda b, i: (b * nq + i, 0))],
        out_specs=pl.BlockSpec((tq, 512), lambda b, i: (b * nq + i, 0)),
        out_shape=jax.ShapeDtypeStruct((batch * seq, 512), BF16),
        scratch_shapes=[pltpu.VMEM((tq, 512), F32)],
        compiler_params=_cp(("parallel", "parallel")),
        name="nsa_prompt",
    )(qb, ck, cv, kvb, zd)


def _qbd(q, rows_per_head):
    hd = NSA_BLOCK
    lane_grp = lax.broadcasted_iota(jnp.int32, (rows_per_head, LANES), 1) // hd
    parts = []
    for head in range(NSA_HEADS):
        g = head // NSA_GROUP
        qh = q[:, head * hd:(head + 1) * hd]
        two = jnp.concatenate([qh, qh], axis=1)
        parts.append(jnp.where(lane_grp == g, two, 0.0))
    return jnp.concatenate(parts, axis=0)


def _nsa_sample_cmp_kernel(pt_ref, *refs, npg, t_new, past_len):
    kpages = refs[0:npg]
    vpages = refs[npg:2 * npg]
    q_ref, kn_ref, vn_ref, wt_ref, wn_ref, ocmp_ref, sel_ref, pk_ref, pv_ref = refs[2 * npg:]
    c = pl.program_id(1)
    nch = pl.num_programs(1)
    hd = NSA_BLOCK
    page = kpages[0].shape[1]
    bpp = page // NSA_BLOCK
    nbc = npg * bpp
    nb_pad = pk_ref.shape[1]
    n_past = past_len // NSA_BLOCK

    @pl.when(c == 0)
    def _():
        pk_ref[...] = jnp.zeros(pk_ref.shape, F32)
        pv_ref[...] = jnp.zeros(pv_ref.shape, F32)

    seg = jnp.where(lax.broadcasted_iota(jnp.int32, (npg * page, nbc), 0) // NSA_BLOCK
                    == lax.broadcasted_iota(jnp.int32, (npg * page, nbc), 1), 1.0, 0.0).astype(BF16)
    col = lax.broadcasted_iota(jnp.int32, (nbc, nb_pad), 1)
    place = jnp.where(col == c * nbc + lax.broadcasted_iota(jnp.int32, (nbc, nb_pad), 0), 1.0, 0.0).astype(BF16)
    for pages, w_t, pooled in ((kpages, wt_ref[0], pk_ref), (vpages, wt_ref[1], pv_ref)):
        x = jnp.concatenate([r[...] * w_t for r in pages], axis=1)
        pooled[...] += _dot_exact_rhs(_dot_exact_rhs(x, seg), place)

    @pl.when(c == nch - 1)
    def _():
        newk = jnp.sum(kn_ref[...] * wn_ref[0][0:t_new], axis=0, keepdims=True)
        newv = jnp.sum(vn_ref[...] * wn_ref[1][0:t_new], axis=0, keepdims=True)
        q = q_ref[...]
        rows = NSA_HEADS * t_new
        qbd = _qbd(q, t_new)
        trow = lax.broadcasted_iota(jnp.int32, (rows, 1), 0) % t_new
        pos = past_len + trow
        blk = lax.broadcasted_iota(jnp.int32, (rows, nb_pad), 1)
        vis = ((blk + 1) * NSA_BLOCK <= pos + 1) & (blk < n_past)
        vis_new = (n_past + 1) * NSA_BLOCK <= pos + 1
        s_past = jnp.where(vis, _dot(qbd, pk_ref[...]), NEG)
        s_new = jnp.where(vis_new, jnp.sum(qbd * newk, axis=-1, keepdims=True), NEG)
        m = jnp.maximum(jnp.max(s_past, axis=-1, keepdims=True), s_new)
        e_past = jnp.where(vis, jnp.exp(s_past - m), 0.0)
        e_new = jnp.where(vis_new, jnp.exp(s_new - m), 0.0)
        den = jnp.maximum(jnp.sum(e_past, axis=-1, keepdims=True) + e_new, 1e-30)
        pr = e_past / den
        o = _dot_nt(pr, pv_ref[...]) + (e_new / den) * newv
        outs = []
        for head in range(NSA_HEADS):
            g = head // NSA_GROUP
            outs.append(o[head * t_new:(head + 1) * t_new, g * hd:(g + 1) * hd])
        ocmp_ref[...] = jnp.concatenate(outs, axis=1)
        imps = []
        for g in range(NSA_KV_HEADS):
            acc = pr[g * NSA_GROUP * t_new:(g * NSA_GROUP + 1) * t_new]
            for h in range(1, NSA_GROUP):
                acc = acc + pr[(g * NSA_GROUP + h) * t_new:(g * NSA_GROUP + h + 1) * t_new]
            imps.append(acc)
        imp = jnp.concatenate(imps, axis=0)
        cur = (past_len + lax.broadcasted_iota(jnp.int32, (NSA_KV_HEADS * t_new, 1), 0) % t_new) // NSA_BLOCK
        sel_ref[...] = _select(imp, cur, n_past)


def _nsa_sample_cmp(page_table, pool_k, pool_v, q, kn, vn, wt, wn, t_new, npg):
    db, n_pages = page_table.shape
    page = pool_k.shape[2]
    past_len = n_pages * page
    nch = n_pages // npg
    nb_pad = -(-(past_len // NSA_BLOCK) // LANES) * LANES
    row0 = q.shape[0] // t_new - db

    def page_spec(j):
        return pl.BlockSpec((None, LANES, page), lambda b, c, pt: (pt[b, c * npg + j], 0, 0))

    tok = lambda w: pl.BlockSpec((t_new, w), lambda b, c, pt: (row0 + b, 0))
    full3 = lambda a: pl.BlockSpec(a.shape, lambda b, c, pt: (0, 0, 0))
    kern = functools.partial(_nsa_sample_cmp_kernel, npg=npg, t_new=t_new, past_len=past_len)
    grid_spec = pltpu.PrefetchScalarGridSpec(
        num_scalar_prefetch=1,
        grid=(db, nch),
        in_specs=[page_spec(j) for j in range(npg)] + [page_spec(j) for j in range(npg)]
        + [tok(512), tok(LANES), tok(LANES), full3(wt), full3(wn)],
        out_specs=[pl.BlockSpec((None, t_new, 512), lambda b, c, pt: (b, 0, 0)),
                   pl.BlockSpec((None, NSA_KV_HEADS * t_new, nb_pad), lambda b, c, pt: (b, 0, 0))],
        scratch_shapes=[pltpu.VMEM((LANES, nb_pad), F32), pltpu.VMEM((LANES, nb_pad), F32)],
    )
    return pl.pallas_call(
        kern,
        grid_spec=grid_spec,
        out_shape=[jax.ShapeDtypeStruct((db, t_new, 512), F32),
                   jax.ShapeDtypeStruct((db, NSA_KV_HEADS * t_new, nb_pad), F32)],
        compiler_params=_cp(("parallel", "arbitrary")),
        name="nsa_sample_cmp",
    )(page_table, *([pool_k] * npg), *([pool_v] * npg), q, kn, vn, wt, wn)


def _online_update(s, mask, v, m_ref, l_ref, acc_ref, v_transposed=False):
    s = jnp.where(mask, s, NEG)
    m_old = m_ref[...]
    m_new = jnp.maximum(m_old, jnp.max(s, axis=-1, keepdims=True))
    alpha = jnp.exp(m_old - m_new)
    e = jnp.where(mask, jnp.exp(s - m_new), 0.0)
    l_ref[...] = alpha * l_ref[...] + jnp.sum(e, axis=-1, keepdims=True)
    acc_ref[...] = alpha * acc_ref[...] + (_dot_nt(e, v) if v_transposed else _dot(e, v))
    m_ref[...] = m_new


def _nsa_sample_sel_kernel(pt_ref, *refs, npg, t_new, past_len):
    kpages = refs[0:npg]
    vpages = refs[npg:2 * npg]
    (q_ref, sel_ref, ksn_ref, vsn_ref, wk_ref, wv_ref, kwn_ref, vwn_ref, ocmp_ref, gate_ref,
     o_ref, m_ref, l_ref, acc_ref, pad_ref) = refs[2 * npg:]
    c = pl.program_id(1)
    nch = pl.num_programs(1)
    hd = NSA_BLOCK
    page = kpages[0].shape[1]
    bpp = page // NSA_BLOCK
    nbc = npg * bpp
    rows = NSA_HEADS * t_new
    q = q_ref[...]
    qbd = _qbd(q, t_new)

    @pl.when(c == 0)
    def _():
        m_ref[...] = jnp.full(m_ref.shape, NEG, F32)
        l_ref[...] = jnp.zeros(l_ref.shape, F32)
        acc_ref[...] = jnp.zeros(acc_ref.shape, F32)

    sel = sel_ref[...]
    nb_pad = sel.shape[1]
    sel_rows = jnp.concatenate([sel[(head // NSA_GROUP) * t_new:(head // NSA_GROUP + 1) * t_new]
                                for head in range(NSA_HEADS)], axis=0)
    blk_iota = lax.broadcasted_iota(jnp.int32, (nb_pad, npg * page), 0)
    key_blk = c * nbc + lax.broadcasted_iota(jnp.int32, (nb_pad, npg * page), 1) // NSA_BLOCK
    expand = jnp.where(blk_iota == key_blk, 1.0, 0.0).astype(BF16)
    smask = _dot(sel_rows, expand) > 0.5
    kcat = jnp.concatenate([r[...].astype(BF16) for r in kpages], axis=1)
    vcat = jnp.concatenate([r[...].astype(BF16) for r in vpages], axis=1)
    _online_update(_dot(qbd, kcat), smask, vcat, m_ref, l_ref, acc_ref, v_transposed=True)

    @pl.when(c == nch - 1)
    def _():
        trow = lax.broadcasted_iota(jnp.int32, (rows, 1), 0) % t_new
        pad_ref[...] = jnp.zeros(pad_ref.shape, F32)
        pad_ref[0:t_new, :] = ksn_ref[...]
        kn = pad_ref[...]
        pad_ref[0:t_new, :] = vsn_ref[...]
        vn = pad_ref[...]
        jn = lax.broadcasted_iota(jnp.int32, (rows, LANES), 1)
        _online_update(_dot_nt(qbd, kn), jn <= trow, vn, m_ref, l_ref, acc_ref)
        o_sel = acc_ref[...] / jnp.maximum(l_ref[...], 1e-30)
        wb = wk_ref.shape[1]
        jw = lax.broadcasted_iota(jnp.int32, (rows, wb), 1)
        rel = trow + wb - jw
        wmask = jnp.concatenate([(rel >= 0) & (rel < NSA_WINDOW), jn <= trow], axis=1)
        pad_ref[0:t_new, :] = kwn_ref[...]
        s_win = jnp.concatenate([_dot(qbd, wk_ref[...]), _dot_nt(qbd, pad_ref[...])], axis=1)
        p_win = _msoftmax(s_win, wmask)
        pad_ref[0:t_new, :] = vwn_ref[...]
        o_win = _dot_nt(p_win[:, 0:wb], wv_ref[...]) + _dot(p_win[:, wb:], pad_ref[...])
        gate = gate_ref[...]
        ocmp = ocmp_ref[...]
        outs = []
        for head in range(NSA_HEADS):
            g = head // NSA_GROUP
            r0, r1 = head * t_new, (head + 1) * t_new
            outs.append(_gate_col(gate, head, 0) * ocmp[:, head * hd:(head + 1) * hd]
                        + _gate_col(gate, head, 1) * o_sel[r0:r1, g * hd:(g + 1) * hd]
                        + _gate_col(gate, head, 2) * o_win[r0:r1, g * hd:(g + 1) * hd])
        o_ref[...] = jnp.concatenate(outs, axis=1)


def _nsa_sample_sel(page_table, pool_k, pool_v, q, sel, ksn, vsn, win_k, win_v, kwn, vwn, ocmp, zd, t_new, npg):
    db, n_pages = page_table.shape
    page = pool_k.shape[2]
    past_len = n_pages * page
    nch = n_pages // npg
    row0 = q.shape[0] // t_new - db
    rows = NSA_HEADS * t_new

    def page_spec(j):
        return pl.BlockSpec((None, LANES, page), lambda b, c, pt: (pt[b, c * npg + j], 0, 0))

    tok = lambda w: pl.BlockSpec((t_new, w), lambda b, c, pt: (row0 + b, 0))
    per_seq = lambda a: pl.BlockSpec((None,) + a.shape[1:], lambda b, c, pt: (b, 0, 0))
    kern = functools.partial(_nsa_sample_sel_kernel, npg=npg, t_new=t_new, past_len=past_len)
    grid_spec = pltpu.PrefetchScalarGridSpec(
        num_scalar_prefetch=1,
        grid=(db, nch),
        in_specs=[page_spec(j) for j in range(npg)] + [page_spec(j) for j in range(npg)]
        + [tok(512), per_seq(sel), tok(LANES), tok(LANES), per_seq(win_k), per_seq(win_v), tok(LANES), tok(LANES),
           per_seq(ocmp), tok(LANES)],
        out_specs=pl.BlockSpec((t_new, 512), lambda b, c, pt: (b, 0)),
        scratch_shapes=[pltpu.VMEM((rows, 1), F32), pltpu.VMEM((rows, 1), F32), pltpu.VMEM((rows, LANES), F32),
                        pltpu.VMEM((LANES, LANES), F32)],
    )
    return pl.pallas_call(
        kern,
        grid_spec=grid_spec,
        out_shape=jax.ShapeDtypeStruct((db * t_new, 512), F32),
        compiler_params=_cp(("parallel", "arbitrary")),
        name="nsa_sample_sel",
    )(page_table, *([pool_k] * npg), *([pool_v] * npg), q, sel, ksn, vsn, win_k, win_v, kwn, vwn, ocmp, zd)


def _nsa_sample_kernel(pt_ref, ck_hbm, cv_hbm, sk_hbm, sv_hbm, q_ref, kcn_ref, vcn_ref, ksn_ref, vsn_ref,
                       kwn_ref, vwn_ref, gate_ref, wk_ref, wv_ref, wt_ref, wn_ref, o_ref,
                       buf_ref, sem_ref, pk_ref, pv_ref, m_ref, l_ref, acc_ref, pad_ref, *, npg, t_new, past_len):
    b = pl.program_id(0)
    nseq = pl.num_programs(0)
    hd = NSA_BLOCK
    page = buf_ref.shape[3]
    n_pages = past_len // page
    nch = n_pages // npg
    nbc = npg * page // NSA_BLOCK
    nb_pad = pk_ref.shape[1]
    n_past = past_len // NSA_BLOCK
    rows = NSA_HEADS * t_new
    nkeys = npg * page

    def chunk_copies(seq, c, slot):
        pools = (ck_hbm, cv_hbm) if c < nch else (sk_hbm, sv_hbm)
        first = (c % nch) * npg
        copies = []
        for j in range(npg):
            pg = pt_ref[seq, first + j]
            copies.append(pltpu.make_async_copy(pools[0].at[pg], buf_ref.at[slot, j], sem_ref.at[slot]))
            copies.append(pltpu.make_async_copy(pools[1].at[pg], buf_ref.at[slot, npg + j], sem_ref.at[slot]))
        return copies

    @pl.when(b == 0)
    def _():
        for cp in chunk_copies(0, 0, 0):
            cp.start()

    q = q_ref[...]
    qbd = _qbd(q, t_new)
    trow = lax.broadcasted_iota(jnp.int32, (rows, 1), 0) % t_new
    pos = past_len + trow
    seg = jnp.where(lax.broadcasted_iota(jnp.int32, (nkeys, nbc), 0) // NSA_BLOCK
                    == lax.broadcasted_iota(jnp.int32, (nkeys, nbc), 1), 1.0, 0.0).astype(BF16)
    expand = _expand_blocks(nbc, nkeys)
    pk_ref[...] = jnp.zeros(pk_ref.shape, F32)
    pv_ref[...] = jnp.zeros(pv_ref.shape, F32)
    m_ref[...] = jnp.full(m_ref.shape, NEG, F32)
    l_ref[...] = jnp.zeros(l_ref.shape, F32)
    acc_ref[...] = jnp.zeros(acc_ref.shape, F32)
    sel_rows = None
    o_cmp = None

    for c in range(2 * nch):
        slot = c % 2
        if c + 1 < 2 * nch:
            for cp in chunk_copies(b, c + 1, 1 - slot):
                cp.start()
        else:
            @pl.when(b + 1 < nseq)
            def _():
                for cp in chunk_copies(b + 1, 0, 1 - slot):
                    cp.start()
        for cp in chunk_copies(b, c, slot):
            cp.wait()

        if c < nch:
            col = lax.broadcasted_iota(jnp.int32, (nbc, nb_pad), 1)
            place = jnp.where(col == c * nbc + lax.broadcasted_iota(jnp.int32, (nbc, nb_pad), 0),
                              1.0, 0.0).astype(BF16)
            for base, w_t, pooled in ((0, wt_ref[0], pk_ref), (npg, wt_ref[1], pv_ref)):
                x = jnp.concatenate([(buf_ref[slot, base + j] * w_t).astype(BF16) for j in range(npg)], axis=1)
                pooled[...] += _dot_exact_rhs(jnp.dot(x, seg, preferred_element_type=F32), place)
        else:
            c0 = (c - nch) * nbc
            smask = jnp.dot(sel_rows[:, c0:c0 + nbc], expand, preferred_element_type=F32) > 0.5
            kcat = jnp.concatenate([buf_ref[slot, j].astype(BF16) for j in range(npg)], axis=1)
            vcat = jnp.concatenate([buf_ref[slot, npg + j].astype(BF16) for j in range(npg)], axis=1)
            _online_update(_dot(qbd, kcat), smask, vcat, m_ref, l_ref, acc_ref, v_transposed=True)

        if c == nch - 1:
            newk = jnp.sum(kcn_ref[...] * wn_ref[0][0:t_new], axis=0, keepdims=True)
            newv = jnp.sum(vcn_ref[...] * wn_ref[1][0:t_new], axis=0, keepdims=True)
            blk = lax.broadcasted_iota(jnp.int32, (rows, nb_pad), 1)
            vis = ((blk + 1) * NSA_BLOCK <= pos + 1) & (blk < n_past)
            vis_new = (n_past + 1) * NSA_BLOCK <= pos + 1
            s_past = jnp.where(vis, _dot(qbd, pk_ref[...]), NEG)
            s_new = jnp.where(vis_new, jnp.sum(qbd * newk, axis=-1, keepdims=True), NEG)
            mx = jnp.maximum(jnp.max(s_past, axis=-1, keepdims=True), s_new)
            e_past = jnp.where(vis, jnp.exp(s_past - mx), 0.0)
            e_new = jnp.where(vis_new, jnp.exp(s_new - mx), 0.0)
            den = jnp.maximum(jnp.sum(e_past, axis=-1, keepdims=True) + e_new, 1e-30)
            pr = e_past / den
            o_cmp = _dot_nt(pr, pv_ref[...]) + (e_new / den) * newv
            imps = []
            for g in range(NSA_KV_HEADS):
                acc = pr[g * NSA_GROUP * t_new:(g * NSA_GROUP + 1) * t_new]
                for h in range(1, NSA_GROUP):
                    acc = acc + pr[(g * NSA_GROUP + h) * t_new:(g * NSA_GROUP + h + 1) * t_new]
                imps.append(acc)
            imp = jnp.concatenate(imps, axis=0)
            cur = (past_len + lax.broadcasted_iota(jnp.int32, (NSA_KV_HEADS * t_new, 1), 0) % t_new) // NSA_BLOCK
            sel = _select(imp, cur, n_past).astype(BF16)
            sel_rows = jnp.concatenate([sel[(head // NSA_GROUP) * t_new:(head // NSA_GROUP + 1) * t_new]
                                        for head in range(NSA_HEADS)], axis=0)

    pad_ref[...] = jnp.zeros(pad_ref.shape, F32)
    pad_ref[0:t_new, :] = ksn_ref[...]
    kn = pad_ref[...]
    pad_ref[0:t_new, :] = vsn_ref[...]
    vn = pad_ref[...]
    jn = lax.broadcasted_iota(jnp.int32, (rows, LANES), 1)
    _online_update(_dot_nt(qbd, kn), jn <= trow, vn, m_ref, l_ref, acc_ref)
    o_sel = acc_ref[...] / jnp.maximum(l_ref[...], 1e-30)
    wb = wk_ref.shape[1]
    jw = lax.broadcasted_iota(jnp.int32, (rows, wb), 1)
    rel = trow + wb - jw
    wmask = jnp.concatenate([(rel >= 0) & (rel < NSA_WINDOW), jn <= trow], axis=1)
    pad_ref[0:t_new, :] = kwn_ref[...]
    s_win = jnp.concatenate([_dot(qbd, wk_ref[...]), _dot_nt(qbd, pad_ref[...])], axis=1)
    p_win = _msoftmax(s_win, wmask)
    pad_ref[0:t_new, :] = vwn_ref[...]
    o_win = _dot_nt(p_win[:, 0:wb], wv_ref[...]) + _dot(p_win[:, wb:], pad_ref[...])
    gate = gate_ref[...]
    outs = []
    for head in range(NSA_HEADS):
        g = head // NSA_GROUP
        r0, r1 = head * t_new, (head + 1) * t_new
        gl = slice(g * hd, (g + 1) * hd)
        outs.append(_gate_col(gate, head, 0) * o_cmp[r0:r1, gl] + _gate_col(gate, head, 1) * o_sel[r0:r1, gl]
                    + _gate_col(gate, head, 2) * o_win[r0:r1, gl])
    o_ref[...] = jnp.concatenate(outs, axis=1)


def _nsa_sample(page_table, pool_ck, pool_cv, pool_sk, pool_sv, q, kc, vc, ks, vs, kw, vw, zd, win_k, win_v,
                wt, wn, t_new, npg):
    db, n_pages = page_table.shape
    page = pool_ck.shape[2]
    past_len = n_pages * page
    nb_pad = -(-(past_len // NSA_BLOCK) // LANES) * LANES
    row0 = q.shape[0] // t_new - db
    rows = NSA_HEADS * t_new
    assert n_pages % npg == 0 and (2 * n_pages // npg) % 2 == 0
    tok = lambda w: pl.BlockSpec((t_new, w), lambda b, pt: (row0 + b, 0))
    per_seq = lambda a: pl.BlockSpec((None,) + a.shape[1:], lambda b, pt: (b, 0, 0))
    full3 = lambda a: pl.BlockSpec(a.shape, lambda b, pt: (0, 0, 0))
    hbm = pl.BlockSpec(memory_space=pl.ANY)
    kern = functools.partial(_nsa_sample_kernel, npg=npg, t_new=t_new, past_len=past_len)
    grid_spec = pltpu.PrefetchScalarGridSpec(
        num_scalar_prefetch=1,
        grid=(db,),
        in_specs=[hbm, hbm, hbm, hbm, tok(512)] + [tok(LANES)] * 7 + [per_seq(win_k), per_seq(win_v),
                                                                      full3(wt), full3(wn)],
        out_specs=pl.BlockSpec((t_new, 512), lambda b, pt: (b, 0)),
        scratch_shapes=[pltpu.VMEM((2, 2 * npg, LANES, page), F32), pltpu.SemaphoreType.DMA((2,)),
                        pltpu.VMEM((LANES, nb_pad), F32), pltpu.VMEM((LANES, nb_pad), F32),
                        pltpu.VMEM((rows, 1), F32), pltpu.VMEM((rows, 1), F32), pltpu.VMEM((rows, LANES), F32),
                        pltpu.VMEM((LANES, LANES), F32)],
    )
    return pl.pallas_call(
        kern,
        grid_spec=grid_spec,
        out_shape=jax.ShapeDtypeStruct((db * t_new, 512), F32),
        compiler_params=_cp(("arbitrary",)),
        name="nsa_sample",
    )(page_table, pool_ck, pool_cv, pool_sk, pool_sv, q, kc, vc, ks, vs, kw, vw, zd, win_k, win_v, wt, wn)


def _log_sigmoid(x):
    return jnp.minimum(x, 0.0) - jnp.log(1.0 + jnp.exp(-jnp.abs(x)))


def _gla_kernel(za_ref, zd_ref, wgh_ref, wgl_ref, bg_ref, gout_ref, s0_ref, o_ref, s_ref,
                st_ref, kp_ref, bp_ref, vp_ref, za_pad_ref, zd_pad_ref, *, nchunk, t_valid):
    j = pl.program_id(1)
    short = za_ref.shape[0] < GLA_CHUNK
    nj = pl.num_programs(1)
    C = GLA_CHUNK
    sub = GLA_SUB
    nsub = C // sub
    dk = 64
    dv = 128
    hk = GLA_HEADS * dk
    hv = GLA_HEADS * dv

    @pl.when(j == 0)
    def _():
        st_ref[...] = jnp.concatenate([s0_ref[h].T for h in range(GLA_HEADS)], axis=1)
        kp_ref[...] = jnp.zeros(kp_ref.shape, F32)
        bp_ref[...] = jnp.zeros(bp_ref.shape, F32)
        vp_ref[...] = jnp.zeros(vp_ref.shape, F32)

    tril = jnp.where(lax.broadcasted_iota(jnp.int32, (C, C), 0) >= lax.broadcasted_iota(jnp.int32, (C, C), 1),
                     1.0, 0.0).astype(BF16)
    head_ones = jnp.where(lax.broadcasted_iota(jnp.int32, (hk, LANES), 0) // dk
                          == lax.broadcasted_iota(jnp.int32, (hk, LANES), 1), 1.0, 0.0).astype(BF16)
    tmod = lax.broadcasted_iota(jnp.int32, (C, 1), 0) % sub
    gout = gout_ref[...]

    def chunk(c, carry):
        r0 = pl.multiple_of(c * C, C)
        if short:
            za_pad_ref[...] = jnp.zeros(za_pad_ref.shape, F32)
            zd_pad_ref[...] = jnp.zeros(zd_pad_ref.shape, F32)
            za_pad_ref[0:za_ref.shape[0], :] = za_ref[...]
            zd_pad_ref[0:zd_ref.shape[0], :] = zd_ref[...]
            z = za_pad_ref[...]
            zd = zd_pad_ref[...]
        else:
            z = za_ref[pl.ds(r0, C), :]
            zd = zd_ref[pl.ds(r0, C), :]
        q = z[:, 0:hk] * (dk ** -0.5)
        k = z[:, hk:2 * hk]
        v = z[:, 2 * hk:2 * hk + hv]
        r = z[:, 2 * hk + hv:2 * hk + 2 * hv]
        lr = zd[:, 0:GLA_GATE_RANK]
        la = _log_sigmoid(_dot_hi(lr, wgh_ref[...], wgl_ref[...]) + bg_ref[...]) * (1.0 / GLA_TAU)
        tglob = (j * nchunk + c) * C + lax.broadcasted_iota(jnp.int32, (C, 1), 0)
        la = jnp.where(tglob < t_valid, la, 0.0)
        b = _dot_exact_lhs(tril, la)
        st = st_ref[...]
        qe = q * jnp.exp(b)
        kp_ref[sub:sub + C, :] = k
        bp_ref[sub:sub + C, :] = b
        vp_ref[sub:sub + C, :] = v
        xs = []
        for d in range(sub):
            kd = kp_ref[sub - d:sub - d + C, :]
            bd = bp_ref[sub - d:sub - d + C, :]
            xs.append(jnp.where(tmod >= d, q * kd * jnp.exp(b - bd), 0.0).astype(BF16))
        rr = jnp.dot(jnp.concatenate(xs, axis=0), head_ones, preferred_element_type=F32)
        outs = []
        for h in range(GLA_HEADS):
            kh = slice(h * dk, (h + 1) * dk)
            vh = slice(h * dv, (h + 1) * dv)
            o = _dot_nt(qe[:, kh], st[:, kh])
            for d in range(sub):
                o = o + rr[d * C:(d + 1) * C, h:h + 1] * vp_ref[sub - d:sub - d + C, vh]
            offs = [jnp.zeros((sub, dv), F32)]
            for i in range(1, nsub):
                anchor = b[i * sub - 1:i * sub, kh]
                qt = q[i * sub:(i + 1) * sub, kh] * jnp.exp(b[i * sub:(i + 1) * sub, kh] - anchor)
                kt = k[0:i * sub, kh] * jnp.exp(anchor - b[0:i * sub, kh])
                offs.append(_dot(_dot_nt(qt, kt), v[0:i * sub, vh]))
            o = o + jnp.concatenate(offs, axis=0)
            on = o * lax.rsqrt(jnp.mean(o * o, axis=-1, keepdims=True) + EPS) * gout
            rh = r[:, vh]
            outs.append(on * (rh * _sigmoid(rh)))
        o_all = jnp.concatenate(outs, axis=1)
        if short:
            o_ref[...] = o_all[0:o_ref.shape[0], :].astype(o_ref.dtype)
        else:
            o_ref[pl.ds(r0, C), :] = o_all.astype(o_ref.dtype)
        b_last = b[C - 1:C, :]
        kk = k * jnp.exp(b_last - b)
        upd = jnp.concatenate([_dot_tn(v[:, h * dv:(h + 1) * dv], kk[:, h * dk:(h + 1) * dk])
                               for h in range(GLA_HEADS)], axis=1)
        st_ref[...] = st * jnp.exp(b_last) + upd
        return carry

    lax.fori_loop(0, nchunk, chunk, 0)

    @pl.when(j == nj - 1)
    def _():
        st = st_ref[...]
        for h in range(GLA_HEADS):
            s_ref[h] = st[:, h * dk:(h + 1) * dk].T


def _gla(za, zd, wgh, wgl, bg, gout, s0, batch, t_seq, ct, row0, out_dtype):
    nj = t_seq // ct
    blk0 = row0 // ct
    hv = GLA_HEADS * 128
    kern = functools.partial(_gla_kernel, nchunk=max(1, ct // GLA_CHUNK), t_valid=t_seq)
    full = lambda a: pl.BlockSpec(a.shape, lambda b, j: (0,) * a.ndim)
    return pl.pallas_call(
        kern,
        grid=(batch, nj),
        in_specs=[pl.BlockSpec((ct, za.shape[1]), lambda b, j: (blk0 + b * nj + j, 0)),
                  pl.BlockSpec((ct, LANES), lambda b, j: (blk0 + b * nj + j, 0)),
                  full(wgh), full(wgl), full(bg), full(gout),
                  pl.BlockSpec((None,) + s0.shape[1:], lambda b, j: (b, 0, 0, 0))],
        out_specs=[pl.BlockSpec((ct, hv), lambda b, j: (b * nj + j, 0)),
                   pl.BlockSpec((None,) + s0.shape[1:], lambda b, j: (b, 0, 0, 0))],
        out_shape=[jax.ShapeDtypeStruct((batch * t_seq, hv), out_dtype), jax.ShapeDtypeStruct(s0.shape, F32)],
        scratch_shapes=[pltpu.VMEM((128, GLA_HEADS * 64), F32),
                        pltpu.VMEM((GLA_SUB + GLA_CHUNK, GLA_HEADS * 64), F32),
                        pltpu.VMEM((GLA_SUB + GLA_CHUNK, GLA_HEADS * 64), F32),
                        pltpu.VMEM((GLA_SUB + GLA_CHUNK, hv), F32),
                        pltpu.VMEM((GLA_CHUNK, za.shape[1]), F32),
                        pltpu.VMEM((GLA_CHUNK, LANES), F32)],
        compiler_params=_cp(("parallel", "arbitrary")),
        name="gla",
    )(za, zd, wgh, wgl, bg, gout, s0)


def _mem_kv_kernel(m_ref, g_ref, w_ref, gk_ref, k_ref, v_ref):
    x = m_ref[...]
    u = x * lax.rsqrt(jnp.mean(x * x, axis=-1, keepdims=True) + EPS) * g_ref[...]
    kv = jnp.dot(u.astype(BF16), w_ref[...], preferred_element_type=F32)
    half = kv.shape[1] // 2
    k_ref[...] = _seg_rms(kv[:, 0:half], half // MEM_HEADS, gk_ref[...])
    v_ref[...] = kv[:, half:]


def _mem_kv(mem, g, w, gk, tm):
    n, d = mem.shape
    half = w.shape[1] // 2
    full = lambda a: pl.BlockSpec(a.shape, lambda i: (0, 0))
    sd = jax.ShapeDtypeStruct((n, half), F32)
    return pl.pallas_call(
        _mem_kv_kernel,
        grid=(n // tm,),
        in_specs=[pl.BlockSpec((tm, d), lambda i: (i, 0)), full(g), full(w), full(gk)],
        out_specs=[pl.BlockSpec((tm, half), lambda i: (i, 0))] * 2,
        out_shape=[sd, sd],
        compiler_params=_cp(("parallel",)),
        name="mem_kv",
    )(mem, g, w, gk)


def _mem_prompt_kernel(q_ref, k_ref, v_ref, o_ref):
    q = q_ref[...]
    k = k_ref[...].astype(BF16)
    v = v_ref[...].astype(BF16)
    hd = q.shape[1] // MEM_HEADS
    outs = []
    for h in range(MEM_HEADS):
        sl = slice(h * hd, (h + 1) * hd)
        s = _dot_nt(q[:, sl], k[:, sl])
        m = jnp.max(s, axis=-1, keepdims=True)
        e = jnp.exp(s - m)
        outs.append(_dot(e / jnp.sum(e, axis=-1, keepdims=True), v[:, sl]))
    o_ref[...] = jnp.concatenate(outs, axis=1).astype(BF16)


def _mem_prompt(qm, mk, mv, batch, seq, mlen, tq):
    nq = seq // tq
    w = qm.shape[1]
    return pl.pallas_call(
        _mem_prompt_kernel,
        grid=(batch, nq),
        in_specs=[pl.BlockSpec((tq, w), lambda b, i: (b * nq + i, 0)),
                  pl.BlockSpec((mlen, w), lambda b, i: (b, 0)),
                  pl.BlockSpec((mlen, w), lambda b, i: (b, 0))],
        out_specs=pl.BlockSpec((tq, w), lambda b, i: (b * nq + i, 0)),
        out_shape=jax.ShapeDtypeStruct((batch * seq, w), BF16),
        compiler_params=_cp(("parallel", "parallel")),
        name="mem_prompt",
    )(qm, mk, mv)


def _mem_sample_kernel(q_ref, k_ref, v_ref, o_ref, *, sb, t_new):
    w = q_ref.shape[1]
    hd = w // MEM_HEADS
    rows = MEM_HEADS * t_new
    lane_head = lax.broadcasted_iota(jnp.int32, (t_new, w), 1) // hd
    for s_i in range(sb):
        q = q_ref[s_i * t_new:(s_i + 1) * t_new, :].astype(F32)
        qbd = jnp.concatenate([jnp.where(lane_head == h, q, 0.0) for h in range(MEM_HEADS)], axis=0)
        k = k_ref[s_i].astype(BF16)
        v = v_ref[s_i].astype(BF16)
        s = _dot_nt(qbd, k)
        m = jnp.max(s, axis=-1, keepdims=True)
        e = jnp.exp(s - m)
        o = _dot(e / jnp.sum(e, axis=-1, keepdims=True), v)
        o_ref[s_i * t_new:(s_i + 1) * t_new, :] = jnp.concatenate(
            [o[h * t_new:(h + 1) * t_new, h * hd:(h + 1) * hd] for h in range(MEM_HEADS)], axis=1).astype(BF16)


def _mem_sample(qm, ck, cv, row0_blocks, db, t_new, sb):
    w = qm.shape[1]
    mlen = ck.shape[1]
    kern = functools.partial(_mem_sample_kernel, sb=sb, t_new=t_new)
    return pl.pallas_call(
        kern,
        grid=(db // sb,),
        in_specs=[pl.BlockSpec((sb * t_new, w), lambda i: (row0_blocks + i, 0)),
                  pl.BlockSpec((sb, mlen, w), lambda i: (i, 0, 0)),
                  pl.BlockSpec((sb, mlen, w), lambda i: (i, 0, 0))],
        out_specs=pl.BlockSpec((sb * t_new, w), lambda i: (i, 0)),
        out_shape=jax.ShapeDtypeStruct((db * t_new, w), BF16),
        compiler_params=_cp(("parallel",)),
        name="mem_sample",
    )(qm, ck, cv)


def _merge_kernel(x_ref, og_ref, on_ref, om_ref, gate_ref, wg_ref, wn_ref, wm_ref, wo_ref, gf_ref,
                  wrh_ref, wrl_ref, br_ref, h_ref, hn_ref, cmb_ref, *, n_experts, n_tiles):
    @pl.when(pl.program_id(0) >= n_tiles)
    def _():
        h_ref[...] = jnp.zeros(h_ref.shape, h_ref.dtype)
        hn_ref[...] = jnp.zeros(hn_ref.shape, hn_ref.dtype)
        cmb_ref[...] = jnp.zeros(cmb_ref.shape, cmb_ref.dtype)

    @pl.when(pl.program_id(0) < n_tiles)
    def _():
        _merge_body(x_ref, og_ref, on_ref, om_ref, gate_ref, wg_ref, wn_ref, wm_ref, wo_ref, gf_ref,
                    wrh_ref, wrl_ref, br_ref, h_ref, hn_ref, cmb_ref, n_experts)


def _merge_body(x_ref, og_ref, on_ref, om_ref, gate_ref, wg_ref, wn_ref, wm_ref, wo_ref, gf_ref,
                wrh_ref, wrl_ref, br_ref, h_ref, hn_ref, cmb_ref, n_experts):
    d = x_ref.shape[1]
    gate = gate_ref[...].astype(F32)
    y = (gate[:, 0:d] * jnp.dot(og_ref[...], wg_ref[...], preferred_element_type=F32)
         + gate[:, d:2 * d] * jnp.dot(on_ref[...], wn_ref[...], preferred_element_type=F32)
         + gate[:, 2 * d:3 * d] * jnp.dot(om_ref[...], wm_ref[...], preferred_element_type=F32))
    h = x_ref[...] + jnp.dot(y.astype(BF16), wo_ref[...], preferred_element_type=F32)
    h_ref[...] = h
    hn = h * lax.rsqrt(jnp.mean(h * h, axis=-1, keepdims=True) + EPS) * gf_ref[...]
    hn_ref[...] = hn.astype(BF16)
    logits = _dot_hi(hn, wrh_ref[...], wrl_ref[...]) + br_ref[...]
    lane = lax.broadcasted_iota(jnp.int32, logits.shape, 1).astype(F32)
    work = jnp.where(lane < n_experts, logits, NEG)
    top = jnp.max(work, axis=-1, keepdims=True)
    chosen = jnp.zeros(logits.shape, F32)
    for _ in range(TOP_K):
        mx = jnp.max(work, axis=-1, keepdims=True)
        first = jnp.min(jnp.where(work == mx, lane, float(LANES)), axis=-1, keepdims=True)
        pick = lane == first
        chosen = jnp.where(pick, 1.0, chosen)
        work = jnp.where(pick, NEG, work)
    e = jnp.where(chosen > 0.5, jnp.exp(logits - top), 0.0)
    cmb_ref[...] = e / jnp.sum(e, axis=-1, keepdims=True)


def _merge(x, og, on, om, gate, wg, wn, wm, wo, gf, wrh, wrl, br, n_experts, tm, n_pad):
    n, d = x.shape
    n_tiles = n // tm
    row = lambda a: pl.BlockSpec((tm, a.shape[1]), lambda i: (jnp.minimum(i, n_tiles - 1), 0))
    full = lambda a: pl.BlockSpec(a.shape, lambda i: (0, 0))
    kern = functools.partial(_merge_kernel, n_experts=n_experts, n_tiles=n_tiles)
    return pl.pallas_call(
        kern,
        grid=(n_pad // tm,),
        in_specs=[row(x), row(og), row(on), row(om), row(gate), full(wg), full(wn), full(wm), full(wo), full(gf),
                  full(wrh), full(wrl), full(br)],
        out_specs=[pl.BlockSpec((tm, d), lambda i: (i, 0)), pl.BlockSpec((tm, d), lambda i: (i, 0)),
                   pl.BlockSpec((tm, LANES), lambda i: (i, 0))],
        out_shape=[jax.ShapeDtypeStruct((n_pad, d), F32), jax.ShapeDtypeStruct((n_pad, d), BF16),
                   jax.ShapeDtypeStruct((n_pad, LANES), F32)],
        compiler_params=_cp(("parallel",)),
        name="merge_router",
    )(x, og, on, om, gate, wg, wn, wm, wo, gf, wrh, wrl, br)


MOE_SUBTILE = 1024
MOE_SUBTILES = 2
MOE_ROW_BLOCK = 160


def _moe_kernel(hn_ref, h_ref, cmb_ref, wgu_ref, bgu_ref, wd_ref, bd_ref, y_ref,
                rank_ref, rankt_ref, cmbt_ref, cnt_ref, *, ts):
    e = pl.program_id(1)
    ns = hn_ref.shape[0] // ts
    rb = MOE_ROW_BLOCK

    @pl.when(e == 0)
    def _():
        y_ref[...] = h_ref[...]
        lower = jnp.where(lax.broadcasted_iota(jnp.int32, (ts, ts), 0) > lax.broadcasted_iota(jnp.int32, (ts, ts), 1),
                          1.0, 0.0).astype(BF16)
        for s in range(ns):
            cmb = cmb_ref[s * ts:(s + 1) * ts, :]
            sel = jnp.where(cmb > 0.0, 1.0, 0.0)
            rank = jnp.where(sel > 0.0, jnp.dot(lower, sel.astype(BF16), preferred_element_type=F32), -1.0)
            rank_ref[s] = rank
            rankt_ref[s] = rank.T
            cmbt_ref[s] = cmb.T
            cnt_ref[s] = jnp.broadcast_to(jnp.sum(sel, axis=0, keepdims=True), cnt_ref.shape[1:])

    lane = lax.broadcasted_iota(jnp.int32, (1, LANES), 1)
    pick = jnp.where(lax.broadcasted_iota(jnp.int32, (LANES, LANES), 0) == e, 1.0, 0.0).astype(BF16)
    n_iter = 0
    rank_cols = []
    for s in range(ns):
        cnt = jnp.sum(jnp.where(lane == e, cnt_ref[s, 0:1, :], 0.0)).astype(jnp.int32)
        n_iter = jnp.maximum(n_iter, (cnt + rb - 1) // rb)
        r = rank_ref[s]
        r_hi = r.astype(BF16)
        r_lo = (r - r_hi.astype(F32)).astype(BF16)
        rc = (jnp.dot(r_hi, pick, preferred_element_type=F32)
              + jnp.dot(r_lo, pick, preferred_element_type=F32))
        rank_cols.append(jnp.concatenate([rc] * (-(-rb // LANES)), axis=1)[:, 0:rb])
    dff2 = wgu_ref.shape[1]

    def body(i, carry):
        r0 = i * rb
        rows = (r0 + lax.broadcasted_iota(jnp.int32, (rb, ts), 0)).astype(F32)
        xs, ws = [], []
        for s in range(ns):
            hit = rankt_ref[s, pl.ds(e, 1), :] == rows
            xs.append(jnp.dot(jnp.where(hit, 1.0, 0.0).astype(BF16), hn_ref[s * ts:(s + 1) * ts, :],
                              preferred_element_type=F32).astype(BF16))
            ws.append(jnp.sum(jnp.where(hit, cmbt_ref[s, pl.ds(e, 1), :], 0.0), axis=-1, keepdims=True))
        x = jnp.concatenate(xs, axis=0)
        gu = jnp.dot(x, wgu_ref[...], preferred_element_type=F32) + bgu_ref[...]
        gate = jnp.minimum(gu, SWIGLU_LIMIT)
        up = pltpu.roll(jnp.clip(gu, -SWIGLU_LIMIT, SWIGLU_LIMIT), dff2 - 1, 1)
        act = (up + 1.0) * gate * _sigmoid(SWIGLU_ALPHA * gate)
        out = jnp.dot(act.astype(BF16), wd_ref[...], preferred_element_type=F32) + bd_ref[...]
        cols = (r0 + lax.broadcasted_iota(jnp.int32, (ts, rb), 1)).astype(F32)
        for s in range(ns):
            yw = (out[s * rb:(s + 1) * rb, :] * ws[s]).astype(BF16)
            pt = jnp.where(rank_cols[s] == cols, 1.0, 0.0).astype(BF16)
            y_ref[s * ts:(s + 1) * ts, :] += jnp.dot(pt, yw, preferred_element_type=F32)
        return carry

    lax.fori_loop(0, n_iter, body, 0)


def _moe(hn, h, cmb, wgu, bgu, wd, bd, ts, ns):
    n, d = h.shape
    ne, _, dff2 = wgu.shape
    tt = ts * ns
    once = lambda w: pl.BlockSpec((tt, w), lambda i, e: (i, 0), pipeline_mode=pl.Buffered(1))
    return pl.pallas_call(
        functools.partial(_moe_kernel, ts=ts),
        grid=(n // tt, ne),
        in_specs=[once(d), once(d), once(LANES),
                  pl.BlockSpec((None, d, dff2), lambda i, e: (e, 0, 0)),
                  pl.BlockSpec((None, 1, dff2), lambda i, e: (e, 0, 0)),
                  pl.BlockSpec((None, dff2, d), lambda i, e: (e, 0, 0)),
                  pl.BlockSpec((None, 1, d), lambda i, e: (e, 0, 0))],
        out_specs=pl.BlockSpec((tt, d), lambda i, e: (i, 0)),
        out_shape=jax.ShapeDtypeStruct((n, d), F32),
        scratch_shapes=[pltpu.VMEM((ns, ts, LANES), F32), pltpu.VMEM((ns, LANES, ts), F32),
                        pltpu.VMEM((ns, LANES, ts), F32), pltpu.VMEM((ns, 8, LANES), F32)],
        compiler_params=_cp(("parallel", "arbitrary")),
        name="moe",
    )(hn, h, cmb, wgu, bgu, wd, bd)


def _rope_table(pos):
    half = 8
    inv = 1.0 / (ROPE_THETA ** (jnp.arange(half, dtype=F32) / half))
    ang = pos.astype(F32)[:, None] * inv[None, :]
    cos, sin = jnp.cos(ang), jnp.sin(ang)
    n = pos.shape[0]
    one = jnp.ones((n, 64 - 2 * half), F32)
    zero8 = jnp.zeros((n, half), F32)
    zero = jnp.zeros((n, 64 - 2 * half), F32)
    c = jnp.concatenate([cos, cos, one], axis=1)
    s1 = jnp.concatenate([-sin, zero8, zero], axis=1)
    s2 = jnp.concatenate([zero8, sin, zero], axis=1)
    return jnp.concatenate([c, c, s1, s1, s2, s2], axis=1)


def _hi_lo(w):
    hi = w.astype(BF16)
    return hi, (w - hi.astype(F32)).astype(BF16)


def kernel(x_prompt, x_sample, cache_cmp_k, cache_cmp_v, cache_sel_k, cache_sel_v, cache_win_k, cache_win_v, state_gla, cache_mem_k, cache_mem_v, page_table, mem_prompt, g_attn, w_in, w_gla_gate, b_gla_gate, g_gla_out, g_q_nsa, g_k_nsa, w_cmp_pos, g_q_mem, g_k_mem, g_mem, w_mem_kv, w_up_gla, w_up_nsa, w_up_mem, w_out, g_ffn, w_router, b_router, w_gate_up, b_gate_up, w_down, b_down):
    B, S, D = x_prompt.shape
    DB, T, _ = x_sample.shape
    n_pool, page = cache_cmp_k.shape[1:3]
    n_pages = page_table.shape[1]
    past_len = n_pages * page
    wb = cache_win_k.shape[2]
    mlen = mem_prompt.shape[1]
    ne = w_router.shape[2]
    Np, Ns = B * S, DB * T
    N = Np + Ns
    TM = 256
    assert cache_cmp_k.shape[0] == 1 and D == 1024 and S % TM == 0 and Ns % TM == 0 and TM % T == 0
    assert T <= NSA_BLOCK and past_len % NSA_BLOCK == 0 and S % NSA_BLOCK == 0

    gq, gk, gv, gr, glr, nq, nkv, ng, mq, mg = np.cumsum(
        [0, 256, 256, 512, 512, GLA_GATE_RANK, 512, 768, 3 * NSA_HEADS, 512]).tolist()
    w = w_in[0]
    wa = w[:, gq:glr].astype(BF16)
    wbm = jnp.concatenate([w[:, nq:ng], w[:, mq:mg]], axis=1).astype(BF16)
    wc = w[:, mg:].astype(BF16)
    wd = jnp.concatenate([w[:, glr:nq], w[:, ng:mq], jnp.zeros((D, LANES - GLA_GATE_RANK - 3 * NSA_HEADS), F32)],
                         axis=1).astype(BF16)

    x_all = jnp.concatenate([x_prompt.reshape(Np, D), x_sample.reshape(Ns, D)], axis=0)
    za, zb, zc, zd = _project(x_all, g_attn, wa, wbm, wc, wd, TM)

    pos_sample = past_len + jnp.arange(T, dtype=jnp.int32)
    tab = jnp.concatenate([_rope_table(jnp.arange(S, dtype=jnp.int32)),
                           jnp.tile(_rope_table(pos_sample), (TM // T, 1))], axis=0)
    gq_t = jnp.tile(g_q_nsa[0], NSA_HEADS)[None, :]
    gk_t = jnp.tile(g_k_nsa[0], (1, NSA_KV_HEADS))
    gm_t = jnp.tile(g_q_mem[0], MEM_HEADS)[None, :]
    q, kc, vc, ks, vs, kw, vw, qm, qb, kvb = _prep(zb, tab, gq_t, gk_t, gm_t, TM, Np // TM, S // TM)

    w2 = jnp.tile(w_cmp_pos[0], (1, 1, NSA_KV_HEADS))
    ck, cv = _compress(kc, vc, w2, Np, 512)
    o_nsa_p = _nsa_prompt(qb, ck, cv, kvb, zd, B, S, 128)

    rows_minor = lambda a: jnp.transpose(a[0], (0, 2, 3, 1)).reshape(a.shape[1], LANES, a.shape[2])
    pool = rows_minor
    npg = 16 if n_pages % 32 == 0 else n_pages // 2
    wt = jnp.tile(jnp.transpose(w2, (0, 2, 1)), (1, 1, page // NSA_BLOCK))
    o_nsa_s = _nsa_sample(page_table, pool(cache_cmp_k), pool(cache_cmp_v), pool(cache_sel_k), pool(cache_sel_v),
                          q, kc, vc, ks, vs, kw, vw, zd, rows_minor(cache_win_k), rows_minor(cache_win_v),
                          wt, w2, T, npg)
    o_nsa = jnp.concatenate([o_nsa_p, o_nsa_s.astype(BF16)], axis=0)

    wgh, wgl = _hi_lo(w_gla_gate[0])
    bg = b_gla_gate[0][None, :]
    gout = g_gla_out[0][None, :]
    s0_p = jnp.zeros((B,) + state_gla.shape[2:], F32)
    o_gla_p, s_gla_p = _gla(za, zd, wgh, wgl, bg, gout, s0_p, B, S, 512, 0, BF16)
    o_gla_s, s_gla_s = _gla(za, zd, wgh, wgl, bg, gout, state_gla[0], DB, T, T, Np, F32)
    o_gla = jnp.concatenate([o_gla_p, o_gla_s.astype(BF16)], axis=0)

    gkm = jnp.tile(g_k_mem[0], MEM_HEADS)[None, :]
    mem_k, mem_v = _mem_kv(mem_prompt.reshape(B * mlen, D), g_mem, w_mem_kv[0].astype(BF16), gkm, TM)
    o_mem_p = _mem_prompt(qm, mem_k, mem_v, B, S, mlen, 512)
    mw = MEM_HEADS * cache_mem_k.shape[-1]
    sb = 8
    o_mem_s = _mem_sample(qm, cache_mem_k[0].reshape(DB, mlen, mw), cache_mem_v[0].reshape(DB, mlen, mw),
                          Np // (sb * T), DB, T, sb)
    o_mem = jnp.concatenate([o_mem_p, o_mem_s], axis=0)

    moe_ts = MOE_SUBTILE if N % MOE_SUBTILE == 0 else TM
    n_pad = -(-N // (moe_ts * MOE_SUBTILES)) * (moe_ts * MOE_SUBTILES)
    wr = jnp.pad(w_router[0], ((0, 0), (0, LANES - ne)))
    wrh, wrl = _hi_lo(wr)
    br = jnp.pad(b_router[0], (0, LANES - ne))[None, :]
    h, hn, cmb = _merge(x_all, o_gla, o_nsa, o_mem, zc, w_up_gla[0].astype(BF16), w_up_nsa[0].astype(BF16),
                        w_up_mem[0].astype(BF16), w_out[0].astype(BF16), g_ffn, wrh, wrl, br, ne, TM, n_pad)

    wd_b = w_down[0].astype(BF16)
    wd_exp = jnp.stack([wd_b, jnp.zeros_like(wd_b)], axis=2).reshape(ne, -1, D)
    y = _moe(hn, h, cmb, w_gate_up[0].astype(BF16), b_gate_up[0][:, None, :], wd_exp, b_down[0][:, None, :],
             moe_ts, MOE_SUBTILES)

    kvshape = (1, B, S, NSA_KV_HEADS, NSA_BLOCK)
    p_rows = lambda a: a[:Np].reshape(kvshape)
    s_rows = lambda a: a[Np:].reshape(1, DB, T, NSA_KV_HEADS, NSA_BLOCK)
    wbp = min(NSA_WINDOW, S)
    s_win = lambda cache, new: jnp.concatenate(
        [cache[:, :, T:], new[Np:].reshape(1, DB, T, NSA_KV_HEADS, NSA_BLOCK)], axis=2)
    mshape = (1, B, mlen, MEM_HEADS, mw // MEM_HEADS)
    return (y[:Np].reshape(B, S, D), y[Np:N].reshape(DB, T, D),
            p_rows(kc), p_rows(vc), p_rows(ks), p_rows(vs),
            p_rows(kw)[:, :, S - wbp:], p_rows(vw)[:, :, S - wbp:],
            s_gla_p[None], mem_k.reshape(mshape), mem_v.reshape(mshape),
            s_rows(kc), s_rows(vc), s_rows(ks), s_rows(vs),
            s_win(cache_win_k, kw), s_win(cache_win_v, vw), s_gla_s[None])
```

```python
import functools

import numpy as np
import jax
import jax.numpy as jnp
from jax import lax
from jax.experimental import pallas as pl
from jax.experimental.pallas import tpu as pltpu

F32 = jnp.float32
BF16 = jnp.bfloat16

GLA_HEADS = 4
GLA_GATE_RANK = 16
GLA_TAU = 16.0
GLA_CHUNK = 64
GLA_SUB = 16
NSA_HEADS = 8
NSA_KV_HEADS = 2
NSA_GROUP = NSA_HEADS // NSA_KV_HEADS
NSA_BLOCK = 64
NSA_TOPN = 16
NSA_WINDOW = 512
MEM_HEADS = 4
ROPE_THETA = 500000.0
N_BRANCH = 3
TOP_K = 4
SWIGLU_LIMIT = 7.0
SWIGLU_ALPHA = 1.702
EPS = 1e-6

LANES = 128
NEG = -1e30
VMEM_LIMIT = 56 * 1024 * 1024


def _cp(sem):
    return pltpu.CompilerParams(dimension_semantics=sem, vmem_limit_bytes=VMEM_LIMIT)


def _dot(a, b):
    return jnp.dot(a.astype(BF16), b.astype(BF16), preferred_element_type=F32)


def _dot_nt(a, b):
    return lax.dot_general(a.astype(BF16), b.astype(BF16), (((1,), (1,)), ((), ())),
                           preferred_element_type=F32)


def _dot_tn(a, b):
    return lax.dot_general(a.astype(BF16), b.astype(BF16), (((0,), (0,)), ((), ())),
                           preferred_element_type=F32)


def _split3(x):
    hi = x.astype(BF16)
    r = x - hi.astype(F32)
    mid = r.astype(BF16)
    lo = (r - mid.astype(F32)).astype(BF16)
    return hi, mid, lo


def _dot_exact_rhs(x, m):
    hi, mid, lo = _split3(x)
    d = functools.partial(jnp.dot, preferred_element_type=F32)
    return d(hi, m) + d(mid, m) + d(lo, m)


def _dot_exact_lhs(m, x):
    hi, mid, lo = _split3(x)
    d = functools.partial(jnp.dot, preferred_element_type=F32)
    return d(m, hi) + d(m, mid) + d(m, lo)


def _dot_hi(x, w_hi, w_lo):
    hi, mid, _ = _split3(x)
    d = functools.partial(jnp.dot, preferred_element_type=F32)
    return d(hi, w_hi) + d(mid, w_hi) + d(hi, w_lo)


def _msoftmax(s, mask):
    s = jnp.where(mask, s, NEG)
    m = jnp.max(s, axis=-1, keepdims=True)
    e = jnp.where(mask, jnp.exp(s - m), 0.0)
    return e / jnp.maximum(jnp.sum(e, axis=-1, keepdims=True), 1e-30)


def _seg_ones(width, seg):
    r = lax.broadcasted_iota(jnp.int32, (width, width), 0) // seg
    c = lax.broadcasted_iota(jnp.int32, (width, width), 1) // seg
    return jnp.where(r == c, 1.0, 0.0).astype(BF16)


def _seg_rms(x, seg, gain):
    ms = _dot_exact_rhs(x * x, _seg_ones(x.shape[-1], seg)) * (1.0 / seg)
    return x * lax.rsqrt(ms + EPS) * gain


def _sigmoid(x):
    return 1.0 / (1.0 + jnp.exp(-x))


def _proj_kernel(x_ref, g_ref, wa_ref, wb_ref, wc_ref, wd_ref, za_ref, zb_ref, zc_ref, zd_ref):
    x = x_ref[...]
    u = x * lax.rsqrt(jnp.mean(x * x, axis=-1, keepdims=True) + EPS) * g_ref[...]
    ub = u.astype(BF16)
    za_ref[...] = jnp.dot(ub, wa_ref[...], preferred_element_type=F32)
    zb_ref[...] = jnp.dot(ub, wb_ref[...], preferred_element_type=F32)
    zc_ref[...] = _sigmoid(jnp.dot(ub, wc_ref[...], preferred_element_type=F32)).astype(BF16)
    zd_ref[...] = jnp.dot(ub, wd_ref[...], preferred_element_type=F32)


def _project(x, g, wa, wb, wc, wd, tm):
    n, d = x.shape
    full = lambda w: pl.BlockSpec(w.shape, lambda i: (0, 0))
    row = lambda w: pl.BlockSpec((tm, w), lambda i: (i, 0))
    return pl.pallas_call(
        _proj_kernel,
        grid=(n // tm,),
        in_specs=[row(d), full(g), full(wa), full(wb), full(wc), full(wd)],
        out_specs=[row(wa.shape[1]), row(wb.shape[1]), row(wc.shape[1]), row(wd.shape[1])],
        out_shape=[jax.ShapeDtypeStruct((n, wa.shape[1]), F32), jax.ShapeDtypeStruct((n, wb.shape[1]), F32),
                   jax.ShapeDtypeStruct((n, wc.shape[1]), BF16), jax.ShapeDtypeStruct((n, wd.shape[1]), F32)],
        compiler_params=_cp(("parallel",)),
        name="proj",
    )(x, g, wa, wb, wc, wd)


def _rope(x, tab):
    c, s1, s2 = tab[:, 0:LANES], tab[:, LANES:2 * LANES], tab[:, 2 * LANES:3 * LANES]
    half = 8
    return x * c + pltpu.roll(x, LANES - half, 1) * s1 + pltpu.roll(x, half, 1) * s2


def _prep_kernel(zb_ref, tab_ref, gq_ref, gk_ref, gm_ref,
                 q_ref, kc_ref, vc_ref, ks_ref, vs_ref, kw_ref, vw_ref, qm_ref, qb_ref, kvb_ref):
    z = zb_ref[...]
    tab = tab_ref[...]
    hd = NSA_BLOCK
    nq = NSA_HEADS * hd
    qn = _seg_rms(z[:, 0:nq], hd, gq_ref[...])
    q = jnp.concatenate([_rope(qn[:, j * LANES:(j + 1) * LANES], tab) for j in range(nq // LANES)], axis=1)
    q = q * (hd ** -0.5)
    q_ref[...] = q
    qb_ref[...] = q.astype(BF16)
    kv = z[:, nq:nq + 6 * LANES]
    gk = gk_ref[...]
    ks = []
    for j, ref in ((0, kc_ref), (1, ks_ref), (2, kw_ref)):
        kj = _rope(_seg_rms(kv[:, 2 * j * LANES:(2 * j + 1) * LANES], hd, gk[j:j + 1, :]), tab)
        ref[...] = kj
        ks.append(kj)
    vs = []
    for j, ref in ((0, vc_ref), (1, vs_ref), (2, vw_ref)):
        vj = kv[:, (2 * j + 1) * LANES:(2 * j + 2) * LANES]
        ref[...] = vj
        vs.append(vj)
    kvb_ref[...] = jnp.concatenate([ks[1], vs[1], ks[2], vs[2]], axis=1).astype(BF16)
    zm = z[:, nq + 6 * LANES:]
    mhd = zm.shape[1] // MEM_HEADS
    qm_ref[...] = (_seg_rms(zm, mhd, gm_ref[...]) * (mhd ** -0.5)).astype(BF16)


def _prep(zb, tab, gq, gk, gm, tm, n_prompt_tiles, tab_tiles):
    n = zb.shape[0]
    row = lambda w: pl.BlockSpec((tm, w), lambda i: (i, 0))
    full = lambda a: pl.BlockSpec(a.shape, lambda i: (0, 0))
    tab_spec = pl.BlockSpec((tm, 3 * LANES), lambda i: (jnp.where(i < n_prompt_tiles, i % tab_tiles, tab_tiles), 0))
    sd = lambda w, dt=F32: jax.ShapeDtypeStruct((n, w), dt)
    return pl.pallas_call(
        _prep_kernel,
        grid=(n // tm,),
        in_specs=[row(zb.shape[1]), tab_spec, full(gq), full(gk), full(gm)],
        out_specs=[row(512)] + [row(LANES)] * 6 + [row(512), row(512), row(512)],
        out_shape=[sd(512)] + [sd(LANES)] * 6 + [sd(512, BF16), sd(512, BF16), sd(512, BF16)],
        compiler_params=_cp(("parallel",)),
        name="nsa_prep",
    )(zb, tab, gq, gk, gm)


def _pool_rows(x, w):
    nblk = x.shape[0] // NSA_BLOCK
    return jnp.sum(x.reshape(nblk, NSA_BLOCK, LANES) * w[None], axis=1)


def _compress_kernel(k_ref, v_ref, w_ref, ck_ref, cv_ref):
    ck_ref[...] = _pool_rows(k_ref[...], w_ref[0])
    cv_ref[...] = _pool_rows(v_ref[...], w_ref[1])


def _compress(kc, vc, w2, n_rows, tm):
    row = pl.BlockSpec((tm, LANES), lambda i: (i, 0))
    out = pl.BlockSpec((tm // NSA_BLOCK, LANES), lambda i: (i, 0))
    sd = jax.ShapeDtypeStruct((n_rows // NSA_BLOCK, LANES), F32)
    return pl.pallas_call(
        _compress_kernel,
        grid=(n_rows // tm,),
        in_specs=[row, row, pl.BlockSpec(w2.shape, lambda i: (0, 0, 0))],
        out_specs=[out, out],
        out_shape=[sd, sd],
        compiler_params=_cp(("parallel",)),
        name="compress",
    )(kc, vc, w2)


def _select(imp, cur, n_cand):
    n_iota = lax.broadcasted_iota(jnp.int32, imp.shape, 1)
    cnt = jnp.zeros(imp.shape, F32)
    for m in range(n_cand):
        col = imp[:, m:m + 1]
        beats = (col > imp) | ((col == imp) & (n_iota > m))
        cnt = cnt + jnp.where(beats & (cur > m), 1.0, 0.0)
    keep = (n_iota == cur) | ((n_iota < cur) & (cnt < NSA_TOPN - 1))
    return jnp.where(keep, 1.0, 0.0)


def _expand_blocks(nb, nkeys, first_key=0):
    kb = (lax.broadcasted_iota(jnp.int32, (nb, nkeys), 1) + first_key) // NSA_BLOCK
    return jnp.where(kb == lax.broadcasted_iota(jnp.int32, (nb, nkeys), 0), 1.0, 0.0).astype(BF16)


def _gate_col(gate, head, j):
    c = GLA_GATE_RANK + head * 3 + j
    return _sigmoid(gate[:, c:c + 1])


def _attend(qh, k, bias, v_ones, hd):
    s = _dot_nt(qh, k) + bias
    e = jnp.exp((s - jnp.max(s, axis=-1, keepdims=True)).astype(BF16))
    pv = jnp.dot(e, v_ones, preferred_element_type=F32)
    return pv[:, 0:hd] / pv[:, hd:hd + 1]


def _nsa_prompt_kernel(q_ref, ck_ref, cv_ref, kv_ref, gate_ref, o_ref, osel_ref, *, tq, seq, wslab, nbr):
    i = pl.program_id(1)
    nq = pl.num_programs(1)
    t0 = i * tq
    hd = NSA_BLOCK
    nb = seq // NSA_BLOCK
    q = q_ref[...]
    gate = gate_ref[...]
    pos = t0 + lax.broadcasted_iota(jnp.int32, (tq, 1), 0)
    cur = pos // NSA_BLOCK
    blk = lax.broadcasted_iota(jnp.int32, (tq, nb), 1)
    vis = (blk + 1) * NSA_BLOCK <= pos + 1
    wstart = pl.multiple_of(jnp.maximum(t0 + tq - wslab, 0), LANES)
    wpos = wstart + lax.broadcasted_iota(jnp.int32, (tq, wslab), 1)
    wrel = pos - wpos
    wbias = jnp.where((wrel >= 0) & (wrel < NSA_WINDOW), 0.0, NEG)
    ck = ck_ref[...]
    cv = cv_ref[...]
    outs = []
    for g in range(NSA_KV_HEADS):
        ckg = ck[:, g * hd:(g + 1) * hd]
        cvg = cv[:, g * hd:(g + 1) * hd]
        o_cmp = []
        imp = jnp.zeros((tq, nb), F32)
        for h in range(NSA_GROUP):
            qh = q[:, (g * NSA_GROUP + h) * hd:(g * NSA_GROUP + h + 1) * hd]
            pr = _msoftmax(_dot_nt(qh, ckg), vis)
            imp = imp + pr
            o_cmp.append(_dot(pr, cvg))
        sel = _select(imp, cur, nb).astype(BF16)
        for j in range(nbr):
            klen = (j + 1) * seq // nbr

            @pl.when(i * nbr // nq == j)
            def _(g=g, klen=klen, sel=sel):
                kpos = lax.broadcasted_iota(jnp.int32, (tq, klen), 1)
                picked = jnp.dot(sel, _expand_blocks(nb, klen), preferred_element_type=F32) > 0.5
                sbias = jnp.where(picked & (kpos <= pos), 0.0, NEG)
                ksg = kv_ref[0:klen, g * hd:(g + 1) * hd]
                vsg = kv_ref[0:klen, LANES + g * hd:LANES + (g + 1) * hd]
                vs1 = jnp.concatenate([vsg, jnp.ones_like(vsg)], axis=1)
                for h in range(NSA_GROUP):
                    hl = slice((g * NSA_GROUP + h) * hd, (g * NSA_GROUP + h + 1) * hd)
                    osel_ref[:, hl] = _attend(q[:, hl], ksg, sbias, vs1, hd)

        kwg = kv_ref[pl.ds(wstart, wslab), 2 * LANES + g * hd:2 * LANES + (g + 1) * hd]
        vwg = kv_ref[pl.ds(wstart, wslab), 3 * LANES + g * hd:3 * LANES + (g + 1) * hd]
        vw1 = jnp.concatenate([vwg, jnp.ones_like(vwg)], axis=1)
        for h in range(NSA_GROUP):
            head = g * NSA_GROUP + h
            hl = slice(head * hd, (head + 1) * hd)
            o_win = _attend(q[:, hl], kwg, wbias, vw1, hd)
            outs.append(_gate_col(gate, head, 0) * o_cmp[h] + _gate_col(gate, head, 1) * osel_ref[:, hl]
                        + _gate_col(gate, head, 2) * o_win)
    o_ref[...] = jnp.concatenate(outs, axis=1).astype(BF16)


def _nsa_prompt(qb, ck, cv, kvb, zd, batch, seq, tq):
    nq = seq // tq
    nb = seq // NSA_BLOCK
    wslab = min(seq, NSA_WINDOW + tq)
    nbr = 4 if nq % 4 == 0 else 1
    kern = functools.partial(_nsa_prompt_kernel, tq=tq, seq=seq, wslab=wslab, nbr=nbr)
    return pl.pallas_call(
        kern,
        grid=(batch, nq),
        in_specs=[pl.BlockSpec((tq, 512), lambda b, i: (b * nq + i, 0)),
                  pl.BlockSpec((nb, LANES), lambda b, i: (b, 0)),
                  pl.BlockSpec((nb, LANES), lambda b, i: (b, 0)),
                  pl.BlockSpec((seq, 512), lambda b, i: (b, 0)),
                  pl.BlockSpec((tq, LANES), lambda b, i: (b * nq + i, 0))],
        out_specs=pl.BlockSpec((tq, 512), lambda b, i: (b * nq + i, 0)),
        out_shape=jax.ShapeDtypeStruct((batch * seq, 512), BF16),
        scratch_shapes=[pltpu.VMEM((tq, 512), F32)],
        compiler_params=_cp(("parallel", "parallel")),
        name="nsa_prompt",
    )(qb, ck, cv, kvb, zd)


def _qbd(q, rows_per_head):
    hd = NSA_BLOCK
    lane_grp = lax.broadcasted_iota(jnp.int32, (rows_per_head, LANES), 1) // hd
    parts = []
    for head in range(NSA_HEADS):
        g = head // NSA_GROUP
        qh = q[:, head * hd:(head + 1) * hd]
        two = jnp.concatenate([qh, qh], axis=1)
        parts.append(jnp.where(lane_grp == g, two, 0.0))
    return jnp.concatenate(parts, axis=0)


def _nsa_sample_cmp_kernel(pt_ref, *refs, npg, t_new, past_len):
    kpages = refs[0:npg]
    vpages = refs[npg:2 * npg]
    q_ref, kn_ref, vn_ref, wt_ref, wn_ref, ocmp_ref, sel_ref, pk_ref, pv_ref = refs[2 * npg:]
    c = pl.program_id(1)
    nch = pl.num_programs(1)
    hd = NSA_BLOCK
    page = kpages[0].shape[1]
    bpp = page // NSA_BLOCK
    nbc = npg * bpp
    nb_pad = pk_ref.shape[1]
    n_past = past_len // NSA_BLOCK

    @pl.when(c == 0)
    def _():
        pk_ref[...] = jnp.zeros(pk_ref.shape, F32)
        pv_ref[...] = jnp.zeros(pv_ref.shape, F32)

    seg = jnp.where(lax.broadcasted_iota(jnp.int32, (npg * page, nbc), 0) // NSA_BLOCK
                    == lax.broadcasted_iota(jnp.int32, (npg * page, nbc), 1), 1.0, 0.0).astype(BF16)
    col = lax.broadcasted_iota(jnp.int32, (nbc, nb_pad), 1)
    place = jnp.where(col == c * nbc + lax.broadcasted_iota(jnp.int32, (nbc, nb_pad), 0), 1.0, 0.0).astype(BF16)
    for pages, w_t, pooled in ((kpages, wt_ref[0], pk_ref), (vpages, wt_ref[1], pv_ref)):
        x = jnp.concatenate([r[...] * w_t for r in pages], axis=1)
        pooled[...] += _dot_exact_rhs(_dot_exact_rhs(x, seg), place)

    @pl.when(c == nch - 1)
    def _():
        newk = jnp.sum(kn_ref[...] * wn_ref[0][0:t_new], axis=0, keepdims=True)
        newv = jnp.sum(vn_ref[...] * wn_ref[1][0:t_new], axis=0, keepdims=True)
        q = q_ref[...]
        rows = NSA_HEADS * t_new
        qbd = _qbd(q, t_new)
        trow = lax.broadcasted_iota(jnp.int32, (rows, 1), 0) % t_new
        pos = past_len + trow
        blk = lax.broadcasted_iota(jnp.int32, (rows, nb_pad), 1)
        vis = ((blk + 1) * NSA_BLOCK <= pos + 1) & (blk < n_past)
        vis_new = (n_past + 1) * NSA_BLOCK <= pos + 1
        s_past = jnp.where(vis, _dot(qbd, pk_ref[...]), NEG)
        s_new = jnp.where(vis_new, jnp.sum(qbd * newk, axis=-1, keepdims=True), NEG)
        m = jnp.maximum(jnp.max(s_past, axis=-1, keepdims=True), s_new)
        e_past = jnp.where(vis, jnp.exp(s_past - m), 0.0)
        e_new = jnp.where(vis_new, jnp.exp(s_new - m), 0.0)
        den = jnp.maximum(jnp.sum(e_past, axis=-1, keepdims=True) + e_new, 1e-30)
        pr = e_past / den
        o = _dot_nt(pr, pv_ref[...]) + (e_new / den) * newv
        outs = []
        for head in range(NSA_HEADS):
            g = head // NSA_GROUP
            outs.append(o[head * t_new:(head + 1) * t_new, g * hd:(g + 1) * hd])
        ocmp_ref[...] = jnp.concatenate(outs, axis=1)
        imps = []
        for g in range(NSA_KV_HEADS):
            acc = pr[g * NSA_GROUP * t_new:(g * NSA_GROUP + 1) * t_new]
            for h in range(1, NSA_GROUP):
                acc = acc + pr[(g * NSA_GROUP + h) * t_new:(g * NSA_GROUP + h + 1) * t_new]
            imps.append(acc)
        imp = jnp.concatenate(imps, axis=0)
        cur = (past_len + lax.broadcasted_iota(jnp.int32, (NSA_KV_HEADS * t_new, 1), 0) % t_new) // NSA_BLOCK
        sel_ref[...] = _select(imp, cur, n_past)


def _nsa_sample_cmp(page_table, pool_k, pool_v, q, kn, vn, wt, wn, t_new, npg):
    db, n_pages = page_table.shape
    page = pool_k.shape[2]
    past_len = n_pages * page
    nch = n_pages // npg
    nb_pad = -(-(past_len // NSA_BLOCK) // LANES) * LANES
    row0 = q.shape[0] // t_new - db

    def page_spec(j):
        return pl.BlockSpec((None, LANES, page), lambda b, c, pt: (pt[b, c * npg + j], 0, 0))

    tok = lambda w: pl.BlockSpec((t_new, w), lambda b, c, pt: (row0 + b, 0))
    full3 = lambda a: pl.BlockSpec(a.shape, lambda b, c, pt: (0, 0, 0))
    kern = functools.partial(_nsa_sample_cmp_kernel, npg=npg, t_new=t_new, past_len=past_len)
    grid_spec = pltpu.PrefetchScalarGridSpec(
        num_scalar_prefetch=1,
        grid=(db, nch),
        in_specs=[page_spec(j) for j in range(npg)] + [page_spec(j) for j in range(npg)]
        + [tok(512), tok(LANES), tok(LANES), full3(wt), full3(wn)],
        out_specs=[pl.BlockSpec((None, t_new, 512), lambda b, c, pt: (b, 0, 0)),
                   pl.BlockSpec((None, NSA_KV_HEADS * t_new, nb_pad), lambda b, c, pt: (b, 0, 0))],
        scratch_shapes=[pltpu.VMEM((LANES, nb_pad), F32), pltpu.VMEM((LANES, nb_pad), F32)],
    )
    return pl.pallas_call(
        kern,
        grid_spec=grid_spec,
        out_shape=[jax.ShapeDtypeStruct((db, t_new, 512), F32),
                   jax.ShapeDtypeStruct((db, NSA_KV_HEADS * t_new, nb_pad), F32)],
        compiler_params=_cp(("parallel", "arbitrary")),
        name="nsa_sample_cmp",
    )(page_table, *([pool_k] * npg), *([pool_v] * npg), q, kn, vn, wt, wn)


def _online_update(s, mask, v, m_ref, l_ref, acc_ref, v_transposed=False):
    s = jnp.where(mask, s, NEG)
    m_old = m_ref[...]
    m_new = jnp.maximum(m_old, jnp.max(s, axis=-1, keepdims=True))
    alpha = jnp.exp(m_old - m_new)
    e = jnp.where(mask, jnp.exp(s - m_new), 0.0)
    l_ref[...] = alpha * l_ref[...] + jnp.sum(e, axis=-1, keepdims=True)
    acc_ref[...] = alpha * acc_ref[...] + (_dot_nt(e, v) if v_transposed else _dot(e, v))
    m_ref[...] = m_new


def _nsa_sample_sel_kernel(pt_ref, *refs, npg, t_new, past_len):
    kpages = refs[0:npg]
    vpages = refs[npg:2 * npg]
    (q_ref, sel_ref, ksn_ref, vsn_ref, wk_ref, wv_ref, kwn_ref, vwn_ref, ocmp_ref, gate_ref,
     o_ref, m_ref, l_ref, acc_ref, pad_ref) = refs[2 * npg:]
    c = pl.program_id(1)
    nch = pl.num_programs(1)
    hd = NSA_BLOCK
    page = kpages[0].shape[1]
    bpp = page // NSA_BLOCK
    nbc = npg * bpp
    rows = NSA_HEADS * t_new
    q = q_ref[...]
    qbd = _qbd(q, t_new)

    @pl.when(c == 0)
    def _():
        m_ref[...] = jnp.full(m_ref.shape, NEG, F32)
        l_ref[...] = jnp.zeros(l_ref.shape, F32)
        acc_ref[...] = jnp.zeros(acc_ref.shape, F32)

    sel = sel_ref[...]
    nb_pad = sel.shape[1]
    sel_rows = jnp.concatenate([sel[(head // NSA_GROUP) * t_new:(head // NSA_GROUP + 1) * t_new]
                                for head in range(NSA_HEADS)], axis=0)
    blk_iota = lax.broadcasted_iota(jnp.int32, (nb_pad, npg * page), 0)
    key_blk = c * nbc + lax.broadcasted_iota(jnp.int32, (nb_pad, npg * page), 1) // NSA_BLOCK
    expand = jnp.where(blk_iota == key_blk, 1.0, 0.0).astype(BF16)
    smask = _dot(sel_rows, expand) > 0.5
    kcat = jnp.concatenate([r[...].astype(BF16) for r in kpages], axis=1)
    vcat = jnp.concatenate([r[...].astype(BF16) for r in vpages], axis=1)
    _online_update(_dot(qbd, kcat), smask, vcat, m_ref, l_ref, acc_ref, v_transposed=True)

    @pl.when(c == nch - 1)
    def _():
        trow = lax.broadcasted_iota(jnp.int32, (rows, 1), 0) % t_new
        pad_ref[...] = jnp.zeros(pad_ref.shape, F32)
        pad_ref[0:t_new, :] = ksn_ref[...]
        kn = pad_ref[...]
        pad_ref[0:t_new, :] = vsn_ref[...]
        vn = pad_ref[...]
        jn = lax.broadcasted_iota(jnp.int32, (rows, LANES), 1)
        _online_update(_dot_nt(qbd, kn), jn <= trow, vn, m_ref, l_ref, acc_ref)
        o_sel = acc_ref[...] / jnp.maximum(l_ref[...], 1e-30)
        wb = wk_ref.shape[1]
        jw = lax.broadcasted_iota(jnp.int32, (rows, wb), 1)
        rel = trow + wb - jw
        wmask = jnp.concatenate([(rel >= 0) & (rel < NSA_WINDOW), jn <= trow], axis=1)
        pad_ref[0:t_new, :] = kwn_ref[...]
        s_win = jnp.concatenate([_dot(qbd, wk_ref[...]), _dot_nt(qbd, pad_ref[...])], axis=1)
        p_win = _msoftmax(s_win, wmask)
        pad_ref[0:t_new, :] = vwn_ref[...]
        o_win = _dot_nt(p_win[:, 0:wb], wv_ref[...]) + _dot(p_win[:, wb:], pad_ref[...])
        gate = gate_ref[...]
        ocmp = ocmp_ref[...]
        outs = []
        for head in range(NSA_HEADS):
            g = head // NSA_GROUP
            r0, r1 = head * t_new, (head + 1) * t_new
            outs.append(_gate_col(gate, head, 0) * ocmp[:, head * hd:(head + 1) * hd]
                        + _gate_col(gate, head, 1) * o_sel[r0:r1, g * hd:(g + 1) * hd]
                        + _gate_col(gate, head, 2) * o_win[r0:r1, g * hd:(g + 1) * hd])
        o_ref[...] = jnp.concatenate(outs, axis=1)


def _nsa_sample_sel(page_table, pool_k, pool_v, q, sel, ksn, vsn, win_k, win_v, kwn, vwn, ocmp, zd, t_new, npg):
    db, n_pages = page_table.shape
    page = pool_k.shape[2]
    past_len = n_pages * page
    nch = n_pages // npg
    row0 = q.shape[0] // t_new - db
    rows = NSA_HEADS * t_new

    def page_spec(j):
        return pl.BlockSpec((None, LANES, page), lambda b, c, pt: (pt[b, c * npg + j], 0, 0))

    tok = lambda w: pl.BlockSpec((t_new, w), lambda b, c, pt: (row0 + b, 0))
    per_seq = lambda a: pl.BlockSpec((None,) + a.shape[1:], lambda b, c, pt: (b, 0, 0))
    kern = functools.partial(_nsa_sample_sel_kernel, npg=npg, t_new=t_new, past_len=past_len)
    grid_spec = pltpu.PrefetchScalarGridSpec(
        num_scalar_prefetch=1,
        grid=(db, nch),
        in_specs=[page_spec(j) for j in range(npg)] + [page_spec(j) for j in range(npg)]
        + [tok(512), per_seq(sel), tok(LANES), tok(LANES), per_seq(win_k), per_seq(win_v), tok(LANES), tok(LANES),
           per_seq(ocmp), tok(LANES)],
        out_specs=pl.BlockSpec((t_new, 512), lambda b, c, pt: (b, 0)),
        scratch_shapes=[pltpu.VMEM((rows, 1), F32), pltpu.VMEM((rows, 1), F32), pltpu.VMEM((rows, LANES), F32),
                        pltpu.VMEM((LANES, LANES), F32)],
    )
    return pl.pallas_call(
        kern,
        grid_spec=grid_spec,
        out_shape=jax.ShapeDtypeStruct((db * t_new, 512), F32),
        compiler_params=_cp(("parallel", "arbitrary")),
        name="nsa_sample_sel",
    )(page_table, *([pool_k] * npg), *([pool_v] * npg), q, sel, ksn, vsn, win_k, win_v, kwn, vwn, ocmp, zd)


def _nsa_sample_kernel(pt_ref, ck_hbm, cv_hbm, sk_hbm, sv_hbm, q_ref, kcn_ref, vcn_ref, ksn_ref, vsn_ref,
                       kwn_ref, vwn_ref, gate_ref, wk_ref, wv_ref, wt_ref, wn_ref, o_ref,
                       buf_ref, sem_ref, pk_ref, pv_ref, m_ref, l_ref, acc_ref, pad_ref, bias_ref,
                       *, npg, t_new, past_len):
    b = pl.program_id(0)
    nseq = pl.num_programs(0)
    hd = NSA_BLOCK
    page = buf_ref.shape[3]
    n_pages = past_len // page
    nch = n_pages // npg
    nbc = npg * page // NSA_BLOCK
    nb_pad = pk_ref.shape[1]
    n_past = past_len // NSA_BLOCK
    rows = NSA_HEADS * t_new
    nkeys = npg * page

    def chunk_copies(seq, c, slot):
        pools = (ck_hbm, cv_hbm) if c < nch else (sk_hbm, sv_hbm)
        first = (c % nch) * npg
        copies = []
        for j in range(npg):
            pg = pt_ref[seq, first + j]
            copies.append(pltpu.make_async_copy(pools[0].at[pg], buf_ref.at[slot, j], sem_ref.at[slot]))
            copies.append(pltpu.make_async_copy(pools[1].at[pg], buf_ref.at[slot, npg + j], sem_ref.at[slot]))
        return copies

    @pl.when(b == 0)
    def _():
        for cp in chunk_copies(0, 0, 0):
            cp.start()

    q = q_ref[...]
    qbd = _qbd(q, t_new)
    trow = lax.broadcasted_iota(jnp.int32, (rows, 1), 0) % t_new
    pos = past_len + trow
    seg = jnp.where(lax.broadcasted_iota(jnp.int32, (nkeys, nbc), 0) // NSA_BLOCK
                    == lax.broadcasted_iota(jnp.int32, (nkeys, nbc), 1), 1.0, 0.0).astype(BF16)
    expand = _expand_blocks(nbc, nkeys)
    if n_past < nb_pad:
        pk_ref[...] = jnp.zeros(pk_ref.shape, F32)
        pv_ref[...] = jnp.zeros(pv_ref.shape, F32)
    m_ref[...] = jnp.full(m_ref.shape, 0.1 * NEG, F32)
    l_ref[...] = jnp.zeros(l_ref.shape, F32)
    acc_ref[...] = jnp.zeros(acc_ref.shape, F32)
    o_cmp = None

    for c in range(2 * nch):
        slot = c % 2
        if c + 1 < 2 * nch:
            for cp in chunk_copies(b, c + 1, 1 - slot):
                cp.start()
        else:
            @pl.when(b + 1 < nseq)
            def _():
                for cp in chunk_copies(b + 1, 0, 1 - slot):
                    cp.start()
        for cp in chunk_copies(b, c, slot):
            cp.wait()

        if c < nch:
            for base, w_t, pooled in ((0, wt_ref[0], pk_ref), (npg, wt_ref[1], pv_ref)):
                x = jnp.concatenate([(buf_ref[slot, base + j] * w_t).astype(BF16) for j in range(npg)], axis=1)
                pooled[:, c * nbc:(c + 1) * nbc] = jnp.dot(x, seg, preferred_element_type=F32)
        else:
            k0 = (c - nch) * nkeys
            kcat = jnp.concatenate([buf_ref[slot, j].astype(BF16) for j in range(npg)], axis=1)
            vcat = jnp.concatenate([buf_ref[slot, npg + j].astype(BF16) for j in range(npg)], axis=1)
            s = _dot(qbd, kcat) + bias_ref[:, k0:k0 + nkeys]
            m_old = m_ref[...]
            m_new = jnp.maximum(m_old, jnp.max(s, axis=-1, keepdims=True))
            alpha = jnp.exp(m_old - m_new)
            e = jnp.exp(s - m_new)
            l_ref[...] = alpha * l_ref[...] + jnp.sum(e, axis=-1, keepdims=True)
            acc_ref[...] = alpha * acc_ref[...] + _dot_nt(e, vcat)
            m_ref[...] = m_new

        if c == nch - 1:
            newk = jnp.sum(kcn_ref[...] * wn_ref[0][0:t_new], axis=0, keepdims=True)
            newv = jnp.sum(vcn_ref[...] * wn_ref[1][0:t_new], axis=0, keepdims=True)
            blk = lax.broadcasted_iota(jnp.int32, (rows, nb_pad), 1)
            vis = ((blk + 1) * NSA_BLOCK <= pos + 1) & (blk < n_past)
            vis_new = (n_past + 1) * NSA_BLOCK <= pos + 1
            s_past = jnp.where(vis, _dot(qbd, pk_ref[...]), NEG)
            s_new = jnp.where(vis_new, jnp.sum(qbd * newk, axis=-1, keepdims=True), NEG)
            mx = jnp.maximum(jnp.max(s_past, axis=-1, keepdims=True), s_new)
            e_past = jnp.where(vis, jnp.exp(s_past - mx), 0.0)
            e_new = jnp.where(vis_new, jnp.exp(s_new - mx), 0.0)
            den = jnp.maximum(jnp.sum(e_past, axis=-1, keepdims=True) + e_new, 1e-30)
            pr = e_past / den
            o_cmp = _dot_nt(pr, pv_ref[...]) + (e_new / den) * newv
            imps = []
            for g in range(NSA_KV_HEADS):
                acc = pr[g * NSA_GROUP * t_new:(g * NSA_GROUP + 1) * t_new]
                for h in range(1, NSA_GROUP):
                    acc = acc + pr[(g * NSA_GROUP + h) * t_new:(g * NSA_GROUP + h + 1) * t_new]
                imps.append(acc)
            imp = jnp.concatenate(imps, axis=0)
            cur = (past_len + lax.broadcasted_iota(jnp.int32, (NSA_KV_HEADS * t_new, 1), 0) % t_new) // NSA_BLOCK
            sel = _select(imp, cur, n_past).astype(BF16)
            sel_rows = jnp.concatenate([sel[(head // NSA_GROUP) * t_new:(head // NSA_GROUP + 1) * t_new]
                                        for head in range(NSA_HEADS)], axis=0)
            for cc in range(nch):
                picked = jnp.dot(sel_rows[:, cc * nbc:(cc + 1) * nbc], expand, preferred_element_type=F32) > 0.5
                bias_ref[:, cc * nkeys:(cc + 1) * nkeys] = jnp.where(picked, 0.0, NEG)

    pad_ref[...] = jnp.zeros(pad_ref.shape, F32)
    pad_ref[0:t_new, :] = ksn_ref[...]
    kn = pad_ref[...]
    pad_ref[0:t_new, :] = vsn_ref[...]
    vn = pad_ref[...]
    jn = lax.broadcasted_iota(jnp.int32, (rows, LANES), 1)
    _online_update(_dot_nt(qbd, kn), jn <= trow, vn, m_ref, l_ref, acc_ref)
    o_sel = acc_ref[...] / jnp.maximum(l_ref[...], 1e-30)
    wb = wk_ref.shape[1]
    jw = lax.broadcasted_iota(jnp.int32, (rows, wb), 1)
    rel = trow + wb - jw
    wmask = jnp.concatenate([(rel >= 0) & (rel < NSA_WINDOW), jn <= trow], axis=1)
    pad_ref[0:t_new, :] = kwn_ref[...]
    s_win = jnp.concatenate([_dot(qbd, wk_ref[...]), _dot_nt(qbd, pad_ref[...])], axis=1)
    p_win = _msoftmax(s_win, wmask)
    pad_ref[0:t_new, :] = vwn_ref[...]
    o_win = _dot_nt(p_win[:, 0:wb], wv_ref[...]) + _dot(p_win[:, wb:], pad_ref[...])
    gate = gate_ref[...]
    outs = []
    for head in range(NSA_HEADS):
        g = head // NSA_GROUP
        r0, r1 = head * t_new, (head + 1) * t_new
        gl = slice(g * hd, (g + 1) * hd)
        outs.append(_gate_col(gate, head, 0) * o_cmp[r0:r1, gl] + _gate_col(gate, head, 1) * o_sel[r0:r1, gl]
                    + _gate_col(gate, head, 2) * o_win[r0:r1, gl])
    o_ref[...] = jnp.concatenate(outs, axis=1)


def _nsa_sample(page_table, pool_ck, pool_cv, pool_sk, pool_sv, q, kc, vc, ks, vs, kw, vw, zd, win_k, win_v,
                wt, wn, t_new, npg):
    db, n_pages = page_table.shape
    page = pool_ck.shape[2]
    past_len = n_pages * page
    nb_pad = -(-(past_len // NSA_BLOCK) // LANES) * LANES
    row0 = q.shape[0] // t_new - db
    rows = NSA_HEADS * t_new
    assert n_pages % npg == 0 and (2 * n_pages // npg) % 2 == 0
    tok = lambda w: pl.BlockSpec((t_new, w), lambda b, pt: (row0 + b, 0))
    per_seq = lambda a: pl.BlockSpec((None,) + a.shape[1:], lambda b, pt: (b, 0, 0))
    full3 = lambda a: pl.BlockSpec(a.shape, lambda b, pt: (0, 0, 0))
    hbm = pl.BlockSpec(memory_space=pl.ANY)
    kern = functools.partial(_nsa_sample_kernel, npg=npg, t_new=t_new, past_len=past_len)
    grid_spec = pltpu.PrefetchScalarGridSpec(
        num_scalar_prefetch=1,
        grid=(db,),
        in_specs=[hbm, hbm, hbm, hbm, tok(512)] + [tok(LANES)] * 7 + [per_seq(win_k), per_seq(win_v),
                                                                      full3(wt), full3(wn)],
        out_specs=pl.BlockSpec((t_new, 512), lambda b, pt: (b, 0)),
        scratch_shapes=[pltpu.VMEM((2, 2 * npg, LANES, page), F32), pltpu.SemaphoreType.DMA((2,)),
                        pltpu.VMEM((LANES, nb_pad), F32), pltpu.VMEM((LANES, nb_pad), F32),
                        pltpu.VMEM((rows, 1), F32), pltpu.VMEM((rows, 1), F32), pltpu.VMEM((rows, LANES), F32),
                        pltpu.VMEM((LANES, LANES), F32), pltpu.VMEM((rows, past_len), F32)],
    )
    return pl.pallas_call(
        kern,
        grid_spec=grid_spec,
        out_shape=jax.ShapeDtypeStruct((db * t_new, 512), F32),
        compiler_params=_cp(("arbitrary",)),
        name="nsa_sample",
    )(page_table, pool_ck, pool_cv, pool_sk, pool_sv, q, kc, vc, ks, vs, kw, vw, zd, win_k, win_v, wt, wn)


def _log_sigmoid(x):
    return jnp.minimum(x, 0.0) - jnp.log(1.0 + jnp.exp(-jnp.abs(x)))


def _gla_kernel(za_ref, zd_ref, wgh_ref, wgl_ref, bg_ref, gout_ref, s0_ref, o_ref, s_ref,
                st_ref, kp_ref, bp_ref, vp_ref, za_pad_ref, zd_pad_ref, *, nchunk, t_valid):
    j = pl.program_id(1)
    short = za_ref.shape[0] < GLA_CHUNK
    nj = pl.num_programs(1)
    C = GLA_CHUNK
    sub = GLA_SUB
    nsub = C // sub
    dk = 64
    dv = 128
    hk = GLA_HEADS * dk
    hv = GLA_HEADS * dv

    @pl.when(j == 0)
    def _():
        st_ref[...] = jnp.concatenate([s0_ref[h].T for h in range(GLA_HEADS)], axis=1)
        kp_ref[...] = jnp.zeros(kp_ref.shape, F32)
        bp_ref[...] = jnp.zeros(bp_ref.shape, F32)
        vp_ref[...] = jnp.zeros(vp_ref.shape, F32)

    tril = jnp.where(lax.broadcasted_iota(jnp.int32, (C, C), 0) >= lax.broadcasted_iota(jnp.int32, (C, C), 1),
                     1.0, 0.0).astype(BF16)
    head_ones = jnp.where(lax.broadcasted_iota(jnp.int32, (hk, LANES), 0) // dk
                          == lax.broadcasted_iota(jnp.int32, (hk, LANES), 1), 1.0, 0.0).astype(BF16)
    tmod = lax.broadcasted_iota(jnp.int32, (C, 1), 0) % sub
    gout = gout_ref[...]

    def chunk(c, carry):
        r0 = pl.multiple_of(c * C, C)
        if short:
            za_pad_ref[...] = jnp.zeros(za_pad_ref.shape, F32)
            zd_pad_ref[...] = jnp.zeros(zd_pad_ref.shape, F32)
            za_pad_ref[0:za_ref.shape[0], :] = za_ref[...]
            zd_pad_ref[0:zd_ref.shape[0], :] = zd_ref[...]
            z = za_pad_ref[...]
            zd = zd_pad_ref[...]
        else:
            z = za_ref[pl.ds(r0, C), :]
            zd = zd_ref[pl.ds(r0, C), :]
        q = z[:, 0:hk] * (dk ** -0.5)
        k = z[:, hk:2 * hk]
        v = z[:, 2 * hk:2 * hk + hv]
        r = z[:, 2 * hk + hv:2 * hk + 2 * hv]
        lr = zd[:, 0:GLA_GATE_RANK]
        la = _log_sigmoid(_dot_hi(lr, wgh_ref[...], wgl_ref[...]) + bg_ref[...]) * (1.0 / GLA_TAU)
        tglob = (j * nchunk + c) * C + lax.broadcasted_iota(jnp.int32, (C, 1), 0)
        la = jnp.where(tglob < t_valid, la, 0.0)
        b = _dot_exact_lhs(tril, la)
        st = st_ref[...]
        qe = q * jnp.exp(b)
        kp_ref[sub:sub + C, :] = k
        bp_ref[sub:sub + C, :] = b
        vp_ref[sub:sub + C, :] = v
        xs = []
        for d in range(sub):
            kd = kp_ref[sub - d:sub - d + C, :]
            bd = bp_ref[sub - d:sub - d + C, :]
            xs.append(jnp.where(tmod >= d, q * kd * jnp.exp(b - bd), 0.0).astype(BF16))
        rr = jnp.dot(jnp.concatenate(xs, axis=0), head_ones, preferred_element_type=F32)
        outs = []
        for h in range(GLA_HEADS):
            kh = slice(h * dk, (h + 1) * dk)
            vh = slice(h * dv, (h + 1) * dv)
            o = _dot_nt(qe[:, kh], st[:, kh])
            for d in range(sub):
                o = o + rr[d * C:(d + 1) * C, h:h + 1] * vp_ref[sub - d:sub - d + C, vh]
            offs = [jnp.zeros((sub, dv), F32)]
            for i in range(1, nsub):
                anchor = b[i * sub - 1:i * sub, kh]
                qt = q[i * sub:(i + 1) * sub, kh] * jnp.exp(b[i * sub:(i + 1) * sub, kh] - anchor)
                kt = k[0:i * sub, kh] * jnp.exp(anchor - b[0:i * sub, kh])
                offs.append(_dot(_dot_nt(qt, kt), v[0:i * sub, vh]))
            o = o + jnp.concatenate(offs, axis=0)
            on = o * lax.rsqrt(jnp.mean(o * o, axis=-1, keepdims=True) + EPS) * gout
            rh = r[:, vh]
            outs.append(on * (rh * _sigmoid(rh)))
        o_all = jnp.concatenate(outs, axis=1)
        if short:
            o_ref[...] = o_all[0:o_ref.shape[0], :].astype(o_ref.dtype)
        else:
            o_ref[pl.ds(r0, C), :] = o_all.astype(o_ref.dtype)
        b_last = b[C - 1:C, :]
        kk = k * jnp.exp(b_last - b)
        upd = jnp.concatenate([_dot_tn(v[:, h * dv:(h + 1) * dv], kk[:, h * dk:(h + 1) * dk])
                               for h in range(GLA_HEADS)], axis=1)
        st_ref[...] = st * jnp.exp(b_last) + upd
        return carry

    lax.fori_loop(0, nchunk, chunk, 0)

    @pl.when(j == nj - 1)
    def _():
        st = st_ref[...]
        for h in range(GLA_HEADS):
            s_ref[h] = st[:, h * dk:(h + 1) * dk].T


def _gla(za, zd, wgh, wgl, bg, gout, s0, batch, t_seq, ct, row0, out_dtype):
    nj = t_seq // ct
    blk0 = row0 // ct
    hv = GLA_HEADS * 128
    kern = functools.partial(_gla_kernel, nchunk=max(1, ct // GLA_CHUNK), t_valid=t_seq)
    full = lambda a: pl.BlockSpec(a.shape, lambda b, j: (0,) * a.ndim)
    return pl.pallas_call(
        kern,
        grid=(batch, nj),
        in_specs=[pl.BlockSpec((ct, za.shape[1]), lambda b, j: (blk0 + b * nj + j, 0)),
                  pl.BlockSpec((ct, LANES), lambda b, j: (blk0 + b * nj + j, 0)),
                  full(wgh), full(wgl), full(bg), full(gout),
                  pl.BlockSpec((None,) + s0.shape[1:], lambda b, j: (b, 0, 0, 0))],
        out_specs=[pl.BlockSpec((ct, hv), lambda b, j: (b * nj + j, 0)),
                   pl.BlockSpec((None,) + s0.shape[1:], lambda b, j: (b, 0, 0, 0))],
        out_shape=[jax.ShapeDtypeStruct((batch * t_seq, hv), out_dtype), jax.ShapeDtypeStruct(s0.shape, F32)],
        scratch_shapes=[pltpu.VMEM((128, GLA_HEADS * 64), F32),
                        pltpu.VMEM((GLA_SUB + GLA_CHUNK, GLA_HEADS * 64), F32),
                        pltpu.VMEM((GLA_SUB + GLA_CHUNK, GLA_HEADS * 64), F32),
                        pltpu.VMEM((GLA_SUB + GLA_CHUNK, hv), F32),
                        pltpu.VMEM((GLA_CHUNK, za.shape[1]), F32),
                        pltpu.VMEM((GLA_CHUNK, LANES), F32)],
        compiler_params=_cp(("parallel", "arbitrary")),
        name="gla",
    )(za, zd, wgh, wgl, bg, gout, s0)


def _mem_kv_kernel(m_ref, g_ref, w_ref, gk_ref, k_ref, v_ref):
    x = m_ref[...]
    u = x * lax.rsqrt(jnp.mean(x * x, axis=-1, keepdims=True) + EPS) * g_ref[...]
    kv = jnp.dot(u.astype(BF16), w_ref[...], preferred_element_type=F32)
    half = kv.shape[1] // 2
    k_ref[...] = _seg_rms(kv[:, 0:half], half // MEM_HEADS, gk_ref[...])
    v_ref[...] = kv[:, half:]


def _mem_kv(mem, g, w, gk, tm):
    n, d = mem.shape
    half = w.shape[1] // 2
    full = lambda a: pl.BlockSpec(a.shape, lambda i: (0, 0))
    sd = jax.ShapeDtypeStruct((n, half), F32)
    return pl.pallas_call(
        _mem_kv_kernel,
        grid=(n // tm,),
        in_specs=[pl.BlockSpec((tm, d), lambda i: (i, 0)), full(g), full(w), full(gk)],
        out_specs=[pl.BlockSpec((tm, half), lambda i: (i, 0))] * 2,
        out_shape=[sd, sd],
        compiler_params=_cp(("parallel",)),
        name="mem_kv",
    )(mem, g, w, gk)


def _mem_prompt_kernel(q_ref, k_ref, v_ref, o_ref):
    q = q_ref[...]
    k = k_ref[...].astype(BF16)
    v = v_ref[...].astype(BF16)
    hd = q.shape[1] // MEM_HEADS
    outs = []
    for h in range(MEM_HEADS):
        sl = slice(h * hd, (h + 1) * hd)
        s = _dot_nt(q[:, sl], k[:, sl])
        m = jnp.max(s, axis=-1, keepdims=True)
        e = jnp.exp(s - m)
        outs.append(_dot(e / jnp.sum(e, axis=-1, keepdims=True), v[:, sl]))
    o_ref[...] = jnp.concatenate(outs, axis=1).astype(BF16)


def _mem_prompt(qm, mk, mv, batch, seq, mlen, tq):
    nq = seq // tq
    w = qm.shape[1]
    return pl.pallas_call(
        _mem_prompt_kernel,
        grid=(batch, nq),
        in_specs=[pl.BlockSpec((tq, w), lambda b, i: (b * nq + i, 0)),
                  pl.BlockSpec((mlen, w), lambda b, i: (b, 0)),
                  pl.BlockSpec((mlen, w), lambda b, i: (b, 0))],
        out_specs=pl.BlockSpec((tq, w), lambda b, i: (b * nq + i, 0)),
        out_shape=jax.ShapeDtypeStruct((batch * seq, w), BF16),
        compiler_params=_cp(("parallel", "parallel")),
        name="mem_prompt",
    )(qm, mk, mv)


def _mem_sample_kernel(q_ref, k_ref, v_ref, o_ref, *, sb, t_new):
    w = q_ref.shape[1]
    hd = w // MEM_HEADS
    rows = MEM_HEADS * t_new
    lane_head = lax.broadcasted_iota(jnp.int32, (t_new, w), 1) // hd
    for s_i in range(sb):
        q = q_ref[s_i * t_new:(s_i + 1) * t_new, :].astype(F32)
        qbd = jnp.concatenate([jnp.where(lane_head == h, q, 0.0) for h in range(MEM_HEADS)], axis=0)
        k = k_ref[s_i].astype(BF16)
        v = v_ref[s_i].astype(BF16)
        s = _dot_nt(qbd, k)
        m = jnp.max(s, axis=-1, keepdims=True)
        e = jnp.exp(s - m)
        o = _dot(e / jnp.sum(e, axis=-1, keepdims=True), v)
        o_ref[s_i * t_new:(s_i + 1) * t_new, :] = jnp.concatenate(
            [o[h * t_new:(h + 1) * t_new, h * hd:(h + 1) * hd] for h in range(MEM_HEADS)], axis=1).astype(BF16)


def _mem_sample(qm, ck, cv, row0_blocks, db, t_new, sb):
    w = qm.shape[1]
    mlen = ck.shape[1]
    kern = functools.partial(_mem_sample_kernel, sb=sb, t_new=t_new)
    return pl.pallas_call(
        kern,
        grid=(db // sb,),
        in_specs=[pl.BlockSpec((sb * t_new, w), lambda i: (row0_blocks + i, 0)),
                  pl.BlockSpec((sb, mlen, w), lambda i: (i, 0, 0)),
                  pl.BlockSpec((sb, mlen, w), lambda i: (i, 0, 0))],
        out_specs=pl.BlockSpec((sb * t_new, w), lambda i: (i, 0)),
        out_shape=jax.ShapeDtypeStruct((db * t_new, w), BF16),
        compiler_params=_cp(("parallel",)),
        name="mem_sample",
    )(qm, ck, cv)


def _merge_kernel(x_ref, og_ref, on_ref, om_ref, gate_ref, wg_ref, wn_ref, wm_ref, wo_ref, gf_ref,
                  wrh_ref, wrl_ref, br_ref, h_ref, hn_ref, cmb_ref, *, n_experts, n_tiles):
    @pl.when(pl.program_id(0) >= n_tiles)
    def _():
        h_ref[...] = jnp.zeros(h_ref.shape, h_ref.dtype)
        hn_ref[...] = jnp.zeros(hn_ref.shape, hn_ref.dtype)
        cmb_ref[...] = jnp.zeros(cmb_ref.shape, cmb_ref.dtype)

    @pl.when(pl.program_id(0) < n_tiles)
    def _():
        _merge_body(x_ref, og_ref, on_ref, om_ref, gate_ref, wg_ref, wn_ref, wm_ref, wo_ref, gf_ref,
                    wrh_ref, wrl_ref, br_ref, h_ref, hn_ref, cmb_ref, n_experts)


def _merge_body(x_ref, og_ref, on_ref, om_ref, gate_ref, wg_ref, wn_ref, wm_ref, wo_ref, gf_ref,
                wrh_ref, wrl_ref, br_ref, h_ref, hn_ref, cmb_ref, n_experts):
    d = x_ref.shape[1]
    gate = gate_ref[...].astype(F32)
    y = (gate[:, 0:d] * jnp.dot(og_ref[...], wg_ref[...], preferred_element_type=F32)
         + gate[:, d:2 * d] * jnp.dot(on_ref[...], wn_ref[...], preferred_element_type=F32)
         + gate[:, 2 * d:3 * d] * jnp.dot(om_ref[...], wm_ref[...], preferred_element_type=F32))
    h = x_ref[...] + jnp.dot(y.astype(BF16), wo_ref[...], preferred_element_type=F32)
    h_ref[...] = h
    hn = h * lax.rsqrt(jnp.mean(h * h, axis=-1, keepdims=True) + EPS) * gf_ref[...]
    hn_ref[...] = hn.astype(BF16)
    logits = _dot_hi(hn, wrh_ref[...], wrl_ref[...]) + br_ref[...]
    lane = lax.broadcasted_iota(jnp.int32, logits.shape, 1).astype(F32)
    work = jnp.where(lane < n_experts, logits, NEG)
    top = jnp.max(work, axis=-1, keepdims=True)
    chosen = jnp.zeros(logits.shape, F32)
    for _ in range(TOP_K):
        mx = jnp.max(work, axis=-1, keepdims=True)
        first = jnp.min(jnp.where(work == mx, lane, float(LANES)), axis=-1, keepdims=True)
        pick = lane == first
        chosen = jnp.where(pick, 1.0, chosen)
        work = jnp.where(pick, NEG, work)
    e = jnp.where(chosen > 0.5, jnp.exp(logits - top), 0.0)
    cmb_ref[...] = e / jnp.sum(e, axis=-1, keepdims=True)


def _merge(x, og, on, om, gate, wg, wn, wm, wo, gf, wrh, wrl, br, n_experts, tm, n_pad):
    n, d = x.shape
    n_tiles = n // tm
    row = lambda a: pl.BlockSpec((tm, a.shape[1]), lambda i: (jnp.minimum(i, n_tiles - 1), 0))
    full = lambda a: pl.BlockSpec(a.shape, lambda i: (0, 0))
    kern = functools.partial(_merge_kernel, n_experts=n_experts, n_tiles=n_tiles)
    return pl.pallas_call(
        kern,
        grid=(n_pad // tm,),
        in_specs=[row(x), row(og), row(on), row(om), row(gate), full(wg), full(wn), full(wm), full(wo), full(gf),
                  full(wrh), full(wrl), full(br)],
        out_specs=[pl.BlockSpec((tm, d), lambda i: (i, 0)), pl.BlockSpec((tm, d), lambda i: (i, 0)),
                   pl.BlockSpec((tm, LANES), lambda i: (i, 0))],
        out_shape=[jax.ShapeDtypeStruct((n_pad, d), F32), jax.ShapeDtypeStruct((n_pad, d), BF16),
                   jax.ShapeDtypeStruct((n_pad, LANES), F32)],
        compiler_params=_cp(("parallel",)),
        name="merge_router",
    )(x, og, on, om, gate, wg, wn, wm, wo, gf, wrh, wrl, br)


MOE_SUBTILE = 1024
MOE_SUBTILES = 2
MOE_ROW_BLOCK = 160


SPLIT_COLS = 512


def _split_gate_up_kernel(w_ref, wg_ref, wu_ref):
    half = SPLIT_COLS // 2
    r = lax.broadcasted_iota(jnp.int32, (SPLIT_COLS, half), 0)
    c = lax.broadcasted_iota(jnp.int32, (SPLIT_COLS, half), 1)
    even = jnp.where(r == 2 * c, 1.0, 0.0).astype(BF16)
    odd = jnp.where(r == 2 * c + 1, 1.0, 0.0).astype(BF16)
    for t in range(w_ref.shape[1] // SPLIT_COLS):
        blk = w_ref[:, t * SPLIT_COLS:(t + 1) * SPLIT_COLS].astype(BF16)
        wg_ref[:, t * half:(t + 1) * half] = jnp.dot(blk, even, preferred_element_type=F32).astype(BF16)
        wu_ref[:, t * half:(t + 1) * half] = jnp.dot(blk, odd, preferred_element_type=F32).astype(BF16)


def _split_gate_up(w):
    ne, d, dff2 = w.shape
    sd = jax.ShapeDtypeStruct((ne, d, dff2 // 2), BF16)
    out = pl.BlockSpec((None, d, dff2 // 2), lambda e: (e, 0, 0))
    return pl.pallas_call(
        _split_gate_up_kernel,
        grid=(ne,),
        in_specs=[pl.BlockSpec((None, d, dff2), lambda e: (e, 0, 0))],
        out_specs=[out, out],
        out_shape=[sd, sd],
        compiler_params=_cp(("parallel",)),
        name="split_gate_up",
    )(w)


def _moe_kernel(hn_ref, h_ref, cmb_ref, wg_ref, wu_ref, bg_ref, bu_ref, wd_ref, bd_ref, y_ref,
                rank_ref, rankt_ref, cmbt_ref, cnt_ref, *, ts):
    e = pl.program_id(1)
    ns = hn_ref.shape[0] // ts
    rb = MOE_ROW_BLOCK

    @pl.when(e == 0)
    def _():
        y_ref[...] = h_ref[...]
        lower = jnp.where(lax.broadcasted_iota(jnp.int32, (ts, ts), 0) > lax.broadcasted_iota(jnp.int32, (ts, ts), 1),
                          1.0, 0.0).astype(BF16)
        for s in range(ns):
            cmb = cmb_ref[s * ts:(s + 1) * ts, :]
            sel = jnp.where(cmb > 0.0, 1.0, 0.0)
            rank = jnp.where(sel > 0.0, jnp.dot(lower, sel.astype(BF16), preferred_element_type=F32), -1.0)
            rank_ref[s] = rank
            rankt_ref[s] = rank.T
            cmbt_ref[s] = cmb.T
            cnt_ref[s] = jnp.broadcast_to(jnp.sum(sel, axis=0, keepdims=True), cnt_ref.shape[1:])

    lane = lax.broadcasted_iota(jnp.int32, (1, LANES), 1)
    pick = jnp.where(lax.broadcasted_iota(jnp.int32, (LANES, LANES), 0) == e, 1.0, 0.0).astype(BF16)
    n_iter = 0
    rank_cols = []
    for s in range(ns):
        cnt = jnp.sum(jnp.where(lane == e, cnt_ref[s, 0:1, :], 0.0)).astype(jnp.int32)
        n_iter = jnp.maximum(n_iter, (cnt + rb - 1) // rb)
        r = rank_ref[s]
        r_hi = r.astype(BF16)
        r_lo = (r - r_hi.astype(F32)).astype(BF16)
        rc = (jnp.dot(r_hi, pick, preferred_element_type=F32)
              + jnp.dot(r_lo, pick, preferred_element_type=F32))
        rank_cols.append(jnp.concatenate([rc] * (-(-rb // LANES)), axis=1)[:, 0:rb])

    def body(i, carry):
        r0 = i * rb
        rows = (r0 + lax.broadcasted_iota(jnp.int32, (rb, ts), 0)).astype(F32)
        xs, ws = [], []
        for s in range(ns):
            hit = rankt_ref[s, pl.ds(e, 1), :] == rows
            xs.append(jnp.dot(jnp.where(hit, 1.0, 0.0).astype(BF16), hn_ref[s * ts:(s + 1) * ts, :],
                              preferred_element_type=F32).astype(BF16))
            ws.append(jnp.sum(jnp.where(hit, cmbt_ref[s, pl.ds(e, 1), :], 0.0), axis=-1, keepdims=True))
        x = jnp.concatenate(xs, axis=0)
        gate = jnp.minimum(jnp.dot(x, wg_ref[...], preferred_element_type=F32) + bg_ref[...], SWIGLU_LIMIT)
        up = jnp.clip(jnp.dot(x, wu_ref[...], preferred_element_type=F32) + bu_ref[...], -SWIGLU_LIMIT, SWIGLU_LIMIT)
        act = (up + 1.0) * gate * _sigmoid(SWIGLU_ALPHA * gate)
        out = jnp.dot(act.astype(BF16), wd_ref[...], preferred_element_type=F32) + bd_ref[...]
        cols = (r0 + lax.broadcasted_iota(jnp.int32, (ts, rb), 1)).astype(F32)
        for s in range(ns):
            yw = (out[s * rb:(s + 1) * rb, :] * ws[s]).astype(BF16)
            pt = jnp.where(rank_cols[s] == cols, 1.0, 0.0).astype(BF16)
            y_ref[s * ts:(s + 1) * ts, :] += jnp.dot(pt, yw, preferred_element_type=F32)
        return carry

    lax.fori_loop(0, n_iter, body, 0)


def _moe(hn, h, cmb, wg, wu, bg, bu, wd, bd, ts, ns):
    n, d = h.shape
    ne, _, dff = wg.shape
    tt = ts * ns
    once = lambda w: pl.BlockSpec((tt, w), lambda i, e: (i, 0), pipeline_mode=pl.Buffered(1))
    per_expert = lambda a: pl.BlockSpec((None,) + a.shape[1:], lambda i, e: (e, 0, 0))
    return pl.pallas_call(
        functools.partial(_moe_kernel, ts=ts),
        grid=(n // tt, ne),
        in_specs=[once(d), once(d), once(LANES), per_expert(wg), per_expert(wu), per_expert(bg), per_expert(bu),
                  per_expert(wd), per_expert(bd)],
        out_specs=pl.BlockSpec((tt, d), lambda i, e: (i, 0)),
        out_shape=jax.ShapeDtypeStruct((n, d), F32),
        scratch_shapes=[pltpu.VMEM((ns, ts, LANES), F32), pltpu.VMEM((ns, LANES, ts), F32),
                        pltpu.VMEM((ns, LANES, ts), F32), pltpu.VMEM((ns, 8, LANES), F32)],
        compiler_params=_cp(("parallel", "arbitrary")),
        name="moe",
    )(hn, h, cmb, wg, wu, bg, bu, wd, bd)


def _rope_table(pos):
    half = 8
    inv = 1.0 / (ROPE_THETA ** (jnp.arange(half, dtype=F32) / half))
    ang = pos.astype(F32)[:, None] * inv[None, :]
    cos, sin = jnp.cos(ang), jnp.sin(ang)
    n = pos.shape[0]
    one = jnp.ones((n, 64 - 2 * half), F32)
    zero8 = jnp.zeros((n, half), F32)
    zero = jnp.zeros((n, 64 - 2 * half), F32)
    c = jnp.concatenate([cos, cos, one], axis=1)
    s1 = jnp.concatenate([-sin, zero8, zero], axis=1)
    s2 = jnp.concatenate([zero8, sin, zero], axis=1)
    return jnp.concatenate([c, c, s1, s1, s2, s2], axis=1)


def _hi_lo(w):
    hi = w.astype(BF16)
    return hi, (w - hi.astype(F32)).astype(BF16)


def kernel(x_prompt, x_sample, cache_cmp_k, cache_cmp_v, cache_sel_k, cache_sel_v, cache_win_k, cache_win_v, state_gla, cache_mem_k, cache_mem_v, page_table, mem_prompt, g_attn, w_in, w_gla_gate, b_gla_gate, g_gla_out, g_q_nsa, g_k_nsa, w_cmp_pos, g_q_mem, g_k_mem, g_mem, w_mem_kv, w_up_gla, w_up_nsa, w_up_mem, w_out, g_ffn, w_router, b_router, w_gate_up, b_gate_up, w_down, b_down):
    B, S, D = x_prompt.shape
    DB, T, _ = x_sample.shape
    n_pool, page = cache_cmp_k.shape[1:3]
    n_pages = page_table.shape[1]
    past_len = n_pages * page
    wb = cache_win_k.shape[2]
    mlen = mem_prompt.shape[1]
    ne = w_router.shape[2]
    Np, Ns = B * S, DB * T
    N = Np + Ns
    TM = 256
    assert cache_cmp_k.shape[0] == 1 and D == 1024 and S % TM == 0 and Ns % TM == 0 and TM % T == 0
    assert T <= NSA_BLOCK and past_len % NSA_BLOCK == 0 and S % NSA_BLOCK == 0

    gq, gk, gv, gr, glr, nq, nkv, ng, mq, mg = np.cumsum(
        [0, 256, 256, 512, 512, GLA_GATE_RANK, 512, 768, 3 * NSA_HEADS, 512]).tolist()
    w = w_in[0]
    wa = w[:, gq:glr].astype(BF16)
    wbm = jnp.concatenate([w[:, nq:ng], w[:, mq:mg]], axis=1).astype(BF16)
    wc = w[:, mg:].astype(BF16)
    wd = jnp.concatenate([w[:, glr:nq], w[:, ng:mq], jnp.zeros((D, LANES - GLA_GATE_RANK - 3 * NSA_HEADS), F32)],
                         axis=1).astype(BF16)

    x_all = jnp.concatenate([x_prompt.reshape(Np, D), x_sample.reshape(Ns, D)], axis=0)
    za, zb, zc, zd = _project(x_all, g_attn, wa, wbm, wc, wd, TM)

    pos_sample = past_len + jnp.arange(T, dtype=jnp.int32)
    tab = jnp.concatenate([_rope_table(jnp.arange(S, dtype=jnp.int32)),
                           jnp.tile(_rope_table(pos_sample), (TM // T, 1))], axis=0)
    gq_t = jnp.tile(g_q_nsa[0], NSA_HEADS)[None, :]
    gk_t = jnp.tile(g_k_nsa[0], (1, NSA_KV_HEADS))
    gm_t = jnp.tile(g_q_mem[0], MEM_HEADS)[None, :]
    q, kc, vc, ks, vs, kw, vw, qm, qb, kvb = _prep(zb, tab, gq_t, gk_t, gm_t, TM, Np // TM, S // TM)

    w2 = jnp.tile(w_cmp_pos[0], (1, 1, NSA_KV_HEADS))
    ck, cv = _compress(kc, vc, w2, Np, 512)
    o_nsa_p = _nsa_prompt(qb, ck, cv, kvb, zd, B, S, 128)

    rows_minor = lambda a: jnp.transpose(a[0], (0, 2, 3, 1)).reshape(a.shape[1], LANES, a.shape[2])
    pool = rows_minor
    npg = 32 if n_pages % 64 == 0 else n_pages // 2
    wt = jnp.tile(jnp.transpose(w2, (0, 2, 1)), (1, 1, page // NSA_BLOCK))
    o_nsa_s = _nsa_sample(page_table, pool(cache_cmp_k), pool(cache_cmp_v), pool(cache_sel_k), pool(cache_sel_v),
                          q, kc, vc, ks, vs, kw, vw, zd, rows_minor(cache_win_k), rows_minor(cache_win_v),
                          wt, w2, T, npg)
    o_nsa = jnp.concatenate([o_nsa_p, o_nsa_s.astype(BF16)], axis=0)

    wgh, wgl = _hi_lo(w_gla_gate[0])
    bg = b_gla_gate[0][None, :]
    gout = g_gla_out[0][None, :]
    s0_p = jnp.zeros((B,) + state_gla.shape[2:], F32)
    o_gla_p, s_gla_p = _gla(za, zd, wgh, wgl, bg, gout, s0_p, B, S, 512, 0, BF16)
    o_gla_s, s_gla_s = _gla(za, zd, wgh, wgl, bg, gout, state_gla[0], DB, T, T, Np, F32)
    o_gla = jnp.concatenate([o_gla_p, o_gla_s.astype(BF16)], axis=0)

    gkm = jnp.tile(g_k_mem[0], MEM_HEADS)[None, :]
    mem_k, mem_v = _mem_kv(mem_prompt.reshape(B * mlen, D), g_mem, w_mem_kv[0].astype(BF16), gkm, TM)
    o_mem_p = _mem_prompt(qm, mem_k, mem_v, B, S, mlen, 512)
    mw = MEM_HEADS * cache_mem_k.shape[-1]
    sb = 8
    o_mem_s = _mem_sample(qm, cache_mem_k[0].reshape(DB, mlen, mw), cache_mem_v[0].reshape(DB, mlen, mw),
                          Np // (sb * T), DB, T, sb)
    o_mem = jnp.concatenate([o_mem_p, o_mem_s], axis=0)

    moe_ts = MOE_SUBTILE if N % MOE_SUBTILE == 0 else TM
    n_pad = -(-N // (moe_ts * MOE_SUBTILES)) * (moe_ts * MOE_SUBTILES)
    wr = jnp.pad(w_router[0], ((0, 0), (0, LANES - ne)))
    wrh, wrl = _hi_lo(wr)
    br = jnp.pad(b_router[0], (0, LANES - ne))[None, :]
    h, hn, cmb = _merge(x_all, o_gla, o_nsa, o_mem, zc, w_up_gla[0].astype(BF16), w_up_nsa[0].astype(BF16),
                        w_up_mem[0].astype(BF16), w_out[0].astype(BF16), g_ffn, wrh, wrl, br, ne, TM, n_pad)

    wg, wu = _split_gate_up(w_gate_up[0])
    y = _moe(hn, h, cmb, wg, wu, b_gate_up[0][:, None, 0::2], b_gate_up[0][:, None, 1::2],
             w_down[0].astype(BF16), b_down[0][:, None, :], moe_ts, MOE_SUBTILES)

    kvshape = (1, B, S, NSA_KV_HEADS, NSA_BLOCK)
    p_rows = lambda a: a[:Np].reshape(kvshape)
    s_rows = lambda a: a[Np:].reshape(1, DB, T, NSA_KV_HEADS, NSA_BLOCK)
    wbp = min(NSA_WINDOW, S)
    s_win = lambda cache, new: jnp.concatenate(
        [cache[:, :, T:], new[Np:].reshape(1, DB, T, NSA_KV_HEADS, NSA_BLOCK)], axis=2)
    mshape = (1, B, mlen, MEM_HEADS, mw // MEM_HEADS)
    return (y[:Np].reshape(B, S, D), y[Np:N].reshape(DB, T, D),
            p_rows(kc), p_rows(vc), p_rows(ks), p_rows(vs),
            p_rows(kw)[:, :, S - wbp:], p_rows(vw)[:, :, S - wbp:],
            s_gla_p[None], mem_k.reshape(mshape), mem_v.reshape(mshape),
            s_rows(kc), s_rows(vc), s_rows(ks), s_rows(vs),
            s_win(cache_win_k, kw), s_win(cache_win_v, vw), s_gla_s[None])
```

```python
import functools

import numpy as np
import jax
import jax.numpy as jnp
from jax import lax
from jax.experimental import pallas as pl
from jax.experimental.pallas import tpu as pltpu

F32 = jnp.float32
BF16 = jnp.bfloat16

GLA_HEADS = 4
GLA_GATE_RANK = 16
GLA_TAU = 16.0
GLA_CHUNK = 64
GLA_SUB = 16
NSA_HEADS = 8
NSA_KV_HEADS = 2
NSA_GROUP = NSA_HEADS // NSA_KV_HEADS
NSA_BLOCK = 64
NSA_TOPN = 16
NSA_WINDOW = 512
MEM_HEADS = 4
ROPE_THETA = 500000.0
N_BRANCH = 3
TOP_K = 4
SWIGLU_LIMIT = 7.0
SWIGLU_ALPHA = 1.702
EPS = 1e-6

LANES = 128
NEG = -1e30
VMEM_LIMIT = 56 * 1024 * 1024


def _cp(sem):
    return pltpu.CompilerParams(dimension_semantics=sem, vmem_limit_bytes=VMEM_LIMIT)


def _dot(a, b):
    return jnp.dot(a.astype(BF16), b.astype(BF16), preferred_element_type=F32)


def _dot_nt(a, b):
    return lax.dot_general(a.astype(BF16), b.astype(BF16), (((1,), (1,)), ((), ())),
                           preferred_element_type=F32)


def _dot_tn(a, b):
    return lax.dot_general(a.astype(BF16), b.astype(BF16), (((0,), (0,)), ((), ())),
                           preferred_element_type=F32)


def _split3(x):
    hi = x.astype(BF16)
    r = x - hi.astype(F32)
    mid = r.astype(BF16)
    lo = (r - mid.astype(F32)).astype(BF16)
    return hi, mid, lo


def _dot_exact_rhs(x, m):
    hi, mid, lo = _split3(x)
    d = functools.partial(jnp.dot, preferred_element_type=F32)
    return d(hi, m) + d(mid, m) + d(lo, m)


def _dot_exact_lhs(m, x):
    hi, mid, lo = _split3(x)
    d = functools.partial(jnp.dot, preferred_element_type=F32)
    return d(m, hi) + d(m, mid) + d(m, lo)


def _dot_hi(x, w_hi, w_lo):
    hi, mid, _ = _split3(x)
    d = functools.partial(jnp.dot, preferred_element_type=F32)
    return d(hi, w_hi) + d(mid, w_hi) + d(hi, w_lo)


def _msoftmax(s, mask):
    s = jnp.where(mask, s, NEG)
    m = jnp.max(s, axis=-1, keepdims=True)
    e = jnp.where(mask, jnp.exp(s - m), 0.0)
    return e / jnp.maximum(jnp.sum(e, axis=-1, keepdims=True), 1e-30)


def _seg_ones(width, seg):
    r = lax.broadcasted_iota(jnp.int32, (width, width), 0) // seg
    c = lax.broadcasted_iota(jnp.int32, (width, width), 1) // seg
    return jnp.where(r == c, 1.0, 0.0).astype(BF16)


def _seg_rms(x, seg, gain):
    ms = _dot_exact_rhs(x * x, _seg_ones(x.shape[-1], seg)) * (1.0 / seg)
    return x * lax.rsqrt(ms + EPS) * gain


def _sigmoid(x):
    return 1.0 / (1.0 + jnp.exp(-x))


def _proj_kernel(x_ref, g_ref, wa_ref, wb_ref, wc_ref, wd_ref, za_ref, zb_ref, zc_ref, zd_ref):
    x = x_ref[...]
    u = x * lax.rsqrt(jnp.mean(x * x, axis=-1, keepdims=True) + EPS) * g_ref[...]
    ub = u.astype(BF16)
    za_ref[...] = jnp.dot(ub, wa_ref[...], preferred_element_type=F32)
    zb_ref[...] = jnp.dot(ub, wb_ref[...], preferred_element_type=F32)
    zc_ref[...] = _sigmoid(jnp.dot(ub, wc_ref[...], preferred_element_type=F32)).astype(BF16)
    zd_ref[...] = jnp.dot(ub, wd_ref[...], preferred_element_type=F32)


def _project(x, g, wa, wb, wc, wd, tm):
    n, d = x.shape
    full = lambda w: pl.BlockSpec(w.shape, lambda i: (0, 0))
    row = lambda w: pl.BlockSpec((tm, w), lambda i: (i, 0))
    return pl.pallas_call(
        _proj_kernel,
        grid=(n // tm,),
        in_specs=[row(d), full(g), full(wa), full(wb), full(wc), full(wd)],
        out_specs=[row(wa.shape[1]), row(wb.shape[1]), row(wc.shape[1]), row(wd.shape[1])],
        out_shape=[jax.ShapeDtypeStruct((n, wa.shape[1]), F32), jax.ShapeDtypeStruct((n, wb.shape[1]), F32),
                   jax.ShapeDtypeStruct((n, wc.shape[1]), BF16), jax.ShapeDtypeStruct((n, wd.shape[1]), F32)],
        compiler_params=_cp(("parallel",)),
        name="proj",
    )(x, g, wa, wb, wc, wd)


def _rope(x, tab):
    c, s1, s2 = tab[:, 0:LANES], tab[:, LANES:2 * LANES], tab[:, 2 * LANES:3 * LANES]
    half = 8
    return x * c + pltpu.roll(x, LANES - half, 1) * s1 + pltpu.roll(x, half, 1) * s2


def _prep_common(zb_ref, tab_ref, gq_ref, gk_ref, gm_ref):
    z = zb_ref[...]
    tab = tab_ref[...]
    hd = NSA_BLOCK
    nq = NSA_HEADS * hd
    qn = _seg_rms(z[:, 0:nq], hd, gq_ref[...])
    q = jnp.concatenate([_rope(qn[:, j * LANES:(j + 1) * LANES], tab) for j in range(nq // LANES)], axis=1)
    q = q * (hd ** -0.5)
    kv = z[:, nq:nq + 6 * LANES]
    gk = gk_ref[...]
    ks = [_rope(_seg_rms(kv[:, 2 * j * LANES:(2 * j + 1) * LANES], hd, gk[j:j + 1, :]), tab) for j in range(3)]
    vs = [kv[:, (2 * j + 1) * LANES:(2 * j + 2) * LANES] for j in range(3)]
    zm = z[:, nq + 6 * LANES:]
    mhd = zm.shape[1] // MEM_HEADS
    qm = (_seg_rms(zm, mhd, gm_ref[...]) * (mhd ** -0.5)).astype(BF16)
    return q, ks, vs, qm


def _prep_sample_kernel(zb_ref, tab_ref, gq_ref, gk_ref, gm_ref,
                        q_ref, kc_ref, vc_ref, ks_ref, vs_ref, kw_ref, vw_ref, qm_ref):
    q, ks, vs, qm = _prep_common(zb_ref, tab_ref, gq_ref, gk_ref, gm_ref)
    q_ref[...] = q
    for ref, val in zip((kc_ref, ks_ref, kw_ref, vc_ref, vs_ref, vw_ref), ks + vs):
        ref[...] = val
    qm_ref[...] = qm


def _prep_prompt_kernel(zb_ref, tab_ref, gq_ref, gk_ref, gm_ref,
                        kc_ref, vc_ref, kct_ref, vct_ref, kst_ref, vst_ref, kwt_ref, vwt_ref,
                        qp_ref, ktsb_ref, ktwb_ref, vs1_ref, vw1_ref, qm_ref):
    q, ks, vs, qm = _prep_common(zb_ref, tab_ref, gq_ref, gk_ref, gm_ref)
    kc_ref[...] = ks[0]
    vc_ref[...] = vs[0]
    for ref, val in zip((kct_ref, kst_ref, kwt_ref, vct_ref, vst_ref, vwt_ref), ks + vs):
        ref[...] = val.T
    ktsb_ref[...] = ks[1].T.astype(BF16)
    ktwb_ref[...] = ks[2].T.astype(BF16)
    low = lax.broadcasted_iota(jnp.int32, (q.shape[0], LANES), 1) < NSA_BLOCK
    vs1_ref[...] = jnp.concatenate([jnp.where(low, vs[1], 1.0), jnp.where(low, 1.0, vs[1])], axis=1).astype(BF16)
    vw1_ref[...] = jnp.concatenate([jnp.where(low, vs[2], 1.0), jnp.where(low, 1.0, vs[2])], axis=1).astype(BF16)
    parts = []
    for head in range(NSA_HEADS):
        g = head // NSA_GROUP
        chunk = q[:, (head // 2) * LANES:(head // 2 + 1) * LANES]
        if (head % 2) != g:
            chunk = pltpu.roll(chunk, NSA_BLOCK, 1)
        parts.append(jnp.where(low if g == 0 else jnp.logical_not(low), chunk, 0.0))
    qp_ref[...] = jnp.concatenate(parts, axis=1).astype(BF16)
    qm_ref[...] = qm


def _prep_sample(zb, tab, gq, gk, gm, tm, row0, n_rows):
    blk0 = row0 // tm
    row = lambda w: pl.BlockSpec((tm, w), lambda i: (i, 0))
    full = lambda a: pl.BlockSpec(a.shape, lambda i: (0, 0))
    sd = lambda w, dt=F32: jax.ShapeDtypeStruct((n_rows, w), dt)
    return pl.pallas_call(
        _prep_sample_kernel,
        grid=(n_rows // tm,),
        in_specs=[pl.BlockSpec((tm, zb.shape[1]), lambda i: (blk0 + i, 0)), full(tab), full(gq), full(gk), full(gm)],
        out_specs=[row(512)] + [row(LANES)] * 6 + [row(512)],
        out_shape=[sd(512)] + [sd(LANES)] * 6 + [sd(512, BF16)],
        compiler_params=_cp(("parallel",)),
        name="nsa_prep_sample",
    )(zb, tab, gq, gk, gm)


def _prep_prompt(zb, tab, gq, gk, gm, tm, batch, seq):
    nt = seq // tm
    n = batch * seq
    full = lambda a: pl.BlockSpec(a.shape, lambda b, i: (0, 0))
    row = lambda w: pl.BlockSpec((tm, w), lambda b, i: (b * nt + i, 0))
    tr = pl.BlockSpec((None, LANES, tm), lambda b, i: (b, 0, i))
    sd = lambda w, dt=F32: jax.ShapeDtypeStruct((n, w), dt)
    sdt = lambda dt=F32: jax.ShapeDtypeStruct((batch, LANES, seq), dt)
    return pl.pallas_call(
        _prep_prompt_kernel,
        grid=(batch, nt),
        in_specs=[row(zb.shape[1]), pl.BlockSpec((tm, 3 * LANES), lambda b, i: (i, 0)), full(gq), full(gk), full(gm)],
        out_specs=[row(LANES), row(LANES)] + [tr] * 6 + [row(NSA_HEADS * LANES), tr, tr, row(2 * LANES), row(2 * LANES),
                                                       row(512)],
        out_shape=[sd(LANES), sd(LANES)] + [sdt()] * 6 + [sd(NSA_HEADS * LANES, BF16), sdt(BF16), sdt(BF16),
                                                          sd(2 * LANES, BF16), sd(2 * LANES, BF16), sd(512, BF16)],
        compiler_params=_cp(("parallel", "parallel")),
        name="nsa_prep_prompt",
    )(zb, tab, gq, gk, gm)


def _pool_rows(x, w):
    nblk = x.shape[0] // NSA_BLOCK
    return jnp.sum(x.reshape(nblk, NSA_BLOCK, LANES) * w[None], axis=1)


def _compress_kernel(k_ref, v_ref, w_ref, ck_ref, cv_ref):
    ck_ref[...] = _pool_rows(k_ref[...], w_ref[0])
    cv_ref[...] = _pool_rows(v_ref[...], w_ref[1])


def _compress(kc, vc, w2, n_rows, tm):
    row = pl.BlockSpec((tm, LANES), lambda i: (i, 0))
    out = pl.BlockSpec((tm // NSA_BLOCK, LANES), lambda i: (i, 0))
    sd = jax.ShapeDtypeStruct((n_rows // NSA_BLOCK, LANES), F32)
    return pl.pallas_call(
        _compress_kernel,
        grid=(n_rows // tm,),
        in_specs=[row, row, pl.BlockSpec(w2.shape, lambda i: (0, 0, 0))],
        out_specs=[out, out],
        out_shape=[sd, sd],
        compiler_params=_cp(("parallel",)),
        name="compress",
    )(kc, vc, w2)


def _select(imp, cur, n_cand):
    n_iota = lax.broadcasted_iota(jnp.int32, imp.shape, 1)
    cnt = jnp.zeros(imp.shape, F32)
    for m in range(n_cand):
        col = imp[:, m:m + 1]
        beats = (col > imp) | ((col == imp) & (n_iota > m))
        cnt = cnt + jnp.where(beats & (cur > m), 1.0, 0.0)
    keep = (n_iota == cur) | ((n_iota < cur) & (cnt < NSA_TOPN - 1))
    return jnp.where(keep, 1.0, 0.0)


def _expand_blocks(nb, nkeys, first_key=0):
    kb = (lax.broadcasted_iota(jnp.int32, (nb, nkeys), 1) + first_key) // NSA_BLOCK
    return jnp.where(kb == lax.broadcasted_iota(jnp.int32, (nb, nkeys), 0), 1.0, 0.0).astype(BF16)


def _gate_col(gate, head, j):
    c = GLA_GATE_RANK + head * 3 + j
    return _sigmoid(gate[:, c:c + 1])


def _select_t(imp_t, cur_t, n_cand):
    n_iota = lax.broadcasted_iota(jnp.int32, imp_t.shape, 0)
    cnt = jnp.zeros(imp_t.shape, F32)
    for m in range(n_cand):
        row = imp_t[m:m + 1, :]
        beats = (row > imp_t) | ((row == imp_t) & (n_iota > m))
        cnt = cnt + jnp.where(beats & (cur_t > m), 1.0, 0.0)
    keep = (n_iota == cur_t) | ((n_iota < cur_t) & (cnt < NSA_TOPN - 1))
    return jnp.where(keep, 1.0, 0.0)


def _attend(qh, kt, bias, v_ones):
    s = jnp.dot(qh, kt, preferred_element_type=F32) + bias
    e = jnp.exp((s - jnp.max(s, axis=-1, keepdims=True)).astype(BF16))
    pv = jnp.dot(e, v_ones, preferred_element_type=F32)
    return pv / pltpu.roll(pv, NSA_BLOCK, 1)


def _nsa_prompt_kernel(q_ref, ck_ref, cv_ref, kts_ref, ktw_ref, vs1_ref, vw1_ref, gate_ref, o_ref, osel_ref,
                       *, tq, seq, wslab, nbr):
    i = pl.program_id(1)
    nq = pl.num_programs(1)
    t0 = i * tq
    nb = seq // NSA_BLOCK
    q = q_ref[...]
    pos = t0 + lax.broadcasted_iota(jnp.int32, (tq, 1), 0)
    pos_t = t0 + lax.broadcasted_iota(jnp.int32, (1, tq), 1)
    cur_t = pos_t // NSA_BLOCK
    vis_t = (lax.broadcasted_iota(jnp.int32, (nb, tq), 0) + 1) * NSA_BLOCK <= pos_t + 1
    wstart = pl.multiple_of(jnp.maximum(t0 + tq - wslab, 0), LANES)
    wrel = pos - (wstart + lax.broadcasted_iota(jnp.int32, (tq, wslab), 1))
    wbias = jnp.where((wrel >= 0) & (wrel < NSA_WINDOW), 0.0, NEG)
    ckb = ck_ref[...].astype(BF16)
    cvb = cv_ref[...].astype(BF16)
    cmp_r, win_r = [], []
    for g in range(NSA_KV_HEADS):
        heads = [g * NSA_GROUP + h for h in range(NSA_GROUP)]
        imp_t = jnp.zeros((nb, tq), F32)
        for head in heads:
            qh = q[:, head * LANES:(head + 1) * LANES]
            s_t = jnp.where(vis_t, _dot_nt(ckb, qh), NEG)
            e_t = jnp.where(vis_t, jnp.exp(s_t - jnp.max(s_t, axis=0, keepdims=True)), 0.0)
            pr_t = e_t / jnp.maximum(jnp.sum(e_t, axis=0, keepdims=True), 1e-30)
            imp_t = imp_t + pr_t
            cmp_r.append(_dot_tn(pr_t, cvb))
        sel_t = _select_t(imp_t, cur_t, nb).astype(BF16)
        for j in range(nbr):
            klen = (j + 1) * seq // nbr

            @pl.when(i * nbr // nq == j)
            def _(g=g, klen=klen, sel_t=sel_t, heads=heads):
                kpos = lax.broadcasted_iota(jnp.int32, (tq, klen), 1)
                picked = _dot_tn(sel_t, _expand_blocks(nb, klen)) > 0.5
                sbias = jnp.where(picked & (kpos <= pos), 0.0, NEG)
                kt = kts_ref[:, 0:klen]
                v1 = vs1_ref[0:klen, g * LANES:(g + 1) * LANES]
                for head in heads:
                    hl = slice(head * LANES, (head + 1) * LANES)
                    osel_ref[:, hl] = _attend(q[:, hl], kt, sbias, v1)

        kt = ktw_ref[:, pl.ds(wstart, wslab)]
        v1 = vw1_ref[pl.ds(wstart, wslab), g * LANES:(g + 1) * LANES]
        for head in heads:
            win_r.append(_attend(q[:, head * LANES:(head + 1) * LANES], kt, wbias, v1))

    low = lax.broadcasted_iota(jnp.int32, (tq, LANES), 1) < NSA_BLOCK
    pair = lambda r: jnp.concatenate([jnp.where(low, r[c], r[NSA_GROUP + c]) for c in range(NSA_GROUP)], axis=1)
    sel_r = [osel_ref[:, head * LANES:(head + 1) * LANES] for head in range(NSA_HEADS)]
    sig = _sigmoid(gate_ref[...])
    s_hi = sig.astype(BF16)
    s_lo = (sig - s_hi.astype(F32)).astype(BF16)
    width = NSA_GROUP * LANES
    row = lax.broadcasted_iota(jnp.int32, (LANES, width), 0)
    col = lax.broadcasted_iota(jnp.int32, (LANES, width), 1)
    head_of = jnp.where(col % LANES < NSA_BLOCK, col // LANES, NSA_GROUP + col // LANES)
    out = None
    for j, r in enumerate((cmp_r, sel_r, win_r)):
        pick = jnp.where(row == GLA_GATE_RANK + head_of * 3 + j, 1.0, 0.0).astype(BF16)
        gj = jnp.dot(s_hi, pick, preferred_element_type=F32) + jnp.dot(s_lo, pick, preferred_element_type=F32)
        out = gj * pair(r) if out is None else out + gj * pair(r)
    o_ref[...] = out.astype(BF16)


def _nsa_prompt(qp, ck, cv, kts, ktw, vs1, vw1, zd, batch, seq, tq):
    nq = seq // tq
    nb = seq // NSA_BLOCK
    wslab = min(seq, NSA_WINDOW + tq)
    nbr = 4 if nq % 4 == 0 else 1
    kern = functools.partial(_nsa_prompt_kernel, tq=tq, seq=seq, wslab=wslab, nbr=nbr)
    per_b = lambda a: pl.BlockSpec((None,) + a.shape[1:], lambda b, i: (b, 0, 0))
    rows_b = lambda a: pl.BlockSpec((seq, a.shape[1]), lambda b, i: (b, 0))
    return pl.pallas_call(
        kern,
        grid=(batch, nq),
        in_specs=[pl.BlockSpec((tq, qp.shape[1]), lambda b, i: (b * nq + i, 0)),
                  pl.BlockSpec((nb, LANES), lambda b, i: (b, 0)),
                  pl.BlockSpec((nb, LANES), lambda b, i: (b, 0)),
                  per_b(kts), per_b(ktw), rows_b(vs1), rows_b(vw1),
                  pl.BlockSpec((tq, LANES), lambda b, i: (b * nq + i, 0))],
        out_specs=pl.BlockSpec((tq, 512), lambda b, i: (b * nq + i, 0)),
        out_shape=jax.ShapeDtypeStruct((batch * seq, 512), BF16),
        scratch_shapes=[pltpu.VMEM((tq, NSA_HEADS * LANES), F32)],
        compiler_params=_cp(("parallel", "parallel")),
        name="nsa_prompt",
    )(qp, ck, cv, kts, ktw, vs1, vw1, zd)


def _qbd(q, rows_per_head):
    hd = NSA_BLOCK
    lane_grp = lax.broadcasted_iota(jnp.int32, (rows_per_head, LANES), 1) // hd
    parts = []
    for head in range(NSA_HEADS):
        g = head // NSA_GROUP
        qh = q[:, head * hd:(head + 1) * hd]
        two = jnp.concatenate([qh, qh], axis=1)
        parts.append(jnp.where(lane_grp == g, two, 0.0))
    return jnp.concatenate(parts, axis=0)


def _nsa_sample_cmp_kernel(pt_ref, *refs, npg, t_new, past_len):
    kpages = refs[0:npg]
    vpages = refs[npg:2 * npg]
    q_ref, kn_ref, vn_ref, wt_ref, wn_ref, ocmp_ref, sel_ref, pk_ref, pv_ref = refs[2 * npg:]
    c = pl.program_id(1)
    nch = pl.num_programs(1)
    hd = NSA_BLOCK
    page = kpages[0].shape[1]
    bpp = page // NSA_BLOCK
    nbc = npg * bpp
    nb_pad = pk_ref.shape[1]
    n_past = past_len // NSA_BLOCK

    @pl.when(c == 0)
    def _():
        pk_ref[...] = jnp.zeros(pk_ref.shape, F32)
        pv_ref[...] = jnp.zeros(pv_ref.shape, F32)

    seg = jnp.where(lax.broadcasted_iota(jnp.int32, (npg * page, nbc), 0) // NSA_BLOCK
                    == lax.broadcasted_iota(jnp.int32, (npg * page, nbc), 1), 1.0, 0.0).astype(BF16)
    col = lax.broadcasted_iota(jnp.int32, (nbc, nb_pad), 1)
    place = jnp.where(col == c * nbc + lax.broadcasted_iota(jnp.int32, (nbc, nb_pad), 0), 1.0, 0.0).astype(BF16)
    for pages, w_t, pooled in ((kpages, wt_ref[0], pk_ref), (vpages, wt_ref[1], pv_ref)):
        x = jnp.concatenate([r[...] * w_t for r in pages], axis=1)
        pooled[...] += _dot_exact_rhs(_dot_exact_rhs(x, seg), place)

    @pl.when(c == nch - 1)
    def _():
        newk = jnp.sum(kn_ref[...] * wn_ref[0][0:t_new], axis=0, keepdims=True)
        newv = jnp.sum(vn_ref[...] * wn_ref[1][0:t_new], axis=0, keepdims=True)
        q = q_ref[...]
        rows = NSA_HEADS * t_new
        qbd = _qbd(q, t_new)
        trow = lax.broadcasted_iota(jnp.int32, (rows, 1), 0) % t_new
        pos = past_len + trow
        blk = lax.broadcasted_iota(jnp.int32, (rows, nb_pad), 1)
        vis = ((blk + 1) * NSA_BLOCK <= pos + 1) & (blk < n_past)
        vis_new = (n_past + 1) * NSA_BLOCK <= pos + 1
        s_past = jnp.where(vis, _dot(qbd, pk_ref[...]), NEG)
        s_new = jnp.where(vis_new, jnp.sum(qbd * newk, axis=-1, keepdims=True), NEG)
        m = jnp.maximum(jnp.max(s_past, axis=-1, keepdims=True), s_new)
        e_past = jnp.where(vis, jnp.exp(s_past - m), 0.0)
        e_new = jnp.where(vis_new, jnp.exp(s_new - m), 0.0)
        den = jnp.maximum(jnp.sum(e_past, axis=-1, keepdims=True) + e_new, 1e-30)
        pr = e_past / den
        o = _dot_nt(pr, pv_ref[...]) + (e_new / den) * newv
        outs = []
        for head in range(NSA_HEADS):
            g = head // NSA_GROUP
            outs.append(o[head * t_new:(head + 1) * t_new, g * hd:(g + 1) * hd])
        ocmp_ref[...] = jnp.concatenate(outs, axis=1)
        imps = []
        for g in range(NSA_KV_HEADS):
            acc = pr[g * NSA_GROUP * t_new:(g * NSA_GROUP + 1) * t_new]
            for h in range(1, NSA_GROUP):
                acc = acc + pr[(g * NSA_GROUP + h) * t_new:(g * NSA_GROUP + h + 1) * t_new]
            imps.append(acc)
        imp = jnp.concatenate(imps, axis=0)
        cur = (past_len + lax.broadcasted_iota(jnp.int32, (NSA_KV_HEADS * t_new, 1), 0) % t_new) // NSA_BLOCK
        sel_ref[...] = _select(imp, cur, n_past)


def _nsa_sample_cmp(page_table, pool_k, pool_v, q, kn, vn, wt, wn, t_new, npg):
    db, n_pages = page_table.shape
    page = pool_k.shape[2]
    past_len = n_pages * page
    nch = n_pages // npg
    nb_pad = -(-(past_len // NSA_BLOCK) // LANES) * LANES
    row0 = q.shape[0] // t_new - db

    def page_spec(j):
        return pl.BlockSpec((None, LANES, page), lambda b, c, pt: (pt[b, c * npg + j], 0, 0))

    tok = lambda w: pl.BlockSpec((t_new, w), lambda b, c, pt: (row0 + b, 0))
    full3 = lambda a: pl.BlockSpec(a.shape, lambda b, c, pt: (0, 0, 0))
    kern = functools.partial(_nsa_sample_cmp_kernel, npg=npg, t_new=t_new, past_len=past_len)
    grid_spec = pltpu.PrefetchScalarGridSpec(
        num_scalar_prefetch=1,
        grid=(db, nch),
        in_specs=[page_spec(j) for j in range(npg)] + [page_spec(j) for j in range(npg)]
        + [tok(512), tok(LANES), tok(LANES), full3(wt), full3(wn)],
        out_specs=[pl.BlockSpec((None, t_new, 512), lambda b, c, pt: (b, 0, 0)),
                   pl.BlockSpec((None, NSA_KV_HEADS * t_new, nb_pad), lambda b, c, pt: (b, 0, 0))],
        scratch_shapes=[pltpu.VMEM((LANES, nb_pad), F32), pltpu.VMEM((LANES, nb_pad), F32)],
    )
    return pl.pallas_call(
        kern,
        grid_spec=grid_spec,
        out_shape=[jax.ShapeDtypeStruct((db, t_new, 512), F32),
                   jax.ShapeDtypeStruct((db, NSA_KV_HEADS * t_new, nb_pad), F32)],
        compiler_params=_cp(("parallel", "arbitrary")),
        name="nsa_sample_cmp",
    )(page_table, *([pool_k] * npg), *([pool_v] * npg), q, kn, vn, wt, wn)


def _online_update(s, mask, v, m_ref, l_ref, acc_ref, v_transposed=False):
    s = jnp.where(mask, s, NEG)
    m_old = m_ref[...]
    m_new = jnp.maximum(m_old, jnp.max(s, axis=-1, keepdims=True))
    alpha = jnp.exp(m_old - m_new)
    e = jnp.where(mask, jnp.exp(s - m_new), 0.0)
    l_ref[...] = alpha * l_ref[...] + jnp.sum(e, axis=-1, keepdims=True)
    acc_ref[...] = alpha * acc_ref[...] + (_dot_nt(e, v) if v_transposed else _dot(e, v))
    m_ref[...] = m_new


def _nsa_sample_sel_kernel(pt_ref, *refs, npg, t_new, past_len):
    kpages = refs[0:npg]
    vpages = refs[npg:2 * npg]
    (q_ref, sel_ref, ksn_ref, vsn_ref, wk_ref, wv_ref, kwn_ref, vwn_ref, ocmp_ref, gate_ref,
     o_ref, m_ref, l_ref, acc_ref, pad_ref) = refs[2 * npg:]
    c = pl.program_id(1)
    nch = pl.num_programs(1)
    hd = NSA_BLOCK
    page = kpages[0].shape[1]
    bpp = page // NSA_BLOCK
    nbc = npg * bpp
    rows = NSA_HEADS * t_new
    q = q_ref[...]
    qbd = _qbd(q, t_new)

    @pl.when(c == 0)
    def _():
        m_ref[...] = jnp.full(m_ref.shape, NEG, F32)
        l_ref[...] = jnp.zeros(l_ref.shape, F32)
        acc_ref[...] = jnp.zeros(acc_ref.shape, F32)

    sel = sel_ref[...]
    nb_pad = sel.shape[1]
    sel_rows = jnp.concatenate([sel[(head // NSA_GROUP) * t_new:(head // NSA_GROUP + 1) * t_new]
                                for head in range(NSA_HEADS)], axis=0)
    blk_iota = lax.broadcasted_iota(jnp.int32, (nb_pad, npg * page), 0)
    key_blk = c * nbc + lax.broadcasted_iota(jnp.int32, (nb_pad, npg * page), 1) // NSA_BLOCK
    expand = jnp.where(blk_iota == key_blk, 1.0, 0.0).astype(BF16)
    smask = _dot(sel_rows, expand) > 0.5
    kcat = jnp.concatenate([r[...].astype(BF16) for r in kpages], axis=1)
    vcat = jnp.concatenate([r[...].astype(BF16) for r in vpages], axis=1)
    _online_update(_dot(qbd, kcat), smask, vcat, m_ref, l_ref, acc_ref, v_transposed=True)

    @pl.when(c == nch - 1)
    def _():
        trow = lax.broadcasted_iota(jnp.int32, (rows, 1), 0) % t_new
        pad_ref[...] = jnp.zeros(pad_ref.shape, F32)
        pad_ref[0:t_new, :] = ksn_ref[...]
        kn = pad_ref[...]
        pad_ref[0:t_new, :] = vsn_ref[...]
        vn = pad_ref[...]
        jn = lax.broadcasted_iota(jnp.int32, (rows, LANES), 1)
        _online_update(_dot_nt(qbd, kn), jn <= trow, vn, m_ref, l_ref, acc_ref)
        o_sel = acc_ref[...] / jnp.maximum(l_ref[...], 1e-30)
        wb = wk_ref.shape[1]
        jw = lax.broadcasted_iota(jnp.int32, (rows, wb), 1)
        rel = trow + wb - jw
        wmask = jnp.concatenate([(rel >= 0) & (rel < NSA_WINDOW), jn <= trow], axis=1)
        pad_ref[0:t_new, :] = kwn_ref[...]
        s_win = jnp.concatenate([_dot(qbd, wk_ref[...]), _dot_nt(qbd, pad_ref[...])], axis=1)
        p_win = _msoftmax(s_win, wmask)
        pad_ref[0:t_new, :] = vwn_ref[...]
        o_win = _dot_nt(p_win[:, 0:wb], wv_ref[...]) + _dot(p_win[:, wb:], pad_ref[...])
        gate = gate_ref[...]
        ocmp = ocmp_ref[...]
        outs = []
        for head in range(NSA_HEADS):
            g = head // NSA_GROUP
            r0, r1 = head * t_new, (head + 1) * t_new
            outs.append(_gate_col(gate, head, 0) * ocmp[:, head * hd:(head + 1) * hd]
                        + _gate_col(gate, head, 1) * o_sel[r0:r1, g * hd:(g + 1) * hd]
                        + _gate_col(gate, head, 2) * o_win[r0:r1, g * hd:(g + 1) * hd])
        o_ref[...] = jnp.concatenate(outs, axis=1)


def _nsa_sample_sel(page_table, pool_k, pool_v, q, sel, ksn, vsn, win_k, win_v, kwn, vwn, ocmp, zd, t_new, npg):
    db, n_pages = page_table.shape
    page = pool_k.shape[2]
    past_len = n_pages * page
    nch = n_pages // npg
    row0 = q.shape[0] // t_new - db
    rows = NSA_HEADS * t_new

    def page_spec(j):
        return pl.BlockSpec((None, LANES, page), lambda b, c, pt: (pt[b, c * npg + j], 0, 0))

    tok = lambda w: pl.BlockSpec((t_new, w), lambda b, c, pt: (row0 + b, 0))
    per_seq = lambda a: pl.BlockSpec((None,) + a.shape[1:], lambda b, c, pt: (b, 0, 0))
    kern = functools.partial(_nsa_sample_sel_kernel, npg=npg, t_new=t_new, past_len=past_len)
    grid_spec = pltpu.PrefetchScalarGridSpec(
        num_scalar_prefetch=1,
        grid=(db, nch),
        in_specs=[page_spec(j) for j in range(npg)] + [page_spec(j) for j in range(npg)]
        + [tok(512), per_seq(sel), tok(LANES), tok(LANES), per_seq(win_k), per_seq(win_v), tok(LANES), tok(LANES),
           per_seq(ocmp), tok(LANES)],
        out_specs=pl.BlockSpec((t_new, 512), lambda b, c, pt: (b, 0)),
        scratch_shapes=[pltpu.VMEM((rows, 1), F32), pltpu.VMEM((rows, 1), F32), pltpu.VMEM((rows, LANES), F32),
                        pltpu.VMEM((LANES, LANES), F32)],
    )
    return pl.pallas_call(
        kern,
        grid_spec=grid_spec,
        out_shape=jax.ShapeDtypeStruct((db * t_new, 512), F32),
        compiler_params=_cp(("parallel", "arbitrary")),
        name="nsa_sample_sel",
    )(page_table, *([pool_k] * npg), *([pool_v] * npg), q, sel, ksn, vsn, win_k, win_v, kwn, vwn, ocmp, zd)


def _nsa_sample_kernel(pt_ref, ck_hbm, cv_hbm, sk_hbm, sv_hbm, q_ref, kcn_ref, vcn_ref, ksn_ref, vsn_ref,
                       kwn_ref, vwn_ref, gate_ref, wk_ref, wv_ref, wt_ref, wn_ref, o_ref,
                       buf_ref, sem_ref, pk_ref, pv_ref, m_ref, l_ref, acc_ref, pad_ref, bias_ref,
                       *, npg, t_new, past_len):
    b = pl.program_id(0)
    nseq = pl.num_programs(0)
    hd = NSA_BLOCK
    page = buf_ref.shape[3]
    n_pages = past_len // page
    nch = n_pages // npg
    nbc = npg * page // NSA_BLOCK
    nb_pad = pk_ref.shape[1]
    n_past = past_len // NSA_BLOCK
    rows = NSA_HEADS * t_new
    nkeys = npg * page

    def chunk_copies(seq, c, slot):
        pools = (ck_hbm, cv_hbm) if c < nch else (sk_hbm, sv_hbm)
        first = (c % nch) * npg
        copies = []
        for j in range(npg):
            pg = pt_ref[seq, first + j]
            copies.append(pltpu.make_async_copy(pools[0].at[pg], buf_ref.at[slot, j], sem_ref.at[slot]))
            copies.append(pltpu.make_async_copy(pools[1].at[pg], buf_ref.at[slot, npg + j], sem_ref.at[slot]))
        return copies

    @pl.when(b == 0)
    def _():
        for cp in chunk_copies(0, 0, 0):
            cp.start()

    q = q_ref[...]
    qbd = _qbd(q, t_new)
    trow = lax.broadcasted_iota(jnp.int32, (rows, 1), 0) % t_new
    pos = past_len + trow
    seg = jnp.where(lax.broadcasted_iota(jnp.int32, (nkeys, nbc), 0) // NSA_BLOCK
                    == lax.broadcasted_iota(jnp.int32, (nkeys, nbc), 1), 1.0, 0.0).astype(BF16)
    expand = _expand_blocks(nbc, nkeys)
    if n_past < nb_pad:
        pk_ref[...] = jnp.zeros(pk_ref.shape, F32)
        pv_ref[...] = jnp.zeros(pv_ref.shape, F32)
    m_ref[...] = jnp.full(m_ref.shape, 0.1 * NEG, F32)
    l_ref[...] = jnp.zeros(l_ref.shape, F32)
    acc_ref[...] = jnp.zeros(acc_ref.shape, F32)
    o_cmp = None

    for c in range(2 * nch):
        slot = c % 2
        if c + 1 < 2 * nch:
            for cp in chunk_copies(b, c + 1, 1 - slot):
                cp.start()
        else:
            @pl.when(b + 1 < nseq)
            def _():
                for cp in chunk_copies(b + 1, 0, 1 - slot):
                    cp.start()
        for cp in chunk_copies(b, c, slot):
            cp.wait()

        if c < nch:
            for base, w_t, pooled in ((0, wt_ref[0], pk_ref), (npg, wt_ref[1], pv_ref)):
                x = jnp.concatenate([(buf_ref[slot, base + j] * w_t).astype(BF16) for j in range(npg)], axis=1)
                pooled[:, c * nbc:(c + 1) * nbc] = jnp.dot(x, seg, preferred_element_type=F32)
        else:
            k0 = (c - nch) * nkeys
            kcat = jnp.concatenate([buf_ref[slot, j].astype(BF16) for j in range(npg)], axis=1)
            vcat = jnp.concatenate([buf_ref[slot, npg + j].astype(BF16) for j in range(npg)], axis=1)
            s = _dot(qbd, kcat) + bias_ref[:, k0:k0 + nkeys]
            m_old = m_ref[...]
            m_new = jnp.maximum(m_old, jnp.max(s, axis=-1, keepdims=True))
            alpha = jnp.exp(m_old - m_new)
            e = jnp.exp(s - m_new)
            l_ref[...] = alpha * l_ref[...] + jnp.sum(e, axis=-1, keepdims=True)
            acc_ref[...] = alpha * acc_ref[...] + _dot_nt(e, vcat)
            m_ref[...] = m_new

        if c == nch - 1:
            newk = jnp.sum(kcn_ref[...] * wn_ref[0][0:t_new], axis=0, keepdims=True)
            newv = jnp.sum(vcn_ref[...] * wn_ref[1][0:t_new], axis=0, keepdims=True)
            blk = lax.broadcasted_iota(jnp.int32, (rows, nb_pad), 1)
            vis = ((blk + 1) * NSA_BLOCK <= pos + 1) & (blk < n_past)
            vis_new = (n_past + 1) * NSA_BLOCK <= pos + 1
            s_past = jnp.where(vis, _dot(qbd, pk_ref[...]), NEG)
            s_new = jnp.where(vis_new, jnp.sum(qbd * newk, axis=-1, keepdims=True), NEG)
            mx = jnp.maximum(jnp.max(s_past, axis=-1, keepdims=True), s_new)
            e_past = jnp.where(vis, jnp.exp(s_past - mx), 0.0)
            e_new = jnp.where(vis_new, jnp.exp(s_new - mx), 0.0)
            den = jnp.maximum(jnp.sum(e_past, axis=-1, keepdims=True) + e_new, 1e-30)
            pr = e_past / den
            o_cmp = _dot_nt(pr, pv_ref[...]) + (e_new / den) * newv
            imps = []
            for g in range(NSA_KV_HEADS):
                acc = pr[g * NSA_GROUP * t_new:(g * NSA_GROUP + 1) * t_new]
                for h in range(1, NSA_GROUP):
                    acc = acc + pr[(g * NSA_GROUP + h) * t_new:(g * NSA_GROUP + h + 1) * t_new]
                imps.append(acc)
            imp = jnp.concatenate(imps, axis=0)
            cur = (past_len + lax.broadcasted_iota(jnp.int32, (NSA_KV_HEADS * t_new, 1), 0) % t_new) // NSA_BLOCK
            sel = _select(imp, cur, n_past).astype(BF16)
            sel_rows = jnp.concatenate([sel[(head // NSA_GROUP) * t_new:(head // NSA_GROUP + 1) * t_new]
                                        for head in range(NSA_HEADS)], axis=0)
            for cc in range(nch):
                picked = jnp.dot(sel_rows[:, cc * nbc:(cc + 1) * nbc], expand, preferred_element_type=F32) > 0.5
                bias_ref[:, cc * nkeys:(cc + 1) * nkeys] = jnp.where(picked, 0.0, NEG)

    pad_ref[...] = jnp.zeros(pad_ref.shape, F32)
    pad_ref[0:t_new, :] = ksn_ref[...]
    kn = pad_ref[...]
    pad_ref[0:t_new, :] = vsn_ref[...]
    vn = pad_ref[...]
    jn = lax.broadcasted_iota(jnp.int32, (rows, LANES), 1)
    _online_update(_dot_nt(qbd, kn), jn <= trow, vn, m_ref, l_ref, acc_ref)
    o_sel = acc_ref[...] / jnp.maximum(l_ref[...], 1e-30)
    wb = wk_ref.shape[1]
    jw = lax.broadcasted_iota(jnp.int32, (rows, wb), 1)
    rel = trow + wb - jw
    wmask = jnp.concatenate([(rel >= 0) & (rel < NSA_WINDOW), jn <= trow], axis=1)
    pad_ref[0:t_new, :] = kwn_ref[...]
    s_win = jnp.concatenate([_dot(qbd, wk_ref[...]), _dot_nt(qbd, pad_ref[...])], axis=1)
    p_win = _msoftmax(s_win, wmask)
    pad_ref[0:t_new, :] = vwn_ref[...]
    o_win = _dot_nt(p_win[:, 0:wb], wv_ref[...]) + _dot(p_win[:, wb:], pad_ref[...])
    gate = gate_ref[...]
    outs = []
    for head in range(NSA_HEADS):
        g = head // NSA_GROUP
        r0, r1 = head * t_new, (head + 1) * t_new
        gl = slice(g * hd, (g + 1) * hd)
        outs.append(_gate_col(gate, head, 0) * o_cmp[r0:r1, gl] + _gate_col(gate, head, 1) * o_sel[r0:r1, gl]
                    + _gate_col(gate, head, 2) * o_win[r0:r1, gl])
    o_ref[...] = jnp.concatenate(outs, axis=1)


def _nsa_sample(page_table, pool_ck, pool_cv, pool_sk, pool_sv, q, kc, vc, ks, vs, kw, vw, zd, win_k, win_v,
                wt, wn, t_new, npg, zd_row0):
    db, n_pages = page_table.shape
    page = pool_ck.shape[2]
    past_len = n_pages * page
    nb_pad = -(-(past_len // NSA_BLOCK) // LANES) * LANES
    row0 = q.shape[0] // t_new - db
    rows = NSA_HEADS * t_new
    assert n_pages % npg == 0 and (2 * n_pages // npg) % 2 == 0
    tok = lambda w: pl.BlockSpec((t_new, w), lambda b, pt: (row0 + b, 0))
    per_seq = lambda a: pl.BlockSpec((None,) + a.shape[1:], lambda b, pt: (b, 0, 0))
    full3 = lambda a: pl.BlockSpec(a.shape, lambda b, pt: (0, 0, 0))
    hbm = pl.BlockSpec(memory_space=pl.ANY)
    kern = functools.partial(_nsa_sample_kernel, npg=npg, t_new=t_new, past_len=past_len)
    grid_spec = pltpu.PrefetchScalarGridSpec(
        num_scalar_prefetch=1,
        grid=(db,),
        in_specs=[hbm, hbm, hbm, hbm, tok(512)] + [tok(LANES)] * 6
        + [pl.BlockSpec((t_new, LANES), lambda b, pt: (zd_row0 + b, 0)), per_seq(win_k), per_seq(win_v),
           full3(wt), full3(wn)],
        out_specs=pl.BlockSpec((t_new, 512), lambda b, pt: (b, 0)),
        scratch_shapes=[pltpu.VMEM((2, 2 * npg, LANES, page), F32), pltpu.SemaphoreType.DMA((2,)),
                        pltpu.VMEM((LANES, nb_pad), F32), pltpu.VMEM((LANES, nb_pad), F32),
                        pltpu.VMEM((rows, 1), F32), pltpu.VMEM((rows, 1), F32), pltpu.VMEM((rows, LANES), F32),
                        pltpu.VMEM((LANES, LANES), F32), pltpu.VMEM((rows, past_len), F32)],
    )
    return pl.pallas_call(
        kern,
        grid_spec=grid_spec,
        out_shape=jax.ShapeDtypeStruct((db * t_new, 512), F32),
        compiler_params=_cp(("arbitrary",)),
        name="nsa_sample",
    )(page_table, pool_ck, pool_cv, pool_sk, pool_sv, q, kc, vc, ks, vs, kw, vw, zd, win_k, win_v, wt, wn)


def _log_sigmoid(x):
    return jnp.minimum(x, 0.0) - jnp.log(1.0 + jnp.exp(-jnp.abs(x)))


def _gla_kernel(za_ref, zd_ref, wgh_ref, wgl_ref, bg_ref, gout_ref, s0_ref, o_ref, s_ref,
                st_ref, kp_ref, bp_ref, vp_ref, za_pad_ref, zd_pad_ref, *, nchunk, t_valid):
    j = pl.program_id(1)
    short = za_ref.shape[0] < GLA_CHUNK
    nj = pl.num_programs(1)
    C = GLA_CHUNK
    sub = GLA_SUB
    nsub = C // sub
    dk = 64
    dv = 128
    hk = GLA_HEADS * dk
    hv = GLA_HEADS * dv

    @pl.when(j == 0)
    def _():
        st_ref[...] = jnp.concatenate([s0_ref[h].T for h in range(GLA_HEADS)], axis=1)
        kp_ref[...] = jnp.zeros(kp_ref.shape, F32)
        bp_ref[...] = jnp.zeros(bp_ref.shape, F32)
        vp_ref[...] = jnp.zeros(vp_ref.shape, F32)

    tril = jnp.where(lax.broadcasted_iota(jnp.int32, (C, C), 0) >= lax.broadcasted_iota(jnp.int32, (C, C), 1),
                     1.0, 0.0).astype(BF16)
    head_ones = jnp.where(lax.broadcasted_iota(jnp.int32, (hk, LANES), 0) // dk
                          == lax.broadcasted_iota(jnp.int32, (hk, LANES), 1), 1.0, 0.0).astype(BF16)
    tmod = lax.broadcasted_iota(jnp.int32, (C, 1), 0) % sub
    gout = gout_ref[...]

    def chunk(c, carry):
        r0 = pl.multiple_of(c * C, C)
        if short:
            za_pad_ref[...] = jnp.zeros(za_pad_ref.shape, F32)
            zd_pad_ref[...] = jnp.zeros(zd_pad_ref.shape, F32)
            za_pad_ref[0:za_ref.shape[0], :] = za_ref[...]
            zd_pad_ref[0:zd_ref.shape[0], :] = zd_ref[...]
            z = za_pad_ref[...]
            zd = zd_pad_ref[...]
        else:
            z = za_ref[pl.ds(r0, C), :]
            zd = zd_ref[pl.ds(r0, C), :]
        q = z[:, 0:hk] * (dk ** -0.5)
        k = z[:, hk:2 * hk]
        v = z[:, 2 * hk:2 * hk + hv]
        r = z[:, 2 * hk + hv:2 * hk + 2 * hv]
        lr = zd[:, 0:GLA_GATE_RANK]
        la = _log_sigmoid(_dot_hi(lr, wgh_ref[...], wgl_ref[...]) + bg_ref[...]) * (1.0 / GLA_TAU)
        tglob = (j * nchunk + c) * C + lax.broadcasted_iota(jnp.int32, (C, 1), 0)
        la = jnp.where(tglob < t_valid, la, 0.0)
        b = _dot_exact_lhs(tril, la)
        st = st_ref[...]
        qe = q * jnp.exp(b)
        kp_ref[sub:sub + C, :] = k
        bp_ref[sub:sub + C, :] = b
        vp_ref[sub:sub + C, :] = v
        xs = []
        for d in range(sub):
            kd = kp_ref[sub - d:sub - d + C, :]
            bd = bp_ref[sub - d:sub - d + C, :]
            xs.append(jnp.where(tmod >= d, q * kd * jnp.exp(b - bd), 0.0).astype(BF16))
        rr = jnp.dot(jnp.concatenate(xs, axis=0), head_ones, preferred_element_type=F32)
        outs = []
        for h in range(GLA_HEADS):
            kh = slice(h * dk, (h + 1) * dk)
            vh = slice(h * dv, (h + 1) * dv)
            o = _dot_nt(qe[:, kh], st[:, kh])
            for d in range(sub):
                o = o + rr[d * C:(d + 1) * C, h:h + 1] * vp_ref[sub - d:sub - d + C, vh]
            offs = [jnp.zeros((sub, dv), F32)]
            for i in range(1, nsub):
                anchor = b[i * sub - 1:i * sub, kh]
                qt = q[i * sub:(i + 1) * sub, kh] * jnp.exp(b[i * sub:(i + 1) * sub, kh] - anchor)
                kt = k[0:i * sub, kh] * jnp.exp(anchor - b[0:i * sub, kh])
                offs.append(_dot(_dot_nt(qt, kt), v[0:i * sub, vh]))
            o = o + jnp.concatenate(offs, axis=0)
            on = o * lax.rsqrt(jnp.mean(o * o, axis=-1, keepdims=True) + EPS) * gout
            rh = r[:, vh]
            outs.append(on * (rh * _sigmoid(rh)))
        o_all = jnp.concatenate(outs, axis=1)
        if short:
            o_ref[...] = o_all[0:o_ref.shape[0], :].astype(o_ref.dtype)
        else:
            o_ref[pl.ds(r0, C), :] = o_all.astype(o_ref.dtype)
        b_last = b[C - 1:C, :]
        kk = k * jnp.exp(b_last - b)
        upd = jnp.concatenate([_dot_tn(v[:, h * dv:(h + 1) * dv], kk[:, h * dk:(h + 1) * dk])
                               for h in range(GLA_HEADS)], axis=1)
        st_ref[...] = st * jnp.exp(b_last) + upd
        return carry

    lax.fori_loop(0, nchunk, chunk, 0)

    @pl.when(j == nj - 1)
    def _():
        st = st_ref[...]
        for h in range(GLA_HEADS):
            s_ref[h] = st[:, h * dk:(h + 1) * dk].T


def _gla(za, zd, wgh, wgl, bg, gout, s0, batch, t_seq, ct, row0, out_dtype):
    nj = t_seq // ct
    blk0 = row0 // ct
    hv = GLA_HEADS * 128
    kern = functools.partial(_gla_kernel, nchunk=max(1, ct // GLA_CHUNK), t_valid=t_seq)
    full = lambda a: pl.BlockSpec(a.shape, lambda b, j: (0,) * a.ndim)
    return pl.pallas_call(
        kern,
        grid=(batch, nj),
        in_specs=[pl.BlockSpec((ct, za.shape[1]), lambda b, j: (blk0 + b * nj + j, 0)),
                  pl.BlockSpec((ct, LANES), lambda b, j: (blk0 + b * nj + j, 0)),
                  full(wgh), full(wgl), full(bg), full(gout),
                  pl.BlockSpec((None,) + s0.shape[1:], lambda b, j: (b, 0, 0, 0))],
        out_specs=[pl.BlockSpec((ct, hv), lambda b, j: (b * nj + j, 0)),
                   pl.BlockSpec((None,) + s0.shape[1:], lambda b, j: (b, 0, 0, 0))],
        out_shape=[jax.ShapeDtypeStruct((batch * t_seq, hv), out_dtype), jax.ShapeDtypeStruct(s0.shape, F32)],
        scratch_shapes=[pltpu.VMEM((128, GLA_HEADS * 64), F32),
                        pltpu.VMEM((GLA_SUB + GLA_CHUNK, GLA_HEADS * 64), F32),
                        pltpu.VMEM((GLA_SUB + GLA_CHUNK, GLA_HEADS * 64), F32),
                        pltpu.VMEM((GLA_SUB + GLA_CHUNK, hv), F32),
                        pltpu.VMEM((GLA_CHUNK, za.shape[1]), F32),
                        pltpu.VMEM((GLA_CHUNK, LANES), F32)],
        compiler_params=_cp(("parallel", "arbitrary")),
        name="gla",
    )(za, zd, wgh, wgl, bg, gout, s0)


def _mem_kv_kernel(m_ref, g_ref, w_ref, gk_ref, k_ref, v_ref):
    x = m_ref[...]
    u = x * lax.rsqrt(jnp.mean(x * x, axis=-1, keepdims=True) + EPS) * g_ref[...]
    kv = jnp.dot(u.astype(BF16), w_ref[...], preferred_element_type=F32)
    half = kv.shape[1] // 2
    k_ref[...] = _seg_rms(kv[:, 0:half], half // MEM_HEADS, gk_ref[...])
    v_ref[...] = kv[:, half:]


def _mem_kv(mem, g, w, gk, tm):
    n, d = mem.shape
    half = w.shape[1] // 2
    full = lambda a: pl.BlockSpec(a.shape, lambda i: (0, 0))
    sd = jax.ShapeDtypeStruct((n, half), F32)
    return pl.pallas_call(
        _mem_kv_kernel,
        grid=(n // tm,),
        in_specs=[pl.BlockSpec((tm, d), lambda i: (i, 0)), full(g), full(w), full(gk)],
        out_specs=[pl.BlockSpec((tm, half), lambda i: (i, 0))] * 2,
        out_shape=[sd, sd],
        compiler_params=_cp(("parallel",)),
        name="mem_kv",
    )(mem, g, w, gk)


def _mem_prompt_kernel(q_ref, k_ref, v_ref, o_ref):
    q = q_ref[...]
    k = k_ref[...].astype(BF16)
    v = v_ref[...].astype(BF16)
    hd = q.shape[1] // MEM_HEADS
    outs = []
    for h in range(MEM_HEADS):
        sl = slice(h * hd, (h + 1) * hd)
        s = _dot_nt(q[:, sl], k[:, sl])
        m = jnp.max(s, axis=-1, keepdims=True)
        e = jnp.exp(s - m)
        outs.append(_dot(e / jnp.sum(e, axis=-1, keepdims=True), v[:, sl]))
    o_ref[...] = jnp.concatenate(outs, axis=1).astype(BF16)


def _mem_prompt(qm, mk, mv, batch, seq, mlen, tq):
    nq = seq // tq
    w = qm.shape[1]
    return pl.pallas_call(
        _mem_prompt_kernel,
        grid=(batch, nq),
        in_specs=[pl.BlockSpec((tq, w), lambda b, i: (b * nq + i, 0)),
                  pl.BlockSpec((mlen, w), lambda b, i: (b, 0)),
                  pl.BlockSpec((mlen, w), lambda b, i: (b, 0))],
        out_specs=pl.BlockSpec((tq, w), lambda b, i: (b * nq + i, 0)),
        out_shape=jax.ShapeDtypeStruct((batch * seq, w), BF16),
        compiler_params=_cp(("parallel", "parallel")),
        name="mem_prompt",
    )(qm, mk, mv)


def _mem_sample_kernel(q_ref, k_ref, v_ref, o_ref, *, sb, t_new):
    w = q_ref.shape[1]
    hd = w // MEM_HEADS
    rows = MEM_HEADS * t_new
    lane_head = lax.broadcasted_iota(jnp.int32, (t_new, w), 1) // hd
    for s_i in range(sb):
        q = q_ref[s_i * t_new:(s_i + 1) * t_new, :].astype(F32)
        qbd = jnp.concatenate([jnp.where(lane_head == h, q, 0.0) for h in range(MEM_HEADS)], axis=0)
        k = k_ref[s_i].astype(BF16)
        v = v_ref[s_i].astype(BF16)
        s = _dot_nt(qbd, k)
        m = jnp.max(s, axis=-1, keepdims=True)
        e = jnp.exp(s - m)
        o = _dot(e / jnp.sum(e, axis=-1, keepdims=True), v)
        o_ref[s_i * t_new:(s_i + 1) * t_new, :] = jnp.concatenate(
            [o[h * t_new:(h + 1) * t_new, h * hd:(h + 1) * hd] for h in range(MEM_HEADS)], axis=1).astype(BF16)


def _mem_sample(qm, ck, cv, row0_blocks, db, t_new, sb):
    w = qm.shape[1]
    mlen = ck.shape[1]
    kern = functools.partial(_mem_sample_kernel, sb=sb, t_new=t_new)
    return pl.pallas_call(
        kern,
        grid=(db // sb,),
        in_specs=[pl.BlockSpec((sb * t_new, w), lambda i: (row0_blocks + i, 0)),
                  pl.BlockSpec((sb, mlen, w), lambda i: (i, 0, 0)),
                  pl.BlockSpec((sb, mlen, w), lambda i: (i, 0, 0))],
        out_specs=pl.BlockSpec((sb * t_new, w), lambda i: (i, 0)),
        out_shape=jax.ShapeDtypeStruct((db * t_new, w), BF16),
        compiler_params=_cp(("parallel",)),
        name="mem_sample",
    )(qm, ck, cv)


def _merge_kernel(x_ref, og_ref, on_ref, om_ref, gate_ref, wg_ref, wn_ref, wm_ref, wo_ref, gf_ref,
                  wrh_ref, wrl_ref, br_ref, h_ref, hn_ref, cmb_ref, *, n_experts, n_tiles):
    @pl.when(pl.program_id(0) >= n_tiles)
    def _():
        h_ref[...] = jnp.zeros(h_ref.shape, h_ref.dtype)
        hn_ref[...] = jnp.zeros(hn_ref.shape, hn_ref.dtype)
        cmb_ref[...] = jnp.zeros(cmb_ref.shape, cmb_ref.dtype)

    @pl.when(pl.program_id(0) < n_tiles)
    def _():
        _merge_body(x_ref, og_ref, on_ref, om_ref, gate_ref, wg_ref, wn_ref, wm_ref, wo_ref, gf_ref,
                    wrh_ref, wrl_ref, br_ref, h_ref, hn_ref, cmb_ref, n_experts)


def _merge_body(x_ref, og_ref, on_ref, om_ref, gate_ref, wg_ref, wn_ref, wm_ref, wo_ref, gf_ref,
                wrh_ref, wrl_ref, br_ref, h_ref, hn_ref, cmb_ref, n_experts):
    d = x_ref.shape[1]
    gate = gate_ref[...].astype(F32)
    y = (gate[:, 0:d] * jnp.dot(og_ref[...], wg_ref[...], preferred_element_type=F32)
         + gate[:, d:2 * d] * jnp.dot(on_ref[...], wn_ref[...], preferred_element_type=F32)
         + gate[:, 2 * d:3 * d] * jnp.dot(om_ref[...], wm_ref[...], preferred_element_type=F32))
    h = x_ref[...] + jnp.dot(y.astype(BF16), wo_ref[...], preferred_element_type=F32)
    h_ref[...] = h
    hn = h * lax.rsqrt(jnp.mean(h * h, axis=-1, keepdims=True) + EPS) * gf_ref[...]
    hn_ref[...] = hn.astype(BF16)
    logits = _dot_hi(hn, wrh_ref[...], wrl_ref[...]) + br_ref[...]
    lane = lax.broadcasted_iota(jnp.int32, logits.shape, 1).astype(F32)
    work = jnp.where(lane < n_experts, logits, NEG)
    top = jnp.max(work, axis=-1, keepdims=True)
    chosen = jnp.zeros(logits.shape, F32)
    for _ in range(TOP_K):
        mx = jnp.max(work, axis=-1, keepdims=True)
        first = jnp.min(jnp.where(work == mx, lane, float(LANES)), axis=-1, keepdims=True)
        pick = lane == first
        chosen = jnp.where(pick, 1.0, chosen)
        work = jnp.where(pick, NEG, work)
    e = jnp.where(chosen > 0.5, jnp.exp(logits - top), 0.0)
    cmb_ref[...] = e / jnp.sum(e, axis=-1, keepdims=True)


def _merge(x, og, on, om, gate, wg, wn, wm, wo, gf, wrh, wrl, br, n_experts, tm, n_pad):
    n, d = x.shape
    n_tiles = n // tm
    row = lambda a: pl.BlockSpec((tm, a.shape[1]), lambda i: (jnp.minimum(i, n_tiles - 1), 0))
    full = lambda a: pl.BlockSpec(a.shape, lambda i: (0, 0))
    kern = functools.partial(_merge_kernel, n_experts=n_experts, n_tiles=n_tiles)
    return pl.pallas_call(
        kern,
        grid=(n_pad // tm,),
        in_specs=[row(x), row(og), row(on), row(om), row(gate), full(wg), full(wn), full(wm), full(wo), full(gf),
                  full(wrh), full(wrl), full(br)],
        out_specs=[pl.BlockSpec((tm, d), lambda i: (i, 0)), pl.BlockSpec((tm, d), lambda i: (i, 0)),
                   pl.BlockSpec((tm, LANES), lambda i: (i, 0))],
        out_shape=[jax.ShapeDtypeStruct((n_pad, d), F32), jax.ShapeDtypeStruct((n_pad, d), BF16),
                   jax.ShapeDtypeStruct((n_pad, LANES), F32)],
        compiler_params=_cp(("parallel",)),
        name="merge_router",
    )(x, og, on, om, gate, wg, wn, wm, wo, gf, wrh, wrl, br)


MOE_SUBTILE = 1024
MOE_SUBTILES = 2
MOE_ROW_BLOCK = 160


SPLIT_COLS = 512


def _split_gate_up_kernel(w_ref, wg_ref, wu_ref):
    half = SPLIT_COLS // 2
    r = lax.broadcasted_iota(jnp.int32, (SPLIT_COLS, half), 0)
    c = lax.broadcasted_iota(jnp.int32, (SPLIT_COLS, half), 1)
    even = jnp.where(r == 2 * c, 1.0, 0.0).astype(BF16)
    odd = jnp.where(r == 2 * c + 1, 1.0, 0.0).astype(BF16)
    for t in range(w_ref.shape[1] // SPLIT_COLS):
        blk = w_ref[:, t * SPLIT_COLS:(t + 1) * SPLIT_COLS].astype(BF16)
        wg_ref[:, t * half:(t + 1) * half] = jnp.dot(blk, even, preferred_element_type=F32).astype(BF16)
        wu_ref[:, t * half:(t + 1) * half] = jnp.dot(blk, odd, preferred_element_type=F32).astype(BF16)


def _split_gate_up(w):
    ne, d, dff2 = w.shape
    sd = jax.ShapeDtypeStruct((ne, d, dff2 // 2), BF16)
    out = pl.BlockSpec((None, d, dff2 // 2), lambda e: (e, 0, 0))
    return pl.pallas_call(
        _split_gate_up_kernel,
        grid=(ne,),
        in_specs=[pl.BlockSpec((None, d, dff2), lambda e: (e, 0, 0))],
        out_specs=[out, out],
        out_shape=[sd, sd],
        compiler_params=_cp(("parallel",)),
        name="split_gate_up",
    )(w)


def _moe_kernel(hn_ref, h_ref, cmb_ref, wg_ref, wu_ref, bg_ref, bu_ref, wd_ref, bd_ref, y_ref,
                rank_ref, rankt_ref, cmbt_ref, cnt_ref, *, ts):
    e = pl.program_id(1)
    ns = hn_ref.shape[0] // ts
    rb = MOE_ROW_BLOCK

    @pl.when(e == 0)
    def _():
        y_ref[...] = h_ref[...]
        lower = jnp.where(lax.broadcasted_iota(jnp.int32, (ts, ts), 0) > lax.broadcasted_iota(jnp.int32, (ts, ts), 1),
                          1.0, 0.0).astype(BF16)
        for s in range(ns):
            cmb = cmb_ref[s * ts:(s + 1) * ts, :]
            sel = jnp.where(cmb > 0.0, 1.0, 0.0)
            rank = jnp.where(sel > 0.0, jnp.dot(lower, sel.astype(BF16), preferred_element_type=F32), -1.0)
            rank_ref[s] = rank
            rankt_ref[s] = rank.T
            cmbt_ref[s] = cmb.T
            cnt_ref[s] = jnp.broadcast_to(jnp.sum(sel, axis=0, keepdims=True), cnt_ref.shape[1:])

    lane = lax.broadcasted_iota(jnp.int32, (1, LANES), 1)
    pick = jnp.where(lax.broadcasted_iota(jnp.int32, (LANES, LANES), 0) == e, 1.0, 0.0).astype(BF16)
    n_iter = 0
    rank_cols = []
    for s in range(ns):
        cnt = jnp.sum(jnp.where(lane == e, cnt_ref[s, 0:1, :], 0.0)).astype(jnp.int32)
        n_iter = jnp.maximum(n_iter, (cnt + rb - 1) // rb)
        r = rank_ref[s]
        r_hi = r.astype(BF16)
        r_lo = (r - r_hi.astype(F32)).astype(BF16)
        rc = (jnp.dot(r_hi, pick, preferred_element_type=F32)
              + jnp.dot(r_lo, pick, preferred_element_type=F32))
        rank_cols.append(jnp.concatenate([rc] * (-(-rb // LANES)), axis=1)[:, 0:rb])

    def body(i, carry):
        r0 = i * rb
        rows = (r0 + lax.broadcasted_iota(jnp.int32, (rb, ts), 0)).astype(F32)
        xs, ws = [], []
        for s in range(ns):
            hit = rankt_ref[s, pl.ds(e, 1), :] == rows
            xs.append(jnp.dot(jnp.where(hit, 1.0, 0.0).astype(BF16), hn_ref[s * ts:(s + 1) * ts, :],
                              preferred_element_type=F32).astype(BF16))
            ws.append(jnp.sum(jnp.where(hit, cmbt_ref[s, pl.ds(e, 1), :], 0.0), axis=-1, keepdims=True))
        x = jnp.concatenate(xs, axis=0)
        gate = jnp.minimum(jnp.dot(x, wg_ref[...], preferred_element_type=F32) + bg_ref[...], SWIGLU_LIMIT)
        up = jnp.clip(jnp.dot(x, wu_ref[...], preferred_element_type=F32) + bu_ref[...], -SWIGLU_LIMIT, SWIGLU_LIMIT)
        act = (up + 1.0) * gate * _sigmoid(SWIGLU_ALPHA * gate)
        out = jnp.dot(act.astype(BF16), wd_ref[...], preferred_element_type=F32) + bd_ref[...]
        cols = (r0 + lax.broadcasted_iota(jnp.int32, (ts, rb), 1)).astype(F32)
        for s in range(ns):
            yw = (out[s * rb:(s + 1) * rb, :] * ws[s]).astype(BF16)
            pt = jnp.where(rank_cols[s] == cols, 1.0, 0.0).astype(BF16)
            y_ref[s * ts:(s + 1) * ts, :] += jnp.dot(pt, yw, preferred_element_type=F32)
        return carry

    lax.fori_loop(0, n_iter, body, 0)


def _moe(hn, h, cmb, wg, wu, bg, bu, wd, bd, ts, ns):
    n, d = h.shape
    ne, _, dff = wg.shape
    tt = ts * ns
    once = lambda w: pl.BlockSpec((tt, w), lambda i, e: (i, 0), pipeline_mode=pl.Buffered(1))
    per_expert = lambda a: pl.BlockSpec((None,) + a.shape[1:], lambda i, e: (e, 0, 0))
    return pl.pallas_call(
        functools.partial(_moe_kernel, ts=ts),
        grid=(n // tt, ne),
        in_specs=[once(d), once(d), once(LANES), per_expert(wg), per_expert(wu), per_expert(bg), per_expert(bu),
                  per_expert(wd), per_expert(bd)],
        out_specs=pl.BlockSpec((tt, d), lambda i, e: (i, 0)),
        out_shape=jax.ShapeDtypeStruct((n, d), F32),
        scratch_shapes=[pltpu.VMEM((ns, ts, LANES), F32), pltpu.VMEM((ns, LANES, ts), F32),
                        pltpu.VMEM((ns, LANES, ts), F32), pltpu.VMEM((ns, 8, LANES), F32)],
        compiler_params=_cp(("parallel", "arbitrary")),
        name="moe",
    )(hn, h, cmb, wg, wu, bg, bu, wd, bd)


def _rope_table(pos):
    half = 8
    inv = 1.0 / (ROPE_THETA ** (jnp.arange(half, dtype=F32) / half))
    ang = pos.astype(F32)[:, None] * inv[None, :]
    cos, sin = jnp.cos(ang), jnp.sin(ang)
    n = pos.shape[0]
    one = jnp.ones((n, 64 - 2 * half), F32)
    zero8 = jnp.zeros((n, half), F32)
    zero = jnp.zeros((n, 64 - 2 * half), F32)
    c = jnp.concatenate([cos, cos, one], axis=1)
    s1 = jnp.concatenate([-sin, zero8, zero], axis=1)
    s2 = jnp.concatenate([zero8, sin, zero], axis=1)
    return jnp.concatenate([c, c, s1, s1, s2, s2], axis=1)


def _hi_lo(w):
    hi = w.astype(BF16)
    return hi, (w - hi.astype(F32)).astype(BF16)


def kernel(x_prompt, x_sample, cache_cmp_k, cache_cmp_v, cache_sel_k, cache_sel_v, cache_win_k, cache_win_v, state_gla, cache_mem_k, cache_mem_v, page_table, mem_prompt, g_attn, w_in, w_gla_gate, b_gla_gate, g_gla_out, g_q_nsa, g_k_nsa, w_cmp_pos, g_q_mem, g_k_mem, g_mem, w_mem_kv, w_up_gla, w_up_nsa, w_up_mem, w_out, g_ffn, w_router, b_router, w_gate_up, b_gate_up, w_down, b_down):
    B, S, D = x_prompt.shape
    DB, T, _ = x_sample.shape
    n_pool, page = cache_cmp_k.shape[1:3]
    n_pages = page_table.shape[1]
    past_len = n_pages * page
    wb = cache_win_k.shape[2]
    mlen = mem_prompt.shape[1]
    ne = w_router.shape[2]
    Np, Ns = B * S, DB * T
    N = Np + Ns
    TM = 256
    assert cache_cmp_k.shape[0] == 1 and D == 1024 and S % TM == 0 and Ns % TM == 0 and TM % T == 0
    assert T <= NSA_BLOCK and past_len % NSA_BLOCK == 0 and S % NSA_BLOCK == 0

    gq, gk, gv, gr, glr, nq, nkv, ng, mq, mg = np.cumsum(
        [0, 256, 256, 512, 512, GLA_GATE_RANK, 512, 768, 3 * NSA_HEADS, 512]).tolist()
    w = w_in[0]
    wa = w[:, gq:glr].astype(BF16)
    wbm = jnp.concatenate([w[:, nq:ng], w[:, mq:mg]], axis=1).astype(BF16)
    wc = w[:, mg:].astype(BF16)
    wd = jnp.concatenate([w[:, glr:nq], w[:, ng:mq], jnp.zeros((D, LANES - GLA_GATE_RANK - 3 * NSA_HEADS), F32)],
                         axis=1).astype(BF16)

    x_all = jnp.concatenate([x_prompt.reshape(Np, D), x_sample.reshape(Ns, D)], axis=0)
    za, zb, zc, zd = _project(x_all, g_attn, wa, wbm, wc, wd, TM)

    pos_sample = past_len + jnp.arange(T, dtype=jnp.int32)
    tab_p = _rope_table(jnp.arange(S, dtype=jnp.int32))
    tab_s = jnp.tile(_rope_table(pos_sample), (TM // T, 1))
    gq_t = jnp.tile(g_q_nsa[0], NSA_HEADS)[None, :]
    gk_t = jnp.tile(g_k_nsa[0], (1, NSA_KV_HEADS))
    gm_t = jnp.tile(g_q_mem[0], MEM_HEADS)[None, :]
    (kc_p, vc_p, kct, vct, kst, vst, kwt, vwt, qp, ktsb, ktwb, vs1, vw1, qm_p) = _prep_prompt(
        zb, tab_p, gq_t, gk_t, gm_t, TM, B, S)
    q, kc, vc, ks, vs, kw, vw, qm_s = _prep_sample(zb, tab_s, gq_t, gk_t, gm_t, TM, Np, Ns)

    w2 = jnp.tile(w_cmp_pos[0], (1, 1, NSA_KV_HEADS))
    ck, cv = _compress(kc_p, vc_p, w2, Np, 512)
    o_nsa_p = _nsa_prompt(qp, ck, cv, ktsb, ktwb, vs1, vw1, zd, B, S, 128)

    rows_minor = lambda a: jnp.transpose(a[0], (0, 2, 3, 1)).reshape(a.shape[1], LANES, a.shape[2])
    pool = rows_minor
    npg = 32 if n_pages % 64 == 0 else n_pages // 2
    wt = jnp.tile(jnp.transpose(w2, (0, 2, 1)), (1, 1, page // NSA_BLOCK))
    o_nsa_s = _nsa_sample(page_table, pool(cache_cmp_k), pool(cache_cmp_v), pool(cache_sel_k), pool(cache_sel_v),
                          q, kc, vc, ks, vs, kw, vw, zd, rows_minor(cache_win_k), rows_minor(cache_win_v),
                          wt, w2, T, npg, Np // T)
    hd_cols = np.arange(NSA_BLOCK)
    nsa_cols = np.concatenate([np.concatenate([c * NSA_BLOCK + hd_cols, (NSA_GROUP + c) * NSA_BLOCK + hd_cols])
                               for c in range(NSA_GROUP)])
    o_nsa = jnp.concatenate([o_nsa_p, o_nsa_s[:, nsa_cols].astype(BF16)], axis=0)

    wgh, wgl = _hi_lo(w_gla_gate[0])
    bg = b_gla_gate[0][None, :]
    gout = g_gla_out[0][None, :]
    s0_p = jnp.zeros((B,) + state_gla.shape[2:], F32)
    o_gla_p, s_gla_p = _gla(za, zd, wgh, wgl, bg, gout, s0_p, B, S, 512, 0, BF16)
    o_gla_s, s_gla_s = _gla(za, zd, wgh, wgl, bg, gout, state_gla[0], DB, T, T, Np, F32)
    o_gla = jnp.concatenate([o_gla_p, o_gla_s.astype(BF16)], axis=0)

    gkm = jnp.tile(g_k_mem[0], MEM_HEADS)[None, :]
    mem_k, mem_v = _mem_kv(mem_prompt.reshape(B * mlen, D), g_mem, w_mem_kv[0].astype(BF16), gkm, TM)
    o_mem_p = _mem_prompt(qm_p, mem_k, mem_v, B, S, mlen, 512)
    mw = MEM_HEADS * cache_mem_k.shape[-1]
    sb = 8
    o_mem_s = _mem_sample(qm_s, cache_mem_k[0].reshape(DB, mlen, mw), cache_mem_v[0].reshape(DB, mlen, mw),
                          0, DB, T, sb)
    o_mem = jnp.concatenate([o_mem_p, o_mem_s], axis=0)

    moe_ts = MOE_SUBTILE if N % MOE_SUBTILE == 0 else TM
    n_pad = -(-N // (moe_ts * MOE_SUBTILES)) * (moe_ts * MOE_SUBTILES)
    wr = jnp.pad(w_router[0], ((0, 0), (0, LANES - ne)))
    wrh, wrl = _hi_lo(wr)
    br = jnp.pad(b_router[0], (0, LANES - ne))[None, :]
    h, hn, cmb = _merge(x_all, o_gla, o_nsa, o_mem, zc, w_up_gla[0].astype(BF16), w_up_nsa[0][nsa_cols].astype(BF16),
                        w_up_mem[0].astype(BF16), w_out[0].astype(BF16), g_ffn, wrh, wrl, br, ne, TM, n_pad)

    wg, wu = _split_gate_up(w_gate_up[0])
    y = _moe(hn, h, cmb, wg, wu, b_gate_up[0][:, None, 0::2], b_gate_up[0][:, None, 1::2],
             w_down[0].astype(BF16), b_down[0][:, None, :], moe_ts, MOE_SUBTILES)

    p_rows = lambda a: jnp.transpose(a.reshape(1, B, NSA_KV_HEADS, NSA_BLOCK, S), (0, 1, 4, 2, 3))
    s_rows = lambda a: a.reshape(1, DB, T, NSA_KV_HEADS, NSA_BLOCK)
    wbp = min(NSA_WINDOW, S)
    s_win = lambda cache, new: jnp.concatenate([cache[:, :, T:], s_rows(new)], axis=2)
    mshape = (1, B, mlen, MEM_HEADS, mw // MEM_HEADS)
    return (y[:Np].reshape(B, S, D), y[Np:N].reshape(DB, T, D),
            p_rows(kct), p_rows(vct), p_rows(kst), p_rows(vst),
            p_rows(kwt)[:, :, S - wbp:], p_rows(vwt)[:, :, S - wbp:],
            s_gla_p[None], mem_k.reshape(mshape), mem_v.reshape(mshape),
            s_rows(kc), s_rows(vc), s_rows(ks), s_rows(vs),
            s_win(cache_win_k, kw), s_win(cache_win_v, vw), s_gla_s[None])
```

```python
import functools

import numpy as np
import jax
import jax.numpy as jnp
from jax import lax
from jax.experimental import pallas as pl
from jax.experimental.pallas import tpu as pltpu

F32 = jnp.float32
BF16 = jnp.bfloat16

GLA_HEADS = 4
GLA_GATE_RANK = 16
GLA_TAU = 16.0
GLA_CHUNK = 64
GLA_SUB = 16
NSA_HEADS = 8
NSA_KV_HEADS = 2
NSA_GROUP = NSA_HEADS // NSA_KV_HEADS
NSA_BLOCK = 64
NSA_TOPN = 16
NSA_WINDOW = 512
MEM_HEADS = 4
ROPE_THETA = 500000.0
N_BRANCH = 3
TOP_K = 4
SWIGLU_LIMIT = 7.0
SWIGLU_ALPHA = 1.702
EPS = 1e-6

LANES = 128
NEG = -1e30
VMEM_LIMIT = 56 * 1024 * 1024


def _cp(sem):
    return pltpu.CompilerParams(dimension_semantics=sem, vmem_limit_bytes=VMEM_LIMIT)


def _dot(a, b):
    return jnp.dot(a.astype(BF16), b.astype(BF16), preferred_element_type=F32)


def _dot_nt(a, b):
    return lax.dot_general(a.astype(BF16), b.astype(BF16), (((1,), (1,)), ((), ())),
                           preferred_element_type=F32)


def _dot_tn(a, b):
    return lax.dot_general(a.astype(BF16), b.astype(BF16), (((0,), (0,)), ((), ())),
                           preferred_element_type=F32)


def _split3(x):
    hi = x.astype(BF16)
    r = x - hi.astype(F32)
    mid = r.astype(BF16)
    lo = (r - mid.astype(F32)).astype(BF16)
    return hi, mid, lo


def _dot_exact_rhs(x, m):
    hi, mid, lo = _split3(x)
    d = functools.partial(jnp.dot, preferred_element_type=F32)
    return d(hi, m) + d(mid, m) + d(lo, m)


def _dot_exact_lhs(m, x):
    hi, mid, lo = _split3(x)
    d = functools.partial(jnp.dot, preferred_element_type=F32)
    return d(m, hi) + d(m, mid) + d(m, lo)


def _dot_hi(x, w_hi, w_lo):
    hi, mid, _ = _split3(x)
    d = functools.partial(jnp.dot, preferred_element_type=F32)
    return d(hi, w_hi) + d(mid, w_hi) + d(hi, w_lo)


def _msoftmax(s, mask):
    s = jnp.where(mask, s, NEG)
    m = jnp.max(s, axis=-1, keepdims=True)
    e = jnp.where(mask, jnp.exp(s - m), 0.0)
    return e / jnp.maximum(jnp.sum(e, axis=-1, keepdims=True), 1e-30)


def _seg_ones(width, seg):
    r = lax.broadcasted_iota(jnp.int32, (width, width), 0) // seg
    c = lax.broadcasted_iota(jnp.int32, (width, width), 1) // seg
    return jnp.where(r == c, 1.0, 0.0).astype(BF16)


def _seg_rms(x, seg, gain):
    ms = _dot_exact_rhs(x * x, _seg_ones(x.shape[-1], seg)) * (1.0 / seg)
    return x * lax.rsqrt(ms + EPS) * gain


def _sigmoid(x):
    return 1.0 / (1.0 + jnp.exp(-x))


def _two_part_specs(parts, tm):
    n_first = parts[0].shape[0] // tm
    n_second = parts[1].shape[0] // tm
    first = pl.BlockSpec((tm, parts[0].shape[1]), lambda i: (jnp.minimum(i, n_first - 1), 0))
    second = pl.BlockSpec((tm, parts[1].shape[1]), lambda i: (jnp.clip(i - n_first, 0, n_second - 1), 0))
    return [first, second]


def _proj_kernel(xp_ref, xs_ref, *refs, n_first):
    @pl.when(pl.program_id(0) < n_first)
    def _():
        _proj_body(xp_ref, *refs)

    @pl.when(pl.program_id(0) >= n_first)
    def _():
        _proj_body(xs_ref, *refs)


def _proj_body(x_ref, g_ref, wa_ref, wb_ref, wc_ref, wd_ref, za_ref, zb_ref, zc_ref, zd_ref):
    x = x_ref[...]
    u = x * lax.rsqrt(jnp.mean(x * x, axis=-1, keepdims=True) + EPS) * g_ref[...]
    ub = u.astype(BF16)
    za_ref[...] = jnp.dot(ub, wa_ref[...], preferred_element_type=F32)
    zb_ref[...] = jnp.dot(ub, wb_ref[...], preferred_element_type=F32)
    zc_ref[...] = _sigmoid(jnp.dot(ub, wc_ref[...], preferred_element_type=F32)).astype(BF16)
    zd_ref[...] = jnp.dot(ub, wd_ref[...], preferred_element_type=F32)


def _project(xp, xs, g, wa, wb, wc, wd, tm):
    n = xp.shape[0] + xs.shape[0]
    full = lambda w: pl.BlockSpec(w.shape, lambda i: (0, 0))
    row = lambda w: pl.BlockSpec((tm, w), lambda i: (i, 0))
    return pl.pallas_call(
        functools.partial(_proj_kernel, n_first=xp.shape[0] // tm),
        grid=(n // tm,),
        in_specs=_two_part_specs((xp, xs), tm) + [full(g), full(wa), full(wb), full(wc), full(wd)],
        out_specs=[row(wa.shape[1]), row(wb.shape[1]), row(wc.shape[1]), row(wd.shape[1])],
        out_shape=[jax.ShapeDtypeStruct((n, wa.shape[1]), F32), jax.ShapeDtypeStruct((n, wb.shape[1]), F32),
                   jax.ShapeDtypeStruct((n, wc.shape[1]), BF16), jax.ShapeDtypeStruct((n, wd.shape[1]), F32)],
        compiler_params=_cp(("parallel",)),
        name="proj",
    )(xp, xs, g, wa, wb, wc, wd)


def _rope(x, tab):
    c, s1, s2 = tab[:, 0:LANES], tab[:, LANES:2 * LANES], tab[:, 2 * LANES:3 * LANES]
    half = 8
    return x * c + pltpu.roll(x, LANES - half, 1) * s1 + pltpu.roll(x, half, 1) * s2


def _prep_common(zb_ref, tab_ref, gq_ref, gk_ref, gm_ref):
    z = zb_ref[...]
    tab = tab_ref[...]
    hd = NSA_BLOCK
    nq = NSA_HEADS * hd
    qn = _seg_rms(z[:, 0:nq], hd, gq_ref[...])
    q = jnp.concatenate([_rope(qn[:, j * LANES:(j + 1) * LANES], tab) for j in range(nq // LANES)], axis=1)
    q = q * (hd ** -0.5)
    kv = z[:, nq:nq + 6 * LANES]
    gk = gk_ref[...]
    ks = [_rope(_seg_rms(kv[:, 2 * j * LANES:(2 * j + 1) * LANES], hd, gk[j:j + 1, :]), tab) for j in range(3)]
    vs = [kv[:, (2 * j + 1) * LANES:(2 * j + 2) * LANES] for j in range(3)]
    zm = z[:, nq + 6 * LANES:]
    mhd = zm.shape[1] // MEM_HEADS
    qm = (_seg_rms(zm, mhd, gm_ref[...]) * (mhd ** -0.5)).astype(BF16)
    return q, ks, vs, qm


def _prep_sample_kernel(zb_ref, tab_ref, gq_ref, gk_ref, gm_ref,
                        q_ref, kc_ref, vc_ref, ks_ref, vs_ref, kw_ref, vw_ref, qm_ref):
    q, ks, vs, qm = _prep_common(zb_ref, tab_ref, gq_ref, gk_ref, gm_ref)
    q_ref[...] = q
    for ref, val in zip((kc_ref, ks_ref, kw_ref, vc_ref, vs_ref, vw_ref), ks + vs):
        ref[...] = val
    qm_ref[...] = qm


def _prep_prompt_kernel(zb_ref, tab_ref, gq_ref, gk_ref, gm_ref,
                        kc_ref, vc_ref, kct_ref, vct_ref, kst_ref, vst_ref, kwt_ref, vwt_ref,
                        qp_ref, ktsb_ref, ktwb_ref, vs1_ref, vw1_ref, qm_ref):
    q, ks, vs, qm = _prep_common(zb_ref, tab_ref, gq_ref, gk_ref, gm_ref)
    kc_ref[...] = ks[0]
    vc_ref[...] = vs[0]
    for ref, val in zip((kct_ref, kst_ref, kwt_ref, vct_ref, vst_ref, vwt_ref), ks + vs):
        ref[...] = val.T
    ktsb_ref[...] = ks[1].T.astype(BF16)
    ktwb_ref[...] = ks[2].T.astype(BF16)
    low = lax.broadcasted_iota(jnp.int32, (q.shape[0], LANES), 1) < NSA_BLOCK
    vs1_ref[...] = jnp.concatenate([jnp.where(low, vs[1], 1.0), jnp.where(low, 1.0, vs[1])], axis=1).astype(BF16)
    vw1_ref[...] = jnp.concatenate([jnp.where(low, vs[2], 1.0), jnp.where(low, 1.0, vs[2])], axis=1).astype(BF16)
    parts = []
    for head in range(NSA_HEADS):
        g = head // NSA_GROUP
        chunk = q[:, (head // 2) * LANES:(head // 2 + 1) * LANES]
        if (head % 2) != g:
            chunk = pltpu.roll(chunk, NSA_BLOCK, 1)
        parts.append(jnp.where(low if g == 0 else jnp.logical_not(low), chunk, 0.0))
    qp_ref[...] = jnp.concatenate(parts, axis=1).astype(BF16)
    qm_ref[...] = qm


def _prep_sample(zb, tab, gq, gk, gm, tm, row0, n_rows):
    blk0 = row0 // tm
    row = lambda w: pl.BlockSpec((tm, w), lambda i: (i, 0))
    full = lambda a: pl.BlockSpec(a.shape, lambda i: (0, 0))
    sd = lambda w, dt=F32: jax.ShapeDtypeStruct((n_rows, w), dt)
    return pl.pallas_call(
        _prep_sample_kernel,
        grid=(n_rows // tm,),
        in_specs=[pl.BlockSpec((tm, zb.shape[1]), lambda i: (blk0 + i, 0)), full(tab), full(gq), full(gk), full(gm)],
        out_specs=[row(512)] + [row(LANES)] * 6 + [row(512)],
        out_shape=[sd(512)] + [sd(LANES)] * 6 + [sd(512, BF16)],
        compiler_params=_cp(("parallel",)),
        name="nsa_prep_sample",
    )(zb, tab, gq, gk, gm)


def _prep_prompt(zb, tab, gq, gk, gm, tm, batch, seq):
    nt = seq // tm
    n = batch * seq
    full = lambda a: pl.BlockSpec(a.shape, lambda b, i: (0, 0))
    row = lambda w: pl.BlockSpec((tm, w), lambda b, i: (b * nt + i, 0))
    tr = pl.BlockSpec((None, LANES, tm), lambda b, i: (b, 0, i))
    sd = lambda w, dt=F32: jax.ShapeDtypeStruct((n, w), dt)
    sdt = lambda dt=F32: jax.ShapeDtypeStruct((batch, LANES, seq), dt)
    return pl.pallas_call(
        _prep_prompt_kernel,
        grid=(batch, nt),
        in_specs=[row(zb.shape[1]), pl.BlockSpec((tm, 3 * LANES), lambda b, i: (i, 0)), full(gq), full(gk), full(gm)],
        out_specs=[row(LANES), row(LANES)] + [tr] * 6 + [row(NSA_HEADS * LANES), tr, tr, row(2 * LANES), row(2 * LANES),
                                                       row(512)],
        out_shape=[sd(LANES), sd(LANES)] + [sdt()] * 6 + [sd(NSA_HEADS * LANES, BF16), sdt(BF16), sdt(BF16),
                                                          sd(2 * LANES, BF16), sd(2 * LANES, BF16), sd(512, BF16)],
        compiler_params=_cp(("parallel", "parallel")),
        name="nsa_prep_prompt",
    )(zb, tab, gq, gk, gm)


def _pool_rows(x, w):
    nblk = x.shape[0] // NSA_BLOCK
    return jnp.sum(x.reshape(nblk, NSA_BLOCK, LANES) * w[None], axis=1)


def _compress_kernel(k_ref, v_ref, w_ref, ck_ref, cv_ref):
    ck_ref[...] = _pool_rows(k_ref[...], w_ref[0])
    cv_ref[...] = _pool_rows(v_ref[...], w_ref[1])


def _compress(kc, vc, w2, n_rows, tm):
    row = pl.BlockSpec((tm, LANES), lambda i: (i, 0))
    out = pl.BlockSpec((tm // NSA_BLOCK, LANES), lambda i: (i, 0))
    sd = jax.ShapeDtypeStruct((n_rows // NSA_BLOCK, LANES), F32)
    return pl.pallas_call(
        _compress_kernel,
        grid=(n_rows // tm,),
        in_specs=[row, row, pl.BlockSpec(w2.shape, lambda i: (0, 0, 0))],
        out_specs=[out, out],
        out_shape=[sd, sd],
        compiler_params=_cp(("parallel",)),
        name="compress",
    )(kc, vc, w2)


def _select(imp, cur, n_cand):
    n_iota = lax.broadcasted_iota(jnp.int32, imp.shape, 1)
    cnt = jnp.zeros(imp.shape, F32)
    for m in range(n_cand):
        col = imp[:, m:m + 1]
        beats = (col > imp) | ((col == imp) & (n_iota > m))
        cnt = cnt + jnp.where(beats & (cur > m), 1.0, 0.0)
    keep = (n_iota == cur) | ((n_iota < cur) & (cnt < NSA_TOPN - 1))
    return jnp.where(keep, 1.0, 0.0)


def _expand_blocks(nb, nkeys, first_key=0):
    kb = (lax.broadcasted_iota(jnp.int32, (nb, nkeys), 1) + first_key) // NSA_BLOCK
    return jnp.where(kb == lax.broadcasted_iota(jnp.int32, (nb, nkeys), 0), 1.0, 0.0).astype(BF16)


def _gate_col(gate, head, j):
    c = GLA_GATE_RANK + head * 3 + j
    return _sigmoid(gate[:, c:c + 1])


def _select_t(imp_t, cur_t, n_cand):
    n_iota = lax.broadcasted_iota(jnp.int32, imp_t.shape, 0)
    cnt = jnp.zeros(imp_t.shape, F32)
    for m in range(n_cand):
        row = imp_t[m:m + 1, :]
        beats = (row > imp_t) | ((row == imp_t) & (n_iota > m))
        cnt = cnt + jnp.where(beats & (cur_t > m), 1.0, 0.0)
    keep = (n_iota == cur_t) | ((n_iota < cur_t) & (cnt < NSA_TOPN - 1))
    return jnp.where(keep, 1.0, 0.0)


def _attend(qh, kt, bias, v_ones):
    s = jnp.dot(qh, kt, preferred_element_type=F32) + bias
    e = jnp.exp((s - jnp.max(s, axis=-1, keepdims=True)).astype(BF16))
    pv = jnp.dot(e, v_ones, preferred_element_type=F32)
    return pv / pltpu.roll(pv, NSA_BLOCK, 1)


def _nsa_prompt_kernel(q_ref, ck_ref, cv_ref, kts_ref, ktw_ref, vs1_ref, vw1_ref, gate_ref, o_ref, osel_ref,
                       *, tq, seq, wslab, nbr):
    i = pl.program_id(1)
    nq = pl.num_programs(1)
    t0 = i * tq
    nb = seq // NSA_BLOCK
    q = q_ref[...]
    pos = t0 + lax.broadcasted_iota(jnp.int32, (tq, 1), 0)
    pos_t = t0 + lax.broadcasted_iota(jnp.int32, (1, tq), 1)
    cur_t = pos_t // NSA_BLOCK
    vis_t = (lax.broadcasted_iota(jnp.int32, (nb, tq), 0) + 1) * NSA_BLOCK <= pos_t + 1
    wstart = pl.multiple_of(jnp.maximum(t0 + tq - wslab, 0), LANES)
    wrel = pos - (wstart + lax.broadcasted_iota(jnp.int32, (tq, wslab), 1))
    wbias = jnp.where((wrel >= 0) & (wrel < NSA_WINDOW), 0.0, NEG)
    ckb = ck_ref[...].astype(BF16)
    cvb = cv_ref[...].astype(BF16)
    cmp_r, win_r = [], []
    for g in range(NSA_KV_HEADS):
        heads = [g * NSA_GROUP + h for h in range(NSA_GROUP)]
        imp_t = jnp.zeros((nb, tq), F32)
        for head in heads:
            qh = q[:, head * LANES:(head + 1) * LANES]
            s_t = jnp.where(vis_t, _dot_nt(ckb, qh), NEG)
            e_t = jnp.where(vis_t, jnp.exp(s_t - jnp.max(s_t, axis=0, keepdims=True)), 0.0)
            pr_t = e_t / jnp.maximum(jnp.sum(e_t, axis=0, keepdims=True), 1e-30)
            imp_t = imp_t + pr_t
            cmp_r.append(_dot_tn(pr_t, cvb))
        sel_t = _select_t(imp_t, cur_t, nb).astype(BF16)
        for j in range(nbr):
            klen = (j + 1) * seq // nbr

            @pl.when(i * nbr // nq == j)
            def _(g=g, klen=klen, sel_t=sel_t, heads=heads):
                kpos = lax.broadcasted_iota(jnp.int32, (tq, klen), 1)
                picked = _dot_tn(sel_t, _expand_blocks(nb, klen)) > 0.5
                sbias = jnp.where(picked & (kpos <= pos), 0.0, NEG)
                kt = kts_ref[:, 0:klen]
                v1 = vs1_ref[0:klen, g * LANES:(g + 1) * LANES]
                for head in heads:
                    hl = slice(head * LANES, (head + 1) * LANES)
                    osel_ref[:, hl] = _attend(q[:, hl], kt, sbias, v1)

        kt = ktw_ref[:, pl.ds(wstart, wslab)]
        v1 = vw1_ref[pl.ds(wstart, wslab), g * LANES:(g + 1) * LANES]
        for head in heads:
            win_r.append(_attend(q[:, head * LANES:(head + 1) * LANES], kt, wbias, v1))

    low = lax.broadcasted_iota(jnp.int32, (tq, LANES), 1) < NSA_BLOCK
    pair = lambda r: jnp.concatenate([jnp.where(low, r[c], r[NSA_GROUP + c]) for c in range(NSA_GROUP)], axis=1)
    sel_r = [osel_ref[:, head * LANES:(head + 1) * LANES] for head in range(NSA_HEADS)]
    sig = _sigmoid(gate_ref[...])
    s_hi = sig.astype(BF16)
    s_lo = (sig - s_hi.astype(F32)).astype(BF16)
    width = NSA_GROUP * LANES
    row = lax.broadcasted_iota(jnp.int32, (LANES, width), 0)
    col = lax.broadcasted_iota(jnp.int32, (LANES, width), 1)
    head_of = jnp.where(col % LANES < NSA_BLOCK, col // LANES, NSA_GROUP + col // LANES)
    out = None
    for j, r in enumerate((cmp_r, sel_r, win_r)):
        pick = jnp.where(row == GLA_GATE_RANK + head_of * 3 + j, 1.0, 0.0).astype(BF16)
        gj = jnp.dot(s_hi, pick, preferred_element_type=F32) + jnp.dot(s_lo, pick, preferred_element_type=F32)
        out = gj * pair(r) if out is None else out + gj * pair(r)
    o_ref[...] = out.astype(BF16)


def _nsa_prompt(qp, ck, cv, kts, ktw, vs1, vw1, zd, batch, seq, tq):
    nq = seq // tq
    nb = seq // NSA_BLOCK
    wslab = min(seq, NSA_WINDOW + tq)
    nbr = 4 if nq % 4 == 0 else 1
    kern = functools.partial(_nsa_prompt_kernel, tq=tq, seq=seq, wslab=wslab, nbr=nbr)
    per_b = lambda a: pl.BlockSpec((None,) + a.shape[1:], lambda b, i: (b, 0, 0))
    rows_b = lambda a: pl.BlockSpec((seq, a.shape[1]), lambda b, i: (b, 0))
    return pl.pallas_call(
        kern,
        grid=(batch, nq),
        in_specs=[pl.BlockSpec((tq, qp.shape[1]), lambda b, i: (b * nq + i, 0)),
                  pl.BlockSpec((nb, LANES), lambda b, i: (b, 0)),
                  pl.BlockSpec((nb, LANES), lambda b, i: (b, 0)),
                  per_b(kts), per_b(ktw), rows_b(vs1), rows_b(vw1),
                  pl.BlockSpec((tq, LANES), lambda b, i: (b * nq + i, 0))],
        out_specs=pl.BlockSpec((tq, 512), lambda b, i: (b * nq + i, 0)),
        out_shape=jax.ShapeDtypeStruct((batch * seq, 512), BF16),
        scratch_shapes=[pltpu.VMEM((tq, NSA_HEADS * LANES), F32)],
        compiler_params=_cp(("parallel", "parallel")),
        name="nsa_prompt",
    )(qp, ck, cv, kts, ktw, vs1, vw1, zd)


def _qbd(q, rows_per_head):
    hd = NSA_BLOCK
    lane_grp = lax.broadcasted_iota(jnp.int32, (rows_per_head, LANES), 1) // hd
    parts = []
    for head in range(NSA_HEADS):
        g = head // NSA_GROUP
        qh = q[:, head * hd:(head + 1) * hd]
        two = jnp.concatenate([qh, qh], axis=1)
        parts.append(jnp.where(lane_grp == g, two, 0.0))
    return jnp.concatenate(parts, axis=0)


def _nsa_sample_cmp_kernel(pt_ref, *refs, npg, t_new, past_len):
    kpages = refs[0:npg]
    vpages = refs[npg:2 * npg]
    q_ref, kn_ref, vn_ref, wt_ref, wn_ref, ocmp_ref, sel_ref, pk_ref, pv_ref = refs[2 * npg:]
    c = pl.program_id(1)
    nch = pl.num_programs(1)
    hd = NSA_BLOCK
    page = kpages[0].shape[1]
    bpp = page // NSA_BLOCK
    nbc = npg * bpp
    nb_pad = pk_ref.shape[1]
    n_past = past_len // NSA_BLOCK

    @pl.when(c == 0)
    def _():
        pk_ref[...] = jnp.zeros(pk_ref.shape, F32)
        pv_ref[...] = jnp.zeros(pv_ref.shape, F32)

    seg = jnp.where(lax.broadcasted_iota(jnp.int32, (npg * page, nbc), 0) // NSA_BLOCK
                    == lax.broadcasted_iota(jnp.int32, (npg * page, nbc), 1), 1.0, 0.0).astype(BF16)
    col = lax.broadcasted_iota(jnp.int32, (nbc, nb_pad), 1)
    place = jnp.where(col == c * nbc + lax.broadcasted_iota(jnp.int32, (nbc, nb_pad), 0), 1.0, 0.0).astype(BF16)
    for pages, w_t, pooled in ((kpages, wt_ref[0], pk_ref), (vpages, wt_ref[1], pv_ref)):
        x = jnp.concatenate([r[...] * w_t for r in pages], axis=1)
        pooled[...] += _dot_exact_rhs(_dot_exact_rhs(x, seg), place)

    @pl.when(c == nch - 1)
    def _():
        newk = jnp.sum(kn_ref[...] * wn_ref[0][0:t_new], axis=0, keepdims=True)
        newv = jnp.sum(vn_ref[...] * wn_ref[1][0:t_new], axis=0, keepdims=True)
        q = q_ref[...]
        rows = NSA_HEADS * t_new
        qbd = _qbd(q, t_new)
        trow = lax.broadcasted_iota(jnp.int32, (rows, 1), 0) % t_new
        pos = past_len + trow
        blk = lax.broadcasted_iota(jnp.int32, (rows, nb_pad), 1)
        vis = ((blk + 1) * NSA_BLOCK <= pos + 1) & (blk < n_past)
        vis_new = (n_past + 1) * NSA_BLOCK <= pos + 1
        s_past = jnp.where(vis, _dot(qbd, pk_ref[...]), NEG)
        s_new = jnp.where(vis_new, jnp.sum(qbd * newk, axis=-1, keepdims=True), NEG)
        m = jnp.maximum(jnp.max(s_past, axis=-1, keepdims=True), s_new)
        e_past = jnp.where(vis, jnp.exp(s_past - m), 0.0)
        e_new = jnp.where(vis_new, jnp.exp(s_new - m), 0.0)
        den = jnp.maximum(jnp.sum(e_past, axis=-1, keepdims=True) + e_new, 1e-30)
        pr = e_past / den
        o = _dot_nt(pr, pv_ref[...]) + (e_new / den) * newv
        outs = []
        for head in range(NSA_HEADS):
            g = head // NSA_GROUP
            outs.append(o[head * t_new:(head + 1) * t_new, g * hd:(g + 1) * hd])
        ocmp_ref[...] = jnp.concatenate(outs, axis=1)
        imps = []
        for g in range(NSA_KV_HEADS):
            acc = pr[g * NSA_GROUP * t_new:(g * NSA_GROUP + 1) * t_new]
            for h in range(1, NSA_GROUP):
                acc = acc + pr[(g * NSA_GROUP + h) * t_new:(g * NSA_GROUP + h + 1) * t_new]
            imps.append(acc)
        imp = jnp.concatenate(imps, axis=0)
        cur = (past_len + lax.broadcasted_iota(jnp.int32, (NSA_KV_HEADS * t_new, 1), 0) % t_new) // NSA_BLOCK
        sel_ref[...] = _select(imp, cur, n_past)


def _nsa_sample_cmp(page_table, pool_k, pool_v, q, kn, vn, wt, wn, t_new, npg):
    db, n_pages = page_table.shape
    page = pool_k.shape[2]
    past_len = n_pages * page
    nch = n_pages // npg
    nb_pad = -(-(past_len // NSA_BLOCK) // LANES) * LANES
    row0 = q.shape[0] // t_new - db

    def page_spec(j):
        return pl.BlockSpec((None, LANES, page), lambda b, c, pt: (pt[b, c * npg + j], 0, 0))

    tok = lambda w: pl.BlockSpec((t_new, w), lambda b, c, pt: (row0 + b, 0))
    full3 = lambda a: pl.BlockSpec(a.shape, lambda b, c, pt: (0, 0, 0))
    kern = functools.partial(_nsa_sample_cmp_kernel, npg=npg, t_new=t_new, past_len=past_len)
    grid_spec = pltpu.PrefetchScalarGridSpec(
        num_scalar_prefetch=1,
        grid=(db, nch),
        in_specs=[page_spec(j) for j in range(npg)] + [page_spec(j) for j in range(npg)]
        + [tok(512), tok(LANES), tok(LANES), full3(wt), full3(wn)],
        out_specs=[pl.BlockSpec((None, t_new, 512), lambda b, c, pt: (b, 0, 0)),
                   pl.BlockSpec((None, NSA_KV_HEADS * t_new, nb_pad), lambda b, c, pt: (b, 0, 0))],
        scratch_shapes=[pltpu.VMEM((LANES, nb_pad), F32), pltpu.VMEM((LANES, nb_pad), F32)],
    )
    return pl.pallas_call(
        kern,
        grid_spec=grid_spec,
        out_shape=[jax.ShapeDtypeStruct((db, t_new, 512), F32),
                   jax.ShapeDtypeStruct((db, NSA_KV_HEADS * t_new, nb_pad), F32)],
        compiler_params=_cp(("parallel", "arbitrary")),
        name="nsa_sample_cmp",
    )(page_table, *([pool_k] * npg), *([pool_v] * npg), q, kn, vn, wt, wn)


def _online_update(s, mask, v, m_ref, l_ref, acc_ref, v_transposed=False):
    s = jnp.where(mask, s, NEG)
    m_old = m_ref[...]
    m_new = jnp.maximum(m_old, jnp.max(s, axis=-1, keepdims=True))
    alpha = jnp.exp(m_old - m_new)
    e = jnp.where(mask, jnp.exp(s - m_new), 0.0)
    l_ref[...] = alpha * l_ref[...] + jnp.sum(e, axis=-1, keepdims=True)
    acc_ref[...] = alpha * acc_ref[...] + (_dot_nt(e, v) if v_transposed else _dot(e, v))
    m_ref[...] = m_new


def _nsa_sample_sel_kernel(pt_ref, *refs, npg, t_new, past_len):
    kpages = refs[0:npg]
    vpages = refs[npg:2 * npg]
    (q_ref, sel_ref, ksn_ref, vsn_ref, wk_ref, wv_ref, kwn_ref, vwn_ref, ocmp_ref, gate_ref,
     o_ref, m_ref, l_ref, acc_ref, pad_ref) = refs[2 * npg:]
    c = pl.program_id(1)
    nch = pl.num_programs(1)
    hd = NSA_BLOCK
    page = kpages[0].shape[1]
    bpp = page // NSA_BLOCK
    nbc = npg * bpp
    rows = NSA_HEADS * t_new
    q = q_ref[...]
    qbd = _qbd(q, t_new)

    @pl.when(c == 0)
    def _():
        m_ref[...] = jnp.full(m_ref.shape, NEG, F32)
        l_ref[...] = jnp.zeros(l_ref.shape, F32)
        acc_ref[...] = jnp.zeros(acc_ref.shape, F32)

    sel = sel_ref[...]
    nb_pad = sel.shape[1]
    sel_rows = jnp.concatenate([sel[(head // NSA_GROUP) * t_new:(head // NSA_GROUP + 1) * t_new]
                                for head in range(NSA_HEADS)], axis=0)
    blk_iota = lax.broadcasted_iota(jnp.int32, (nb_pad, npg * page), 0)
    key_blk = c * nbc + lax.broadcasted_iota(jnp.int32, (nb_pad, npg * page), 1) // NSA_BLOCK
    expand = jnp.where(blk_iota == key_blk, 1.0, 0.0).astype(BF16)
    smask = _dot(sel_rows, expand) > 0.5
    kcat = jnp.concatenate([r[...].astype(BF16) for r in kpages], axis=1)
    vcat = jnp.concatenate([r[...].astype(BF16) for r in vpages], axis=1)
    _online_update(_dot(qbd, kcat), smask, vcat, m_ref, l_ref, acc_ref, v_transposed=True)

    @pl.when(c == nch - 1)
    def _():
        trow = lax.broadcasted_iota(jnp.int32, (rows, 1), 0) % t_new
        pad_ref[...] = jnp.zeros(pad_ref.shape, F32)
        pad_ref[0:t_new, :] = ksn_ref[...]
        kn = pad_ref[...]
        pad_ref[0:t_new, :] = vsn_ref[...]
        vn = pad_ref[...]
        jn = lax.broadcasted_iota(jnp.int32, (rows, LANES), 1)
        _online_update(_dot_nt(qbd, kn), jn <= trow, vn, m_ref, l_ref, acc_ref)
        o_sel = acc_ref[...] / jnp.maximum(l_ref[...], 1e-30)
        wb = wk_ref.shape[1]
        jw = lax.broadcasted_iota(jnp.int32, (rows, wb), 1)
        rel = trow + wb - jw
        wmask = jnp.concatenate([(rel >= 0) & (rel < NSA_WINDOW), jn <= trow], axis=1)
        pad_ref[0:t_new, :] = kwn_ref[...]
        s_win = jnp.concatenate([_dot(qbd, wk_ref[...]), _dot_nt(qbd, pad_ref[...])], axis=1)
        p_win = _msoftmax(s_win, wmask)
        pad_ref[0:t_new, :] = vwn_ref[...]
        o_win = _dot_nt(p_win[:, 0:wb], wv_ref[...]) + _dot(p_win[:, wb:], pad_ref[...])
        gate = gate_ref[...]
        ocmp = ocmp_ref[...]
        outs = []
        for head in range(NSA_HEADS):
            g = head // NSA_GROUP
            r0, r1 = head * t_new, (head + 1) * t_new
            outs.append(_gate_col(gate, head, 0) * ocmp[:, head * hd:(head + 1) * hd]
                        + _gate_col(gate, head, 1) * o_sel[r0:r1, g * hd:(g + 1) * hd]
                        + _gate_col(gate, head, 2) * o_win[r0:r1, g * hd:(g + 1) * hd])
        o_ref[...] = jnp.concatenate(outs, axis=1)


def _nsa_sample_sel(page_table, pool_k, pool_v, q, sel, ksn, vsn, win_k, win_v, kwn, vwn, ocmp, zd, t_new, npg):
    db, n_pages = page_table.shape
    page = pool_k.shape[2]
    past_len = n_pages * page
    nch = n_pages // npg
    row0 = q.shape[0] // t_new - db
    rows = NSA_HEADS * t_new

    def page_spec(j):
        return pl.BlockSpec((None, LANES, page), lambda b, c, pt: (pt[b, c * npg + j], 0, 0))

    tok = lambda w: pl.BlockSpec((t_new, w), lambda b, c, pt: (row0 + b, 0))
    per_seq = lambda a: pl.BlockSpec((None,) + a.shape[1:], lambda b, c, pt: (b, 0, 0))
    kern = functools.partial(_nsa_sample_sel_kernel, npg=npg, t_new=t_new, past_len=past_len)
    grid_spec = pltpu.PrefetchScalarGridSpec(
        num_scalar_prefetch=1,
        grid=(db, nch),
        in_specs=[page_spec(j) for j in range(npg)] + [page_spec(j) for j in range(npg)]
        + [tok(512), per_seq(sel), tok(LANES), tok(LANES), per_seq(win_k), per_seq(win_v), tok(LANES), tok(LANES),
           per_seq(ocmp), tok(LANES)],
        out_specs=pl.BlockSpec((t_new, 512), lambda b, c, pt: (b, 0)),
        scratch_shapes=[pltpu.VMEM((rows, 1), F32), pltpu.VMEM((rows, 1), F32), pltpu.VMEM((rows, LANES), F32),
                        pltpu.VMEM((LANES, LANES), F32)],
    )
    return pl.pallas_call(
        kern,
        grid_spec=grid_spec,
        out_shape=jax.ShapeDtypeStruct((db * t_new, 512), F32),
        compiler_params=_cp(("parallel", "arbitrary")),
        name="nsa_sample_sel",
    )(page_table, *([pool_k] * npg), *([pool_v] * npg), q, sel, ksn, vsn, win_k, win_v, kwn, vwn, ocmp, zd)


def _nsa_sample_kernel(pt_ref, ck_hbm, cv_hbm, sk_hbm, sv_hbm, q_ref, kcn_ref, vcn_ref, ksn_ref, vsn_ref,
                       kwn_ref, vwn_ref, gate_ref, wk_ref, wv_ref, wt_ref, wn_ref, o_ref,
                       buf_ref, sem_ref, pk_ref, pv_ref, m_ref, l_ref, acc_ref, pad_ref, bias_ref,
                       *, npg, t_new, past_len):
    b = pl.program_id(0)
    nseq = pl.num_programs(0)
    hd = NSA_BLOCK
    page = buf_ref.shape[3]
    n_pages = past_len // page
    nch = n_pages // npg
    nbc = npg * page // NSA_BLOCK
    nb_pad = pk_ref.shape[1]
    n_past = past_len // NSA_BLOCK
    rows = NSA_HEADS * t_new
    nkeys = npg * page

    def chunk_copies(seq, c, slot):
        pools = (ck_hbm, cv_hbm) if c < nch else (sk_hbm, sv_hbm)
        first = (c % nch) * npg
        copies = []
        for j in range(npg):
            pg = pt_ref[seq, first + j]
            copies.append(pltpu.make_async_copy(pools[0].at[pg], buf_ref.at[slot, j], sem_ref.at[slot]))
            copies.append(pltpu.make_async_copy(pools[1].at[pg], buf_ref.at[slot, npg + j], sem_ref.at[slot]))
        return copies

    def start_all(copies):
        for n, cp in enumerate(copies):
            cp.start(priority=n % 2)

    @pl.when(b == 0)
    def _():
        start_all(chunk_copies(0, 0, 0))

    q = q_ref[...]
    qbd = _qbd(q, t_new)
    trow = lax.broadcasted_iota(jnp.int32, (rows, 1), 0) % t_new
    pos = past_len + trow
    seg = jnp.where(lax.broadcasted_iota(jnp.int32, (nkeys, nbc), 0) // NSA_BLOCK
                    == lax.broadcasted_iota(jnp.int32, (nkeys, nbc), 1), 1.0, 0.0).astype(BF16)
    expand = _expand_blocks(nbc, nkeys)
    if n_past < nb_pad:
        pk_ref[...] = jnp.zeros(pk_ref.shape, F32)
        pv_ref[...] = jnp.zeros(pv_ref.shape, F32)
    m_ref[...] = jnp.full(m_ref.shape, 0.1 * NEG, F32)
    l_ref[...] = jnp.zeros(l_ref.shape, F32)
    acc_ref[...] = jnp.zeros(acc_ref.shape, F32)
    o_cmp = None

    for c in range(2 * nch):
        slot = c % 2
        if c + 1 < 2 * nch:
            start_all(chunk_copies(b, c + 1, 1 - slot))
        else:
            @pl.when(b + 1 < nseq)
            def _():
                start_all(chunk_copies(b + 1, 0, 1 - slot))
        for cp in chunk_copies(b, c, slot):
            cp.wait()

        if c < nch:
            for base, w_t, pooled in ((0, wt_ref[0], pk_ref), (npg, wt_ref[1], pv_ref)):
                x = jnp.concatenate([(buf_ref[slot, base + j] * w_t).astype(BF16) for j in range(npg)], axis=1)
                pooled[:, c * nbc:(c + 1) * nbc] = jnp.dot(x, seg, preferred_element_type=F32)
        else:
            k0 = (c - nch) * nkeys
            kcat = jnp.concatenate([buf_ref[slot, j].astype(BF16) for j in range(npg)], axis=1)
            vcat = jnp.concatenate([buf_ref[slot, npg + j].astype(BF16) for j in range(npg)], axis=1)
            s = _dot(qbd, kcat) + bias_ref[:, k0:k0 + nkeys]
            m_old = m_ref[...]
            m_new = jnp.maximum(m_old, jnp.max(s, axis=-1, keepdims=True))
            alpha = jnp.exp(m_old - m_new)
            e = jnp.exp(s - m_new)
            l_ref[...] = alpha * l_ref[...] + jnp.sum(e, axis=-1, keepdims=True)
            acc_ref[...] = alpha * acc_ref[...] + _dot_nt(e, vcat)
            m_ref[...] = m_new

        if c == nch - 1:
            newk = jnp.sum(kcn_ref[...] * wn_ref[0][0:t_new], axis=0, keepdims=True)
            newv = jnp.sum(vcn_ref[...] * wn_ref[1][0:t_new], axis=0, keepdims=True)
            blk = lax.broadcasted_iota(jnp.int32, (rows, nb_pad), 1)
            vis = ((blk + 1) * NSA_BLOCK <= pos + 1) & (blk < n_past)
            vis_new = (n_past + 1) * NSA_BLOCK <= pos + 1
            s_past = jnp.where(vis, _dot(qbd, pk_ref[...]), NEG)
            s_new = jnp.where(vis_new, jnp.sum(qbd * newk, axis=-1, keepdims=True), NEG)
            mx = jnp.maximum(jnp.max(s_past, axis=-1, keepdims=True), s_new)
            e_past = jnp.where(vis, jnp.exp(s_past - mx), 0.0)
            e_new = jnp.where(vis_new, jnp.exp(s_new - mx), 0.0)
            den = jnp.maximum(jnp.sum(e_past, axis=-1, keepdims=True) + e_new, 1e-30)
            pr = e_past / den
            o_cmp = _dot_nt(pr, pv_ref[...]) + (e_new / den) * newv
            imps = []
            for g in range(NSA_KV_HEADS):
                acc = pr[g * NSA_GROUP * t_new:(g * NSA_GROUP + 1) * t_new]
                for h in range(1, NSA_GROUP):
                    acc = acc + pr[(g * NSA_GROUP + h) * t_new:(g * NSA_GROUP + h + 1) * t_new]
                imps.append(acc)
            imp = jnp.concatenate(imps, axis=0)
            cur = (past_len + lax.broadcasted_iota(jnp.int32, (NSA_KV_HEADS * t_new, 1), 0) % t_new) // NSA_BLOCK
            sel = _select(imp, cur, n_past).astype(BF16)
            sel_rows = jnp.concatenate([sel[(head // NSA_GROUP) * t_new:(head // NSA_GROUP + 1) * t_new]
                                        for head in range(NSA_HEADS)], axis=0)
            for cc in range(nch):
                picked = jnp.dot(sel_rows[:, cc * nbc:(cc + 1) * nbc], expand, preferred_element_type=F32) > 0.5
                bias_ref[:, cc * nkeys:(cc + 1) * nkeys] = jnp.where(picked, 0.0, NEG)

    pad_ref[...] = jnp.zeros(pad_ref.shape, F32)
    pad_ref[0:t_new, :] = ksn_ref[...]
    kn = pad_ref[...]
    pad_ref[0:t_new, :] = vsn_ref[...]
    vn = pad_ref[...]
    jn = lax.broadcasted_iota(jnp.int32, (rows, LANES), 1)
    _online_update(_dot_nt(qbd, kn), jn <= trow, vn, m_ref, l_ref, acc_ref)
    o_sel = acc_ref[...] / jnp.maximum(l_ref[...], 1e-30)
    wb = wk_ref.shape[1]
    jw = lax.broadcasted_iota(jnp.int32, (rows, wb), 1)
    rel = trow + wb - jw
    wmask = jnp.concatenate([(rel >= 0) & (rel < NSA_WINDOW), jn <= trow], axis=1)
    pad_ref[0:t_new, :] = kwn_ref[...]
    s_win = jnp.concatenate([_dot(qbd, wk_ref[...]), _dot_nt(qbd, pad_ref[...])], axis=1)
    p_win = _msoftmax(s_win, wmask)
    pad_ref[0:t_new, :] = vwn_ref[...]
    o_win = _dot_nt(p_win[:, 0:wb], wv_ref[...]) + _dot(p_win[:, wb:], pad_ref[...])
    gate = gate_ref[...]
    outs = []
    for head in range(NSA_HEADS):
        g = head // NSA_GROUP
        r0, r1 = head * t_new, (head + 1) * t_new
        gl = slice(g * hd, (g + 1) * hd)
        outs.append(_gate_col(gate, head, 0) * o_cmp[r0:r1, gl] + _gate_col(gate, head, 1) * o_sel[r0:r1, gl]
                    + _gate_col(gate, head, 2) * o_win[r0:r1, gl])
    o_ref[...] = jnp.concatenate(outs, axis=1)


def _nsa_sample(page_table, pool_ck, pool_cv, pool_sk, pool_sv, q, kc, vc, ks, vs, kw, vw, zd, win_k, win_v,
                wt, wn, t_new, npg, zd_row0):
    db, n_pages = page_table.shape
    page = pool_ck.shape[2]
    past_len = n_pages * page
    nb_pad = -(-(past_len // NSA_BLOCK) // LANES) * LANES
    row0 = q.shape[0] // t_new - db
    rows = NSA_HEADS * t_new
    assert n_pages % npg == 0 and (2 * n_pages // npg) % 2 == 0
    tok = lambda w: pl.BlockSpec((t_new, w), lambda b, pt: (row0 + b, 0))
    per_seq = lambda a: pl.BlockSpec((None,) + a.shape[1:], lambda b, pt: (b, 0, 0))
    full3 = lambda a: pl.BlockSpec(a.shape, lambda b, pt: (0, 0, 0))
    hbm = pl.BlockSpec(memory_space=pl.ANY)
    kern = functools.partial(_nsa_sample_kernel, npg=npg, t_new=t_new, past_len=past_len)
    grid_spec = pltpu.PrefetchScalarGridSpec(
        num_scalar_prefetch=1,
        grid=(db,),
        in_specs=[hbm, hbm, hbm, hbm, tok(512)] + [tok(LANES)] * 6
        + [pl.BlockSpec((t_new, LANES), lambda b, pt: (zd_row0 + b, 0)), per_seq(win_k), per_seq(win_v),
           full3(wt), full3(wn)],
        out_specs=pl.BlockSpec((t_new, 512), lambda b, pt: (b, 0)),
        scratch_shapes=[pltpu.VMEM((2, 2 * npg, LANES, page), F32), pltpu.SemaphoreType.DMA((2,)),
                        pltpu.VMEM((LANES, nb_pad), F32), pltpu.VMEM((LANES, nb_pad), F32),
                        pltpu.VMEM((rows, 1), F32), pltpu.VMEM((rows, 1), F32), pltpu.VMEM((rows, LANES), F32),
                        pltpu.VMEM((LANES, LANES), F32), pltpu.VMEM((rows, past_len), F32)],
    )
    return pl.pallas_call(
        kern,
        grid_spec=grid_spec,
        out_shape=jax.ShapeDtypeStruct((db * t_new, 512), F32),
        compiler_params=_cp(("arbitrary",)),
        name="nsa_sample",
    )(page_table, pool_ck, pool_cv, pool_sk, pool_sv, q, kc, vc, ks, vs, kw, vw, zd, win_k, win_v, wt, wn)


def _log_sigmoid(x):
    return jnp.minimum(x, 0.0) - jnp.log(1.0 + jnp.exp(-jnp.abs(x)))


def _gla_kernel(za_ref, zd_ref, wgh_ref, wgl_ref, bg_ref, gout_ref, s0_ref, o_ref, s_ref,
                st_ref, kp_ref, bp_ref, vp_ref, za_pad_ref, zd_pad_ref, *, nchunk, t_valid):
    j = pl.program_id(1)
    short = za_ref.shape[0] < GLA_CHUNK
    nj = pl.num_programs(1)
    C = GLA_CHUNK
    sub = GLA_SUB
    nsub = C // sub
    dk = 64
    dv = 128
    hk = GLA_HEADS * dk
    hv = GLA_HEADS * dv

    @pl.when(j == 0)
    def _():
        st_ref[...] = jnp.concatenate([s0_ref[h].T for h in range(GLA_HEADS)], axis=1)
        kp_ref[...] = jnp.zeros(kp_ref.shape, F32)
        bp_ref[...] = jnp.zeros(bp_ref.shape, F32)
        vp_ref[...] = jnp.zeros(vp_ref.shape, F32)

    tril = jnp.where(lax.broadcasted_iota(jnp.int32, (C, C), 0) >= lax.broadcasted_iota(jnp.int32, (C, C), 1),
                     1.0, 0.0).astype(BF16)
    head_ones = jnp.where(lax.broadcasted_iota(jnp.int32, (hk, LANES), 0) // dk
                          == lax.broadcasted_iota(jnp.int32, (hk, LANES), 1), 1.0, 0.0).astype(BF16)
    tmod = lax.broadcasted_iota(jnp.int32, (C, 1), 0) % sub
    gout = gout_ref[...]

    def chunk(c, carry):
        r0 = pl.multiple_of(c * C, C)
        if short:
            za_pad_ref[...] = jnp.zeros(za_pad_ref.shape, F32)
            zd_pad_ref[...] = jnp.zeros(zd_pad_ref.shape, F32)
            za_pad_ref[0:za_ref.shape[0], :] = za_ref[...]
            zd_pad_ref[0:zd_ref.shape[0], :] = zd_ref[...]
            z = za_pad_ref[...]
            zd = zd_pad_ref[...]
        else:
            z = za_ref[pl.ds(r0, C), :]
            zd = zd_ref[pl.ds(r0, C), :]
        q = z[:, 0:hk] * (dk ** -0.5)
        k = z[:, hk:2 * hk]
        v = z[:, 2 * hk:2 * hk + hv]
        r = z[:, 2 * hk + hv:2 * hk + 2 * hv]
        lr = zd[:, 0:GLA_GATE_RANK]
        la = _log_sigmoid(_dot_hi(lr, wgh_ref[...], wgl_ref[...]) + bg_ref[...]) * (1.0 / GLA_TAU)
        tglob = (j * nchunk + c) * C + lax.broadcasted_iota(jnp.int32, (C, 1), 0)
        la = jnp.where(tglob < t_valid, la, 0.0)
        b = _dot_exact_lhs(tril, la)
        st = st_ref[...]
        qe = q * jnp.exp(b)
        kp_ref[sub:sub + C, :] = k
        bp_ref[sub:sub + C, :] = b
        vp_ref[sub:sub + C, :] = v
        xs = []
        for d in range(sub):
            kd = kp_ref[sub - d:sub - d + C, :]
            bd = bp_ref[sub - d:sub - d + C, :]
            xs.append(jnp.where(tmod >= d, q * kd * jnp.exp(b - bd), 0.0).astype(BF16))
        rr = jnp.dot(jnp.concatenate(xs, axis=0), head_ones, preferred_element_type=F32)
        outs = []
        for h in range(GLA_HEADS):
            kh = slice(h * dk, (h + 1) * dk)
            vh = slice(h * dv, (h + 1) * dv)
            o = _dot_nt(qe[:, kh], st[:, kh])
            for d in range(sub):
                o = o + rr[d * C:(d + 1) * C, h:h + 1] * vp_ref[sub - d:sub - d + C, vh]
            offs = [jnp.zeros((sub, dv), F32)]
            for i in range(1, nsub):
                anchor = b[i * sub - 1:i * sub, kh]
                qt = q[i * sub:(i + 1) * sub, kh] * jnp.exp(b[i * sub:(i + 1) * sub, kh] - anchor)
                kt = k[0:i * sub, kh] * jnp.exp(anchor - b[0:i * sub, kh])
                offs.append(_dot(_dot_nt(qt, kt), v[0:i * sub, vh]))
            o = o + jnp.concatenate(offs, axis=0)
            on = o * lax.rsqrt(jnp.mean(o * o, axis=-1, keepdims=True) + EPS) * gout
            rh = r[:, vh]
            outs.append(on * (rh * _sigmoid(rh)))
        o_all = jnp.concatenate(outs, axis=1)
        if short:
            o_ref[...] = o_all[0:o_ref.shape[0], :].astype(o_ref.dtype)
        else:
            o_ref[pl.ds(r0, C), :] = o_all.astype(o_ref.dtype)
        b_last = b[C - 1:C, :]
        kk = k * jnp.exp(b_last - b)
        upd = jnp.concatenate([_dot_tn(v[:, h * dv:(h + 1) * dv], kk[:, h * dk:(h + 1) * dk])
                               for h in range(GLA_HEADS)], axis=1)
        st_ref[...] = st * jnp.exp(b_last) + upd
        return carry

    lax.fori_loop(0, nchunk, chunk, 0)

    @pl.when(j == nj - 1)
    def _():
        st = st_ref[...]
        for h in range(GLA_HEADS):
            s_ref[h] = st[:, h * dk:(h + 1) * dk].T


def _gla(za, zd, wgh, wgl, bg, gout, s0, batch, t_seq, ct, row0, out_dtype):
    nj = t_seq // ct
    blk0 = row0 // ct
    hv = GLA_HEADS * 128
    kern = functools.partial(_gla_kernel, nchunk=max(1, ct // GLA_CHUNK), t_valid=t_seq)
    full = lambda a: pl.BlockSpec(a.shape, lambda b, j: (0,) * a.ndim)
    return pl.pallas_call(
        kern,
        grid=(batch, nj),
        in_specs=[pl.BlockSpec((ct, za.shape[1]), lambda b, j: (blk0 + b * nj + j, 0)),
                  pl.BlockSpec((ct, LANES), lambda b, j: (blk0 + b * nj + j, 0)),
                  full(wgh), full(wgl), full(bg), full(gout),
                  pl.BlockSpec((None,) + s0.shape[1:], lambda b, j: (b, 0, 0, 0))],
        out_specs=[pl.BlockSpec((ct, hv), lambda b, j: (b * nj + j, 0)),
                   pl.BlockSpec((None,) + s0.shape[1:], lambda b, j: (b, 0, 0, 0))],
        out_shape=[jax.ShapeDtypeStruct((batch * t_seq, hv), out_dtype), jax.ShapeDtypeStruct(s0.shape, F32)],
        scratch_shapes=[pltpu.VMEM((128, GLA_HEADS * 64), F32),
                        pltpu.VMEM((GLA_SUB + GLA_CHUNK, GLA_HEADS * 64), F32),
                        pltpu.VMEM((GLA_SUB + GLA_CHUNK, GLA_HEADS * 64), F32),
                        pltpu.VMEM((GLA_SUB + GLA_CHUNK, hv), F32),
                        pltpu.VMEM((GLA_CHUNK, za.shape[1]), F32),
                        pltpu.VMEM((GLA_CHUNK, LANES), F32)],
        compiler_params=_cp(("parallel", "arbitrary")),
        name="gla",
    )(za, zd, wgh, wgl, bg, gout, s0)


def _mem_kv_kernel(m_ref, g_ref, w_ref, gk_ref, k_ref, v_ref):
    x = m_ref[...]
    u = x * lax.rsqrt(jnp.mean(x * x, axis=-1, keepdims=True) + EPS) * g_ref[...]
    kv = jnp.dot(u.astype(BF16), w_ref[...], preferred_element_type=F32)
    half = kv.shape[1] // 2
    k_ref[...] = _seg_rms(kv[:, 0:half], half // MEM_HEADS, gk_ref[...])
    v_ref[...] = kv[:, half:]


def _mem_kv(mem, g, w, gk, tm):
    n, d = mem.shape
    half = w.shape[1] // 2
    full = lambda a: pl.BlockSpec(a.shape, lambda i: (0, 0))
    sd = jax.ShapeDtypeStruct((n, half), F32)
    return pl.pallas_call(
        _mem_kv_kernel,
        grid=(n // tm,),
        in_specs=[pl.BlockSpec((tm, d), lambda i: (i, 0)), full(g), full(w), full(gk)],
        out_specs=[pl.BlockSpec((tm, half), lambda i: (i, 0))] * 2,
        out_shape=[sd, sd],
        compiler_params=_cp(("parallel",)),
        name="mem_kv",
    )(mem, g, w, gk)


def _mem_prompt_kernel(q_ref, k_ref, v_ref, o_ref):
    q = q_ref[...]
    k = k_ref[...].astype(BF16)
    v = v_ref[...].astype(BF16)
    hd = q.shape[1] // MEM_HEADS
    outs = []
    for h in range(MEM_HEADS):
        sl = slice(h * hd, (h + 1) * hd)
        s = _dot_nt(q[:, sl], k[:, sl])
        m = jnp.max(s, axis=-1, keepdims=True)
        e = jnp.exp(s - m)
        outs.append(_dot(e / jnp.sum(e, axis=-1, keepdims=True), v[:, sl]))
    o_ref[...] = jnp.concatenate(outs, axis=1).astype(BF16)


def _mem_prompt(qm, mk, mv, batch, seq, mlen, tq):
    nq = seq // tq
    w = qm.shape[1]
    return pl.pallas_call(
        _mem_prompt_kernel,
        grid=(batch, nq),
        in_specs=[pl.BlockSpec((tq, w), lambda b, i: (b * nq + i, 0)),
                  pl.BlockSpec((mlen, w), lambda b, i: (b, 0)),
                  pl.BlockSpec((mlen, w), lambda b, i: (b, 0))],
        out_specs=pl.BlockSpec((tq, w), lambda b, i: (b * nq + i, 0)),
        out_shape=jax.ShapeDtypeStruct((batch * seq, w), BF16),
        compiler_params=_cp(("parallel", "parallel")),
        name="mem_prompt",
    )(qm, mk, mv)


def _mem_sample_kernel(q_ref, k_ref, v_ref, o_ref, *, sb, t_new):
    w = q_ref.shape[1]
    hd = w // MEM_HEADS
    rows = MEM_HEADS * t_new
    nk = k_ref.shape[1]
    same_head = (lax.broadcasted_iota(jnp.int32, (rows, nk), 1) % MEM_HEADS
                 == lax.broadcasted_iota(jnp.int32, (rows, nk), 0) // t_new)
    bias = jnp.where(same_head, 0.0, NEG)
    for s_i in range(sb):
        q = q_ref[s_i * t_new:(s_i + 1) * t_new, :].astype(F32)
        qs = jnp.concatenate([q[:, h * hd:(h + 1) * hd] for h in range(MEM_HEADS)], axis=0)
        s = _dot_nt(qs, k_ref[s_i]) + bias
        e = jnp.exp(s - jnp.max(s, axis=-1, keepdims=True))
        o = _dot(e / jnp.sum(e, axis=-1, keepdims=True), v_ref[s_i])
        o_ref[s_i * t_new:(s_i + 1) * t_new, :] = jnp.concatenate(
            [o[h * t_new:(h + 1) * t_new, :] for h in range(MEM_HEADS)], axis=1).astype(BF16)


def _mem_sample(qm, ck, cv, row0_blocks, db, t_new, sb):
    w = qm.shape[1]
    kern = functools.partial(_mem_sample_kernel, sb=sb, t_new=t_new)
    return pl.pallas_call(
        kern,
        grid=(db // sb,),
        in_specs=[pl.BlockSpec((sb * t_new, w), lambda i: (row0_blocks + i, 0)),
                  pl.BlockSpec((sb,) + ck.shape[1:], lambda i: (i, 0, 0)),
                  pl.BlockSpec((sb,) + cv.shape[1:], lambda i: (i, 0, 0))],
        out_specs=pl.BlockSpec((sb * t_new, w), lambda i: (i, 0)),
        out_shape=jax.ShapeDtypeStruct((db * t_new, w), BF16),
        compiler_params=_cp(("parallel",)),
        name="mem_sample",
    )(qm, ck, cv)


def _merge_kernel(xp_ref, xs_ref, ogp_ref, ogs_ref, onp_ref, ons_ref, omp_ref, oms_ref, gate_ref,
                  wg_ref, wn_ref, wm_ref, wo_ref, gf_ref, wrh_ref, wrl_ref, br_ref, h_ref, hn_ref, cmb_ref,
                  *, n_experts, n_first, n_tiles):
    pid = pl.program_id(0)
    rest = (gate_ref, wg_ref, wn_ref, wm_ref, wo_ref, gf_ref, wrh_ref, wrl_ref, br_ref, h_ref, hn_ref, cmb_ref,
            n_experts)

    @pl.when(pid >= n_tiles)
    def _():
        h_ref[...] = jnp.zeros(h_ref.shape, h_ref.dtype)
        hn_ref[...] = jnp.zeros(hn_ref.shape, hn_ref.dtype)
        cmb_ref[...] = jnp.zeros(cmb_ref.shape, cmb_ref.dtype)

    @pl.when(pid < n_first)
    def _():
        _merge_body(xp_ref, ogp_ref, onp_ref, omp_ref, *rest)

    @pl.when((pid >= n_first) & (pid < n_tiles))
    def _():
        _merge_body(xs_ref, ogs_ref, ons_ref, oms_ref, *rest)


def _merge_body(x_ref, og_ref, on_ref, om_ref, gate_ref, wg_ref, wn_ref, wm_ref, wo_ref, gf_ref,
                wrh_ref, wrl_ref, br_ref, h_ref, hn_ref, cmb_ref, n_experts):
    d = x_ref.shape[1]
    gate = gate_ref[...].astype(F32)
    y = (gate[:, 0:d] * jnp.dot(og_ref[...], wg_ref[...], preferred_element_type=F32)
         + gate[:, d:2 * d] * jnp.dot(on_ref[...], wn_ref[...], preferred_element_type=F32)
         + gate[:, 2 * d:3 * d] * jnp.dot(om_ref[...], wm_ref[...], preferred_element_type=F32))
    h = x_ref[...] + jnp.dot(y.astype(BF16), wo_ref[...], preferred_element_type=F32)
    h_ref[...] = h
    hn = h * lax.rsqrt(jnp.mean(h * h, axis=-1, keepdims=True) + EPS) * gf_ref[...]
    hn_ref[...] = hn.astype(BF16)
    logits = _dot_hi(hn, wrh_ref[...], wrl_ref[...]) + br_ref[...]
    lane = lax.broadcasted_iota(jnp.int32, logits.shape, 1).astype(F32)
    work = jnp.where(lane < n_experts, logits, NEG)
    top = jnp.max(work, axis=-1, keepdims=True)
    chosen = jnp.zeros(logits.shape, F32)
    for _ in range(TOP_K):
        mx = jnp.max(work, axis=-1, keepdims=True)
        first = jnp.min(jnp.where(work == mx, lane, float(LANES)), axis=-1, keepdims=True)
        pick = lane == first
        chosen = jnp.where(pick, 1.0, chosen)
        work = jnp.where(pick, NEG, work)
    e = jnp.where(chosen > 0.5, jnp.exp(logits - top), 0.0)
    cmb_ref[...] = e / jnp.sum(e, axis=-1, keepdims=True)


def _merge(x, og, on, om, gate, wg, wn, wm, wo, gf, wrh, wrl, br, n_experts, tm, n_pad):
    d = x[0].shape[1]
    n_first = x[0].shape[0] // tm
    n_tiles = n_first + x[1].shape[0] // tm
    row = lambda a: pl.BlockSpec((tm, a.shape[1]), lambda i: (jnp.minimum(i, n_tiles - 1), 0))
    full = lambda a: pl.BlockSpec(a.shape, lambda i: (0, 0))
    kern = functools.partial(_merge_kernel, n_experts=n_experts, n_first=n_first, n_tiles=n_tiles)
    return pl.pallas_call(
        kern,
        grid=(n_pad // tm,),
        in_specs=_two_part_specs(x, tm) + _two_part_specs(og, tm) + _two_part_specs(on, tm) + _two_part_specs(om, tm)
        + [row(gate), full(wg), full(wn), full(wm), full(wo), full(gf), full(wrh), full(wrl), full(br)],
        out_specs=[pl.BlockSpec((tm, d), lambda i: (i, 0)), pl.BlockSpec((tm, d), lambda i: (i, 0)),
                   pl.BlockSpec((tm, LANES), lambda i: (i, 0))],
        out_shape=[jax.ShapeDtypeStruct((n_pad, d), F32), jax.ShapeDtypeStruct((n_pad, d), BF16),
                   jax.ShapeDtypeStruct((n_pad, LANES), F32)],
        compiler_params=_cp(("parallel",)),
        name="merge_router",
    )(*x, *og, *on, *om, gate, wg, wn, wm, wo, gf, wrh, wrl, br)


MOE_SUBTILE = 1024
MOE_SUBTILES = 2
MOE_ROW_BLOCK = 160


SPLIT_COLS = 512


def _split_gate_up_kernel(w_ref, wg_ref, wu_ref):
    half = SPLIT_COLS // 2
    r = lax.broadcasted_iota(jnp.int32, (SPLIT_COLS, half), 0)
    c = lax.broadcasted_iota(jnp.int32, (SPLIT_COLS, half), 1)
    even = jnp.where(r == 2 * c, 1.0, 0.0).astype(BF16)
    odd = jnp.where(r == 2 * c + 1, 1.0, 0.0).astype(BF16)
    for t in range(w_ref.shape[1] // SPLIT_COLS):
        blk = w_ref[:, t * SPLIT_COLS:(t + 1) * SPLIT_COLS].astype(BF16)
        wg_ref[:, t * half:(t + 1) * half] = jnp.dot(blk, even, preferred_element_type=F32).astype(BF16)
        wu_ref[:, t * half:(t + 1) * half] = jnp.dot(blk, odd, preferred_element_type=F32).astype(BF16)


def _split_gate_up(w):
    ne, d, dff2 = w.shape
    sd = jax.ShapeDtypeStruct((ne, d, dff2 // 2), BF16)
    out = pl.BlockSpec((None, d, dff2 // 2), lambda e: (e, 0, 0))
    return pl.pallas_call(
        _split_gate_up_kernel,
        grid=(ne,),
        in_specs=[pl.BlockSpec((None, d, dff2), lambda e: (e, 0, 0))],
        out_specs=[out, out],
        out_shape=[sd, sd],
        compiler_params=_cp(("parallel",)),
        name="split_gate_up",
    )(w)


def _moe_kernel(hn_ref, h_ref, cmb_ref, wg_ref, wu_ref, bg_ref, bu_ref, wd_ref, bd_ref, y_ref,
                rank_ref, rankt_ref, cmbt_ref, cnt_ref, *, ts):
    e = pl.program_id(1)
    ns = hn_ref.shape[0] // ts
    rb = MOE_ROW_BLOCK

    @pl.when(e == 0)
    def _():
        y_ref[...] = h_ref[...]
        lower = jnp.where(lax.broadcasted_iota(jnp.int32, (ts, ts), 0) > lax.broadcasted_iota(jnp.int32, (ts, ts), 1),
                          1.0, 0.0).astype(BF16)
        for s in range(ns):
            cmb = cmb_ref[s * ts:(s + 1) * ts, :]
            sel = jnp.where(cmb > 0.0, 1.0, 0.0)
            rank = jnp.where(sel > 0.0, jnp.dot(lower, sel.astype(BF16), preferred_element_type=F32), -1.0)
            rank_ref[s] = rank
            rankt_ref[s] = rank.T
            cmbt_ref[s] = cmb.T
            cnt_ref[s] = jnp.broadcast_to(jnp.sum(sel, axis=0, keepdims=True), cnt_ref.shape[1:])

    lane = lax.broadcasted_iota(jnp.int32, (1, LANES), 1)
    pick = jnp.where(lax.broadcasted_iota(jnp.int32, (LANES, LANES), 0) == e, 1.0, 0.0).astype(BF16)
    n_iter = 0
    rank_cols = []
    for s in range(ns):
        cnt = jnp.sum(jnp.where(lane == e, cnt_ref[s, 0:1, :], 0.0)).astype(jnp.int32)
        n_iter = jnp.maximum(n_iter, (cnt + rb - 1) // rb)
        r = rank_ref[s]
        r_hi = r.astype(BF16)
        r_lo = (r - r_hi.astype(F32)).astype(BF16)
        rc = (jnp.dot(r_hi, pick, preferred_element_type=F32)
              + jnp.dot(r_lo, pick, preferred_element_type=F32))
        rank_cols.append(jnp.concatenate([rc] * (-(-rb // LANES)), axis=1)[:, 0:rb])

    def body(i, carry):
        r0 = i * rb
        rows = (r0 + lax.broadcasted_iota(jnp.int32, (rb, ts), 0)).astype(F32)
        xs, ws = [], []
        for s in range(ns):
            hit = rankt_ref[s, pl.ds(e, 1), :] == rows
            xs.append(jnp.dot(jnp.where(hit, 1.0, 0.0).astype(BF16), hn_ref[s * ts:(s + 1) * ts, :],
                              preferred_element_type=F32).astype(BF16))
            ws.append(jnp.sum(jnp.where(hit, cmbt_ref[s, pl.ds(e, 1), :], 0.0), axis=-1, keepdims=True))
        x = jnp.concatenate(xs, axis=0)
        gate = jnp.minimum(jnp.dot(x, wg_ref[...], preferred_element_type=F32) + bg_ref[...], SWIGLU_LIMIT)
        up = jnp.clip(jnp.dot(x, wu_ref[...], preferred_element_type=F32) + bu_ref[...], -SWIGLU_LIMIT, SWIGLU_LIMIT)
        act = (up + 1.0) * gate * _sigmoid(SWIGLU_ALPHA * gate)
        out = jnp.dot(act.astype(BF16), wd_ref[...], preferred_element_type=F32) + bd_ref[...]
        cols = (r0 + lax.broadcasted_iota(jnp.int32, (ts, rb), 1)).astype(F32)
        for s in range(ns):
            yw = (out[s * rb:(s + 1) * rb, :] * ws[s]).astype(BF16)
            pt = jnp.where(rank_cols[s] == cols, 1.0, 0.0).astype(BF16)
            y_ref[s * ts:(s + 1) * ts, :] += jnp.dot(pt, yw, preferred_element_type=F32)
        return carry

    lax.fori_loop(0, n_iter, body, 0)


def _moe(hn, h, cmb, wg, wu, bg, bu, wd, bd, ts, ns):
    n, d = h.shape
    ne, _, dff = wg.shape
    tt = ts * ns
    once = lambda w: pl.BlockSpec((tt, w), lambda i, e: (i, 0), pipeline_mode=pl.Buffered(1))
    per_expert = lambda a: pl.BlockSpec((None,) + a.shape[1:], lambda i, e: (e, 0, 0))
    return pl.pallas_call(
        functools.partial(_moe_kernel, ts=ts),
        grid=(n // tt, ne),
        in_specs=[once(d), once(d), once(LANES), per_expert(wg), per_expert(wu), per_expert(bg), per_expert(bu),
                  per_expert(wd), per_expert(bd)],
        out_specs=pl.BlockSpec((tt, d), lambda i, e: (i, 0)),
        out_shape=jax.ShapeDtypeStruct((n, d), F32),
        scratch_shapes=[pltpu.VMEM((ns, ts, LANES), F32), pltpu.VMEM((ns, LANES, ts), F32),
                        pltpu.VMEM((ns, LANES, ts), F32), pltpu.VMEM((ns, 8, LANES), F32)],
        compiler_params=_cp(("parallel", "arbitrary")),
        name="moe",
    )(hn, h, cmb, wg, wu, bg, bu, wd, bd)


def _rope_table(pos):
    half = 8
    inv = 1.0 / (ROPE_THETA ** (jnp.arange(half, dtype=F32) / half))
    ang = pos.astype(F32)[:, None] * inv[None, :]
    cos, sin = jnp.cos(ang), jnp.sin(ang)
    n = pos.shape[0]
    one = jnp.ones((n, 64 - 2 * half), F32)
    zero8 = jnp.zeros((n, half), F32)
    zero = jnp.zeros((n, 64 - 2 * half), F32)
    c = jnp.concatenate([cos, cos, one], axis=1)
    s1 = jnp.concatenate([-sin, zero8, zero], axis=1)
    s2 = jnp.concatenate([zero8, sin, zero], axis=1)
    return jnp.concatenate([c, c, s1, s1, s2, s2], axis=1)


def _hi_lo(w):
    hi = w.astype(BF16)
    return hi, (w - hi.astype(F32)).astype(BF16)


def kernel(x_prompt, x_sample, cache_cmp_k, cache_cmp_v, cache_sel_k, cache_sel_v, cache_win_k, cache_win_v, state_gla, cache_mem_k, cache_mem_v, page_table, mem_prompt, g_attn, w_in, w_gla_gate, b_gla_gate, g_gla_out, g_q_nsa, g_k_nsa, w_cmp_pos, g_q_mem, g_k_mem, g_mem, w_mem_kv, w_up_gla, w_up_nsa, w_up_mem, w_out, g_ffn, w_router, b_router, w_gate_up, b_gate_up, w_down, b_down):
    B, S, D = x_prompt.shape
    DB, T, _ = x_sample.shape
    n_pool, page = cache_cmp_k.shape[1:3]
    n_pages = page_table.shape[1]
    past_len = n_pages * page
    wb = cache_win_k.shape[2]
    mlen = mem_prompt.shape[1]
    ne = w_router.shape[2]
    Np, Ns = B * S, DB * T
    N = Np + Ns
    TM = 256
    assert cache_cmp_k.shape[0] == 1 and D == 1024 and S % TM == 0 and Ns % TM == 0 and TM % T == 0
    assert T <= NSA_BLOCK and past_len % NSA_BLOCK == 0 and S % NSA_BLOCK == 0

    gq, gk, gv, gr, glr, nq, nkv, ng, mq, mg = np.cumsum(
        [0, 256, 256, 512, 512, GLA_GATE_RANK, 512, 768, 3 * NSA_HEADS, 512]).tolist()
    w = w_in[0]
    wa = w[:, gq:glr].astype(BF16)
    wbm = jnp.concatenate([w[:, nq:ng], w[:, mq:mg]], axis=1).astype(BF16)
    wc = w[:, mg:].astype(BF16)
    wd = jnp.concatenate([w[:, glr:nq], w[:, ng:mq], jnp.zeros((D, LANES - GLA_GATE_RANK - 3 * NSA_HEADS), F32)],
                         axis=1).astype(BF16)

    x_parts = (x_prompt.reshape(Np, D), x_sample.reshape(Ns, D))
    za, zb, zc, zd = _project(*x_parts, g_attn, wa, wbm, wc, wd, TM)

    pos_sample = past_len + jnp.arange(T, dtype=jnp.int32)
    tab_p = _rope_table(jnp.arange(S, dtype=jnp.int32))
    tab_s = jnp.tile(_rope_table(pos_sample), (TM // T, 1))
    gq_t = jnp.tile(g_q_nsa[0], NSA_HEADS)[None, :]
    gk_t = jnp.tile(g_k_nsa[0], (1, NSA_KV_HEADS))
    gm_t = jnp.tile(g_q_mem[0], MEM_HEADS)[None, :]
    (kc_p, vc_p, kct, vct, kst, vst, kwt, vwt, qp, ktsb, ktwb, vs1, vw1, qm_p) = _prep_prompt(
        zb, tab_p, gq_t, gk_t, gm_t, TM, B, S)
    q, kc, vc, ks, vs, kw, vw, qm_s = _prep_sample(zb, tab_s, gq_t, gk_t, gm_t, TM, Np, Ns)

    w2 = jnp.tile(w_cmp_pos[0], (1, 1, NSA_KV_HEADS))
    ck, cv = _compress(kc_p, vc_p, w2, Np, 512)
    o_nsa_p = _nsa_prompt(qp, ck, cv, ktsb, ktwb, vs1, vw1, zd, B, S, 128)

    rows_minor = lambda a: jnp.transpose(a[0], (0, 2, 3, 1)).reshape(a.shape[1], LANES, a.shape[2])
    pool = rows_minor
    npg = 32 if n_pages % 64 == 0 else n_pages // 2
    wt = jnp.tile(jnp.transpose(w2, (0, 2, 1)), (1, 1, page // NSA_BLOCK))
    o_nsa_s = _nsa_sample(page_table, pool(cache_cmp_k), pool(cache_cmp_v), pool(cache_sel_k), pool(cache_sel_v),
                          q, kc, vc, ks, vs, kw, vw, zd, rows_minor(cache_win_k), rows_minor(cache_win_v),
                          wt, w2, T, npg, Np // T)
    hd_cols = np.arange(NSA_BLOCK)
    nsa_cols = np.concatenate([np.concatenate([c * NSA_BLOCK + hd_cols, (NSA_GROUP + c) * NSA_BLOCK + hd_cols])
                               for c in range(NSA_GROUP)])
    o_nsa = (o_nsa_p, o_nsa_s[:, nsa_cols].astype(BF16))

    wgh, wgl = _hi_lo(w_gla_gate[0])
    bg = b_gla_gate[0][None, :]
    gout = g_gla_out[0][None, :]
    s0_p = jnp.zeros((B,) + state_gla.shape[2:], F32)
    o_gla_p, s_gla_p = _gla(za, zd, wgh, wgl, bg, gout, s0_p, B, S, 512, 0, BF16)
    o_gla_s, s_gla_s = _gla(za, zd, wgh, wgl, bg, gout, state_gla[0], DB, T, T, Np, F32)
    o_gla = (o_gla_p, o_gla_s.astype(BF16))

    gkm = jnp.tile(g_k_mem[0], MEM_HEADS)[None, :]
    mem_k, mem_v = _mem_kv(mem_prompt.reshape(B * mlen, D), g_mem, w_mem_kv[0].astype(BF16), gkm, TM)
    o_mem_p = _mem_prompt(qm_p, mem_k, mem_v, B, S, mlen, 512)
    mw = MEM_HEADS * cache_mem_k.shape[-1]
    sb = 8
    mem_rows = lambda a: a[0].reshape(DB, mlen * MEM_HEADS, a.shape[-1])
    o_mem_s = _mem_sample(qm_s, mem_rows(cache_mem_k), mem_rows(cache_mem_v), 0, DB, T, sb)
    o_mem = (o_mem_p, o_mem_s)

    moe_ts = MOE_SUBTILE if N % MOE_SUBTILE == 0 else TM
    n_pad = -(-N // (moe_ts * MOE_SUBTILES)) * (moe_ts * MOE_SUBTILES)
    wr = jnp.pad(w_router[0], ((0, 0), (0, LANES - ne)))
    wrh, wrl = _hi_lo(wr)
    br = jnp.pad(b_router[0], (0, LANES - ne))[None, :]
    h, hn, cmb = _merge(x_parts, o_gla, o_nsa, o_mem, zc, w_up_gla[0].astype(BF16), w_up_nsa[0][nsa_cols].astype(BF16),
                        w_up_mem[0].astype(BF16), w_out[0].astype(BF16), g_ffn, wrh, wrl, br, ne, TM, n_pad)

    wg, wu = _split_gate_up(w_gate_up[0])
    y = _moe(hn, h, cmb, wg, wu, b_gate_up[0][:, None, 0::2], b_gate_up[0][:, None, 1::2],
             w_down[0].astype(BF16), b_down[0][:, None, :], moe_ts, MOE_SUBTILES)

    p_rows = lambda a: jnp.transpose(a.reshape(1, B, NSA_KV_HEADS, NSA_BLOCK, S), (0, 1, 4, 2, 3))
    s_rows = lambda a: a.reshape(1, DB, T, NSA_KV_HEADS, NSA_BLOCK)
    wbp = min(NSA_WINDOW, S)
    s_win = lambda cache, new: jnp.concatenate([cache[:, :, T:], s_rows(new)], axis=2)
    mshape = (1, B, mlen, MEM_HEADS, mw // MEM_HEADS)
    return (y[:Np].reshape(B, S, D), y[Np:N].reshape(DB, T, D),
            p_rows(kct), p_rows(vct), p_rows(kst), p_rows(vst),
            p_rows(kwt)[:, :, S - wbp:], p_rows(vwt)[:, :, S - wbp:],
            s_gla_p[None], mem_k.reshape(mshape), mem_v.reshape(mshape),
            s_rows(kc), s_rows(vc), s_rows(ks), s_rows(vs),
            s_win(cache_win_k, kw), s_win(cache_win_v, vw), s_gla_s[None])
```

```python
import functools

import numpy as np
import jax
import jax.numpy as jnp
from jax import lax
from jax.experimental import pallas as pl
from jax.experimental.pallas import tpu as pltpu

F32 = jnp.float32
BF16 = jnp.bfloat16

GLA_HEADS = 4
GLA_GATE_RANK = 16
GLA_TAU = 16.0
GLA_CHUNK = 64
GLA_SUB = 16
NSA_HEADS = 8
NSA_KV_HEADS = 2
NSA_GROUP = NSA_HEADS // NSA_KV_HEADS
NSA_BLOCK = 64
NSA_TOPN = 16
NSA_WINDOW = 512
MEM_HEADS = 4
ROPE_THETA = 500000.0
N_BRANCH = 3
TOP_K = 4
SWIGLU_LIMIT = 7.0
SWIGLU_ALPHA = 1.702
EPS = 1e-6

LANES = 128
NEG = -1e30
VMEM_LIMIT = 56 * 1024 * 1024


def _cp(sem):
    return pltpu.CompilerParams(dimension_semantics=sem, vmem_limit_bytes=VMEM_LIMIT)


def _dot(a, b):
    return jnp.dot(a.astype(BF16), b.astype(BF16), preferred_element_type=F32)


def _dot_nt(a, b):
    return lax.dot_general(a.astype(BF16), b.astype(BF16), (((1,), (1,)), ((), ())),
                           preferred_element_type=F32)


def _dot_tn(a, b):
    return lax.dot_general(a.astype(BF16), b.astype(BF16), (((0,), (0,)), ((), ())),
                           preferred_element_type=F32)


def _split3(x):
    hi = x.astype(BF16)
    r = x - hi.astype(F32)
    mid = r.astype(BF16)
    lo = (r - mid.astype(F32)).astype(BF16)
    return hi, mid, lo


def _dot_exact_rhs(x, m):
    hi, mid, lo = _split3(x)
    d = functools.partial(jnp.dot, preferred_element_type=F32)
    return d(hi, m) + d(mid, m) + d(lo, m)


def _dot_exact_lhs(m, x):
    hi, mid, lo = _split3(x)
    d = functools.partial(jnp.dot, preferred_element_type=F32)
    return d(m, hi) + d(m, mid) + d(m, lo)


def _dot_hi(x, w_hi, w_lo):
    hi, mid, _ = _split3(x)
    d = functools.partial(jnp.dot, preferred_element_type=F32)
    return d(hi, w_hi) + d(mid, w_hi) + d(hi, w_lo)


def _msoftmax(s, mask):
    s = jnp.where(mask, s, NEG)
    m = jnp.max(s, axis=-1, keepdims=True)
    e = jnp.where(mask, jnp.exp(s - m), 0.0)
    return e / jnp.maximum(jnp.sum(e, axis=-1, keepdims=True), 1e-30)


def _seg_ones(width, seg):
    r = lax.broadcasted_iota(jnp.int32, (width, width), 0) // seg
    c = lax.broadcasted_iota(jnp.int32, (width, width), 1) // seg
    return jnp.where(r == c, 1.0, 0.0).astype(BF16)


def _seg_rms(x, seg, gain):
    ms = _dot_exact_rhs(x * x, _seg_ones(x.shape[-1], seg)) * (1.0 / seg)
    return x * lax.rsqrt(ms + EPS) * gain


def _sigmoid(x):
    return 1.0 / (1.0 + jnp.exp(-x))


def _two_part_specs(parts, tm):
    n_first = parts[0].shape[0] // tm
    n_second = parts[1].shape[0] // tm
    first = pl.BlockSpec((tm, parts[0].shape[1]), lambda i: (jnp.minimum(i, n_first - 1), 0))
    second = pl.BlockSpec((tm, parts[1].shape[1]), lambda i: (jnp.clip(i - n_first, 0, n_second - 1), 0))
    return [first, second]


def _proj_kernel(xp_ref, xs_ref, *refs, n_first):
    @pl.when(pl.program_id(0) < n_first)
    def _():
        _proj_body(xp_ref, *refs)

    @pl.when(pl.program_id(0) >= n_first)
    def _():
        _proj_body(xs_ref, *refs)


def _proj_body(x_ref, g_ref, wa_ref, wb_ref, wc_ref, wd_ref, za_ref, zb_ref, zc_ref, zd_ref):
    x = x_ref[...]
    u = x * lax.rsqrt(jnp.mean(x * x, axis=-1, keepdims=True) + EPS) * g_ref[...]
    ub = u.astype(BF16)
    za_ref[...] = jnp.dot(ub, wa_ref[...], preferred_element_type=F32)
    zb_ref[...] = jnp.dot(ub, wb_ref[...], preferred_element_type=F32)
    zc_ref[...] = _sigmoid(jnp.dot(ub, wc_ref[...], preferred_element_type=F32)).astype(BF16)
    zd_ref[...] = jnp.dot(ub, wd_ref[...], preferred_element_type=F32)


def _project(xp, xs, g, wa, wb, wc, wd, tm):
    n = xp.shape[0] + xs.shape[0]
    full = lambda w: pl.BlockSpec(w.shape, lambda i: (0, 0))
    row = lambda w: pl.BlockSpec((tm, w), lambda i: (i, 0))
    return pl.pallas_call(
        functools.partial(_proj_kernel, n_first=xp.shape[0] // tm),
        grid=(n // tm,),
        in_specs=_two_part_specs((xp, xs), tm) + [full(g), full(wa), full(wb), full(wc), full(wd)],
        out_specs=[row(wa.shape[1]), row(wb.shape[1]), row(wc.shape[1]), row(wd.shape[1])],
        out_shape=[jax.ShapeDtypeStruct((n, wa.shape[1]), F32), jax.ShapeDtypeStruct((n, wb.shape[1]), F32),
                   jax.ShapeDtypeStruct((n, wc.shape[1]), BF16), jax.ShapeDtypeStruct((n, wd.shape[1]), F32)],
        compiler_params=_cp(("parallel",)),
        name="proj",
    )(xp, xs, g, wa, wb, wc, wd)


def _rope(x, tab):
    c, s1, s2 = tab[:, 0:LANES], tab[:, LANES:2 * LANES], tab[:, 2 * LANES:3 * LANES]
    half = 8
    return x * c + pltpu.roll(x, LANES - half, 1) * s1 + pltpu.roll(x, half, 1) * s2


def _prep_common(zb_ref, tab_ref, gq_ref, gk_ref, gm_ref):
    z = zb_ref[...]
    tab = tab_ref[...]
    hd = NSA_BLOCK
    nq = NSA_HEADS * hd
    qn = _seg_rms(z[:, 0:nq], hd, gq_ref[...])
    q = jnp.concatenate([_rope(qn[:, j * LANES:(j + 1) * LANES], tab) for j in range(nq // LANES)], axis=1)
    q = q * (hd ** -0.5)
    kv = z[:, nq:nq + 6 * LANES]
    gk = gk_ref[...]
    ks = [_rope(_seg_rms(kv[:, 2 * j * LANES:(2 * j + 1) * LANES], hd, gk[j:j + 1, :]), tab) for j in range(3)]
    vs = [kv[:, (2 * j + 1) * LANES:(2 * j + 2) * LANES] for j in range(3)]
    zm = z[:, nq + 6 * LANES:]
    mhd = zm.shape[1] // MEM_HEADS
    qm = (_seg_rms(zm, mhd, gm_ref[...]) * (mhd ** -0.5)).astype(BF16)
    return q, ks, vs, qm


def _prep_sample_kernel(zb_ref, tab_ref, gq_ref, gk_ref, gm_ref,
                        q_ref, kc_ref, vc_ref, ks_ref, vs_ref, kw_ref, vw_ref, qm_ref):
    q, ks, vs, qm = _prep_common(zb_ref, tab_ref, gq_ref, gk_ref, gm_ref)
    q_ref[...] = q
    for ref, val in zip((kc_ref, ks_ref, kw_ref, vc_ref, vs_ref, vw_ref), ks + vs):
        ref[...] = val
    qm_ref[...] = qm


def _prep_prompt_kernel(zb_ref, tab_ref, gq_ref, gk_ref, gm_ref,
                        kc_ref, vc_ref, kct_ref, vct_ref, kst_ref, vst_ref, kwt_ref, vwt_ref,
                        qp_ref, ktsb_ref, ktwb_ref, vs1_ref, vw1_ref, qm_ref):
    q, ks, vs, qm = _prep_common(zb_ref, tab_ref, gq_ref, gk_ref, gm_ref)
    kc_ref[...] = ks[0]
    vc_ref[...] = vs[0]
    for ref, val in zip((kct_ref, kst_ref, kwt_ref, vct_ref, vst_ref, vwt_ref), ks + vs):
        ref[...] = val.T
    ktsb_ref[...] = ks[1].T.astype(BF16)
    ktwb_ref[...] = ks[2].T.astype(BF16)
    low = lax.broadcasted_iota(jnp.int32, (q.shape[0], LANES), 1) < NSA_BLOCK
    vs1_ref[...] = jnp.concatenate([jnp.where(low, vs[1], 1.0), jnp.where(low, 1.0, vs[1])], axis=1).astype(BF16)
    vw1_ref[...] = jnp.concatenate([jnp.where(low, vs[2], 1.0), jnp.where(low, 1.0, vs[2])], axis=1).astype(BF16)
    parts = []
    for head in range(NSA_HEADS):
        g = head // NSA_GROUP
        chunk = q[:, (head // 2) * LANES:(head // 2 + 1) * LANES]
        if (head % 2) != g:
            chunk = pltpu.roll(chunk, NSA_BLOCK, 1)
        parts.append(jnp.where(low if g == 0 else jnp.logical_not(low), chunk, 0.0))
    qp_ref[...] = jnp.concatenate(parts, axis=1).astype(BF16)
    qm_ref[...] = qm


def _prep_sample(zb, tab, gq, gk, gm, tm, row0, n_rows):
    blk0 = row0 // tm
    row = lambda w: pl.BlockSpec((tm, w), lambda i: (i, 0))
    full = lambda a: pl.BlockSpec(a.shape, lambda i: (0, 0))
    sd = lambda w, dt=F32: jax.ShapeDtypeStruct((n_rows, w), dt)
    return pl.pallas_call(
        _prep_sample_kernel,
        grid=(n_rows // tm,),
        in_specs=[pl.BlockSpec((tm, zb.shape[1]), lambda i: (blk0 + i, 0)), full(tab), full(gq), full(gk), full(gm)],
        out_specs=[row(512)] + [row(LANES)] * 6 + [row(512)],
        out_shape=[sd(512)] + [sd(LANES)] * 6 + [sd(512, BF16)],
        compiler_params=_cp(("parallel",)),
        name="nsa_prep_sample",
    )(zb, tab, gq, gk, gm)


def _prep_prompt(zb, tab, gq, gk, gm, tm, batch, seq):
    nt = seq // tm
    n = batch * seq
    full = lambda a: pl.BlockSpec(a.shape, lambda b, i: (0, 0))
    row = lambda w: pl.BlockSpec((tm, w), lambda b, i: (b * nt + i, 0))
    tr = pl.BlockSpec((None, LANES, tm), lambda b, i: (b, 0, i))
    sd = lambda w, dt=F32: jax.ShapeDtypeStruct((n, w), dt)
    sdt = lambda dt=F32: jax.ShapeDtypeStruct((batch, LANES, seq), dt)
    return pl.pallas_call(
        _prep_prompt_kernel,
        grid=(batch, nt),
        in_specs=[row(zb.shape[1]), pl.BlockSpec((tm, 3 * LANES), lambda b, i: (i, 0)), full(gq), full(gk), full(gm)],
        out_specs=[row(LANES), row(LANES)] + [tr] * 6 + [row(NSA_HEADS * LANES), tr, tr, row(2 * LANES), row(2 * LANES),
                                                       row(512)],
        out_shape=[sd(LANES), sd(LANES)] + [sdt()] * 6 + [sd(NSA_HEADS * LANES, BF16), sdt(BF16), sdt(BF16),
                                                          sd(2 * LANES, BF16), sd(2 * LANES, BF16), sd(512, BF16)],
        compiler_params=_cp(("parallel", "parallel")),
        name="nsa_prep_prompt",
    )(zb, tab, gq, gk, gm)


def _pool_rows(x, w):
    nblk = x.shape[0] // NSA_BLOCK
    return jnp.sum(x.reshape(nblk, NSA_BLOCK, LANES) * w[None], axis=1)


def _compress_kernel(k_ref, v_ref, w_ref, ck_ref, cv_ref):
    ck_ref[...] = _pool_rows(k_ref[...], w_ref[0])
    cv_ref[...] = _pool_rows(v_ref[...], w_ref[1])


def _compress(kc, vc, w2, n_rows, tm):
    row = pl.BlockSpec((tm, LANES), lambda i: (i, 0))
    out = pl.BlockSpec((tm // NSA_BLOCK, LANES), lambda i: (i, 0))
    sd = jax.ShapeDtypeStruct((n_rows // NSA_BLOCK, LANES), F32)
    return pl.pallas_call(
        _compress_kernel,
        grid=(n_rows // tm,),
        in_specs=[row, row, pl.BlockSpec(w2.shape, lambda i: (0, 0, 0))],
        out_specs=[out, out],
        out_shape=[sd, sd],
        compiler_params=_cp(("parallel",)),
        name="compress",
    )(kc, vc, w2)


def _select(imp, cur, n_cand):
    n_iota = lax.broadcasted_iota(jnp.int32, imp.shape, 1)
    cnt = jnp.zeros(imp.shape, F32)
    for m in range(n_cand):
        col = imp[:, m:m + 1]
        beats = (col > imp) | ((col == imp) & (n_iota > m))
        cnt = cnt + jnp.where(beats & (cur > m), 1.0, 0.0)
    keep = (n_iota == cur) | ((n_iota < cur) & (cnt < NSA_TOPN - 1))
    return jnp.where(keep, 1.0, 0.0)


def _expand_blocks(nb, nkeys, first_key=0):
    kb = (lax.broadcasted_iota(jnp.int32, (nb, nkeys), 1) + first_key) // NSA_BLOCK
    return jnp.where(kb == lax.broadcasted_iota(jnp.int32, (nb, nkeys), 0), 1.0, 0.0).astype(BF16)


def _gate_col(gate, head, j):
    c = GLA_GATE_RANK + head * 3 + j
    return _sigmoid(gate[:, c:c + 1])


def _select_t(imp_t, cur_t, n_cand):
    n_iota = lax.broadcasted_iota(jnp.int32, imp_t.shape, 0)
    cnt = jnp.zeros(imp_t.shape, F32)
    for m in range(n_cand):
        row = imp_t[m:m + 1, :]
        beats = (row > imp_t) | ((row == imp_t) & (n_iota > m))
        cnt = cnt + jnp.where(beats & (cur_t > m), 1.0, 0.0)
    keep = (n_iota == cur_t) | ((n_iota < cur_t) & (cnt < NSA_TOPN - 1))
    return jnp.where(keep, 1.0, 0.0)


def _attend(qh, kt, bias, v_ones):
    s = jnp.dot(qh, kt, preferred_element_type=F32) + bias
    e = jnp.exp((s - jnp.max(s, axis=-1, keepdims=True)).astype(BF16))
    pv = jnp.dot(e, v_ones, preferred_element_type=F32)
    return pv / pltpu.roll(pv, NSA_BLOCK, 1)


def _nsa_prompt_kernel(q_ref, ck_ref, cv_ref, kts_ref, ktw_ref, vs1_ref, vw1_ref, gate_ref, o_ref, osel_ref,
                       *, tq, seq, wslab, nbr):
    i = pl.program_id(1)
    nq = pl.num_programs(1)
    t0 = i * tq
    nb = seq // NSA_BLOCK
    q = q_ref[...]
    pos = t0 + lax.broadcasted_iota(jnp.int32, (tq, 1), 0)
    pos_t = t0 + lax.broadcasted_iota(jnp.int32, (1, tq), 1)
    cur_t = pos_t // NSA_BLOCK
    vis_t = (lax.broadcasted_iota(jnp.int32, (nb, tq), 0) + 1) * NSA_BLOCK <= pos_t + 1
    wstart = pl.multiple_of(jnp.maximum(t0 + tq - wslab, 0), LANES)
    wrel = pos - (wstart + lax.broadcasted_iota(jnp.int32, (tq, wslab), 1))
    wbias = jnp.where((wrel >= 0) & (wrel < NSA_WINDOW), 0.0, NEG)
    ckb = ck_ref[...].astype(BF16)
    cvb = cv_ref[...].astype(BF16)
    cmp_r, win_r = [], []
    for g in range(NSA_KV_HEADS):
        heads = [g * NSA_GROUP + h for h in range(NSA_GROUP)]
        imp_t = jnp.zeros((nb, tq), F32)
        for head in heads:
            qh = q[:, head * LANES:(head + 1) * LANES]
            s_t = jnp.where(vis_t, _dot_nt(ckb, qh), NEG)
            e_t = jnp.where(vis_t, jnp.exp(s_t - jnp.max(s_t, axis=0, keepdims=True)), 0.0)
            pr_t = e_t / jnp.maximum(jnp.sum(e_t, axis=0, keepdims=True), 1e-30)
            imp_t = imp_t + pr_t
            cmp_r.append(_dot_tn(pr_t, cvb))
        sel_t = _select_t(imp_t, cur_t, nb).astype(BF16)
        for j in range(nbr):
            klen = (j + 1) * seq // nbr

            @pl.when(i * nbr // nq == j)
            def _(g=g, klen=klen, sel_t=sel_t, heads=heads):
                kpos = lax.broadcasted_iota(jnp.int32, (tq, klen), 1)
                picked = _dot_tn(sel_t, _expand_blocks(nb, klen)) > 0.5
                sbias = jnp.where(picked & (kpos <= pos), 0.0, NEG)
                kt = kts_ref[:, 0:klen]
                v1 = vs1_ref[0:klen, g * LANES:(g + 1) * LANES]
                for head in heads:
                    hl = slice(head * LANES, (head + 1) * LANES)
                    osel_ref[:, hl] = _attend(q[:, hl], kt, sbias, v1)

        kt = ktw_ref[:, pl.ds(wstart, wslab)]
        v1 = vw1_ref[pl.ds(wstart, wslab), g * LANES:(g + 1) * LANES]
        for head in heads:
            win_r.append(_attend(q[:, head * LANES:(head + 1) * LANES], kt, wbias, v1))

    low = lax.broadcasted_iota(jnp.int32, (tq, LANES), 1) < NSA_BLOCK
    pair = lambda r: jnp.concatenate([jnp.where(low, r[c], r[NSA_GROUP + c]) for c in range(NSA_GROUP)], axis=1)
    sel_r = [osel_ref[:, head * LANES:(head + 1) * LANES] for head in range(NSA_HEADS)]
    sig = _sigmoid(gate_ref[...])
    s_hi = sig.astype(BF16)
    s_lo = (sig - s_hi.astype(F32)).astype(BF16)
    width = NSA_GROUP * LANES
    row = lax.broadcasted_iota(jnp.int32, (LANES, width), 0)
    col = lax.broadcasted_iota(jnp.int32, (LANES, width), 1)
    head_of = jnp.where(col % LANES < NSA_BLOCK, col // LANES, NSA_GROUP + col // LANES)
    out = None
    for j, r in enumerate((cmp_r, sel_r, win_r)):
        pick = jnp.where(row == GLA_GATE_RANK + head_of * 3 + j, 1.0, 0.0).astype(BF16)
        gj = jnp.dot(s_hi, pick, preferred_element_type=F32) + jnp.dot(s_lo, pick, preferred_element_type=F32)
        out = gj * pair(r) if out is None else out + gj * pair(r)
    o_ref[...] = out.astype(BF16)


def _nsa_prompt(qp, ck, cv, kts, ktw, vs1, vw1, zd, batch, seq, tq):
    nq = seq // tq
    nb = seq // NSA_BLOCK
    wslab = min(seq, NSA_WINDOW + tq)
    nbr = 8 if nq % 8 == 0 else (4 if nq % 4 == 0 else 1)
    kern = functools.partial(_nsa_prompt_kernel, tq=tq, seq=seq, wslab=wslab, nbr=nbr)
    per_b =lambda a: pl.BlockSpec((None,) + a.shape[1:], lambda b, i: (b, 0, 0))
    rows_b = lambda a: pl.BlockSpec((seq, a.shape[1]), lambda b, i: (b, 0))
    return pl.pallas_call(
        kern,
        grid=(batch, nq),
        in_specs=[pl.BlockSpec((tq, qp.shape[1]), lambda b, i: (b * nq + i, 0)),
                  pl.BlockSpec((nb, LANES), lambda b, i: (b, 0)),
                  pl.BlockSpec((nb, LANES), lambda b, i: (b, 0)),
                  per_b(kts), per_b(ktw), rows_b(vs1), rows_b(vw1),
                  pl.BlockSpec((tq, LANES), lambda b, i: (b * nq + i, 0))],
        out_specs=pl.BlockSpec((tq, 512), lambda b, i: (b * nq + i, 0)),
        out_shape=jax.ShapeDtypeStruct((batch * seq, 512), BF16),
        scratch_shapes=[pltpu.VMEM((tq, NSA_HEADS * LANES), F32)],
        compiler_params=_cp(("parallel", "parallel")),
        name="nsa_prompt",
    )(qp, ck, cv, kts, ktw, vs1, vw1, zd)


def _qbd(q, rows_per_head):
    hd = NSA_BLOCK
    lane_grp = lax.broadcasted_iota(jnp.int32, (rows_per_head, LANES), 1) // hd
    parts = []
    for head in range(NSA_HEADS):
        g = head // NSA_GROUP
        qh = q[:, head * hd:(head + 1) * hd]
        two = jnp.concatenate([qh, qh], axis=1)
        parts.append(jnp.where(lane_grp == g, two, 0.0))
    return jnp.concatenate(parts, axis=0)


def _nsa_sample_cmp_kernel(pt_ref, *refs, npg, t_new, past_len):
    kpages = refs[0:npg]
    vpages = refs[npg:2 * npg]
    q_ref, kn_ref, vn_ref, wt_ref, wn_ref, ocmp_ref, sel_ref, pk_ref, pv_ref = refs[2 * npg:]
    c = pl.program_id(1)
    nch = pl.num_programs(1)
    hd = NSA_BLOCK
    page = kpages[0].shape[1]
    bpp = page // NSA_BLOCK
    nbc = npg * bpp
    nb_pad = pk_ref.shape[1]
    n_past = past_len // NSA_BLOCK

    @pl.when(c == 0)
    def _():
        pk_ref[...] = jnp.zeros(pk_ref.shape, F32)
        pv_ref[...] = jnp.zeros(pv_ref.shape, F32)

    seg = jnp.where(lax.broadcasted_iota(jnp.int32, (npg * page, nbc), 0) // NSA_BLOCK
                    == lax.broadcasted_iota(jnp.int32, (npg * page, nbc), 1), 1.0, 0.0).astype(BF16)
    col = lax.broadcasted_iota(jnp.int32, (nbc, nb_pad), 1)
    place = jnp.where(col == c * nbc + lax.broadcasted_iota(jnp.int32, (nbc, nb_pad), 0), 1.0, 0.0).astype(BF16)
    for pages, w_t, pooled in ((kpages, wt_ref[0], pk_ref), (vpages, wt_ref[1], pv_ref)):
        x = jnp.concatenate([r[...] * w_t for r in pages], axis=1)
        pooled[...] += _dot_exact_rhs(_dot_exact_rhs(x, seg), place)

    @pl.when(c == nch - 1)
    def _():
        newk = jnp.sum(kn_ref[...] * wn_ref[0][0:t_new], axis=0, keepdims=True)
        newv = jnp.sum(vn_ref[...] * wn_ref[1][0:t_new], axis=0, keepdims=True)
        q = q_ref[...]
        rows = NSA_HEADS * t_new
        qbd = _qbd(q, t_new)
        trow = lax.broadcasted_iota(jnp.int32, (rows, 1), 0) % t_new
        pos = past_len + trow
        blk = lax.broadcasted_iota(jnp.int32, (rows, nb_pad), 1)
        vis = ((blk + 1) * NSA_BLOCK <= pos + 1) & (blk < n_past)
        vis_new = (n_past + 1) * NSA_BLOCK <= pos + 1
        s_past = jnp.where(vis, _dot(qbd, pk_ref[...]), NEG)
        s_new = jnp.where(vis_new, jnp.sum(qbd * newk, axis=-1, keepdims=True), NEG)
        m = jnp.maximum(jnp.max(s_past, axis=-1, keepdims=True), s_new)
        e_past = jnp.where(vis, jnp.exp(s_past - m), 0.0)
        e_new = jnp.where(vis_new, jnp.exp(s_new - m), 0.0)
        den = jnp.maximum(jnp.sum(e_past, axis=-1, keepdims=True) + e_new, 1e-30)
        pr = e_past / den
        o = _dot_nt(pr, pv_ref[...]) + (e_new / den) * newv
        outs = []
        for head in range(NSA_HEADS):
            g = head // NSA_GROUP
            outs.append(o[head * t_new:(head + 1) * t_new, g * hd:(g + 1) * hd])
        ocmp_ref[...] = jnp.concatenate(outs, axis=1)
        imps = []
        for g in range(NSA_KV_HEADS):
            acc = pr[g * NSA_GROUP * t_new:(g * NSA_GROUP + 1) * t_new]
            for h in range(1, NSA_GROUP):
                acc = acc + pr[(g * NSA_GROUP + h) * t_new:(g * NSA_GROUP + h + 1) * t_new]
            imps.append(acc)
        imp = jnp.concatenate(imps, axis=0)
        cur = (past_len + lax.broadcasted_iota(jnp.int32, (NSA_KV_HEADS * t_new, 1), 0) % t_new) // NSA_BLOCK
        sel_ref[...] = _select(imp, cur, n_past)


def _nsa_sample_cmp(page_table, pool_k, pool_v, q, kn, vn, wt, wn, t_new, npg):
    db, n_pages = page_table.shape
    page = pool_k.shape[2]
    past_len = n_pages * page
    nch = n_pages // npg
    nb_pad = -(-(past_len // NSA_BLOCK) // LANES) * LANES
    row0 = q.shape[0] // t_new - db

    def page_spec(j):
        return pl.BlockSpec((None, LANES, page), lambda b, c, pt: (pt[b, c * npg + j], 0, 0))

    tok = lambda w: pl.BlockSpec((t_new, w), lambda b, c, pt: (row0 + b, 0))
    full3 = lambda a: pl.BlockSpec(a.shape, lambda b, c, pt: (0, 0, 0))
    kern = functools.partial(_nsa_sample_cmp_kernel, npg=npg, t_new=t_new, past_len=past_len)
    grid_spec = pltpu.PrefetchScalarGridSpec(
        num_scalar_prefetch=1,
        grid=(db, nch),
        in_specs=[page_spec(j) for j in range(npg)] + [page_spec(j) for j in range(npg)]
        + [tok(512), tok(LANES), tok(LANES), full3(wt), full3(wn)],
        out_specs=[pl.BlockSpec((None, t_new, 512), lambda b, c, pt: (b, 0, 0)),
                   pl.BlockSpec((None, NSA_KV_HEADS * t_new, nb_pad), lambda b, c, pt: (b, 0, 0))],
        scratch_shapes=[pltpu.VMEM((LANES, nb_pad), F32), pltpu.VMEM((LANES, nb_pad), F32)],
    )
    return pl.pallas_call(
        kern,
        grid_spec=grid_spec,
        out_shape=[jax.ShapeDtypeStruct((db, t_new, 512), F32),
                   jax.ShapeDtypeStruct((db, NSA_KV_HEADS * t_new, nb_pad), F32)],
        compiler_params=_cp(("parallel", "arbitrary")),
        name="nsa_sample_cmp",
    )(page_table, *([pool_k] * npg), *([pool_v] * npg), q, kn, vn, wt, wn)


def _online_update(s, mask, v, m_ref, l_ref, acc_ref, v_transposed=False):
    s = jnp.where(mask, s, NEG)
    m_old = m_ref[...]
    m_new = jnp.maximum(m_old, jnp.max(s, axis=-1, keepdims=True))
    alpha = jnp.exp(m_old - m_new)
    e = jnp.where(mask, jnp.exp(s - m_new), 0.0)
    l_ref[...] = alpha * l_ref[...] + jnp.sum(e, axis=-1, keepdims=True)
    acc_ref[...] = alpha * acc_ref[...] + (_dot_nt(e, v) if v_transposed else _dot(e, v))
    m_ref[...] = m_new


def _nsa_sample_sel_kernel(pt_ref, *refs, npg, t_new, past_len):
    kpages = refs[0:npg]
    vpages = refs[npg:2 * npg]
    (q_ref, sel_ref, ksn_ref, vsn_ref, wk_ref, wv_ref, kwn_ref, vwn_ref, ocmp_ref, gate_ref,
     o_ref, m_ref, l_ref, acc_ref, pad_ref) = refs[2 * npg:]
    c = pl.program_id(1)
    nch = pl.num_programs(1)
    hd = NSA_BLOCK
    page = kpages[0].shape[1]
    bpp = page // NSA_BLOCK
    nbc = npg * bpp
    rows = NSA_HEADS * t_new
    q = q_ref[...]
    qbd = _qbd(q, t_new)

    @pl.when(c == 0)
    def _():
        m_ref[...] = jnp.full(m_ref.shape, NEG, F32)
        l_ref[...] = jnp.zeros(l_ref.shape, F32)
        acc_ref[...] = jnp.zeros(acc_ref.shape, F32)

    sel = sel_ref[...]
    nb_pad = sel.shape[1]
    sel_rows = jnp.concatenate([sel[(head // NSA_GROUP) * t_new:(head // NSA_GROUP + 1) * t_new]
                                for head in range(NSA_HEADS)], axis=0)
    blk_iota = lax.broadcasted_iota(jnp.int32, (nb_pad, npg * page), 0)
    key_blk = c * nbc + lax.broadcasted_iota(jnp.int32, (nb_pad, npg * page), 1) // NSA_BLOCK
    expand = jnp.where(blk_iota == key_blk, 1.0, 0.0).astype(BF16)
    smask = _dot(sel_rows, expand) > 0.5
    kcat = jnp.concatenate([r[...].astype(BF16) for r in kpages], axis=1)
    vcat = jnp.concatenate([r[...].astype(BF16) for r in vpages], axis=1)
    _online_update(_dot(qbd, kcat), smask, vcat, m_ref, l_ref, acc_ref, v_transposed=True)

    @pl.when(c == nch - 1)
    def _():
        trow = lax.broadcasted_iota(jnp.int32, (rows, 1), 0) % t_new
        pad_ref[...] = jnp.zeros(pad_ref.shape, F32)
        pad_ref[0:t_new, :] = ksn_ref[...]
        kn = pad_ref[...]
        pad_ref[0:t_new, :] = vsn_ref[...]
        vn = pad_ref[...]
        jn = lax.broadcasted_iota(jnp.int32, (rows, LANES), 1)
        _online_update(_dot_nt(qbd, kn), jn <= trow, vn, m_ref, l_ref, acc_ref)
        o_sel = acc_ref[...] / jnp.maximum(l_ref[...], 1e-30)
        wb = wk_ref.shape[1]
        jw = lax.broadcasted_iota(jnp.int32, (rows, wb), 1)
        rel = trow + wb - jw
        wmask = jnp.concatenate([(rel >= 0) & (rel < NSA_WINDOW), jn <= trow], axis=1)
        pad_ref[0:t_new, :] = kwn_ref[...]
        s_win = jnp.concatenate([_dot(qbd, wk_ref[...]), _dot_nt(qbd, pad_ref[...])], axis=1)
        p_win = _msoftmax(s_win, wmask)
        pad_ref[0:t_new, :] = vwn_ref[...]
        o_win = _dot_nt(p_win[:, 0:wb], wv_ref[...]) + _dot(p_win[:, wb:], pad_ref[...])
        gate = gate_ref[...]
        ocmp = ocmp_ref[...]
        outs = []
        for head in range(NSA_HEADS):
            g = head // NSA_GROUP
            r0, r1 = head * t_new, (head + 1) * t_new
            outs.append(_gate_col(gate, head, 0) * ocmp[:, head * hd:(head + 1) * hd]
                        + _gate_col(gate, head, 1) * o_sel[r0:r1, g * hd:(g + 1) * hd]
                        + _gate_col(gate, head, 2) * o_win[r0:r1, g * hd:(g + 1) * hd])
        o_ref[...] = jnp.concatenate(outs, axis=1)


def _nsa_sample_sel(page_table, pool_k, pool_v, q, sel, ksn, vsn, win_k, win_v, kwn, vwn, ocmp, zd, t_new, npg):
    db, n_pages = page_table.shape
    page = pool_k.shape[2]
    past_len = n_pages * page
    nch = n_pages // npg
    row0 = q.shape[0] // t_new - db
    rows = NSA_HEADS * t_new

    def page_spec(j):
        return pl.BlockSpec((None, LANES, page), lambda b, c, pt: (pt[b, c * npg + j], 0, 0))

    tok = lambda w: pl.BlockSpec((t_new, w), lambda b, c, pt: (row0 + b, 0))
    per_seq = lambda a: pl.BlockSpec((None,) + a.shape[1:], lambda b, c, pt: (b, 0, 0))
    kern = functools.partial(_nsa_sample_sel_kernel, npg=npg, t_new=t_new, past_len=past_len)
    grid_spec = pltpu.PrefetchScalarGridSpec(
        num_scalar_prefetch=1,
        grid=(db, nch),
        in_specs=[page_spec(j) for j in range(npg)] + [page_spec(j) for j in range(npg)]
        + [tok(512), per_seq(sel), tok(LANES), tok(LANES), per_seq(win_k), per_seq(win_v), tok(LANES), tok(LANES),
           per_seq(ocmp), tok(LANES)],
        out_specs=pl.BlockSpec((t_new, 512), lambda b, c, pt: (b, 0)),
        scratch_shapes=[pltpu.VMEM((rows, 1), F32), pltpu.VMEM((rows, 1), F32), pltpu.VMEM((rows, LANES), F32),
                        pltpu.VMEM((LANES, LANES), F32)],
    )
    return pl.pallas_call(
        kern,
        grid_spec=grid_spec,
        out_shape=jax.ShapeDtypeStruct((db * t_new, 512), F32),
        compiler_params=_cp(("parallel", "arbitrary")),
        name="nsa_sample_sel",
    )(page_table, *([pool_k] * npg), *([pool_v] * npg), q, sel, ksn, vsn, win_k, win_v, kwn, vwn, ocmp, zd)


def _nsa_sample_kernel(pt_ref, ck_hbm, cv_hbm, sk_hbm, sv_hbm, q_ref, kcn_ref, vcn_ref, ksn_ref, vsn_ref,
                       kwn_ref, vwn_ref, gate_ref, wk_ref, wv_ref, wt_ref, wn_ref, o_ref,
                       buf_ref, sem_ref, pk_ref, pv_ref, m_ref, l_ref, acc_ref, pad_ref, bias_ref,
                       *, npg, t_new, past_len):
    b = pl.program_id(0)
    nseq = pl.num_programs(0)
    hd = NSA_BLOCK
    page = buf_ref.shape[3]
    n_pages = past_len // page
    nch = n_pages // npg
    nbc = npg * page // NSA_BLOCK
    nb_pad = pk_ref.shape[1]
    n_past = past_len // NSA_BLOCK
    rows = NSA_HEADS * t_new
    nkeys = npg * page

    def chunk_copies(seq, c, slot):
        pools = (ck_hbm, cv_hbm) if c < nch else (sk_hbm, sv_hbm)
        first = (c % nch) * npg
        copies = []
        for j in range(npg):
            pg = pt_ref[seq, first + j]
            copies.append(pltpu.make_async_copy(pools[0].at[pg], buf_ref.at[slot, j], sem_ref.at[slot]))
            copies.append(pltpu.make_async_copy(pools[1].at[pg], buf_ref.at[slot, npg + j], sem_ref.at[slot]))
        return copies

    def start_all(copies):
        for n, cp in enumerate(copies):
            cp.start(priority=n % 2)

    @pl.when(b == 0)
    def _():
        start_all(chunk_copies(0, 0, 0))

    q = q_ref[...]
    qbd = _qbd(q, t_new)
    trow = lax.broadcasted_iota(jnp.int32, (rows, 1), 0) % t_new
    pos = past_len + trow
    seg = jnp.where(lax.broadcasted_iota(jnp.int32, (nkeys, nbc), 0) // NSA_BLOCK
                    == lax.broadcasted_iota(jnp.int32, (nkeys, nbc), 1), 1.0, 0.0).astype(BF16)
    expand = _expand_blocks(nbc, nkeys)
    if n_past < nb_pad:
        pk_ref[...] = jnp.zeros(pk_ref.shape, F32)
        pv_ref[...] = jnp.zeros(pv_ref.shape, F32)
    m_ref[...] = jnp.full(m_ref.shape, 0.1 * NEG, F32)
    l_ref[...] = jnp.zeros(l_ref.shape, F32)
    acc_ref[...] = jnp.zeros(acc_ref.shape, F32)
    o_cmp = None

    for c in range(2 * nch):
        slot = c % 2
        if c + 1 < 2 * nch:
            start_all(chunk_copies(b, c + 1, 1 - slot))
        else:
            @pl.when(b + 1 < nseq)
            def _():
                start_all(chunk_copies(b + 1, 0, 1 - slot))
        for cp in chunk_copies(b, c, slot):
            cp.wait()

        if c < nch:
            for base, w_t, pooled in ((0, wt_ref[0], pk_ref), (npg, wt_ref[1], pv_ref)):
                x = jnp.concatenate([(buf_ref[slot, base + j] * w_t).astype(BF16) for j in range(npg)], axis=1)
                pooled[:, c * nbc:(c + 1) * nbc] = jnp.dot(x, seg, preferred_element_type=F32)
        else:
            k0 = (c - nch) * nkeys
            kcat = jnp.concatenate([buf_ref[slot, j].astype(BF16) for j in range(npg)], axis=1)
            vcat = jnp.concatenate([buf_ref[slot, npg + j].astype(BF16) for j in range(npg)], axis=1)
            s = _dot(qbd, kcat) + bias_ref[:, k0:k0 + nkeys]
            m_old = m_ref[...]
            m_new = jnp.maximum(m_old, jnp.max(s, axis=-1, keepdims=True))
            alpha = jnp.exp(m_old - m_new)
            e = jnp.exp(s - m_new)
            l_ref[...] = alpha * l_ref[...] + jnp.sum(e, axis=-1, keepdims=True)
            acc_ref[...] = alpha * acc_ref[...] + _dot_nt(e, vcat)
            m_ref[...] = m_new

        if c == nch - 1:
            newk = jnp.sum(kcn_ref[...] * wn_ref[0][0:t_new], axis=0, keepdims=True)
            newv = jnp.sum(vcn_ref[...] * wn_ref[1][0:t_new], axis=0, keepdims=True)
            blk = lax.broadcasted_iota(jnp.int32, (rows, nb_pad), 1)
            vis = ((blk + 1) * NSA_BLOCK <= pos + 1) & (blk < n_past)
            vis_new = (n_past + 1) * NSA_BLOCK <= pos + 1
            s_past = jnp.where(vis, _dot(qbd, pk_ref[...]), NEG)
            s_new = jnp.where(vis_new, jnp.sum(qbd * newk, axis=-1, keepdims=True), NEG)
            mx = jnp.maximum(jnp.max(s_past, axis=-1, keepdims=True), s_new)
            e_past = jnp.where(vis, jnp.exp(s_past - mx), 0.0)
            e_new = jnp.where(vis_new, jnp.exp(s_new - mx), 0.0)
            den = jnp.maximum(jnp.sum(e_past, axis=-1, keepdims=True) + e_new, 1e-30)
            pr = e_past / den
            o_cmp = _dot_nt(pr, pv_ref[...]) + (e_new / den) * newv
            imps = []
            for g in range(NSA_KV_HEADS):
                acc = pr[g * NSA_GROUP * t_new:(g * NSA_GROUP + 1) * t_new]
                for h in range(1, NSA_GROUP):
                    acc = acc + pr[(g * NSA_GROUP + h) * t_new:(g * NSA_GROUP + h + 1) * t_new]
                imps.append(acc)
            imp = jnp.concatenate(imps, axis=0)
            cur = (past_len + lax.broadcasted_iota(jnp.int32, (NSA_KV_HEADS * t_new, 1), 0) % t_new) // NSA_BLOCK
            sel = _select(imp, cur, n_past).astype(BF16)
            sel_rows = jnp.concatenate([sel[(head // NSA_GROUP) * t_new:(head // NSA_GROUP + 1) * t_new]
                                        for head in range(NSA_HEADS)], axis=0)
            for cc in range(nch):
                picked = jnp.dot(sel_rows[:, cc * nbc:(cc + 1) * nbc], expand, preferred_element_type=F32) > 0.5
                bias_ref[:, cc * nkeys:(cc + 1) * nkeys] = jnp.where(picked, 0.0, NEG)

    pad_ref[...] = jnp.zeros(pad_ref.shape, F32)
    pad_ref[0:t_new, :] = ksn_ref[...]
    kn = pad_ref[...]
    pad_ref[0:t_new, :] = vsn_ref[...]
    vn = pad_ref[...]
    jn = lax.broadcasted_iota(jnp.int32, (rows, LANES), 1)
    _online_update(_dot_nt(qbd, kn), jn <= trow, vn, m_ref, l_ref, acc_ref)
    o_sel = acc_ref[...] / jnp.maximum(l_ref[...], 1e-30)
    wb = wk_ref.shape[1]
    jw = lax.broadcasted_iota(jnp.int32, (rows, wb), 1)
    rel = trow + wb - jw
    wmask = jnp.concatenate([(rel >= 0) & (rel < NSA_WINDOW), jn <= trow], axis=1)
    pad_ref[0:t_new, :] = kwn_ref[...]
    s_win = jnp.concatenate([_dot(qbd, wk_ref[...]), _dot_nt(qbd, pad_ref[...])], axis=1)
    p_win = _msoftmax(s_win, wmask)
    pad_ref[0:t_new, :] = vwn_ref[...]
    o_win = _dot_nt(p_win[:, 0:wb], wv_ref[...]) + _dot(p_win[:, wb:], pad_ref[...])
    gate = gate_ref[...]
    outs = []
    for head in range(NSA_HEADS):
        g = head // NSA_GROUP
        r0, r1 = head * t_new, (head + 1) * t_new
        gl = slice(g * hd, (g + 1) * hd)
        outs.append(_gate_col(gate, head, 0) * o_cmp[r0:r1, gl] + _gate_col(gate, head, 1) * o_sel[r0:r1, gl]
                    + _gate_col(gate, head, 2) * o_win[r0:r1, gl])
    o_ref[...] = jnp.concatenate(outs, axis=1)


def _nsa_sample(page_table, pool_ck, pool_cv, pool_sk, pool_sv, q, kc, vc, ks, vs, kw, vw, zd, win_k, win_v,
                wt, wn, t_new, npg, zd_row0):
    db, n_pages = page_table.shape
    page = pool_ck.shape[2]
    past_len = n_pages * page
    nb_pad = -(-(past_len // NSA_BLOCK) // LANES) * LANES
    row0 = q.shape[0] // t_new - db
    rows = NSA_HEADS * t_new
    assert n_pages % npg == 0 and (2 * n_pages // npg) % 2 == 0
    tok = lambda w: pl.BlockSpec((t_new, w), lambda b, pt: (row0 + b, 0))
    per_seq = lambda a: pl.BlockSpec((None,) + a.shape[1:], lambda b, pt: (b, 0, 0))
    full3 = lambda a: pl.BlockSpec(a.shape, lambda b, pt: (0, 0, 0))
    hbm = pl.BlockSpec(memory_space=pl.ANY)
    kern = functools.partial(_nsa_sample_kernel, npg=npg, t_new=t_new, past_len=past_len)
    grid_spec = pltpu.PrefetchScalarGridSpec(
        num_scalar_prefetch=1,
        grid=(db,),
        in_specs=[hbm, hbm, hbm, hbm, tok(512)] + [tok(LANES)] * 6
        + [pl.BlockSpec((t_new, LANES), lambda b, pt: (zd_row0 + b, 0)), per_seq(win_k), per_seq(win_v),
           full3(wt), full3(wn)],
        out_specs=pl.BlockSpec((t_new, 512), lambda b, pt: (b, 0)),
        scratch_shapes=[pltpu.VMEM((2, 2 * npg, LANES, page), F32), pltpu.SemaphoreType.DMA((2,)),
                        pltpu.VMEM((LANES, nb_pad), F32), pltpu.VMEM((LANES, nb_pad), F32),
                        pltpu.VMEM((rows, 1), F32), pltpu.VMEM((rows, 1), F32), pltpu.VMEM((rows, LANES), F32),
                        pltpu.VMEM((LANES, LANES), F32), pltpu.VMEM((rows, past_len), F32)],
    )
    return pl.pallas_call(
        kern,
        grid_spec=grid_spec,
        out_shape=jax.ShapeDtypeStruct((db * t_new, 512), F32),
        compiler_params=_cp(("arbitrary",)),
        name="nsa_sample",
    )(page_table, pool_ck, pool_cv, pool_sk, pool_sv, q, kc, vc, ks, vs, kw, vw, zd, win_k, win_v, wt, wn)


def _log_sigmoid(x):
    return jnp.minimum(x, 0.0) - jnp.log(1.0 + jnp.exp(-jnp.abs(x)))


def _gla_kernel(*refs, nchunk, t_valid, nseq):
    za_refs, zd_refs = refs[0:nseq], refs[nseq:2 * nseq]
    wgh_ref, wgl_ref, bg_ref, gout_ref, s0_all, o_all_ref, s_all = refs[2 * nseq:2 * nseq + 7]
    scratch = refs[2 * nseq + 7:]
    views = [(za_refs[s], zd_refs[s], s0_all.at[s], o_all_ref.at[s], s_all.at[s]) + tuple(scratch[6 * s:6 * s + 6])
             for s in range(nseq)]
    j = pl.program_id(1)
    short = za_refs[0].shape[0] < GLA_CHUNK
    nj = pl.num_programs(1)
    C = GLA_CHUNK
    sub = GLA_SUB
    nsub = C // sub
    dk = 64
    dv = 128
    hk = GLA_HEADS * dk
    hv = GLA_HEADS * dv

    @pl.when(j == 0)
    def _():
        for (_, _, s0_ref, _, _, st_ref, kp_ref, bp_ref, vp_ref, _, _) in views:
            st_ref[...] = jnp.concatenate([s0_ref[h].T for h in range(GLA_HEADS)], axis=1)
            kp_ref[...] = jnp.zeros(kp_ref.shape, F32)
            bp_ref[...] = jnp.zeros(bp_ref.shape, F32)
            vp_ref[...] = jnp.zeros(vp_ref.shape, F32)

    tril = jnp.where(lax.broadcasted_iota(jnp.int32, (C, C), 0) >= lax.broadcasted_iota(jnp.int32, (C, C), 1),
                     1.0, 0.0).astype(BF16)
    head_ones = jnp.where(lax.broadcasted_iota(jnp.int32, (hk, LANES), 0) // dk
                          == lax.broadcasted_iota(jnp.int32, (hk, LANES), 1), 1.0, 0.0).astype(BF16)
    tmod = lax.broadcasted_iota(jnp.int32, (C, 1), 0) % sub
    gout = gout_ref[...]

    def chunk(c, carry):
        for view in views:
            seq_chunk(c, view)
        return carry

    def seq_chunk(c, view):
        za_ref, zd_ref, _, o_ref, _, st_ref, kp_ref, bp_ref, vp_ref, za_pad_ref, zd_pad_ref = view
        r0 = pl.multiple_of(c * C, C)
        if short:
            za_pad_ref[...] = jnp.zeros(za_pad_ref.shape, F32)
            zd_pad_ref[...] = jnp.zeros(zd_pad_ref.shape, F32)
            za_pad_ref[0:za_ref.shape[0], :] = za_ref[...]
            zd_pad_ref[0:zd_ref.shape[0], :] = zd_ref[...]
            z = za_pad_ref[...]
            zd = zd_pad_ref[...]
        else:
            z = za_ref[pl.ds(r0, C), :]
            zd = zd_ref[pl.ds(r0, C), :]
        q = z[:, 0:hk] * (dk ** -0.5)
        k = z[:, hk:2 * hk]
        v = z[:, 2 * hk:2 * hk + hv]
        r = z[:, 2 * hk + hv:2 * hk + 2 * hv]
        lr = zd[:, 0:GLA_GATE_RANK]
        la = _log_sigmoid(_dot_hi(lr, wgh_ref[...], wgl_ref[...]) + bg_ref[...]) * (1.0 / GLA_TAU)
        tglob = (j * nchunk + c) * C + lax.broadcasted_iota(jnp.int32, (C, 1), 0)
        la = jnp.where(tglob < t_valid, la, 0.0)
        b = _dot_exact_lhs(tril, la)
        st = st_ref[...]
        qe = q * jnp.exp(b)
        kp_ref[sub:sub + C, :] = k
        bp_ref[sub:sub + C, :] = b
        vp_ref[sub:sub + C, :] = v
        xs = []
        for d in range(sub):
            kd = kp_ref[sub - d:sub - d + C, :]
            bd = bp_ref[sub - d:sub - d + C, :]
            xs.append(jnp.where(tmod >= d, q * kd * jnp.exp(b - bd), 0.0).astype(BF16))
        rr = jnp.dot(jnp.concatenate(xs, axis=0), head_ones, preferred_element_type=F32)
        outs = []
        for h in range(GLA_HEADS):
            kh = slice(h * dk, (h + 1) * dk)
            vh = slice(h * dv, (h + 1) * dv)
            o = _dot_nt(qe[:, kh], st[:, kh])
            for d in range(sub):
                o = o + rr[d * C:(d + 1) * C, h:h + 1] * vp_ref[sub - d:sub - d + C, vh]
            offs = [jnp.zeros((sub, dv), F32)]
            for i in range(1, nsub):
                anchor = b[i * sub - 1:i * sub, kh]
                qt = q[i * sub:(i + 1) * sub, kh] * jnp.exp(b[i * sub:(i + 1) * sub, kh] - anchor)
                kt = k[0:i * sub, kh] * jnp.exp(anchor - b[0:i * sub, kh])
                offs.append(_dot(_dot_nt(qt, kt), v[0:i * sub, vh]))
            o = o + jnp.concatenate(offs, axis=0)
            on = o * lax.rsqrt(jnp.mean(o * o, axis=-1, keepdims=True) + EPS) * gout
            rh = r[:, vh]
            outs.append(on * (rh * _sigmoid(rh)))
        o_all = jnp.concatenate(outs, axis=1)
        if short:
            o_ref[...] = o_all[0:o_ref.shape[0], :].astype(o_ref.dtype)
        else:
            o_ref[pl.ds(r0, C), :] = o_all.astype(o_ref.dtype)
        b_last = b[C - 1:C, :]
        kk = k * jnp.exp(b_last - b)
        upd = jnp.concatenate([_dot_tn(v[:, h * dv:(h + 1) * dv], kk[:, h * dk:(h + 1) * dk])
                               for h in range(GLA_HEADS)], axis=1)
        st_ref[...] = st * jnp.exp(b_last) + upd

    lax.fori_loop(0, nchunk, chunk, 0)

    @pl.when(j == nj - 1)
    def _():
        for (_, _, _, _, s_ref, st_ref, _, _, _, _, _) in views:
            st = st_ref[...]
            for h in range(GLA_HEADS):
                s_ref[h] = st[:, h * dk:(h + 1) * dk].T


GLA_SEQS = 2


def _gla(za, zd, wgh, wgl, bg, gout, s0, batch, t_seq, ct, row0, out_dtype):
    nj = t_seq // ct
    blk0 = row0 // ct
    hv = GLA_HEADS * 128
    ns = GLA_SEQS if batch % GLA_SEQS == 0 else 1
    kern = functools.partial(_gla_kernel, nchunk=max(1, ct // GLA_CHUNK), t_valid=t_seq, nseq=ns)
    full = lambda a: pl.BlockSpec(a.shape, lambda b, j: (0,) * a.ndim)
    rows = lambda w, s: pl.BlockSpec((ct, w), lambda b, j: (blk0 + (b * ns + s) * nj + j, 0))
    state = pl.BlockSpec((ns,) + s0.shape[1:], lambda b, j: (b, 0, 0, 0))
    pad_rows = GLA_SUB + GLA_CHUNK
    o, s_new = pl.pallas_call(
        kern,
        grid=(batch // ns, nj),
        in_specs=[rows(za.shape[1], s) for s in range(ns)] + [rows(LANES, s) for s in range(ns)]
        + [full(wgh), full(wgl), full(bg), full(gout), state],
        out_specs=[pl.BlockSpec((ns, ct, hv), lambda b, j: (b, j, 0)), state],
        out_shape=[jax.ShapeDtypeStruct((batch, t_seq, hv), out_dtype), jax.ShapeDtypeStruct(s0.shape, F32)],
        scratch_shapes=[pltpu.VMEM((128, GLA_HEADS * 64), F32),
                        pltpu.VMEM((pad_rows, GLA_HEADS * 64), F32),
                        pltpu.VMEM((pad_rows, GLA_HEADS * 64), F32),
                        pltpu.VMEM((pad_rows, hv), F32),
                        pltpu.VMEM((GLA_CHUNK, za.shape[1]), F32),
                        pltpu.VMEM((GLA_CHUNK, LANES), F32)] * ns,
        compiler_params=_cp(("parallel", "arbitrary")),
        name="gla",
    )(*([za] * ns), *([zd] * ns), wgh, wgl, bg, gout, s0)
    return o.reshape(batch * t_seq, hv), s_new


def _mem_kv_kernel(m_ref, g_ref, w_ref, gk_ref, k_ref, v_ref):
    x = m_ref[...]
    u = x * lax.rsqrt(jnp.mean(x * x, axis=-1, keepdims=True) + EPS) * g_ref[...]
    kv = jnp.dot(u.astype(BF16), w_ref[...], preferred_element_type=F32)
    half = kv.shape[1] // 2
    k_ref[...] = _seg_rms(kv[:, 0:half], half // MEM_HEADS, gk_ref[...])
    v_ref[...] = kv[:, half:]


def _mem_kv(mem, g, w, gk, tm):
    n, d = mem.shape
    half = w.shape[1] // 2
    full = lambda a: pl.BlockSpec(a.shape, lambda i: (0, 0))
    sd = jax.ShapeDtypeStruct((n, half), F32)
    return pl.pallas_call(
        _mem_kv_kernel,
        grid=(n // tm,),
        in_specs=[pl.BlockSpec((tm, d), lambda i: (i, 0)), full(g), full(w), full(gk)],
        out_specs=[pl.BlockSpec((tm, half), lambda i: (i, 0))] * 2,
        out_shape=[sd, sd],
        compiler_params=_cp(("parallel",)),
        name="mem_kv",
    )(mem, g, w, gk)


def _mem_prompt_kernel(q_ref, k_ref, v_ref, o_ref):
    q = q_ref[...]
    k = k_ref[...].astype(BF16)
    v = v_ref[...].astype(BF16)
    hd = q.shape[1] // MEM_HEADS
    outs = []
    for h in range(MEM_HEADS):
        sl = slice(h * hd, (h + 1) * hd)
        s = _dot_nt(q[:, sl], k[:, sl])
        m = jnp.max(s, axis=-1, keepdims=True)
        e = jnp.exp(s - m)
        outs.append(_dot(e / jnp.sum(e, axis=-1, keepdims=True), v[:, sl]))
    o_ref[...] = jnp.concatenate(outs, axis=1).astype(BF16)


def _mem_prompt(qm, mk, mv, batch, seq, mlen, tq):
    nq = seq // tq
    w = qm.shape[1]
    return pl.pallas_call(
        _mem_prompt_kernel,
        grid=(batch, nq),
        in_specs=[pl.BlockSpec((tq, w), lambda b, i: (b * nq + i, 0)),
                  pl.BlockSpec((mlen, w), lambda b, i: (b, 0)),
                  pl.BlockSpec((mlen, w), lambda b, i: (b, 0))],
        out_specs=pl.BlockSpec((tq, w), lambda b, i: (b * nq + i, 0)),
        out_shape=jax.ShapeDtypeStruct((batch * seq, w), BF16),
        compiler_params=_cp(("parallel", "parallel")),
        name="mem_prompt",
    )(qm, mk, mv)


def _mem_sample_kernel(q_ref, k_ref, v_ref, o_ref, *, sb, t_new):
    w = q_ref.shape[1]
    hd = w // MEM_HEADS
    rows = MEM_HEADS * t_new
    nk = k_ref.shape[1]
    same_head = (lax.broadcasted_iota(jnp.int32, (rows, nk), 1) % MEM_HEADS
                 == lax.broadcasted_iota(jnp.int32, (rows, nk), 0) // t_new)
    bias = jnp.where(same_head, 0.0, NEG)
    for s_i in range(sb):
        q = q_ref[s_i * t_new:(s_i + 1) * t_new, :].astype(F32)
        qs = jnp.concatenate([q[:, h * hd:(h + 1) * hd] for h in range(MEM_HEADS)], axis=0)
        s = _dot_nt(qs, k_ref[s_i]) + bias
        e = jnp.exp(s - jnp.max(s, axis=-1, keepdims=True))
        o = _dot(e / jnp.sum(e, axis=-1, keepdims=True), v_ref[s_i])
        o_ref[s_i * t_new:(s_i + 1) * t_new, :] = jnp.concatenate(
            [o[h * t_new:(h + 1) * t_new, :] for h in range(MEM_HEADS)], axis=1).astype(BF16)


def _mem_sample(qm, ck, cv, row0_blocks, db, t_new, sb):
    w = qm.shape[1]
    kern = functools.partial(_mem_sample_kernel, sb=sb, t_new=t_new)
    return pl.pallas_call(
        kern,
        grid=(db // sb,),
        in_specs=[pl.BlockSpec((sb * t_new, w), lambda i: (row0_blocks + i, 0)),
                  pl.BlockSpec((sb,) + ck.shape[1:], lambda i: (i, 0, 0)),
                  pl.BlockSpec((sb,) + cv.shape[1:], lambda i: (i, 0, 0))],
        out_specs=pl.BlockSpec((sb * t_new, w), lambda i: (i, 0)),
        out_shape=jax.ShapeDtypeStruct((db * t_new, w), BF16),
        compiler_params=_cp(("parallel",)),
        name="mem_sample",
    )(qm, ck, cv)


def _merge_kernel(xp_ref, xs_ref, ogp_ref, ogs_ref, onp_ref, ons_ref, omp_ref, oms_ref, gate_ref,
                  wg_ref, wn_ref, wm_ref, wo_ref, gf_ref, wrh_ref, wrl_ref, br_ref, h_ref, hn_ref, cmb_ref,
                  *, n_experts, n_first, n_tiles):
    pid = pl.program_id(0)
    rest = (gate_ref, wg_ref, wn_ref, wm_ref, wo_ref, gf_ref, wrh_ref, wrl_ref, br_ref, h_ref, hn_ref, cmb_ref,
            n_experts)

    @pl.when(pid >= n_tiles)
    def _():
        h_ref[...] = jnp.zeros(h_ref.shape, h_ref.dtype)
        hn_ref[...] = jnp.zeros(hn_ref.shape, hn_ref.dtype)
        cmb_ref[...] = jnp.zeros(cmb_ref.shape, cmb_ref.dtype)

    @pl.when(pid < n_first)
    def _():
        _merge_body(xp_ref, ogp_ref, onp_ref, omp_ref, *rest)

    @pl.when((pid >= n_first) & (pid < n_tiles))
    def _():
        _merge_body(xs_ref, ogs_ref, ons_ref, oms_ref, *rest)


def _merge_body(x_ref, og_ref, on_ref, om_ref, gate_ref, wg_ref, wn_ref, wm_ref, wo_ref, gf_ref,
                wrh_ref, wrl_ref, br_ref, h_ref, hn_ref, cmb_ref, n_experts):
    d = x_ref.shape[1]
    gate = gate_ref[...].astype(F32)
    y = (gate[:, 0:d] * jnp.dot(og_ref[...], wg_ref[...], preferred_element_type=F32)
         + gate[:, d:2 * d] * jnp.dot(on_ref[...], wn_ref[...], preferred_element_type=F32)
         + gate[:, 2 * d:3 * d] * jnp.dot(om_ref[...], wm_ref[...], preferred_element_type=F32))
    h = x_ref[...] + jnp.dot(y.astype(BF16), wo_ref[...], preferred_element_type=F32)
    h_ref[...] = h
    hn = h * lax.rsqrt(jnp.mean(h * h, axis=-1, keepdims=True) + EPS) * gf_ref[...]
    hn_ref[...] = hn.astype(BF16)
    logits = _dot_hi(hn, wrh_ref[...], wrl_ref[...]) + br_ref[...]
    lane = lax.broadcasted_iota(jnp.int32, logits.shape, 1).astype(F32)
    work = jnp.where(lane < n_experts, logits, NEG)
    top = jnp.max(work, axis=-1, keepdims=True)
    chosen = jnp.zeros(logits.shape, F32)
    for _ in range(TOP_K):
        mx = jnp.max(work, axis=-1, keepdims=True)
        first = jnp.min(jnp.where(work == mx, lane, float(LANES)), axis=-1, keepdims=True)
        pick = lane == first
        chosen = jnp.where(pick, 1.0, chosen)
        work = jnp.where(pick, NEG, work)
    e = jnp.where(chosen > 0.5, jnp.exp(logits - top), 0.0)
    cmb_ref[...] = e / jnp.sum(e, axis=-1, keepdims=True)


def _merge(x, og, on, om, gate, wg, wn, wm, wo, gf, wrh, wrl, br, n_experts, tm, n_pad):
    d = x[0].shape[1]
    n_first = x[0].shape[0] // tm
    n_tiles = n_first + x[1].shape[0] // tm
    row = lambda a: pl.BlockSpec((tm, a.shape[1]), lambda i: (jnp.minimum(i, n_tiles - 1), 0))
    full = lambda a: pl.BlockSpec(a.shape, lambda i: (0, 0))
    kern = functools.partial(_merge_kernel, n_experts=n_experts, n_first=n_first, n_tiles=n_tiles)
    return pl.pallas_call(
        kern,
        grid=(n_pad // tm,),
        in_specs=_two_part_specs(x, tm) + _two_part_specs(og, tm) + _two_part_specs(on, tm) + _two_part_specs(om, tm)
        + [row(gate), full(wg), full(wn), full(wm), full(wo), full(gf), full(wrh), full(wrl), full(br)],
        out_specs=[pl.BlockSpec((tm, d), lambda i: (i, 0)), pl.BlockSpec((tm, d), lambda i: (i, 0)),
                   pl.BlockSpec((tm, LANES), lambda i: (i, 0))],
        out_shape=[jax.ShapeDtypeStruct((n_pad, d), F32), jax.ShapeDtypeStruct((n_pad, d), BF16),
                   jax.ShapeDtypeStruct((n_pad, LANES), F32)],
        compiler_params=_cp(("parallel",)),
        name="merge_router",
    )(*x, *og, *on, *om, gate, wg, wn, wm, wo, gf, wrh, wrl, br)


MOE_SUBTILE = 1024
MOE_SUBTILES = 2
MOE_ROW_BLOCK = 160


SPLIT_COLS = 512


def _split_gate_up_kernel(w_ref, wg_ref, wu_ref):
    half = SPLIT_COLS // 2
    r = lax.broadcasted_iota(jnp.int32, (SPLIT_COLS, half), 0)
    c = lax.broadcasted_iota(jnp.int32, (SPLIT_COLS, half), 1)
    even = jnp.where(r == 2 * c, 1.0, 0.0).astype(BF16)
    odd = jnp.where(r == 2 * c + 1, 1.0, 0.0).astype(BF16)
    for t in range(w_ref.shape[1] // SPLIT_COLS):
        blk = w_ref[:, t * SPLIT_COLS:(t + 1) * SPLIT_COLS].astype(BF16)
        wg_ref[:, t * half:(t + 1) * half] = jnp.dot(blk, even, preferred_element_type=F32).astype(BF16)
        wu_ref[:, t * half:(t + 1) * half] = jnp.dot(blk, odd, preferred_element_type=F32).astype(BF16)


def _split_gate_up(w):
    ne, d, dff2 = w.shape
    sd = jax.ShapeDtypeStruct((ne, d, dff2 // 2), BF16)
    out = pl.BlockSpec((None, d, dff2 // 2), lambda e: (e, 0, 0))
    return pl.pallas_call(
        _split_gate_up_kernel,
        grid=(ne,),
        in_specs=[pl.BlockSpec((None, d, dff2), lambda e: (e, 0, 0))],
        out_specs=[out, out],
        out_shape=[sd, sd],
        compiler_params=_cp(("parallel",)),
        name="split_gate_up",
    )(w)


def _moe_kernel(hn_ref, h_ref, cmb_ref, wg_ref, wu_ref, bg_ref, bu_ref, wd_ref, bd_ref, y_ref,
                rank_ref, rankt_ref, cmbt_ref, cnt_ref, *, ts):
    e = pl.program_id(1)
    ns = hn_ref.shape[0] // ts
    rb = MOE_ROW_BLOCK

    @pl.when(e == 0)
    def _():
        y_ref[...] = h_ref[...]
        lower = jnp.where(lax.broadcasted_iota(jnp.int32, (ts, ts), 0) > lax.broadcasted_iota(jnp.int32, (ts, ts), 1),
                          1.0, 0.0).astype(BF16)
        for s in range(ns):
            cmb = cmb_ref[s * ts:(s + 1) * ts, :]
            sel = jnp.where(cmb > 0.0, 1.0, 0.0)
            rank = jnp.where(sel > 0.0, jnp.dot(lower, sel.astype(BF16), preferred_element_type=F32), -1.0)
            rank_ref[s] = rank
            rankt_ref[s] = rank.T
            cmbt_ref[s] = cmb.T
            cnt_ref[s] = jnp.broadcast_to(jnp.sum(sel, axis=0, keepdims=True), cnt_ref.shape[1:])

    lane = lax.broadcasted_iota(jnp.int32, (1, LANES), 1)
    pick = jnp.where(lax.broadcasted_iota(jnp.int32, (LANES, LANES), 0) == e, 1.0, 0.0).astype(BF16)
    n_iter = 0
    rank_cols = []
    for s in range(ns):
        cnt = jnp.sum(jnp.where(lane == e, cnt_ref[s, 0:1, :], 0.0)).astype(jnp.int32)
        n_iter = jnp.maximum(n_iter, (cnt + rb - 1) // rb)
        r = rank_ref[s]
        r_hi = r.astype(BF16)
        r_lo = (r - r_hi.astype(F32)).astype(BF16)
        rc = (jnp.dot(r_hi, pick, preferred_element_type=F32)
              + jnp.dot(r_lo, pick, preferred_element_type=F32))
        rank_cols.append(jnp.concatenate([rc] * (-(-rb // LANES)), axis=1)[:, 0:rb])

    def body(i, carry):
        r0 = i * rb
        rows = (r0 + lax.broadcasted_iota(jnp.int32, (rb, ts), 0)).astype(F32)
        xs, ws = [], []
        for s in range(ns):
            hit = rankt_ref[s, pl.ds(e, 1), :] == rows
            xs.append(jnp.dot(jnp.where(hit, 1.0, 0.0).astype(BF16), hn_ref[s * ts:(s + 1) * ts, :],
                              preferred_element_type=F32).astype(BF16))
            ws.append(jnp.sum(jnp.where(hit, cmbt_ref[s, pl.ds(e, 1), :], 0.0), axis=-1, keepdims=True))
        x = jnp.concatenate(xs, axis=0)
        gate = jnp.minimum(jnp.dot(x, wg_ref[...], preferred_element_type=F32) + bg_ref[...], SWIGLU_LIMIT)
        up = jnp.clip(jnp.dot(x, wu_ref[...], preferred_element_type=F32) + bu_ref[...], -SWIGLU_LIMIT, SWIGLU_LIMIT)
        act = (up + 1.0) * gate * _sigmoid(SWIGLU_ALPHA * gate)
        out = jnp.dot(act.astype(BF16), wd_ref[...], preferred_element_type=F32) + bd_ref[...]
        cols = (r0 + lax.broadcasted_iota(jnp.int32, (ts, rb), 1)).astype(F32)
        for s in range(ns):
            yw = (out[s * rb:(s + 1) * rb, :] * ws[s]).astype(BF16)
            pt = jnp.where(rank_cols[s] == cols, 1.0, 0.0).astype(BF16)
            y_ref[s * ts:(s + 1) * ts, :] += jnp.dot(pt, yw, preferred_element_type=F32)
        return carry

    lax.fori_loop(0, n_iter, body, 0)


def _moe(hn, h, cmb, wg, wu, bg, bu, wd, bd, ts, ns):
    n, d = h.shape
    ne, _, dff = wg.shape
    tt = ts * ns
    once = lambda w: pl.BlockSpec((tt, w), lambda i, e: (i, 0), pipeline_mode=pl.Buffered(1))
    per_expert = lambda a: pl.BlockSpec((None,) + a.shape[1:], lambda i, e: (e, 0, 0))
    return pl.pallas_call(
        functools.partial(_moe_kernel, ts=ts),
        grid=(n // tt, ne),
        in_specs=[once(d), once(d), once(LANES), per_expert(wg), per_expert(wu), per_expert(bg), per_expert(bu),
                  per_expert(wd), per_expert(bd)],
        out_specs=pl.BlockSpec((tt, d), lambda i, e: (i, 0)),
        out_shape=jax.ShapeDtypeStruct((n, d), F32),
        scratch_shapes=[pltpu.VMEM((ns, ts, LANES), F32), pltpu.VMEM((ns, LANES, ts), F32),
                        pltpu.VMEM((ns, LANES, ts), F32), pltpu.VMEM((ns, 8, LANES), F32)],
        compiler_params=_cp(("parallel", "arbitrary")),
        name="moe",
    )(hn, h, cmb, wg, wu, bg, bu, wd, bd)


def _rope_table(pos):
    half = 8
    inv = 1.0 / (ROPE_THETA ** (jnp.arange(half, dtype=F32) / half))
    ang = pos.astype(F32)[:, None] * inv[None, :]
    cos, sin = jnp.cos(ang), jnp.sin(ang)
    n = pos.shape[0]
    one = jnp.ones((n, 64 - 2 * half), F32)
    zero8 = jnp.zeros((n, half), F32)
    zero = jnp.zeros((n, 64 - 2 * half), F32)
    c = jnp.concatenate([cos, cos, one], axis=1)
    s1 = jnp.concatenate([-sin, zero8, zero], axis=1)
    s2 = jnp.concatenate([zero8, sin, zero], axis=1)
    return jnp.concatenate([c, c, s1, s1, s2, s2], axis=1)


def _hi_lo(w):
    hi = w.astype(BF16)
    return hi, (w - hi.astype(F32)).astype(BF16)


def kernel(x_prompt, x_sample, cache_cmp_k, cache_cmp_v, cache_sel_k, cache_sel_v, cache_win_k, cache_win_v, state_gla, cache_mem_k, cache_mem_v, page_table, mem_prompt, g_attn, w_in, w_gla_gate, b_gla_gate, g_gla_out, g_q_nsa, g_k_nsa, w_cmp_pos, g_q_mem, g_k_mem, g_mem, w_mem_kv, w_up_gla, w_up_nsa, w_up_mem, w_out, g_ffn, w_router, b_router, w_gate_up, b_gate_up, w_down, b_down):
    B, S, D = x_prompt.shape
    DB, T, _ = x_sample.shape
    n_pool, page = cache_cmp_k.shape[1:3]
    n_pages = page_table.shape[1]
    past_len = n_pages * page
    wb = cache_win_k.shape[2]
    mlen = mem_prompt.shape[1]
    ne = w_router.shape[2]
    Np, Ns = B * S, DB * T
    N = Np + Ns
    TM = 256
    assert cache_cmp_k.shape[0] == 1 and D == 1024 and S % TM == 0 and Ns % TM == 0 and TM % T == 0
    assert T <= NSA_BLOCK and past_len % NSA_BLOCK == 0 and S % NSA_BLOCK == 0

    gq, gk, gv, gr, glr, nq, nkv, ng, mq, mg = np.cumsum(
        [0, 256, 256, 512, 512, GLA_GATE_RANK, 512, 768, 3 * NSA_HEADS, 512]).tolist()
    w = w_in[0]
    wa = w[:, gq:glr].astype(BF16)
    wbm = jnp.concatenate([w[:, nq:ng], w[:, mq:mg]], axis=1).astype(BF16)
    wc = w[:, mg:].astype(BF16)
    wd = jnp.concatenate([w[:, glr:nq], w[:, ng:mq], jnp.zeros((D, LANES - GLA_GATE_RANK - 3 * NSA_HEADS), F32)],
                         axis=1).astype(BF16)

    x_parts = (x_prompt.reshape(Np, D), x_sample.reshape(Ns, D))
    za, zb, zc, zd = _project(*x_parts, g_attn, wa, wbm, wc, wd, TM)

    pos_sample = past_len + jnp.arange(T, dtype=jnp.int32)
    tab_p = _rope_table(jnp.arange(S, dtype=jnp.int32))
    tab_s = jnp.tile(_rope_table(pos_sample), (TM // T, 1))
    gq_t = jnp.tile(g_q_nsa[0], NSA_HEADS)[None, :]
    gk_t = jnp.tile(g_k_nsa[0], (1, NSA_KV_HEADS))
    gm_t = jnp.tile(g_q_mem[0], MEM_HEADS)[None, :]
    (kc_p, vc_p, kct, vct, kst, vst, kwt, vwt, qp, ktsb, ktwb, vs1, vw1, qm_p) = _prep_prompt(
        zb, tab_p, gq_t, gk_t, gm_t, TM, B, S)
    q, kc, vc, ks, vs, kw, vw, qm_s = _prep_sample(zb, tab_s, gq_t, gk_t, gm_t, TM, Np, Ns)

    w2 = jnp.tile(w_cmp_pos[0], (1, 1, NSA_KV_HEADS))
    ck, cv = _compress(kc_p, vc_p, w2, Np, 512)
    o_nsa_p = _nsa_prompt(qp, ck, cv, ktsb, ktwb, vs1, vw1, zd, B, S, 128)

    rows_minor = lambda a: jnp.transpose(a[0], (0, 2, 3, 1)).reshape(a.shape[1], LANES, a.shape[2])
    pool = rows_minor
    npg = 32 if n_pages % 64 == 0 else n_pages // 2
    wt = jnp.tile(jnp.transpose(w2, (0, 2, 1)), (1, 1, page // NSA_BLOCK))
    o_nsa_s = _nsa_sample(page_table, pool(cache_cmp_k), pool(cache_cmp_v), pool(cache_sel_k), pool(cache_sel_v),
                          q, kc, vc, ks, vs, kw, vw, zd, rows_minor(cache_win_k), rows_minor(cache_win_v),
                          wt, w2, T, npg, Np // T)
    hd_cols = np.arange(NSA_BLOCK)
    nsa_cols = np.concatenate([np.concatenate([c * NSA_BLOCK + hd_cols, (NSA_GROUP + c) * NSA_BLOCK + hd_cols])
                               for c in range(NSA_GROUP)])
    o_nsa = (o_nsa_p, o_nsa_s[:, nsa_cols].astype(BF16))

    wgh, wgl = _hi_lo(w_gla_gate[0])
    bg = b_gla_gate[0][None, :]
    gout = g_gla_out[0][None, :]
    s0_p = jnp.zeros((B,) + state_gla.shape[2:], F32)
    o_gla_p, s_gla_p = _gla(za, zd, wgh, wgl, bg, gout, s0_p, B, S, 512, 0, BF16)
    o_gla_s, s_gla_s = _gla(za, zd, wgh, wgl, bg, gout, state_gla[0], DB, T, T, Np, F32)
    o_gla = (o_gla_p, o_gla_s.astype(BF16))

    gkm = jnp.tile(g_k_mem[0], MEM_HEADS)[None, :]
    mem_k, mem_v = _mem_kv(mem_prompt.reshape(B * mlen, D), g_mem, w_mem_kv[0].astype(BF16), gkm, TM)
    o_mem_p = _mem_prompt(qm_p, mem_k, mem_v, B, S, mlen, 512)
    mw = MEM_HEADS * cache_mem_k.shape[-1]
    sb = 8
    mem_rows = lambda a: a[0].reshape(DB, mlen * MEM_HEADS, a.shape[-1])
    o_mem_s = _mem_sample(qm_s, mem_rows(cache_mem_k), mem_rows(cache_mem_v), 0, DB, T, sb)
    o_mem = (o_mem_p, o_mem_s)

    moe_ts = MOE_SUBTILE if N % MOE_SUBTILE == 0 else TM
    n_pad = -(-N // (moe_ts * MOE_SUBTILES)) * (moe_ts * MOE_SUBTILES)
    wr = jnp.pad(w_router[0], ((0, 0), (0, LANES - ne)))
    wrh, wrl = _hi_lo(wr)
    br = jnp.pad(b_router[0], (0, LANES - ne))[None, :]
    h, hn, cmb = _merge(x_parts, o_gla, o_nsa, o_mem, zc, w_up_gla[0].astype(BF16), w_up_nsa[0][nsa_cols].astype(BF16),
                        w_up_mem[0].astype(BF16), w_out[0].astype(BF16), g_ffn, wrh, wrl, br, ne, TM, n_pad)

    wg, wu = _split_gate_up(w_gate_up[0])
    y = _moe(hn, h, cmb, wg, wu, b_gate_up[0][:, None, 0::2], b_gate_up[0][:, None, 1::2],
             w_down[0].astype(BF16), b_down[0][:, None, :], moe_ts, MOE_SUBTILES)

    p_rows = lambda a: jnp.transpose(a.reshape(1, B, NSA_KV_HEADS, NSA_BLOCK, S), (0, 1, 4, 2, 3))
    s_rows = lambda a: a.reshape(1, DB, T, NSA_KV_HEADS, NSA_BLOCK)
    wbp = min(NSA_WINDOW, S)
    s_win = lambda cache, new: jnp.concatenate([cache[:, :, T:], s_rows(new)], axis=2)
    mshape = (1, B, mlen, MEM_HEADS, mw // MEM_HEADS)
    return (y[:Np].reshape(B, S, D), y[Np:N].reshape(DB, T, D),
            p_rows(kct), p_rows(vct), p_rows(kst), p_rows(vst),
            p_rows(kwt)[:, :, S - wbp:], p_rows(vwt)[:, :, S - wbp:],
            s_gla_p[None], mem_k.reshape(mshape), mem_v.reshape(mshape),
            s_rows(kc), s_rows(vc), s_rows(ks), s_rows(vs),
            s_win(cache_win_k, kw), s_win(cache_win_v, vw), s_gla_s[None])
```

```python
import functools

import numpy as np
import jax
import jax.numpy as jnp
from jax import lax
from jax.experimental import pallas as pl
from jax.experimental.pallas import tpu as pltpu

F32 = jnp.float32
BF16 = jnp.bfloat16

GLA_HEADS = 4
GLA_GATE_RANK = 16
GLA_TAU = 16.0
GLA_CHUNK = 64
GLA_SUB = 16
NSA_HEADS = 8
NSA_KV_HEADS = 2
NSA_GROUP = NSA_HEADS // NSA_KV_HEADS
NSA_BLOCK = 64
NSA_TOPN = 16
NSA_WINDOW = 512
MEM_HEADS = 4
ROPE_THETA = 500000.0
N_BRANCH = 3
TOP_K = 4
SWIGLU_LIMIT = 7.0
SWIGLU_ALPHA = 1.702
EPS = 1e-6

LANES = 128
NEG = -1e30
VMEM_LIMIT = 56 * 1024 * 1024


def _cp(sem):
    return pltpu.CompilerParams(dimension_semantics=sem, vmem_limit_bytes=VMEM_LIMIT)


def _dot(a, b):
    return jnp.dot(a.astype(BF16), b.astype(BF16), preferred_element_type=F32)


def _dot_nt(a, b):
    return lax.dot_general(a.astype(BF16), b.astype(BF16), (((1,), (1,)), ((), ())),
                           preferred_element_type=F32)


def _dot_tn(a, b):
    return lax.dot_general(a.astype(BF16), b.astype(BF16), (((0,), (0,)), ((), ())),
                           preferred_element_type=F32)


def _split3(x):
    hi = x.astype(BF16)
    r = x - hi.astype(F32)
    mid = r.astype(BF16)
    lo = (r - mid.astype(F32)).astype(BF16)
    return hi, mid, lo


def _dot_exact_rhs(x, m):
    hi, mid, lo = _split3(x)
    d = functools.partial(jnp.dot, preferred_element_type=F32)
    return d(hi, m) + d(mid, m) + d(lo, m)


def _dot_exact_lhs(m, x):
    hi, mid, lo = _split3(x)
    d = functools.partial(jnp.dot, preferred_element_type=F32)
    return d(m, hi) + d(m, mid) + d(m, lo)


def _dot_hi(x, w_hi, w_lo):
    hi, mid, _ = _split3(x)
    d = functools.partial(jnp.dot, preferred_element_type=F32)
    return d(hi, w_hi) + d(mid, w_hi) + d(hi, w_lo)


def _msoftmax(s, mask):
    s = jnp.where(mask, s, NEG)
    m = jnp.max(s, axis=-1, keepdims=True)
    e = jnp.where(mask, jnp.exp(s - m), 0.0)
    return e / jnp.maximum(jnp.sum(e, axis=-1, keepdims=True), 1e-30)


def _seg_ones(width, seg):
    r = lax.broadcasted_iota(jnp.int32, (width, width), 0) // seg
    c = lax.broadcasted_iota(jnp.int32, (width, width), 1) // seg
    return jnp.where(r == c, 1.0, 0.0).astype(BF16)


def _seg_rms(x, seg, gain):
    ms = _dot_exact_rhs(x * x, _seg_ones(x.shape[-1], seg)) * (1.0 / seg)
    return x * lax.rsqrt(ms + EPS) * gain


def _sigmoid(x):
    return 1.0 / (1.0 + jnp.exp(-x))


def _two_part_specs(parts, tm):
    n_first = parts[0].shape[0] // tm
    n_second = parts[1].shape[0] // tm
    first = pl.BlockSpec((tm, parts[0].shape[1]), lambda i: (jnp.minimum(i, n_first - 1), 0))
    second = pl.BlockSpec((tm, parts[1].shape[1]), lambda i: (jnp.clip(i - n_first, 0, n_second - 1), 0))
    return [first, second]


def _proj_kernel(xp_ref, xs_ref, *refs, n_first):
    @pl.when(pl.program_id(0) < n_first)
    def _():
        _proj_body(xp_ref, *refs)

    @pl.when(pl.program_id(0) >= n_first)
    def _():
        _proj_body(xs_ref, *refs)


def _proj_body(x_ref, g_ref, wa_ref, wb_ref, wc_ref, wd_ref, za_ref, zb_ref, zc_ref, zd_ref):
    x = x_ref[...]
    u = x * lax.rsqrt(jnp.mean(x * x, axis=-1, keepdims=True) + EPS) * g_ref[...]
    ub = u.astype(BF16)
    za_ref[...] = jnp.dot(ub, wa_ref[...], preferred_element_type=F32)
    zb_ref[...] = jnp.dot(ub, wb_ref[...], preferred_element_type=F32)
    zc_ref[...] = _sigmoid(jnp.dot(ub, wc_ref[...], preferred_element_type=F32)).astype(BF16)
    zd_ref[...] = jnp.dot(ub, wd_ref[...], preferred_element_type=F32)


def _project(xp, xs, g, wa, wb, wc, wd, tm):
    n = xp.shape[0] + xs.shape[0]
    full = lambda w: pl.BlockSpec(w.shape, lambda i: (0, 0))
    row = lambda w: pl.BlockSpec((tm, w), lambda i: (i, 0))
    return pl.pallas_call(
        functools.partial(_proj_kernel, n_first=xp.shape[0] // tm),
        grid=(n // tm,),
        in_specs=_two_part_specs((xp, xs), tm) + [full(g), full(wa), full(wb), full(wc), full(wd)],
        out_specs=[row(wa.shape[1]), row(wb.shape[1]), row(wc.shape[1]), row(wd.shape[1])],
        out_shape=[jax.ShapeDtypeStruct((n, wa.shape[1]), F32), jax.ShapeDtypeStruct((n, wb.shape[1]), F32),
                   jax.ShapeDtypeStruct((n, wc.shape[1]), BF16), jax.ShapeDtypeStruct((n, wd.shape[1]), F32)],
        compiler_params=_cp(("parallel",)),
        name="proj",
    )(xp, xs, g, wa, wb, wc, wd)


def _rope(x, tab):
    c, s1, s2 = tab[:, 0:LANES], tab[:, LANES:2 * LANES], tab[:, 2 * LANES:3 * LANES]
    half = 8
    return x * c + pltpu.roll(x, LANES - half, 1) * s1 + pltpu.roll(x, half, 1) * s2


def _prep_common(zb_ref, tab_ref, gq_ref, gk_ref, gm_ref):
    z = zb_ref[...]
    tab = tab_ref[...]
    hd = NSA_BLOCK
    nq = NSA_HEADS * hd
    qn = _seg_rms(z[:, 0:nq], hd, gq_ref[...])
    q = jnp.concatenate([_rope(qn[:, j * LANES:(j + 1) * LANES], tab) for j in range(nq // LANES)], axis=1)
    q = q * (hd ** -0.5)
    kv = z[:, nq:nq + 6 * LANES]
    gk = gk_ref[...]
    ks = [_rope(_seg_rms(kv[:, 2 * j * LANES:(2 * j + 1) * LANES], hd, gk[j:j + 1, :]), tab) for j in range(3)]
    vs = [kv[:, (2 * j + 1) * LANES:(2 * j + 2) * LANES] for j in range(3)]
    zm = z[:, nq + 6 * LANES:]
    mhd = zm.shape[1] // MEM_HEADS
    qm = (_seg_rms(zm, mhd, gm_ref[...]) * (mhd ** -0.5)).astype(BF16)
    return q, ks, vs, qm


def _prep_sample_kernel(zb_ref, tab_ref, gq_ref, gk_ref, gm_ref,
                        q_ref, kc_ref, vc_ref, ks_ref, vs_ref, kw_ref, vw_ref, qm_ref):
    q, ks, vs, qm = _prep_common(zb_ref, tab_ref, gq_ref, gk_ref, gm_ref)
    q_ref[...] = q
    for ref, val in zip((kc_ref, ks_ref, kw_ref, vc_ref, vs_ref, vw_ref), ks + vs):
        ref[...] = val
    qm_ref[...] = qm


def _prep_prompt_kernel(zb_ref, tab_ref, gq_ref, gk_ref, gm_ref,
                        kc_ref, vc_ref, kct_ref, vct_ref, kst_ref, vst_ref, kwt_ref, vwt_ref,
                        qp_ref, ktsb_ref, ktwb_ref, vs1_ref, vw1_ref, qm_ref):
    q, ks, vs, qm = _prep_common(zb_ref, tab_ref, gq_ref, gk_ref, gm_ref)
    kc_ref[...] = ks[0]
    vc_ref[...] = vs[0]
    for ref, val in zip((kct_ref, kst_ref, kwt_ref, vct_ref, vst_ref, vwt_ref), ks + vs):
        ref[...] = val.T
    ktsb_ref[...] = ks[1].T.astype(BF16)
    ktwb_ref[...] = ks[2].T.astype(BF16)
    low = lax.broadcasted_iota(jnp.int32, (q.shape[0], LANES), 1) < NSA_BLOCK
    vs1_ref[...] = jnp.concatenate([jnp.where(low, vs[1], 1.0), jnp.where(low, 1.0, vs[1])], axis=1).astype(BF16)
    vw1_ref[...] = jnp.concatenate([jnp.where(low, vs[2], 1.0), jnp.where(low, 1.0, vs[2])], axis=1).astype(BF16)
    parts = []
    for head in range(NSA_HEADS):
        g = head // NSA_GROUP
        chunk = q[:, (head // 2) * LANES:(head // 2 + 1) * LANES]
        if (head % 2) != g:
            chunk = pltpu.roll(chunk, NSA_BLOCK, 1)
        parts.append(jnp.where(low if g == 0 else jnp.logical_not(low), chunk, 0.0))
    qp_ref[...] = jnp.concatenate(parts, axis=1).astype(BF16)
    qm_ref[...] = qm


def _prep_sample(zb, tab, gq, gk, gm, tm, row0, n_rows):
    blk0 = row0 // tm
    row = lambda w: pl.BlockSpec((tm, w), lambda i: (i, 0))
    full = lambda a: pl.BlockSpec(a.shape, lambda i: (0, 0))
    sd = lambda w, dt=F32: jax.ShapeDtypeStruct((n_rows, w), dt)
    return pl.pallas_call(
        _prep_sample_kernel,
        grid=(n_rows // tm,),
        in_specs=[pl.BlockSpec((tm, zb.shape[1]), lambda i: (blk0 + i, 0)), full(tab), full(gq), full(gk), full(gm)],
        out_specs=[row(512)] + [row(LANES)] * 6 + [row(512)],
        out_shape=[sd(512)] + [sd(LANES)] * 6 + [sd(512, BF16)],
        compiler_params=_cp(("parallel",)),
        name="nsa_prep_sample",
    )(zb, tab, gq, gk, gm)


def _prep_prompt(zb, tab, gq, gk, gm, tm, batch, seq):
    nt = seq // tm
    n = batch * seq
    full = lambda a: pl.BlockSpec(a.shape, lambda b, i: (0, 0))
    row = lambda w: pl.BlockSpec((tm, w), lambda b, i: (b * nt + i, 0))
    tr = pl.BlockSpec((None, LANES, tm), lambda b, i: (b, 0, i))
    sd = lambda w, dt=F32: jax.ShapeDtypeStruct((n, w), dt)
    sdt = lambda dt=F32: jax.ShapeDtypeStruct((batch, LANES, seq), dt)
    return pl.pallas_call(
        _prep_prompt_kernel,
        grid=(batch, nt),
        in_specs=[row(zb.shape[1]), pl.BlockSpec((tm, 3 * LANES), lambda b, i: (i, 0)), full(gq), full(gk), full(gm)],
        out_specs=[row(LANES), row(LANES)] + [tr] * 6 + [row(NSA_HEADS * LANES), tr, tr, row(2 * LANES), row(2 * LANES),
                                                       row(512)],
        out_shape=[sd(LANES), sd(LANES)] + [sdt()] * 6 + [sd(NSA_HEADS * LANES, BF16), sdt(BF16), sdt(BF16),
                                                          sd(2 * LANES, BF16), sd(2 * LANES, BF16), sd(512, BF16)],
        compiler_params=_cp(("parallel", "parallel")),
        name="nsa_prep_prompt",
    )(zb, tab, gq, gk, gm)


def _pool_rows(x, w):
    nblk = x.shape[0] // NSA_BLOCK
    return jnp.sum(x.reshape(nblk, NSA_BLOCK, LANES) * w[None], axis=1)


def _compress_kernel(k_ref, v_ref, w_ref, ck_ref, cv_ref):
    ck_ref[...] = _pool_rows(k_ref[...], w_ref[0])
    cv_ref[...] = _pool_rows(v_ref[...], w_ref[1])


def _compress(kc, vc, w2, n_rows, tm):
    row = pl.BlockSpec((tm, LANES), lambda i: (i, 0))
    out = pl.BlockSpec((tm // NSA_BLOCK, LANES), lambda i: (i, 0))
    sd = jax.ShapeDtypeStruct((n_rows // NSA_BLOCK, LANES), F32)
    return pl.pallas_call(
        _compress_kernel,
        grid=(n_rows // tm,),
        in_specs=[row, row, pl.BlockSpec(w2.shape, lambda i: (0, 0, 0))],
        out_specs=[out, out],
        out_shape=[sd, sd],
        compiler_params=_cp(("parallel",)),
        name="compress",
    )(kc, vc, w2)


def _select(imp, cur, n_cand):
    n_iota = lax.broadcasted_iota(jnp.int32, imp.shape, 1)
    cnt = jnp.zeros(imp.shape, F32)
    for m in range(n_cand):
        col = imp[:, m:m + 1]
        beats = (col > imp) | ((col == imp) & (n_iota > m))
        cnt = cnt + jnp.where(beats & (cur > m), 1.0, 0.0)
    keep = (n_iota == cur) | ((n_iota < cur) & (cnt < NSA_TOPN - 1))
    return jnp.where(keep, 1.0, 0.0)


def _expand_blocks(nb, nkeys, first_key=0):
    kb = (lax.broadcasted_iota(jnp.int32, (nb, nkeys), 1) + first_key) // NSA_BLOCK
    return jnp.where(kb == lax.broadcasted_iota(jnp.int32, (nb, nkeys), 0), 1.0, 0.0).astype(BF16)


def _gate_col(gate, head, j):
    c = GLA_GATE_RANK + head * 3 + j
    return _sigmoid(gate[:, c:c + 1])


def _select_t(imp_t, cur_t, n_cand):
    n_iota = lax.broadcasted_iota(jnp.int32, imp_t.shape, 0)
    cnt = jnp.zeros(imp_t.shape, F32)
    for m in range(n_cand):
        row = imp_t[m:m + 1, :]
        beats = (row > imp_t) | ((row == imp_t) & (n_iota > m))
        cnt = cnt + jnp.where(beats & (cur_t > m), 1.0, 0.0)
    keep = (n_iota == cur_t) | ((n_iota < cur_t) & (cnt < NSA_TOPN - 1))
    return jnp.where(keep, 1.0, 0.0)


def _attend(qh, kt, bias, v_ones):
    s = jnp.dot(qh, kt, preferred_element_type=F32) + bias
    e = jnp.exp((s - jnp.max(s, axis=-1, keepdims=True)).astype(BF16))
    pv = jnp.dot(e, v_ones, preferred_element_type=F32)
    return pv / pltpu.roll(pv, NSA_BLOCK, 1)


def _nsa_prompt_kernel(q_ref, ck_ref, cv_ref, kts_ref, ktw_ref, vs1_ref, vw1_ref, gate_ref, o_ref, osel_ref,
                       *, tq, seq, wslab, nbr):
    i = pl.program_id(1)
    nq = pl.num_programs(1)
    t0 = i * tq
    nb = seq // NSA_BLOCK
    q = q_ref[...]
    pos = t0 + lax.broadcasted_iota(jnp.int32, (tq, 1), 0)
    pos_t = t0 + lax.broadcasted_iota(jnp.int32, (1, tq), 1)
    cur_t = pos_t // NSA_BLOCK
    vis_t = (lax.broadcasted_iota(jnp.int32, (nb, tq), 0) + 1) * NSA_BLOCK <= pos_t + 1
    wstart = pl.multiple_of(jnp.maximum(t0 + tq - wslab, 0), LANES)
    wrel = pos - (wstart + lax.broadcasted_iota(jnp.int32, (tq, wslab), 1))
    wbias = jnp.where((wrel >= 0) & (wrel < NSA_WINDOW), 0.0, NEG)
    ckb = ck_ref[...].astype(BF16)
    cvb = cv_ref[...].astype(BF16)
    cmp_r, win_r = [], []
    for g in range(NSA_KV_HEADS):
        heads = [g * NSA_GROUP + h for h in range(NSA_GROUP)]
        imp_t = jnp.zeros((nb, tq), F32)
        for head in heads:
            qh = q[:, head * LANES:(head + 1) * LANES]
            s_t = jnp.where(vis_t, _dot_nt(ckb, qh), NEG)
            e_t = jnp.where(vis_t, jnp.exp(s_t - jnp.max(s_t, axis=0, keepdims=True)), 0.0)
            pr_t = e_t / jnp.maximum(jnp.sum(e_t, axis=0, keepdims=True), 1e-30)
            imp_t = imp_t + pr_t
            cmp_r.append(_dot_tn(pr_t, cvb))
        sel_t = _select_t(imp_t, cur_t, nb).astype(BF16)
        for j in range(nbr):
            klen = (j + 1) * seq // nbr

            @pl.when(i * nbr // nq == j)
            def _(g=g, klen=klen, sel_t=sel_t, heads=heads):
                kpos = lax.broadcasted_iota(jnp.int32, (tq, klen), 1)
                picked = _dot_tn(sel_t, _expand_blocks(nb, klen)) > 0.5
                sbias = jnp.where(picked & (kpos <= pos), 0.0, NEG)
                kt = kts_ref[:, 0:klen]
                v1 = vs1_ref[0:klen, g * LANES:(g + 1) * LANES]
                for head in heads:
                    hl = slice(head * LANES, (head + 1) * LANES)
                    osel_ref[:, hl] = _attend(q[:, hl], kt, sbias, v1)

        kt = ktw_ref[:, pl.ds(wstart, wslab)]
        v1 = vw1_ref[pl.ds(wstart, wslab), g * LANES:(g + 1) * LANES]
        for head in heads:
            win_r.append(_attend(q[:, head * LANES:(head + 1) * LANES], kt, wbias, v1))

    low = lax.broadcasted_iota(jnp.int32, (tq, LANES), 1) < NSA_BLOCK
    pair = lambda r: jnp.concatenate([jnp.where(low, r[c], r[NSA_GROUP + c]) for c in range(NSA_GROUP)], axis=1)
    sel_r = [osel_ref[:, head * LANES:(head + 1) * LANES] for head in range(NSA_HEADS)]
    sig = _sigmoid(gate_ref[...])
    s_hi = sig.astype(BF16)
    s_lo = (sig - s_hi.astype(F32)).astype(BF16)
    width = NSA_GROUP * LANES
    row = lax.broadcasted_iota(jnp.int32, (LANES, width), 0)
    col = lax.broadcasted_iota(jnp.int32, (LANES, width), 1)
    head_of = jnp.where(col % LANES < NSA_BLOCK, col // LANES, NSA_GROUP + col // LANES)
    out = None
    for j, r in enumerate((cmp_r, sel_r, win_r)):
        pick = jnp.where(row == GLA_GATE_RANK + head_of * 3 + j, 1.0, 0.0).astype(BF16)
        gj = jnp.dot(s_hi, pick, preferred_element_type=F32) + jnp.dot(s_lo, pick, preferred_element_type=F32)
        out = gj * pair(r) if out is None else out + gj * pair(r)
    o_ref[...] = out.astype(BF16)


def _nsa_prompt(qp, ck, cv, kts, ktw, vs1, vw1, zd, batch, seq, tq):
    nq = seq // tq
    nb = seq // NSA_BLOCK
    wslab = min(seq, NSA_WINDOW + tq)
    nbr = 8 if nq % 8 == 0 else (4 if nq % 4 == 0 else 1)
    kern = functools.partial(_nsa_prompt_kernel, tq=tq, seq=seq, wslab=wslab, nbr=nbr)
    per_b =lambda a: pl.BlockSpec((None,) + a.shape[1:], lambda b, i: (b, 0, 0))
    rows_b = lambda a: pl.BlockSpec((seq, a.shape[1]), lambda b, i: (b, 0))
    return pl.pallas_call(
        kern,
        grid=(batch, nq),
        in_specs=[pl.BlockSpec((tq, qp.shape[1]), lambda b, i: (b * nq + i, 0)),
                  pl.BlockSpec((nb, LANES), lambda b, i: (b, 0)),
                  pl.BlockSpec((nb, LANES), lambda b, i: (b, 0)),
                  per_b(kts), per_b(ktw), rows_b(vs1), rows_b(vw1),
                  pl.BlockSpec((tq, LANES), lambda b, i: (b * nq + i, 0))],
        out_specs=pl.BlockSpec((tq, 512), lambda b, i: (b * nq + i, 0)),
        out_shape=jax.ShapeDtypeStruct((batch * seq, 512), BF16),
        scratch_shapes=[pltpu.VMEM((tq, NSA_HEADS * LANES), F32)],
        compiler_params=_cp(("parallel", "parallel")),
        name="nsa_prompt",
    )(qp, ck, cv, kts, ktw, vs1, vw1, zd)


def _qbd(q, rows_per_head):
    hd = NSA_BLOCK
    lane_grp = lax.broadcasted_iota(jnp.int32, (rows_per_head, LANES), 1) // hd
    parts = []
    for head in range(NSA_HEADS):
        g = head // NSA_GROUP
        qh = q[:, head * hd:(head + 1) * hd]
        two = jnp.concatenate([qh, qh], axis=1)
        parts.append(jnp.where(lane_grp == g, two, 0.0))
    return jnp.concatenate(parts, axis=0)


def _online_update(s, mask, v, m_ref, l_ref, acc_ref, v_transposed=False):
    s = jnp.where(mask, s, NEG)
    m_old = m_ref[...]
    m_new = jnp.maximum(m_old, jnp.max(s, axis=-1, keepdims=True))
    alpha = jnp.exp(m_old - m_new)
    e = jnp.where(mask, jnp.exp(s - m_new), 0.0)
    l_ref[...] = alpha * l_ref[...] + jnp.sum(e, axis=-1, keepdims=True)
    acc_ref[...] = alpha * acc_ref[...] + (_dot_nt(e, v) if v_transposed else _dot(e, v))
    m_ref[...] = m_new


def _nsa_sample_kernel(pt_ref, ck_hbm, cv_hbm, sk_hbm, sv_hbm, q_ref, kcn_ref, vcn_ref, ksn_ref, vsn_ref,
                       kwn_ref, vwn_ref, gate_ref, wk_ref, wv_ref, wt_ref, wn_ref, o_ref,
                       buf_ref, sem_ref, pk_ref, pv_ref, m_ref, l_ref, acc_ref, pad_ref, bias_ref,
                       *, npg, t_new, past_len):
    b = pl.program_id(0)
    nseq = pl.num_programs(0)
    hd = NSA_BLOCK
    page = buf_ref.shape[3]
    n_pages = past_len // page
    nch = n_pages // npg
    nbc = npg * page // NSA_BLOCK
    nb_pad = pk_ref.shape[1]
    n_past = past_len // NSA_BLOCK
    rows = NSA_HEADS * t_new
    nkeys = npg * page

    def chunk_copies(seq, c, slot):
        pools = (ck_hbm, cv_hbm) if c < nch else (sk_hbm, sv_hbm)
        first = (c % nch) * npg
        copies = []
        for j in range(npg):
            pg = pt_ref[seq, first + j]
            copies.append(pltpu.make_async_copy(pools[0].at[pg], buf_ref.at[slot, j], sem_ref.at[slot]))
            copies.append(pltpu.make_async_copy(pools[1].at[pg], buf_ref.at[slot, npg + j], sem_ref.at[slot]))
        return copies

    def start_all(copies):
        for n, cp in enumerate(copies):
            cp.start(priority=n % 2)

    @pl.when(b == 0)
    def _():
        start_all(chunk_copies(0, 0, 0))

    q = q_ref[...]
    qbd = _qbd(q, t_new)
    trow = lax.broadcasted_iota(jnp.int32, (rows, 1), 0) % t_new
    pos = past_len + trow
    seg = jnp.where(lax.broadcasted_iota(jnp.int32, (nkeys, nbc), 0) // NSA_BLOCK
                    == lax.broadcasted_iota(jnp.int32, (nkeys, nbc), 1), 1.0, 0.0).astype(BF16)
    expand = _expand_blocks(nbc, nkeys)
    if n_past < nb_pad:
        pk_ref[...] = jnp.zeros(pk_ref.shape, F32)
        pv_ref[...] = jnp.zeros(pv_ref.shape, F32)
    m_ref[...] = jnp.full(m_ref.shape, 0.1 * NEG, F32)
    l_ref[...] = jnp.zeros(l_ref.shape, F32)
    acc_ref[...] = jnp.zeros(acc_ref.shape, F32)
    o_cmp = None

    for c in range(2 * nch):
        slot = c % 2
        if c + 1 < 2 * nch:
            start_all(chunk_copies(b, c + 1, 1 - slot))
        else:
            @pl.when(b + 1 < nseq)
            def _():
                start_all(chunk_copies(b + 1, 0, 1 - slot))
        for cp in chunk_copies(b, c, slot):
            cp.wait()

        if c < nch:
            for base, w_t, pooled in ((0, wt_ref[0], pk_ref), (npg, wt_ref[1], pv_ref)):
                x = jnp.concatenate([(buf_ref[slot, base + j] * w_t).astype(BF16) for j in range(npg)], axis=1)
                pooled[:, c * nbc:(c + 1) * nbc] = jnp.dot(x, seg, preferred_element_type=F32)
        else:
            k0 = (c - nch) * nkeys
            kcat = jnp.concatenate([buf_ref[slot, j].astype(BF16) for j in range(npg)], axis=1)
            vcat = jnp.concatenate([buf_ref[slot, npg + j].astype(BF16) for j in range(npg)], axis=1)
            s = _dot(qbd, kcat) + bias_ref[:, k0:k0 + nkeys]
            m_old = m_ref[...]
            m_new = jnp.maximum(m_old, jnp.max(s, axis=-1, keepdims=True))
            alpha = jnp.exp(m_old - m_new)
            e = jnp.exp(s - m_new)
            l_ref[...] = alpha * l_ref[...] + jnp.sum(e, axis=-1, keepdims=True)
            acc_ref[...] = alpha * acc_ref[...] + _dot_nt(e, vcat)
            m_ref[...] = m_new

        if c == nch - 1:
            newk = jnp.sum(kcn_ref[...] * wn_ref[0][0:t_new], axis=0, keepdims=True)
            newv = jnp.sum(vcn_ref[...] * wn_ref[1][0:t_new], axis=0, keepdims=True)
            blk = lax.broadcasted_iota(jnp.int32, (rows, nb_pad), 1)
            vis = ((blk + 1) * NSA_BLOCK <= pos + 1) & (blk < n_past)
            vis_new = (n_past + 1) * NSA_BLOCK <= pos + 1
            s_past = jnp.where(vis, _dot(qbd, pk_ref[...]), NEG)
            s_new = jnp.where(vis_new, jnp.sum(qbd * newk, axis=-1, keepdims=True), NEG)
            mx = jnp.maximum(jnp.max(s_past, axis=-1, keepdims=True), s_new)
            e_past = jnp.where(vis, jnp.exp(s_past - mx), 0.0)
            e_new = jnp.where(vis_new, jnp.exp(s_new - mx), 0.0)
            den = jnp.maximum(jnp.sum(e_past, axis=-1, keepdims=True) + e_new, 1e-30)
            pr = e_past / den
            o_cmp = _dot_nt(pr, pv_ref[...]) + (e_new / den) * newv
            imps = []
            for g in range(NSA_KV_HEADS):
                acc = pr[g * NSA_GROUP * t_new:(g * NSA_GROUP + 1) * t_new]
                for h in range(1, NSA_GROUP):
                    acc = acc + pr[(g * NSA_GROUP + h) * t_new:(g * NSA_GROUP + h + 1) * t_new]
                imps.append(acc)
            imp = jnp.concatenate(imps, axis=0)
            cur = (past_len + lax.broadcasted_iota(jnp.int32, (NSA_KV_HEADS * t_new, 1), 0) % t_new) // NSA_BLOCK
            sel = _select(imp, cur, n_past).astype(BF16)
            sel_rows = jnp.concatenate([sel[(head // NSA_GROUP) * t_new:(head // NSA_GROUP + 1) * t_new]
                                        for head in range(NSA_HEADS)], axis=0)
            for cc in range(nch):
                picked = jnp.dot(sel_rows[:, cc * nbc:(cc + 1) * nbc], expand, preferred_element_type=F32) > 0.5
                bias_ref[:, cc * nkeys:(cc + 1) * nkeys] = jnp.where(picked, 0.0, NEG)

    pad_ref[...] = jnp.zeros(pad_ref.shape, F32)
    pad_ref[0:t_new, :] = ksn_ref[...]
    kn = pad_ref[...]
    pad_ref[0:t_new, :] = vsn_ref[...]
    vn = pad_ref[...]
    jn = lax.broadcasted_iota(jnp.int32, (rows, LANES), 1)
    _online_update(_dot_nt(qbd, kn), jn <= trow, vn, m_ref, l_ref, acc_ref)
    o_sel = acc_ref[...] / jnp.maximum(l_ref[...], 1e-30)
    wb = wk_ref.shape[1]
    jw = lax.broadcasted_iota(jnp.int32, (rows, wb), 1)
    rel = trow + wb - jw
    wmask = jnp.concatenate([(rel >= 0) & (rel < NSA_WINDOW), jn <= trow], axis=1)
    pad_ref[0:t_new, :] = kwn_ref[...]
    s_win = jnp.concatenate([_dot(qbd, wk_ref[...]), _dot_nt(qbd, pad_ref[...])], axis=1)
    p_win = _msoftmax(s_win, wmask)
    pad_ref[0:t_new, :] = vwn_ref[...]
    o_win = _dot_nt(p_win[:, 0:wb], wv_ref[...]) + _dot(p_win[:, wb:], pad_ref[...])
    gate = gate_ref[...]
    outs = []
    for head in range(NSA_HEADS):
        g = head // NSA_GROUP
        r0, r1 = head * t_new, (head + 1) * t_new
        gl = slice(g * hd, (g + 1) * hd)
        outs.append(_gate_col(gate, head, 0) * o_cmp[r0:r1, gl] + _gate_col(gate, head, 1) * o_sel[r0:r1, gl]
                    + _gate_col(gate, head, 2) * o_win[r0:r1, gl])
    o_ref[...] = jnp.concatenate(outs, axis=1)


def _nsa_sample(page_table, pool_ck, pool_cv, pool_sk, pool_sv, q, kc, vc, ks, vs, kw, vw, zd, win_k, win_v,
                wt, wn, t_new, npg, zd_row0):
    db, n_pages = page_table.shape
    page = pool_ck.shape[2]
    past_len = n_pages * page
    nb_pad = -(-(past_len // NSA_BLOCK) // LANES) * LANES
    row0 = q.shape[0] // t_new - db
    rows = NSA_HEADS * t_new
    assert n_pages % npg == 0 and (2 * n_pages // npg) % 2 == 0
    tok = lambda w: pl.BlockSpec((t_new, w), lambda b, pt: (row0 + b, 0))
    per_seq = lambda a: pl.BlockSpec((None,) + a.shape[1:], lambda b, pt: (b, 0, 0))
    full3 = lambda a: pl.BlockSpec(a.shape, lambda b, pt: (0, 0, 0))
    hbm = pl.BlockSpec(memory_space=pl.ANY)
    kern = functools.partial(_nsa_sample_kernel, npg=npg, t_new=t_new, past_len=past_len)
    grid_spec = pltpu.PrefetchScalarGridSpec(
        num_scalar_prefetch=1,
        grid=(db,),
        in_specs=[hbm, hbm, hbm, hbm, tok(512)] + [tok(LANES)] * 6
        + [pl.BlockSpec((t_new, LANES), lambda b, pt: (zd_row0 + b, 0)), per_seq(win_k), per_seq(win_v),
           full3(wt), full3(wn)],
        out_specs=pl.BlockSpec((t_new, 512), lambda b, pt: (b, 0)),
        scratch_shapes=[pltpu.VMEM((2, 2 * npg, LANES, page), F32), pltpu.SemaphoreType.DMA((2,)),
                        pltpu.VMEM((LANES, nb_pad), F32), pltpu.VMEM((LANES, nb_pad), F32),
                        pltpu.VMEM((rows, 1), F32), pltpu.VMEM((rows, 1), F32), pltpu.VMEM((rows, LANES), F32),
                        pltpu.VMEM((LANES, LANES), F32), pltpu.VMEM((rows, past_len), F32)],
    )
    return pl.pallas_call(
        kern,
        grid_spec=grid_spec,
        out_shape=jax.ShapeDtypeStruct((db * t_new, 512), F32),
        compiler_params=_cp(("arbitrary",)),
        name="nsa_sample",
    )(page_table, pool_ck, pool_cv, pool_sk, pool_sv, q, kc, vc, ks, vs, kw, vw, zd, win_k, win_v, wt, wn)


def _log_sigmoid(x):
    return jnp.minimum(x, 0.0) - jnp.log(1.0 + jnp.exp(-jnp.abs(x)))


def _gla_kernel(*refs, nchunk, t_valid, nseq):
    za_refs, zd_refs = refs[0:nseq], refs[nseq:2 * nseq]
    wgh_ref, wgl_ref, bg_ref, gout_ref, s0_all, o_all_ref, s_all = refs[2 * nseq:2 * nseq + 7]
    scratch = refs[2 * nseq + 7:]
    views = [(za_refs[s], zd_refs[s], s0_all.at[s], o_all_ref.at[s], s_all.at[s]) + tuple(scratch[6 * s:6 * s + 6])
             for s in range(nseq)]
    j = pl.program_id(1)
    short = za_refs[0].shape[0] < GLA_CHUNK
    nj = pl.num_programs(1)
    C = GLA_CHUNK
    sub = GLA_SUB
    nsub = C // sub
    dk = 64
    dv = 128
    hk = GLA_HEADS * dk
    hv = GLA_HEADS * dv

    @pl.when(j == 0)
    def _():
        for (_, _, s0_ref, _, _, st_ref, kp_ref, bp_ref, vp_ref, _, _) in views:
            st_ref[...] = jnp.concatenate([s0_ref[h].T for h in range(GLA_HEADS)], axis=1)
            kp_ref[...] = jnp.zeros(kp_ref.shape, F32)
            bp_ref[...] = jnp.zeros(bp_ref.shape, F32)
            vp_ref[...] = jnp.zeros(vp_ref.shape, F32)

    tril = jnp.where(lax.broadcasted_iota(jnp.int32, (C, C), 0) >= lax.broadcasted_iota(jnp.int32, (C, C), 1),
                     1.0, 0.0).astype(BF16)
    head_ones = jnp.where(lax.broadcasted_iota(jnp.int32, (hk, LANES), 0) // dk
                          == lax.broadcasted_iota(jnp.int32, (hk, LANES), 1), 1.0, 0.0).astype(BF16)
    tmod = lax.broadcasted_iota(jnp.int32, (C, 1), 0) % sub
    gout = gout_ref[...]

    def chunk(c, carry):
        for view in views:
            seq_chunk(c, view)
        return carry

    def seq_chunk(c, view):
        za_ref, zd_ref, _, o_ref, _, st_ref, kp_ref, bp_ref, vp_ref, za_pad_ref, zd_pad_ref = view
        r0 = pl.multiple_of(c * C, C)
        if short:
            za_pad_ref[...] = jnp.zeros(za_pad_ref.shape, F32)
            zd_pad_ref[...] = jnp.zeros(zd_pad_ref.shape, F32)
            za_pad_ref[0:za_ref.shape[0], :] = za_ref[...]
            zd_pad_ref[0:zd_ref.shape[0], :] = zd_ref[...]
            z = za_pad_ref[...]
            zd = zd_pad_ref[...]
        else:
            z = za_ref[pl.ds(r0, C), :]
            zd = zd_ref[pl.ds(r0, C), :]
        q = z[:, 0:hk] * (dk ** -0.5)
        k = z[:, hk:2 * hk]
        v = z[:, 2 * hk:2 * hk + hv]
        r = z[:, 2 * hk + hv:2 * hk + 2 * hv]
        lr = zd[:, 0:GLA_GATE_RANK]
        la = _log_sigmoid(_dot_hi(lr, wgh_ref[...], wgl_ref[...]) + bg_ref[...]) * (1.0 / GLA_TAU)
        tglob = (j * nchunk + c) * C + lax.broadcasted_iota(jnp.int32, (C, 1), 0)
        la = jnp.where(tglob < t_valid, la, 0.0)
        b = _dot_exact_lhs(tril, la)
        st = st_ref[...]
        qe = q * jnp.exp(b)
        kp_ref[sub:sub + C, :] = k
        bp_ref[sub:sub + C, :] = b
        vp_ref[sub:sub + C, :] = v
        xs = []
        for d in range(sub):
            kd = kp_ref[sub - d:sub - d + C, :]
            bd = bp_ref[sub - d:sub - d + C, :]
            xs.append(jnp.where(tmod >= d, q * kd * jnp.exp(b - bd), 0.0).astype(BF16))
        rr = jnp.dot(jnp.concatenate(xs, axis=0), head_ones, preferred_element_type=F32)
        outs = []
        for h in range(GLA_HEADS):
            kh = slice(h * dk, (h + 1) * dk)
            vh = slice(h * dv, (h + 1) * dv)
            o = _dot_nt(qe[:, kh], st[:, kh])
            for d in range(sub):
                o = o + rr[d * C:(d + 1) * C, h:h + 1] * vp_ref[sub - d:sub - d + C, vh]
            offs = [jnp.zeros((sub, dv), F32)]
            for i in range(1, nsub):
                anchor = b[i * sub - 1:i * sub, kh]
                qt = q[i * sub:(i + 1) * sub, kh] * jnp.exp(b[i * sub:(i + 1) * sub, kh] - anchor)
                kt = k[0:i * sub, kh] * jnp.exp(anchor - b[0:i * sub, kh])
                offs.append(_dot(_dot_nt(qt, kt), v[0:i * sub, vh]))
            o = o + jnp.concatenate(offs, axis=0)
            on = o * lax.rsqrt(jnp.mean(o * o, axis=-1, keepdims=True) + EPS) * gout
            rh = r[:, vh]
            outs.append(on * (rh * _sigmoid(rh)))
        o_all = jnp.concatenate(outs, axis=1)
        if short:
            o_ref[...] = o_all[0:o_ref.shape[0], :].astype(o_ref.dtype)
        else:
            o_ref[pl.ds(r0, C), :] = o_all.astype(o_ref.dtype)
        b_last = b[C - 1:C, :]
        kk = k * jnp.exp(b_last - b)
        upd = jnp.concatenate([_dot_tn(v[:, h * dv:(h + 1) * dv], kk[:, h * dk:(h + 1) * dk])
                               for h in range(GLA_HEADS)], axis=1)
        st_ref[...] = st * jnp.exp(b_last) + upd

    lax.fori_loop(0, nchunk, chunk, 0)

    @pl.when(j == nj - 1)
    def _():
        for (_, _, _, _, s_ref, st_ref, _, _, _, _, _) in views:
            st = st_ref[...]
            for h in range(GLA_HEADS):
                s_ref[h] = st[:, h * dk:(h + 1) * dk].T


GLA_SEQS = 2


def _gla(za, zd, wgh, wgl, bg, gout, s0, batch, t_seq, ct, row0, out_dtype):
    nj = t_seq // ct
    blk0 = row0 // ct
    hv = GLA_HEADS * 128
    ns = GLA_SEQS if batch % GLA_SEQS == 0 else 1
    kern = functools.partial(_gla_kernel, nchunk=max(1, ct // GLA_CHUNK), t_valid=t_seq, nseq=ns)
    full = lambda a: pl.BlockSpec(a.shape, lambda b, j: (0,) * a.ndim)
    rows = lambda w, s: pl.BlockSpec((ct, w), lambda b, j: (blk0 + (b * ns + s) * nj + j, 0))
    state = pl.BlockSpec((ns,) + s0.shape[1:], lambda b, j: (b, 0, 0, 0))
    pad_rows = GLA_SUB + GLA_CHUNK
    o, s_new = pl.pallas_call(
        kern,
        grid=(batch // ns, nj),
        in_specs=[rows(za.shape[1], s) for s in range(ns)] + [rows(LANES, s) for s in range(ns)]
        + [full(wgh), full(wgl), full(bg), full(gout), state],
        out_specs=[pl.BlockSpec((ns, ct, hv), lambda b, j: (b, j, 0)), state],
        out_shape=[jax.ShapeDtypeStruct((batch, t_seq, hv), out_dtype), jax.ShapeDtypeStruct(s0.shape, F32)],
        scratch_shapes=[pltpu.VMEM((128, GLA_HEADS * 64), F32),
                        pltpu.VMEM((pad_rows, GLA_HEADS * 64), F32),
                        pltpu.VMEM((pad_rows, GLA_HEADS * 64), F32),
                        pltpu.VMEM((pad_rows, hv), F32),
                        pltpu.VMEM((GLA_CHUNK, za.shape[1]), F32),
                        pltpu.VMEM((GLA_CHUNK, LANES), F32)] * ns,
        compiler_params=_cp(("parallel", "arbitrary")),
        name="gla",
    )(*([za] * ns), *([zd] * ns), wgh, wgl, bg, gout, s0)
    return o.reshape(batch * t_seq, hv), s_new


def _mem_kv_kernel(m_ref, g_ref, w_ref, gk_ref, k_ref, v_ref):
    x = m_ref[...]
    u = x * lax.rsqrt(jnp.mean(x * x, axis=-1, keepdims=True) + EPS) * g_ref[...]
    kv = jnp.dot(u.astype(BF16), w_ref[...], preferred_element_type=F32)
    half = kv.shape[1] // 2
    k_ref[...] = _seg_rms(kv[:, 0:half], half // MEM_HEADS, gk_ref[...])
    v_ref[...] = kv[:, half:]


def _mem_kv(mem, g, w, gk, tm):
    n, d = mem.shape
    half = w.shape[1] // 2
    full = lambda a: pl.BlockSpec(a.shape, lambda i: (0, 0))
    sd = jax.ShapeDtypeStruct((n, half), F32)
    return pl.pallas_call(
        _mem_kv_kernel,
        grid=(n // tm,),
        in_specs=[pl.BlockSpec((tm, d), lambda i: (i, 0)), full(g), full(w), full(gk)],
        out_specs=[pl.BlockSpec((tm, half), lambda i: (i, 0))] * 2,
        out_shape=[sd, sd],
        compiler_params=_cp(("parallel",)),
        name="mem_kv",
    )(mem, g, w, gk)


def _mem_prompt_kernel(q_ref, k_ref, v_ref, o_ref):
    q = q_ref[...]
    k = k_ref[...].astype(BF16)
    v = v_ref[...].astype(BF16)
    hd = q.shape[1] // MEM_HEADS
    outs = []
    for h in range(MEM_HEADS):
        sl = slice(h * hd, (h + 1) * hd)
        s = _dot_nt(q[:, sl], k[:, sl])
        m = jnp.max(s, axis=-1, keepdims=True)
        e = jnp.exp(s - m)
        outs.append(_dot(e / jnp.sum(e, axis=-1, keepdims=True), v[:, sl]))
    o_ref[...] = jnp.concatenate(outs, axis=1).astype(BF16)


def _mem_prompt(qm, mk, mv, batch, seq, mlen, tq):
    nq = seq // tq
    w = qm.shape[1]
    return pl.pallas_call(
        _mem_prompt_kernel,
        grid=(batch, nq),
        in_specs=[pl.BlockSpec((tq, w), lambda b, i: (b * nq + i, 0)),
                  pl.BlockSpec((mlen, w), lambda b, i: (b, 0)),
                  pl.BlockSpec((mlen, w), lambda b, i: (b, 0))],
        out_specs=pl.BlockSpec((tq, w), lambda b, i: (b * nq + i, 0)),
        out_shape=jax.ShapeDtypeStruct((batch * seq, w), BF16),
        compiler_params=_cp(("parallel", "parallel")),
        name="mem_prompt",
    )(qm, mk, mv)


def _mem_sample_kernel(q_ref, k_ref, v_ref, o_ref, *, sb, t_new):
    w = q_ref.shape[1]
    hd = w // MEM_HEADS
    rows = MEM_HEADS * t_new
    nk = k_ref.shape[1]
    same_head = (lax.broadcasted_iota(jnp.int32, (rows, nk), 1) % MEM_HEADS
                 == lax.broadcasted_iota(jnp.int32, (rows, nk), 0) // t_new)
    bias = jnp.where(same_head, 0.0, NEG)
    for s_i in range(sb):
        q = q_ref[s_i * t_new:(s_i + 1) * t_new, :].astype(F32)
        qs = jnp.concatenate([q[:, h * hd:(h + 1) * hd] for h in range(MEM_HEADS)], axis=0)
        s = _dot_nt(qs, k_ref[s_i]) + bias
        e = jnp.exp(s - jnp.max(s, axis=-1, keepdims=True))
        o = _dot(e / jnp.sum(e, axis=-1, keepdims=True), v_ref[s_i])
        o_ref[s_i * t_new:(s_i + 1) * t_new, :] = jnp.concatenate(
            [o[h * t_new:(h + 1) * t_new, :] for h in range(MEM_HEADS)], axis=1).astype(BF16)


def _mem_sample(qm, ck, cv, row0_blocks, db, t_new, sb):
    w = qm.shape[1]
    kern = functools.partial(_mem_sample_kernel, sb=sb, t_new=t_new)
    return pl.pallas_call(
        kern,
        grid=(db // sb,),
        in_specs=[pl.BlockSpec((sb * t_new, w), lambda i: (row0_blocks + i, 0)),
                  pl.BlockSpec((sb,) + ck.shape[1:], lambda i: (i, 0, 0)),
                  pl.BlockSpec((sb,) + cv.shape[1:], lambda i: (i, 0, 0))],
        out_specs=pl.BlockSpec((sb * t_new, w), lambda i: (i, 0)),
        out_shape=jax.ShapeDtypeStruct((db * t_new, w), BF16),
        compiler_params=_cp(("parallel",)),
        name="mem_sample",
    )(qm, ck, cv)


def _merge_kernel(xp_ref, xs_ref, ogp_ref, ogs_ref, onp_ref, ons_ref, omp_ref, oms_ref, gate_ref,
                  wg_ref, wn_ref, wm_ref, wo_ref, gf_ref, wrh_ref, wrl_ref, br_ref, h_ref, hn_ref, cmb_ref,
                  *, n_experts, n_first, n_tiles):
    pid = pl.program_id(0)
    rest = (gate_ref, wg_ref, wn_ref, wm_ref, wo_ref, gf_ref, wrh_ref, wrl_ref, br_ref, h_ref, hn_ref, cmb_ref,
            n_experts)

    @pl.when(pid >= n_tiles)
    def _():
        h_ref[...] = jnp.zeros(h_ref.shape, h_ref.dtype)
        hn_ref[...] = jnp.zeros(hn_ref.shape, hn_ref.dtype)
        cmb_ref[...] = jnp.zeros(cmb_ref.shape, cmb_ref.dtype)

    @pl.when(pid < n_first)
    def _():
        _merge_body(xp_ref, ogp_ref, onp_ref, omp_ref, *rest)

    @pl.when((pid >= n_first) & (pid < n_tiles))
    def _():
        _merge_body(xs_ref, ogs_ref, ons_ref, oms_ref, *rest)


def _merge_body(x_ref, og_ref, on_ref, om_ref, gate_ref, wg_ref, wn_ref, wm_ref, wo_ref, gf_ref,
                wrh_ref, wrl_ref, br_ref, h_ref, hn_ref, cmb_ref, n_experts):
    d = x_ref.shape[1]
    gate = gate_ref[...].astype(F32)
    y = (gate[:, 0:d] * jnp.dot(og_ref[...], wg_ref[...], preferred_element_type=F32)
         + gate[:, d:2 * d] * jnp.dot(on_ref[...], wn_ref[...], preferred_element_type=F32)
         + gate[:, 2 * d:3 * d] * jnp.dot(om_ref[...], wm_ref[...], preferred_element_type=F32))
    h = x_ref[...] + jnp.dot(y.astype(BF16), wo_ref[...], preferred_element_type=F32)
    h_ref[...] = h
    hn = h * lax.rsqrt(jnp.mean(h * h, axis=-1, keepdims=True) + EPS) * gf_ref[...]
    hn_ref[...] = hn.astype(BF16)
    logits = _dot_hi(hn, wrh_ref[...], wrl_ref[...]) + br_ref[...]
    lane = lax.broadcasted_iota(jnp.int32, logits.shape, 1).astype(F32)
    work = jnp.where(lane < n_experts, logits, NEG)
    top = jnp.max(work, axis=-1, keepdims=True)
    chosen = jnp.zeros(logits.shape, F32)
    for _ in range(TOP_K):
        mx = jnp.max(work, axis=-1, keepdims=True)
        first = jnp.min(jnp.where(work == mx, lane, float(LANES)), axis=-1, keepdims=True)
        pick = lane == first
        chosen = jnp.where(pick, 1.0, chosen)
        work = jnp.where(pick, NEG, work)
    e = jnp.where(chosen > 0.5, jnp.exp(logits - top), 0.0)
    cmb_ref[...] = e / jnp.sum(e, axis=-1, keepdims=True)


def _merge(x, og, on, om, gate, wg, wn, wm, wo, gf, wrh, wrl, br, n_experts, tm, n_pad):
    d = x[0].shape[1]
    n_first = x[0].shape[0] // tm
    n_tiles = n_first + x[1].shape[0] // tm
    row = lambda a: pl.BlockSpec((tm, a.shape[1]), lambda i: (jnp.minimum(i, n_tiles - 1), 0))
    full = lambda a: pl.BlockSpec(a.shape, lambda i: (0, 0))
    kern = functools.partial(_merge_kernel, n_experts=n_experts, n_first=n_first, n_tiles=n_tiles)
    return pl.pallas_call(
        kern,
        grid=(n_pad // tm,),
        in_specs=_two_part_specs(x, tm) + _two_part_specs(og, tm) + _two_part_specs(on, tm) + _two_part_specs(om, tm)
        + [row(gate), full(wg), full(wn), full(wm), full(wo), full(gf), full(wrh), full(wrl), full(br)],
        out_specs=[pl.BlockSpec((tm, d), lambda i: (i, 0)), pl.BlockSpec((tm, d), lambda i: (i, 0)),
                   pl.BlockSpec((tm, LANES), lambda i: (i, 0))],
        out_shape=[jax.ShapeDtypeStruct((n_pad, d), F32), jax.ShapeDtypeStruct((n_pad, d), BF16),
                   jax.ShapeDtypeStruct((n_pad, LANES), F32)],
        compiler_params=_cp(("parallel",)),
        name="merge_router",
    )(*x, *og, *on, *om, gate, wg, wn, wm, wo, gf, wrh, wrl, br)


MOE_SUBTILE = 1024
MOE_SUBTILES = 2
MOE_ROW_BLOCK = 160


SPLIT_COLS = 512


def _split_gate_up_kernel(w_ref, wg_ref, wu_ref):
    half = SPLIT_COLS // 2
    r = lax.broadcasted_iota(jnp.int32, (SPLIT_COLS, half), 0)
    c = lax.broadcasted_iota(jnp.int32, (SPLIT_COLS, half), 1)
    even = jnp.where(r == 2 * c, 1.0, 0.0).astype(BF16)
    odd = jnp.where(r == 2 * c + 1, 1.0, 0.0).astype(BF16)
    for t in range(w_ref.shape[1] // SPLIT_COLS):
        blk = w_ref[:, t * SPLIT_COLS:(t + 1) * SPLIT_COLS].astype(BF16)
        wg_ref[:, t * half:(t + 1) * half] = jnp.dot(blk, even, preferred_element_type=F32).astype(BF16)
        wu_ref[:, t * half:(t + 1) * half] = jnp.dot(blk, odd, preferred_element_type=F32).astype(BF16)


def _split_gate_up(w):
    ne, d, dff2 = w.shape
    sd = jax.ShapeDtypeStruct((ne, d, dff2 // 2), BF16)
    out = pl.BlockSpec((None, d, dff2 // 2), lambda e: (e, 0, 0))
    return pl.pallas_call(
        _split_gate_up_kernel,
        grid=(ne,),
        in_specs=[pl.BlockSpec((None, d, dff2), lambda e: (e, 0, 0))],
        out_specs=[out, out],
        out_shape=[sd, sd],
        compiler_params=_cp(("parallel",)),
        name="split_gate_up",
    )(w)


def _moe_kernel(hn_ref, h_ref, cmb_ref, wg_ref, wu_ref, bg_ref, bu_ref, wd_ref, bd_ref, y_ref,
                rank_ref, rankt_ref, cmbt_ref, cnt_ref, *, ts):
    e = pl.program_id(1)
    ns = hn_ref.shape[0] // ts
    rb = MOE_ROW_BLOCK

    @pl.when(e == 0)
    def _():
        y_ref[...] = h_ref[...]
        lower = jnp.where(lax.broadcasted_iota(jnp.int32, (ts, ts), 0) > lax.broadcasted_iota(jnp.int32, (ts, ts), 1),
                          1.0, 0.0).astype(BF16)
        for s in range(ns):
            cmb = cmb_ref[s * ts:(s + 1) * ts, :]
            sel = jnp.where(cmb > 0.0, 1.0, 0.0)
            rank = jnp.where(sel > 0.0, jnp.dot(lower, sel.astype(BF16), preferred_element_type=F32), -1.0)
            rank_ref[s] = rank
            rankt_ref[s] = rank.T
            cmbt_ref[s] = cmb.T
            cnt_ref[s] = jnp.broadcast_to(jnp.sum(sel, axis=0, keepdims=True), cnt_ref.shape[1:])

    lane = lax.broadcasted_iota(jnp.int32, (1, LANES), 1)
    pick = jnp.where(lax.broadcasted_iota(jnp.int32, (LANES, LANES), 0) == e, 1.0, 0.0).astype(BF16)
    n_iter = 0
    rank_cols = []
    for s in range(ns):
        cnt = jnp.sum(jnp.where(lane == e, cnt_ref[s, 0:1, :], 0.0)).astype(jnp.int32)
        n_iter = jnp.maximum(n_iter, (cnt + rb - 1) // rb)
        r = rank_ref[s]
        r_hi = r.astype(BF16)
        r_lo = (r - r_hi.astype(F32)).astype(BF16)
        rc = (jnp.dot(r_hi, pick, preferred_element_type=F32)
              + jnp.dot(r_lo, pick, preferred_element_type=F32))
        rank_cols.append(jnp.concatenate([rc] * (-(-rb // LANES)), axis=1)[:, 0:rb])

    def body(i, carry):
        r0 = i * rb
        rows = (r0 + lax.broadcasted_iota(jnp.int32, (rb, ts), 0)).astype(F32)
        xs, ws = [], []
        for s in range(ns):
            hit = rankt_ref[s, pl.ds(e, 1), :] == rows
            xs.append(jnp.dot(jnp.where(hit, 1.0, 0.0).astype(BF16), hn_ref[s * ts:(s + 1) * ts, :],
                              preferred_element_type=F32).astype(BF16))
            ws.append(jnp.sum(jnp.where(hit, cmbt_ref[s, pl.ds(e, 1), :], 0.0), axis=-1, keepdims=True))
        x = jnp.concatenate(xs, axis=0)
        gate = jnp.minimum(jnp.dot(x, wg_ref[...], preferred_element_type=F32) + bg_ref[...], SWIGLU_LIMIT)
        up = jnp.clip(jnp.dot(x, wu_ref[...], preferred_element_type=F32) + bu_ref[...], -SWIGLU_LIMIT, SWIGLU_LIMIT)
        act = (up + 1.0) * gate * _sigmoid(SWIGLU_ALPHA * gate)
        out = jnp.dot(act.astype(BF16), wd_ref[...], preferred_element_type=F32) + bd_ref[...]
        cols = (r0 + lax.broadcasted_iota(jnp.int32, (ts, rb), 1)).astype(F32)
        for s in range(ns):
            yw = (out[s * rb:(s + 1) * rb, :] * ws[s]).astype(BF16)
            pt = jnp.where(rank_cols[s] == cols, 1.0, 0.0).astype(BF16)
            y_ref[s * ts:(s + 1) * ts, :] += jnp.dot(pt, yw, preferred_element_type=F32)
        return carry

    lax.fori_loop(0, n_iter, body, 0)


def _moe(hn, h, cmb, wg, wu, bg, bu, wd, bd, ts, ns):
    n, d = h.shape
    ne, _, dff = wg.shape
    tt = ts * ns
    once = lambda w: pl.BlockSpec((tt, w), lambda i, e: (i, 0), pipeline_mode=pl.Buffered(1))
    per_expert = lambda a: pl.BlockSpec((None,) + a.shape[1:], lambda i, e: (e, 0, 0))
    return pl.pallas_call(
        functools.partial(_moe_kernel, ts=ts),
        grid=(n // tt, ne),
        in_specs=[once(d), once(d), once(LANES), per_expert(wg), per_expert(wu), per_expert(bg), per_expert(bu),
                  per_expert(wd), per_expert(bd)],
        out_specs=pl.BlockSpec((tt, d), lambda i, e: (i, 0)),
        out_shape=jax.ShapeDtypeStruct((n, d), F32),
        scratch_shapes=[pltpu.VMEM((ns, ts, LANES), F32), pltpu.VMEM((ns, LANES, ts), F32),
                        pltpu.VMEM((ns, LANES, ts), F32), pltpu.VMEM((ns, 8, LANES), F32)],
        compiler_params=_cp(("parallel", "arbitrary")),
        name="moe",
    )(hn, h, cmb, wg, wu, bg, bu, wd, bd)


def _rope_table(pos):
    half = 8
    inv = 1.0 / (ROPE_THETA ** (jnp.arange(half, dtype=F32) / half))
    ang = pos.astype(F32)[:, None] * inv[None, :]
    cos, sin = jnp.cos(ang), jnp.sin(ang)
    n = pos.shape[0]
    one = jnp.ones((n, 64 - 2 * half), F32)
    zero8 = jnp.zeros((n, half), F32)
    zero = jnp.zeros((n, 64 - 2 * half), F32)
    c = jnp.concatenate([cos, cos, one], axis=1)
    s1 = jnp.concatenate([-sin, zero8, zero], axis=1)
    s2 = jnp.concatenate([zero8, sin, zero], axis=1)
    return jnp.concatenate([c, c, s1, s1, s2, s2], axis=1)


def _hi_lo(w):
    hi = w.astype(BF16)
    return hi, (w - hi.astype(F32)).astype(BF16)


def kernel(x_prompt, x_sample, cache_cmp_k, cache_cmp_v, cache_sel_k, cache_sel_v, cache_win_k, cache_win_v, state_gla, cache_mem_k, cache_mem_v, page_table, mem_prompt, g_attn, w_in, w_gla_gate, b_gla_gate, g_gla_out, g_q_nsa, g_k_nsa, w_cmp_pos, g_q_mem, g_k_mem, g_mem, w_mem_kv, w_up_gla, w_up_nsa, w_up_mem, w_out, g_ffn, w_router, b_router, w_gate_up, b_gate_up, w_down, b_down):
    B, S, D = x_prompt.shape
    DB, T, _ = x_sample.shape
    n_pool, page = cache_cmp_k.shape[1:3]
    n_pages = page_table.shape[1]
    past_len = n_pages * page
    wb = cache_win_k.shape[2]
    mlen = mem_prompt.shape[1]
    ne = w_router.shape[2]
    Np, Ns = B * S, DB * T
    N = Np + Ns
    TM = 256
    assert cache_cmp_k.shape[0] == 1 and D == 1024 and S % TM == 0 and Ns % TM == 0 and TM % T == 0
    assert T <= NSA_BLOCK and past_len % NSA_BLOCK == 0 and S % NSA_BLOCK == 0

    gq, gk, gv, gr, glr, nq, nkv, ng, mq, mg = np.cumsum(
        [0, 256, 256, 512, 512, GLA_GATE_RANK, 512, 768, 3 * NSA_HEADS, 512]).tolist()
    w = w_in[0]
    wa = w[:, gq:glr].astype(BF16)
    wbm = jnp.concatenate([w[:, nq:ng], w[:, mq:mg]], axis=1).astype(BF16)
    wc = w[:, mg:].astype(BF16)
    wd = jnp.concatenate([w[:, glr:nq], w[:, ng:mq], jnp.zeros((D, LANES - GLA_GATE_RANK - 3 * NSA_HEADS), F32)],
                         axis=1).astype(BF16)

    x_parts = (x_prompt.reshape(Np, D), x_sample.reshape(Ns, D))
    za, zb, zc, zd = _project(*x_parts, g_attn, wa, wbm, wc, wd, TM)

    pos_sample = past_len + jnp.arange(T, dtype=jnp.int32)
    tab_p = _rope_table(jnp.arange(S, dtype=jnp.int32))
    tab_s = jnp.tile(_rope_table(pos_sample), (TM // T, 1))
    gq_t = jnp.tile(g_q_nsa[0], NSA_HEADS)[None, :]
    gk_t = jnp.tile(g_k_nsa[0], (1, NSA_KV_HEADS))
    gm_t = jnp.tile(g_q_mem[0], MEM_HEADS)[None, :]
    (kc_p, vc_p, kct, vct, kst, vst, kwt, vwt, qp, ktsb, ktwb, vs1, vw1, qm_p) = _prep_prompt(
        zb, tab_p, gq_t, gk_t, gm_t, TM, B, S)
    q, kc, vc, ks, vs, kw, vw, qm_s = _prep_sample(zb, tab_s, gq_t, gk_t, gm_t, TM, Np, Ns)

    w2 = jnp.tile(w_cmp_pos[0], (1, 1, NSA_KV_HEADS))
    ck, cv = _compress(kc_p, vc_p, w2, Np, 512)
    o_nsa_p = _nsa_prompt(qp, ck, cv, ktsb, ktwb, vs1, vw1, zd, B, S, 128)

    rows_minor = lambda a: jnp.transpose(a[0], (0, 2, 3, 1)).reshape(a.shape[1], LANES, a.shape[2])
    pool = rows_minor
    npg = 32 if n_pages % 64 == 0 else n_pages // 2
    wt = jnp.tile(jnp.transpose(w2, (0, 2, 1)), (1, 1, page // NSA_BLOCK))
    o_nsa_s = _nsa_sample(page_table, pool(cache_cmp_k), pool(cache_cmp_v), pool(cache_sel_k), pool(cache_sel_v),
                          q, kc, vc, ks, vs, kw, vw, zd, rows_minor(cache_win_k), rows_minor(cache_win_v),
                          wt, w2, T, npg, Np // T)
    hd_cols = np.arange(NSA_BLOCK)
    nsa_cols = np.concatenate([np.concatenate([c * NSA_BLOCK + hd_cols, (NSA_GROUP + c) * NSA_BLOCK + hd_cols])
                               for c in range(NSA_GROUP)])
    o_nsa = (o_nsa_p, o_nsa_s[:, nsa_cols].astype(BF16))

    wgh, wgl = _hi_lo(w_gla_gate[0])
    bg = b_gla_gate[0][None, :]
    gout = g_gla_out[0][None, :]
    s0_p = jnp.zeros((B,) + state_gla.shape[2:], F32)
    o_gla_p, s_gla_p = _gla(za, zd, wgh, wgl, bg, gout, s0_p, B, S, 512, 0, BF16)
    o_gla_s, s_gla_s = _gla(za, zd, wgh, wgl, bg, gout, state_gla[0], DB, T, T, Np, F32)
    o_gla = (o_gla_p, o_gla_s.astype(BF16))

    gkm = jnp.tile(g_k_mem[0], MEM_HEADS)[None, :]
    mem_k, mem_v = _mem_kv(mem_prompt.reshape(B * mlen, D), g_mem, w_mem_kv[0].astype(BF16), gkm, TM)
    o_mem_p = _mem_prompt(qm_p, mem_k, mem_v, B, S, mlen, 512)
    mw = MEM_HEADS * cache_mem_k.shape[-1]
    sb = 8
    mem_rows = lambda a: a[0].reshape(DB, mlen * MEM_HEADS, a.shape[-1])
    o_mem_s = _mem_sample(qm_s, mem_rows(cache_mem_k), mem_rows(cache_mem_v), 0, DB, T, sb)
    o_mem = (o_mem_p, o_mem_s)

    moe_ts = MOE_SUBTILE if N % MOE_SUBTILE == 0 else TM
    n_pad = -(-N // (moe_ts * MOE_SUBTILES)) * (moe_ts * MOE_SUBTILES)
    wr = jnp.pad(w_router[0], ((0, 0), (0, LANES - ne)))
    wrh, wrl = _hi_lo(wr)
    br = jnp.pad(b_router[0], (0, LANES - ne))[None, :]
    h, hn, cmb = _merge(x_parts, o_gla, o_nsa, o_mem, zc, w_up_gla[0].astype(BF16), w_up_nsa[0][nsa_cols].astype(BF16),
                        w_up_mem[0].astype(BF16), w_out[0].astype(BF16), g_ffn, wrh, wrl, br, ne, TM, n_pad)

    wg, wu = _split_gate_up(w_gate_up[0])
    y = _moe(hn, h, cmb, wg, wu, b_gate_up[0][:, None, 0::2], b_gate_up[0][:, None, 1::2],
             w_down[0].astype(BF16), b_down[0][:, None, :], moe_ts, MOE_SUBTILES)

    p_rows = lambda a: jnp.transpose(a.reshape(1, B, NSA_KV_HEADS, NSA_BLOCK, S), (0, 1, 4, 2, 3))
    s_rows = lambda a: a.reshape(1, DB, T, NSA_KV_HEADS, NSA_BLOCK)
    wbp = min(NSA_WINDOW, S)
    s_win = lambda cache, new: jnp.concatenate([cache[:, :, T:], s_rows(new)], axis=2)
    mshape = (1, B, mlen, MEM_HEADS, mw // MEM_HEADS)
    return (y[:Np].reshape(B, S, D), y[Np:N].reshape(DB, T, D),
            p_rows(kct), p_rows(vct), p_rows(kst), p_rows(vst),
            p_rows(kwt)[:, :, S - wbp:], p_rows(vwt)[:, :, S - wbp:],
            s_gla_p[None], mem_k.reshape(mshape), mem_v.reshape(mshape),
            s_rows(kc), s_rows(vc), s_rows(ks), s_rows(vs),
            s_win(cache_win_k, kw), s_win(cache_win_v, vw), s_gla_s[None])
```
